```python
import math
import jax
import jax.numpy as jnp
from jax import lax
import numpy as np

D_MODEL = 2048
BATCH = 2
SEQ = 4096
DEPTH = 4

CTX_LEN = 256
GRID_W = 64

GROUP_DIM = D_MODEL // 4
HEAD_DIM = 64
N_GROUP_HEADS = GROUP_DIM // HEAD_DIM
D_MIX = 4 * GROUP_DIM

RW_W_LORA = 64
RW_A_LORA = 64
RW_G_LORA = 128
RW_DECAY_SCALE = 0.606531
RW_GN_EPS = 64e-5
RW_IN = 3 * GROUP_DIM + 2 * RW_W_LORA + 2 * RW_A_LORA + RW_G_LORA
RW_SPLITS = [GROUP_DIM, 2 * GROUP_DIM, 3 * GROUP_DIM, 3 * GROUP_DIM + 2 * RW_W_LORA, 3 * GROUP_DIM + 2 * RW_W_LORA + 2 * RW_A_LORA]

MB_GROUPS = 2
MB_HEADS_PER_GROUP = N_GROUP_HEADS // MB_GROUPS
MB_STATE = 128
MB_CONV = 5
MB_CHUNK = 128
MB_CONV_DIM = GROUP_DIM + 2 * MB_GROUPS * MB_STATE
MB_IN = GROUP_DIM + MB_CONV_DIM + 2 * N_GROUP_HEADS
MB_SPLITS = [GROUP_DIM, GROUP_DIM + MB_CONV_DIM]

MLA_Q_LORA = 3 * D_MODEL // 16
MLA_KV_LORA = D_MODEL // 16
MLA_NOPE = 64
MLA_ROPE = 32
MLA_V = HEAD_DIM
MLA_IN = MLA_Q_LORA + MLA_KV_LORA + MLA_ROPE
MLA_SPLITS = [MLA_Q_LORA, MLA_Q_LORA + MLA_KV_LORA]
ROPE_BASE = 10000.0

NA_WIN_ROWS = 8
NA_WIN_COLS = 16
NA_IN = 3 * GROUP_DIM
NA_SPLITS = [GROUP_DIM, 2 * GROUP_DIM]

N_IN = RW_IN + MB_IN + MLA_IN + NA_IN
GROUP_SPLITS = [RW_IN, RW_IN + MB_IN, RW_IN + MB_IN + MLA_IN]

N_EXPERTS = 16
N_EXPERT_GROUPS = 4
EXPERTS_PER_GROUP = N_EXPERTS // N_EXPERT_GROUPS
TOP_K = 2
D_EXPERT = D_MODEL // 4
ROUTED_SCALE = 2.5

DEEPNORM_ALPHA = (2 * DEPTH) ** 0.25
DEEPNORM_BETA = (8 * DEPTH) ** -0.25
Q_BLOCK = 128
LN_EPS = 1e-6
RMS_EPS = 1e-6

kernel_name = 'hybrid_dit_rwkv7_ssd_mla_natten_moe'


def layer_norm(x, g, b):
    xf = x.astype(jnp.float32)
    mu = jnp.mean(xf, -1, keepdims=True)
    var = jnp.mean(jnp.square(xf - mu), -1, keepdims=True)
    return ((xf - mu) * lax.rsqrt(var + LN_EPS)).astype(x.dtype) * g + b


def rms_norm(x, g):
    xf = x.astype(jnp.float32)
    return (xf * lax.rsqrt(jnp.mean(xf * xf, -1, keepdims=True) + RMS_EPS)).astype(x.dtype) * g


def shift_prev(x):
    return jnp.pad(x, ((0, 0), (1, 0), (0, 0)))[:, :-1]


def shift_next(x):
    return jnp.pad(x, ((0, 0), (0, 1), (0, 0)))[:, 1:]


def centred_depthwise_conv(x, w, b):
    k = w.shape[0]
    y = lax.conv_general_dilated(x, w[:, None, :], window_strides=(1,), padding=[(k // 2, k // 2)],
                                 dimension_numbers=('NWC', 'WIO', 'NWC'), feature_group_count=x.shape[-1])
    return y + b


def axial_rope(x):
    bsz, length, n_heads, dim = x.shape
    n_freq = dim // 4
    t = jnp.arange(length)
    pos = jnp.stack([t // GRID_W, t % GRID_W], axis=-1).astype(jnp.float32)
    inv_freq = ROPE_BASE ** (-jnp.arange(n_freq, dtype=jnp.float32) / n_freq)
    ang = pos[:, :, None] * inv_freq
    cos = jnp.cos(ang)[None, :, None]
    sin = jnp.sin(ang)[None, :, None]
    xr = x.astype(jnp.float32).reshape(bsz, length, n_heads, 2, 2, n_freq)
    x1, x2 = xr[..., 0, :], xr[..., 1, :]
    out = jnp.stack([x1 * cos - x2 * sin, x2 * cos + x1 * sin], axis=-2)
    return out.reshape(x.shape).astype(x.dtype)


def block_attention(q, k, v, scale):
    bsz, lq, n_heads, dq = q.shape
    qb = jnp.moveaxis(q.reshape(bsz, lq // Q_BLOCK, Q_BLOCK, n_heads, dq), 1, 0)

    def one(q_blk):
        s = jnp.einsum('bqhd,bkhd->bhqk', q_blk, k).astype(jnp.float32) * scale
        p = jax.nn.softmax(s, axis=-1).astype(v.dtype)
        return jnp.einsum('bhqk,bkhd->bqhd', p, v)

    o = lax.map(one, qb)
    return jnp.moveaxis(o, 0, 1).reshape(bsz, lq, n_heads, v.shape[-1])


def neighborhood_attention(q, k, v, k_ctx, v_ctx, rpb, scale):
    bsz, length, n_heads, dim = q.shape
    rows = length // GRID_W
    wr = min(NA_WIN_ROWS, rows)
    grid = lambda t: t.reshape(bsz, rows, GRID_W, n_heads, dim)
    kg, vg = grid(k), grid(v)
    cols = np.arange(GRID_W)
    col_start = np.clip(cols - NA_WIN_COLS // 2, 0, GRID_W - NA_WIN_COLS)
    col_idx = col_start[:, None] + np.arange(NA_WIN_COLS)[None, :]
    col_off = col_idx - cols[:, None] + (NA_WIN_COLS - 1)
    n_loc = wr * NA_WIN_COLS

    def one_row(args):
        r, q_row = args
        rs = jnp.clip(r - wr // 2, 0, rows - wr)
        k_rows = lax.dynamic_slice_in_dim(kg, rs, wr, axis=1)
        v_rows = lax.dynamic_slice_in_dim(vg, rs, wr, axis=1)
        k_win = k_rows[:, :, col_idx]
        v_win = v_rows[:, :, col_idx]
        row_off = rs + jnp.arange(wr) - r + (NA_WIN_ROWS - 1)
        bias = jnp.transpose(rpb[:, row_off][:, :, col_off], (0, 2, 1, 3))
        s_loc = jnp.einsum('bqhd,brqchd->bhqrc', q_row, k_win).astype(jnp.float32) * scale + bias
        s_ctx = jnp.einsum('bqhd,bkhd->bhqk', q_row, k_ctx).astype(jnp.float32) * scale
        s = jnp.concatenate([s_loc.reshape(bsz, n_heads, GRID_W, n_loc), s_ctx], axis=-1)
        p = jax.nn.softmax(s, axis=-1).astype(v.dtype)
        p_loc = p[..., :n_loc].reshape(bsz, n_heads, GRID_W, wr, NA_WIN_COLS)
        return (jnp.einsum('bhqrc,brqchd->bqhd', p_loc, v_win)
                + jnp.einsum('bhqk,bkhd->bqhd', p[..., n_loc:], v_ctx))

    out = lax.map(one_row, (jnp.arange(rows), jnp.moveaxis(grid(q), 1, 0)))
    return jnp.moveaxis(out, 0, 1).reshape(bsz, length, n_heads, dim)


def rwkv_prep(p, mu, w0, w_up, a0, a_up, g_up, k_k, k_a):
    f32 = jnp.float32
    bsz, length = p.shape[:2]
    p = p + mu[0] * (shift_prev(p) - p) + mu[1] * (shift_next(p) - p)
    r, k, v, wd, ad, gd = jnp.split(p, RW_SPLITS, axis=-1)
    heads = lambda t: t.astype(f32).reshape(t.shape[:-1] + (N_GROUP_HEADS, HEAD_DIM))
    w_pre = w0 + jnp.einsum('bldr,drc->bldc', jnp.tanh(wd.reshape(bsz, length, 2, RW_W_LORA)), w_up)
    decay = jnp.exp(-RW_DECAY_SCALE * jax.nn.sigmoid(w_pre.astype(f32)))
    a = jax.nn.sigmoid((a0 + jnp.einsum('bldr,drc->bldc', ad.reshape(bsz, length, 2, RW_A_LORA), a_up)).astype(f32))
    g = jax.nn.sigmoid(gd) @ g_up
    kk = heads(k * k_k)
    kk = kk * lax.rsqrt(jnp.maximum(jnp.sum(kk * kk, -1, keepdims=True), 1e-12))
    k_dir = k.astype(f32)[:, :, None] * (1 + (a - 1) * k_a)
    return heads(r), heads(k), heads(v), kk, heads(decay), heads(a), heads(k_dir), g


def rwkv_scan(r, w, kk, a, v, k, s0, reverse):
    def step(s, inp):
        r_t, w_t, kk_t, a_t, v_t, k_t = inp
        sa = jnp.einsum('bhij,bhj->bhi', s, -kk_t)
        s = s * w_t[:, :, None, :] + sa[..., None] * (kk_t * a_t)[:, :, None, :] + v_t[..., None] * k_t[:, :, None, :]
        return s, jnp.einsum('bhij,bhj->bhi', s, r_t)

    xs = tuple(jnp.moveaxis(t, 1, 0) for t in (r, w, kk, a, v, k))
    s_final, y = lax.scan(step, s0, xs, reverse=reverse)
    return jnp.moveaxis(y, 0, 1), s_final


def rwkv_output(y, r, k, v, g, r_k, gn_g, gn_b):
    mu = jnp.mean(y, -1, keepdims=True)
    var = jnp.mean(jnp.square(y - mu), -1, keepdims=True)
    y = (y - mu) * lax.rsqrt(var + RW_GN_EPS) * gn_g.reshape(N_GROUP_HEADS, HEAD_DIM) + gn_b.reshape(N_GROUP_HEADS, HEAD_DIM)
    y = y + jnp.sum(r * k * r_k, -1, keepdims=True) * v
    bsz, length = y.shape[:2]
    return y.reshape(bsz, length, GROUP_DIM).astype(g.dtype) * g


def rwkv_mixer(p_ctx, p_lat, mu, w0, w_up, a0, a_up, g_up, k_k, k_a, r_k, gn_g, gn_b, need_ctx):
    rc, kc, vc, kkc, wc, ac, kdc, gc = rwkv_prep(p_ctx, mu, w0, w_up, a0, a_up, g_up, k_k, k_a)
    rl, kl, vl, kkl, wl, al, kdl, gl = rwkv_prep(p_lat, mu, w0, w_up, a0, a_up, g_up, k_k, k_a)
    s0 = jnp.zeros((p_lat.shape[0], N_GROUP_HEADS, HEAD_DIM, HEAD_DIM), jnp.float32)
    y_c, s_f = rwkv_scan(rc, wc[:, :, 0], kkc, ac[:, :, 0], vc, kdc[:, :, 0], s0, False)
    y_cb, s_b = rwkv_scan(rc, wc[:, :, 1], kkc, ac[:, :, 1], vc, kdc[:, :, 1], s0, True)
    y_lf, _ = rwkv_scan(rl, wl[:, :, 0], kkl, al[:, :, 0], vl, kdl[:, :, 0], s_f, False)
    y_lb, _ = rwkv_scan(rl, wl[:, :, 1], kkl, al[:, :, 1], vl, kdl[:, :, 1], s_b, True)
    y_lat = rwkv_output(y_lf + y_lb, rl, kl, vl, gl, r_k, gn_g, gn_b)
    y_ctx = rwkv_output(y_c + y_cb, rc, kc, vc, gc, r_k, gn_g, gn_b) if need_ctx else None
    return y_lat, y_ctx


def segsum(x):
    t = x.shape[-1]
    xr = jnp.broadcast_to(x[..., :, None], x.shape + (t,))
    cs = jnp.cumsum(jnp.where(jnp.tril(jnp.ones((t, t), bool), -1), xr, 0.0), axis=-2)
    return jnp.where(jnp.tril(jnp.ones((t, t), bool)), cs, -jnp.inf)


def ssd_scan(xs, dt, a, bm, cm, s0):
    bsz, length, n_heads, hdim = xs.shape
    nc = length // MB_CHUNK
    chunk = lambda t: t.reshape((bsz, nc, MB_CHUNK) + t.shape[2:])
    xdt = chunk(xs * dt[..., None])
    bc, cc = chunk(bm), chunk(cm)
    da = jnp.transpose(chunk(dt * a), (0, 3, 1, 2))
    da_cs = jnp.cumsum(da, axis=-1)
    l_mat = jnp.exp(segsum(da))
    y_diag = jnp.einsum('bclhn,bcshn,bhcls,bcshp->bclhp', cc, bc, l_mat, xdt)
    decay_states = jnp.exp(da_cs[..., -1:] - da_cs)
    states = jnp.einsum('bclhn,bhcl,bclhp->bchpn', bc, decay_states, xdt)
    states = jnp.concatenate([s0[:, None], states], axis=1)
    decay_chunk = jnp.exp(segsum(jnp.pad(da_cs[..., -1], ((0, 0), (0, 0), (1, 0)))))
    new_states = jnp.einsum('bhzc,bchpn->bzhpn', decay_chunk, states)
    y_off = jnp.einsum('bclhn,bchpn,bhcl->bclhp', cc, new_states[:, :-1], jnp.exp(da_cs))
    return (y_diag + y_off).reshape(bsz, length, n_heads, hdim), new_states[:, -1]


def ssd_direction(xs, dt, bm, cm, a, s0, reverse):
    if reverse:
        xs, dt, bm, cm = (jnp.flip(t, 1) for t in (xs, dt, bm, cm))
    y, s = ssd_scan(xs, dt, a, bm, cm, s0)
    return (jnp.flip(y, 1) if reverse else y), s


def mamba_prep(p, conv_w, conv_b, dt_bias):
    bsz, length = p.shape[:2]
    z, xbc, dt_raw = jnp.split(p, MB_SPLITS, axis=-1)
    xbc = jax.nn.silu(centred_depthwise_conv(xbc, conv_w, conv_b))
    xs, bm, cm = jnp.split(xbc, [GROUP_DIM, GROUP_DIM + MB_GROUPS * MB_STATE], axis=-1)
    xs = xs.reshape(bsz, length, N_GROUP_HEADS, HEAD_DIM)
    bm = jnp.repeat(bm.reshape(bsz, length, MB_GROUPS, MB_STATE), MB_HEADS_PER_GROUP, axis=2)
    cm = jnp.repeat(cm.reshape(bsz, length, MB_GROUPS, MB_STATE), MB_HEADS_PER_GROUP, axis=2)
    dt = jax.nn.softplus(dt_raw.reshape(bsz, length, 2, N_GROUP_HEADS).astype(jnp.float32) + dt_bias)
    return z, xs, bm, cm, dt


def mamba_output(y, z, norm_g):
    bsz, length = y.shape[:2]
    y = y.reshape(bsz, length, GROUP_DIM) * jax.nn.silu(z.astype(jnp.float32))
    y = y.reshape(bsz, length, MB_GROUPS, GROUP_DIM // MB_GROUPS)
    y = y * lax.rsqrt(jnp.mean(y * y, -1, keepdims=True) + RMS_EPS)
    return (y.reshape(bsz, length, GROUP_DIM) * norm_g).astype(z.dtype)


def mamba_mixer(p_ctx, p_lat, conv_w, conv_b, a_log, dt_bias, d_skip, norm_g, need_ctx):
    zc, xc, bc, cc, dtc = mamba_prep(p_ctx, conv_w, conv_b, dt_bias)
    zl, xl, bl, cl, dtl = mamba_prep(p_lat, conv_w, conv_b, dt_bias)
    s0 = jnp.zeros((p_lat.shape[0], N_GROUP_HEADS, HEAD_DIM, MB_STATE), jnp.float32)
    a_f = -jnp.exp(a_log[0].astype(jnp.float32))
    a_b = -jnp.exp(a_log[1].astype(jnp.float32))
    y_cf, s_f = ssd_direction(xc, dtc[:, :, 0], bc, cc, a_f, s0, False)
    y_cb, s_b = ssd_direction(xc, dtc[:, :, 1], bc, cc, a_b, s0, True)
    y_lf, _ = ssd_direction(xl, dtl[:, :, 0], bl, cl, a_f, s_f, False)
    y_lb, _ = ssd_direction(xl, dtl[:, :, 1], bl, cl, a_b, s_b, True)
    y_lat = mamba_output(y_lf + y_lb + d_skip[:, None] * xl, zl, norm_g)
    y_ctx = mamba_output(y_cf + y_cb + d_skip[:, None] * xc, zc, norm_g) if need_ctx else None
    return y_lat, y_ctx


def mla_mixer(p_ctx, p_lat, q_norm, w_uq, kv_norm, w_ukv, need_ctx):
    def keys_values(p, rotate):
        bsz, length = p.shape[:2]
        _, kv_d, k_pe = jnp.split(p, MLA_SPLITS, axis=-1)
        kv = (rms_norm(kv_d, kv_norm) @ w_ukv).reshape(bsz, length, N_GROUP_HEADS, MLA_NOPE + MLA_V)
        k_nope, v = jnp.split(kv, [MLA_NOPE], axis=-1)
        k_pe = k_pe[:, :, None, :]
        if rotate:
            k_pe = axial_rope(k_pe)
        k_pe = jnp.broadcast_to(k_pe, (bsz, length, N_GROUP_HEADS, MLA_ROPE))
        return jnp.concatenate([k_nope, k_pe], axis=-1), v

    def queries(p, rotate):
        bsz, length = p.shape[:2]
        q = (rms_norm(p[..., :MLA_Q_LORA], q_norm) @ w_uq).reshape(bsz, length, N_GROUP_HEADS, MLA_NOPE + MLA_ROPE)
        q_nope, q_pe = jnp.split(q, [MLA_NOPE], axis=-1)
        if rotate:
            q_pe = axial_rope(q_pe)
        return jnp.concatenate([q_nope, q_pe], axis=-1)

    scale = (MLA_NOPE + MLA_ROPE) ** -0.5
    k_c, v_c = keys_values(p_ctx, False)
    k_l, v_l = keys_values(p_lat, True)
    bsz, length = p_lat.shape[:2]
    y_lat = block_attention(queries(p_lat, True), jnp.concatenate([k_l, k_c], axis=1),
                            jnp.concatenate([v_l, v_c], axis=1), scale).reshape(bsz, length, GROUP_DIM)
    y_ctx = None
    if need_ctx:
        y_ctx = block_attention(queries(p_ctx, False), k_c, v_c, scale).reshape(bsz, p_ctx.shape[1], GROUP_DIM)
    return y_lat, y_ctx


def natten_mixer(p_ctx, p_lat, rpb, need_ctx):
    heads = lambda t: t.reshape(t.shape[:-1] + (N_GROUP_HEADS, HEAD_DIM))
    q_c, k_c, v_c = (heads(t) for t in jnp.split(p_ctx, NA_SPLITS, axis=-1))
    q_l, k_l, v_l = (heads(t) for t in jnp.split(p_lat, NA_SPLITS, axis=-1))
    scale = HEAD_DIM ** -0.5
    bsz, length = p_lat.shape[:2]
    y_lat = neighborhood_attention(q_l, k_l, v_l, k_c, v_c, rpb, scale).reshape(bsz, length, GROUP_DIM)
    y_ctx = None
    if need_ctx:
        y_ctx = block_attention(q_c, k_c, v_c, scale).reshape(bsz, p_ctx.shape[1], GROUP_DIM)
    return y_lat, y_ctx


def moe_ffn(h, router_w, router_b, w_gate, w_up, w_down):
    n_tok = h.shape[0]
    scores = jax.nn.sigmoid((h @ router_w).astype(jnp.float32))
    grouped = (scores + router_b).reshape(n_tok, N_EXPERT_GROUPS, EXPERTS_PER_GROUP)
    group_score = jnp.sum(lax.top_k(grouped, TOP_K)[0], axis=-1)
    g_sel = jnp.argmax(group_score, axis=-1)
    in_group = jnp.take_along_axis(grouped, g_sel[:, None, None], axis=1)[:, 0]
    _, local = lax.top_k(in_group, TOP_K)
    idx = g_sel[:, None] * EXPERTS_PER_GROUP + local
    w_sel = jnp.take_along_axis(scores, idx, axis=-1)
    w_sel = w_sel / jnp.sum(w_sel, -1, keepdims=True) * ROUTED_SCALE
    gate = jnp.sum(jax.nn.one_hot(idx, N_EXPERTS, dtype=jnp.float32) * w_sel[..., None], axis=1).astype(h.dtype)
    out = jnp.zeros_like(h)
    for e in range(N_EXPERTS):
        hid = jax.nn.silu(h @ w_gate[e]) * (h @ w_up[e])
        out = out + gate[:, e:e + 1] * (hid @ w_down[e])
    return out


def setup_inputs(seed: int = 0) -> dict:
    key = jax.random.key(seed)
    ks = iter(jax.random.split(key, 64))
    f32 = jnp.float32

    def nrm(shape, scale):
        return jax.random.normal(next(ks), shape, f32) * scale

    def unif(shape, lo, hi):
        return jax.random.uniform(next(ks), shape, f32, lo, hi)

    nl, d, gd, nh = DEPTH, D_MODEL, GROUP_DIM, N_GROUP_HEADS
    dt0 = jnp.exp(unif((nl, 2, nh), math.log(1e-3), math.log(1e-1)))
    return {
        'x': nrm((BATCH, SEQ, d), 1.0),
        'c': nrm((BATCH, d), 1.0),
        'ctx': nrm((BATCH, CTX_LEN, d), 1.0),
        'c_ctx': nrm((d,), 1.0),
        'ada_w': nrm((nl, d, 6 * d), 0.5 * d ** -0.5),
        'ada_b': nrm((nl, 6 * d), 0.02),
        'w_in': nrm((nl, d, N_IN), d ** -0.5),
        'w_out': nrm((nl, D_MIX, d), DEEPNORM_BETA * D_MIX ** -0.5),
        'ln1_g': 1.0 + nrm((nl, d), 0.02),
        'ln1_b': nrm((nl, d), 0.02),
        'ln2_g': 1.0 + nrm((nl, d), 0.02),
        'ln2_b': nrm((nl, d), 0.02),
        'rw_mu': unif((nl, 2, RW_IN), 0.0, 0.5),
        'rw_w0': nrm((nl, 2, gd), 1.0),
        'rw_w_up': nrm((nl, 2, RW_W_LORA, gd), 0.1 * RW_W_LORA ** -0.5),
        'rw_a0': nrm((nl, 2, gd), 0.5),
        'rw_a_up': nrm((nl, 2, RW_A_LORA, gd), 0.1 * RW_A_LORA ** -0.5),
        'rw_g_up': nrm((nl, RW_G_LORA, gd), RW_G_LORA ** -0.5),
        'rw_k_k': 0.85 + nrm((nl, gd), 0.02),
        'rw_k_a': 1.0 + nrm((nl, gd), 0.02),
        'rw_r_k': nrm((nl, nh, HEAD_DIM), 0.1),
        'rw_gn_g': 1.0 + nrm((nl, gd), 0.02),
        'rw_gn_b': nrm((nl, gd), 0.02),
        'mb_conv_w': nrm((nl, MB_CONV, MB_CONV_DIM), MB_CONV ** -0.5),
        'mb_conv_b': nrm((nl, MB_CONV_DIM), 0.02),
        'mb_a_log': jnp.log(unif((nl, 2, nh), 1.0, 16.0)),
        'mb_dt_bias': dt0 + jnp.log(-jnp.expm1(-dt0)),
        'mb_d': 1.0 + nrm((nl, nh), 0.02),
        'mb_norm_g': 1.0 + nrm((nl, gd), 0.02),
        'mla_q_norm': 1.0 + nrm((nl, MLA_Q_LORA), 0.02),
        'mla_w_uq': nrm((nl, MLA_Q_LORA, nh * (MLA_NOPE + MLA_ROPE)), MLA_Q_LORA ** -0.5),
        'mla_kv_norm': 1.0 + nrm((nl, MLA_KV_LORA), 0.02),
        'mla_w_ukv': nrm((nl, MLA_KV_LORA, nh * (MLA_NOPE + MLA_V)), MLA_KV_LORA ** -0.5),
        'na_rpb': nrm((nl, nh, 2 * NA_WIN_ROWS - 1, 2 * NA_WIN_COLS - 1), 0.1),
        'router_w': nrm((d, N_EXPERTS), d ** -0.5),
        'router_b': nrm((N_EXPERTS,), 0.01),
        'exp_w_gate': nrm((nl, N_EXPERTS, d, D_EXPERT), d ** -0.5),
        'exp_w_up': nrm((nl, N_EXPERTS, d, D_EXPERT), d ** -0.5),
        'exp_w_down': nrm((nl, N_EXPERTS, D_EXPERT, d), DEEPNORM_BETA * D_EXPERT ** -0.5),
    }


def reference(x, c, ctx, c_ctx, ada_w, ada_b, w_in, w_out, ln1_g, ln1_b, ln2_g, ln2_b,
              rw_mu, rw_w0, rw_w_up, rw_a0, rw_a_up, rw_g_up, rw_k_k, rw_k_a, rw_r_k, rw_gn_g, rw_gn_b,
              mb_conv_w, mb_conv_b, mb_a_log, mb_dt_bias, mb_d, mb_norm_g,
              mla_q_norm, mla_w_uq, mla_kv_norm, mla_w_ukv, na_rpb,
              router_w, router_b, exp_w_gate, exp_w_up, exp_w_down):
    silu_c = jax.nn.silu(c)
    silu_cc = jax.nn.silu(c_ctx)
    for l in range(DEPTH):
        need_ctx = l < DEPTH - 1
        mod = silu_c @ ada_w[l] + ada_b[l]
        mod_c = silu_cc @ ada_w[l] + ada_b[l]
        sh1, sc1, g1, sh2, sc2, g2 = jnp.split(mod[:, None, :], 6, axis=-1)
        csh1, csc1, cg1, csh2, csc2, cg2 = jnp.split(mod_c, 6, axis=-1)

        p_lat = (x * (1 + sc1) + sh1) @ w_in[l]
        p_ctx = (ctx * (1 + csc1) + csh1) @ w_in[l]
        rw_l, mb_l, mla_l, na_l = jnp.split(p_lat, GROUP_SPLITS, axis=-1)
        rw_c, mb_c, mla_c, na_c = jnp.split(p_ctx, GROUP_SPLITS, axis=-1)
        o_rw_l, o_rw_c = rwkv_mixer(rw_c, rw_l, rw_mu[l], rw_w0[l], rw_w_up[l], rw_a0[l], rw_a_up[l], rw_g_up[l],
                                    rw_k_k[l], rw_k_a[l], rw_r_k[l], rw_gn_g[l], rw_gn_b[l], need_ctx)
        o_mb_l, o_mb_c = mamba_mixer(mb_c, mb_l, mb_conv_w[l], mb_conv_b[l], mb_a_log[l], mb_dt_bias[l],
                                     mb_d[l], mb_norm_g[l], need_ctx)
        o_mla_l, o_mla_c = mla_mixer(mla_c, mla_l, mla_q_norm[l], mla_w_uq[l], mla_kv_norm[l], mla_w_ukv[l], need_ctx)
        o_na_l, o_na_c = natten_mixer(na_c, na_l, na_rpb[l], need_ctx)
        y_lat = jnp.concatenate([o_rw_l, o_mb_l, o_mla_l, o_na_l], axis=-1).astype(x.dtype) @ w_out[l]
        x = layer_norm(DEEPNORM_ALPHA * x + g1 * y_lat, ln1_g[l], ln1_b[l])
        if need_ctx:
            y_ctx = jnp.concatenate([o_rw_c, o_mb_c, o_mla_c, o_na_c], axis=-1).astype(ctx.dtype) @ w_out[l]
            ctx = layer_norm(DEEPNORM_ALPHA * ctx + cg1 * y_ctx, ln1_g[l], ln1_b[l])

        h_lat = (x * (1 + sc2) + sh2).reshape(-1, D_MODEL)
        if need_ctx:
            h_ctx = (ctx * (1 + csc2) + csh2).reshape(-1, D_MODEL)
            f = moe_ffn(jnp.concatenate([h_ctx, h_lat], axis=0), router_w, router_b,
                        exp_w_gate[l], exp_w_up[l], exp_w_down[l])
            n_c = h_ctx.shape[0]
            f_lat = f[n_c:].reshape(x.shape)
            ctx = layer_norm(DEEPNORM_ALPHA * ctx + cg2 * f[:n_c].reshape(ctx.shape), ln2_g[l], ln2_b[l])
        else:
            f_lat = moe_ffn(h_lat, router_w, router_b, exp_w_gate[l], exp_w_up[l], exp_w_down[l]).reshape(x.shape)
        x = layer_norm(DEEPNORM_ALPHA * x + g2 * f_lat, ln2_g[l], ln2_b[l])
    return x
```

```python
import functools
import math

import numpy as np
import jax
import jax.numpy as jnp
from jax import lax
from jax.experimental import pallas as pl
from jax.experimental.pallas import tpu as pltpu

F32 = jnp.float32
BF16 = jnp.bfloat16
HIGHEST = lax.Precision.HIGHEST

D_MODEL = 2048
DEPTH = 4
GRID_W = 64
GROUP_DIM = D_MODEL // 4
HEAD_DIM = 64
N_HEADS = GROUP_DIM // HEAD_DIM

RW_LORA = 64
RW_G_LORA = 128
RW_DECAY_SCALE = 0.606531
RW_GN_EPS = 64e-5
RW_IN = 3 * GROUP_DIM + 4 * RW_LORA + RW_G_LORA
RW_CHUNK = 64

MB_GROUPS = 2
MB_STATE = 128
MB_CONV = 5
MB_CHUNK = 128
MB_CONV_DIM = GROUP_DIM + 2 * MB_GROUPS * MB_STATE
MB_IN = GROUP_DIM + MB_CONV_DIM + 2 * N_HEADS

MLA_Q_LORA = 3 * D_MODEL // 16
MLA_KV_LORA = D_MODEL // 16
MLA_NOPE = 64
MLA_ROPE = 32
MLA_QK = MLA_NOPE + MLA_ROPE
MLA_QK_PAD = 128
MLA_IN = MLA_Q_LORA + MLA_KV_LORA + MLA_ROPE
ROPE_BASE = 10000.0

NA_WIN_ROWS = 8
NA_WIN_COLS = 16
NA_IN = 3 * GROUP_DIM
NA_MASK = -1e30

N_EXPERTS = 16
N_EXPERT_GROUPS = 4
EXPERTS_PER_GROUP = N_EXPERTS // N_EXPERT_GROUPS
TOP_K = 2
D_EXPERT = D_MODEL // 4
ROUTED_SCALE = 2.5

DEEPNORM_ALPHA = (2 * DEPTH) ** 0.25
LN_EPS = 1e-6
RMS_EPS = 1e-6

LANE = 128
OFF_RW = 0
OFF_MB_Z = OFF_RW + RW_IN
OFF_MB_XBC = OFF_MB_Z + GROUP_DIM
OFF_MB_DT = OFF_MB_XBC + MB_CONV_DIM
OFF_MLA_Q = OFF_MB_DT + LANE
OFF_MLA_KV = OFF_MLA_Q + MLA_Q_LORA
OFF_MLA_PE = OFF_MLA_KV + MLA_KV_LORA
OFF_NA = OFF_MLA_PE + LANE
N_IN_PAD = OFF_NA + NA_IN

VMEM_LIMIT = 56 * 1024 * 1024


def _params(*sem):
    return pltpu.CompilerParams(dimension_semantics=sem, vmem_limit_bytes=VMEM_LIMIT)


def _pick_tile(n, target, quantum):
    best = None
    for t in range(quantum, min(n, target) + 1, quantum):
        if n % t == 0:
            best = t
    assert best is not None, (n, target, quantum)
    return best


def _mm_body(x_ref, w_ref, o_ref, *, precision):
    o_ref[...] = jnp.dot(x_ref[...], w_ref[...], preferred_element_type=F32,
                         precision=precision).astype(o_ref.dtype)


def matmul(x, w, *, tm_target=512, tn_target=1024, precision=None, out_dtype=F32):
    m, k = x.shape
    n = w.shape[1]
    tm = _pick_tile(m, tm_target, 16)
    tn = _pick_tile(n, tn_target, LANE)
    return pl.pallas_call(
        functools.partial(_mm_body, precision=precision),
        grid=(m // tm, n // tn),
        in_specs=[pl.BlockSpec((tm, k), lambda i, j: (i, 0)),
                  pl.BlockSpec((k, tn), lambda i, j: (0, j))],
        out_specs=pl.BlockSpec((tm, tn), lambda i, j: (i, j)),
        out_shape=jax.ShapeDtypeStruct((m, n), out_dtype),
        compiler_params=_params("parallel", "parallel"),
        name="matmul",
    )(x, w)


def _mod_body(c_ref, w_ref, b_ref, o_ref, *, n_rows):
    w = w_ref[...]
    reps = w.shape[1] // LANE
    o_ref[...] = jnp.zeros_like(o_ref)
    for m in range(n_rows):
        cb = c_ref[m]
        cbt = jnp.concatenate([cb] * reps, axis=1)
        o_ref[m:m + 1, :] = jnp.sum(w * cbt, axis=0, keepdims=True) + b_ref[...]


def modulation(cond, ada_w, ada_b):
    n_rows, k = cond.shape
    depth, _, n = ada_w.shape
    tn = 512
    cond_b = jnp.broadcast_to(cond[:, :, None], (n_rows, k, LANE))
    return pl.pallas_call(
        functools.partial(_mod_body, n_rows=n_rows),
        grid=(depth, n // tn),
        in_specs=[pl.BlockSpec((n_rows, k, LANE), lambda l, j: (0, 0, 0)),
                  pl.BlockSpec((None, k, tn), lambda l, j: (l, 0, j)),
                  pl.BlockSpec((None, 1, tn), lambda l, j: (l, 0, j))],
        out_specs=pl.BlockSpec((None, 8, tn), lambda l, j: (l, 0, j)),
        out_shape=jax.ShapeDtypeStruct((depth, 8, n), F32),
        compiler_params=_params("parallel", "parallel"),
        name="modulation",
    )(cond_b, ada_w, ada_b.reshape(depth, 1, n))


def _bmm(a, b, spec, hi):
    if hi:
        return jnp.einsum(spec, a, b, preferred_element_type=F32, precision=HIGHEST)
    return jnp.einsum(spec, a.astype(BF16), b.astype(BF16), preferred_element_type=F32)


def _rwkv_body(r_ref, k_ref, v_ref, kk_ref, a_ref, lw_ref, y_ref, st_ref, *, reverse):
    @pl.when(pl.program_id(1) == 0)
    def _():
        st_ref[...] = jnp.zeros_like(st_ref)

    r, k, v, kk, a, lw = (ref[0] for ref in (r_ref, k_ref, v_ref, kk_ref, a_ref, lw_ref))
    n_h, c, n = r.shape
    ti = lax.broadcasted_iota(jnp.int32, (c, c), 0)
    si = lax.broadcasted_iota(jnp.int32, (c, c), 1)
    incl = (si >= ti) if reverse else (si <= ti)
    strict = (si > ti) if reverse else (si < ti)
    tri = jnp.broadcast_to(incl.astype(F32)[None], (n_h, c, c))
    cum = _bmm(tri, lw, 'hts,hsn->htn', True)
    tot = cum[:, :1, :] if reverse else cum[:, c - 1:, :]
    b = kk * a
    e_neg = jnp.exp(-cum)
    e_rem = jnp.exp(tot - cum)
    rq = r * jnp.exp(cum)
    kq = kk * jnp.exp(cum - lw)
    kd = k * e_neg
    bd = b * e_neg
    kdc = k * e_rem
    bdc = b * e_rem

    qq = jnp.concatenate([rq, kq], axis=1)
    ak = _bmm(qq, kd, 'htn,hsn->hts', False)
    ab = _bmm(qq, bd, 'htn,hsn->hts', False)
    a_rk = jnp.where(incl, ak[:, :c], 0.0)
    a_kk = jnp.where(strict, ak[:, c:], 0.0)
    a_rb = jnp.where(incl, ab[:, :c], 0.0)
    a_kb = jnp.where(strict, ab[:, c:], 0.0)

    x = jnp.concatenate([_bmm(a_kk, v, 'hts,hsn->htn', False), kq], axis=2)
    power = a_kb
    x = x - _bmm(power, x, 'hts,hsn->htn', True)
    span = 2
    while span < c:
        power = _bmm(power, power, 'hts,hsn->htn', True)
        x = x + _bmm(power, x, 'hts,hsn->htn', True)
        span *= 2
    u0 = x[:, :, :n]

    arx = _bmm(a_rb, x, 'hts,hsn->htn', False)
    y0 = _bmm(a_rk, v, 'hts,hsn->htn', False) - arx[:, :, :n]
    rqp = rq - arx[:, :, n:]
    bx = _bmm(bdc, x, 'hsn,hsm->hnm', False)
    tadd = _bmm(kdc, v, 'hsn,hsm->hnm', False) - bx[:, :, :n]
    eye = lax.broadcasted_iota(jnp.int32, (n, n), 0) == lax.broadcasted_iota(jnp.int32, (n, n), 1)
    p = jnp.where(eye, jnp.broadcast_to(jnp.exp(tot), (n_h, n, n)), 0.0) - bx[:, :, n:]

    t0 = st_ref[...]
    y_ref[0] = _bmm(rqp, t0, 'htn,hnm->htm', True) + y0
    st_ref[...] = _bmm(p, t0, 'hjn,hnm->hjm', True) + tadd


def _scan_chunk_index(c, n_ctx_chunks, n_chunks, reverse):
    if not reverse:
        return c
    return jnp.where(c < n_ctx_chunks, n_ctx_chunks - 1 - c, n_chunks - 1 - (c - n_ctx_chunks))


def rwkv_scan(r, k, v, kk, a, lw, *, n_ctx, reverse):
    bsz, n_h, t, n = r.shape
    c = RW_CHUNK
    n_chunks, n_ctx_chunks = t // c, n_ctx // c
    spec = pl.BlockSpec((1, n_h, c, n),
                        lambda b, i: (b, 0, _scan_chunk_index(i, n_ctx_chunks, n_chunks, reverse), 0))
    return pl.pallas_call(
        functools.partial(_rwkv_body, reverse=reverse),
        grid=(bsz, n_chunks),
        in_specs=[spec] * 6,
        out_specs=spec,
        out_shape=jax.ShapeDtypeStruct((bsz, n_h, t, n), F32),
        scratch_shapes=[pltpu.VMEM((n_h, n, n), F32)],
        compiler_params=_params("parallel", "arbitrary"),
        name="rwkv_scan_bwd" if reverse else "rwkv_scan_fwd",
    )(r, k, v, kk, a, lw)


def _ssd_body(x_ref, b_ref, c_ref, dt_ref, dac_ref, dar_ref, y_ref, st_ref, *, reverse):
    @pl.when(pl.program_id(1) == 0)
    def _():
        st_ref[...] = jnp.zeros_like(st_ref)

    xs = x_ref[0]
    bm = b_ref[0].astype(BF16)
    cm = c_ref[0].astype(BF16)
    dt = dt_ref[0]
    n_h, c, _ = xs.shape
    heads_per_group = n_h // bm.shape[0]
    ti = lax.broadcasted_iota(jnp.int32, (c, c), 0)
    si = lax.broadcasted_iota(jnp.int32, (c, c), 1)
    incl = (si >= ti) if reverse else (si <= ti)
    incl_t = (ti >= si) if reverse else (ti <= si)
    cs_c = jnp.dot(incl.astype(F32), dac_ref[0], preferred_element_type=F32, precision=HIGHEST)
    cs_r = jnp.dot(dar_ref[0], incl_t.astype(F32), preferred_element_type=F32, precision=HIGHEST)
    tot = cs_c[:1, :] if reverse else cs_c[c - 1:, :]
    gmat = jnp.einsum('gln,gsn->gls', cm, bm, preferred_element_type=F32)
    for h in range(n_h):
        g = h // heads_per_group
        cs_h = cs_c[:, h:h + 1]
        seg = cs_h - cs_r[h:h + 1, :]
        lmat = jnp.exp(jnp.where(incl, seg, NA_MASK))
        xdt = xs[h] * dt[:, h:h + 1]
        st = st_ref[h]
        y_diag = jnp.dot((gmat[g] * lmat).astype(BF16), xdt.astype(BF16), preferred_element_type=F32)
        y_off = jnp.einsum('ln,pn->lp', cm[g], st.astype(BF16), preferred_element_type=F32) * jnp.exp(cs_h)
        y_ref[0, h] = y_diag + y_off
        tot_h = tot[:, h:h + 1]
        xdec = (xdt * jnp.exp(tot_h - cs_h)).astype(BF16)
        st_ref[h] = jnp.exp(tot_h) * st + jnp.einsum('lp,ln->pn', xdec, bm[g], preferred_element_type=F32)


def ssd_scan(xs, bm, cm, dt, da, *, n_ctx, reverse):
    bsz, n_h, t, p = xs.shape
    n_g, n = bm.shape[1], bm.shape[3]
    c = MB_CHUNK
    n_chunks, n_ctx_chunks = t // c, n_ctx // c
    idx = lambda i: _scan_chunk_index(i, n_ctx_chunks, n_chunks, reverse)
    da_row = jnp.swapaxes(da, 1, 2)
    return pl.pallas_call(
        functools.partial(_ssd_body, reverse=reverse),
        grid=(bsz, n_chunks),
        in_specs=[pl.BlockSpec((1, n_h, c, p), lambda b, i: (b, 0, idx(i), 0)),
                  pl.BlockSpec((1, n_g, c, n), lambda b, i: (b, 0, idx(i), 0)),
                  pl.BlockSpec((1, n_g, c, n), lambda b, i: (b, 0, idx(i), 0)),
                  pl.BlockSpec((1, c, n_h), lambda b, i: (b, idx(i), 0)),
                  pl.BlockSpec((1, c, n_h), lambda b, i: (b, idx(i), 0)),
                  pl.BlockSpec((1, n_h, c), lambda b, i: (b, 0, idx(i)))],
        out_specs=pl.BlockSpec((1, n_h, c, p), lambda b, i: (b, 0, idx(i), 0)),
        out_shape=jax.ShapeDtypeStruct((bsz, n_h, t, p), F32),
        scratch_shapes=[pltpu.VMEM((n_h, p, n), F32)],
        compiler_params=_params("parallel", "arbitrary"),
        name="ssd_scan_bwd" if reverse else "ssd_scan_fwd",
    )(xs, bm, cm, dt, da, da_row)


def _attn_body(q_ref, k_ref, v_ref, o_ref, *, tk, n_ctx_tiles):
    qi = pl.program_id(2)
    q = q_ref[0, 0]
    tq = q.shape[0]
    dv = v_ref.shape[3]
    n_tiles = k_ref.shape[2] // tk
    n_visible = jnp.where(qi < n_ctx_tiles, n_ctx_tiles, n_tiles)

    def step(j, carry):
        m, l, acc = carry
        start = pl.multiple_of(j * tk, tk)
        kj = k_ref[0, 0, pl.ds(start, tk), :]
        vj = v_ref[0, 0, pl.ds(start, tk), :]
        s = jnp.einsum('qd,kd->qk', q, kj, preferred_element_type=F32)
        m_new = jnp.maximum(m, jnp.max(s, axis=-1, keepdims=True))
        p = jnp.exp(s - m_new)
        alpha = jnp.exp(m - m_new)
        l = alpha * l + jnp.sum(p, axis=-1, keepdims=True)
        acc = alpha * acc + jnp.dot(p.astype(BF16), vj, preferred_element_type=F32)
        return m_new, l, acc

    init = (jnp.full((tq, 1), -jnp.inf, F32), jnp.zeros((tq, 1), F32), jnp.zeros((tq, dv), F32))
    _, l, acc = lax.fori_loop(0, n_visible, step, init)
    o_ref[0, 0] = acc / l


def attention(q, k, v, *, n_ctx, tile):
    bsz, n_h, t, dq = q.shape
    dv = v.shape[3]
    return pl.pallas_call(
        functools.partial(_attn_body, tk=tile, n_ctx_tiles=n_ctx // tile),
        grid=(bsz, n_h, t // tile),
        in_specs=[pl.BlockSpec((1, 1, tile, dq), lambda b, h, i: (b, h, i, 0)),
                  pl.BlockSpec((1, 1, t, dq), lambda b, h, i: (b, h, 0, 0)),
                  pl.BlockSpec((1, 1, t, dv), lambda b, h, i: (b, h, 0, 0))],
        out_specs=pl.BlockSpec((1, 1, tile, dv), lambda b, h, i: (b, h, i, 0)),
        out_shape=jax.ShapeDtypeStruct((bsz, n_h, t, dv), F32),
        compiler_params=_params("parallel", "parallel", "arbitrary"),
        name="attention",
    )(q, k, v)


def _na_body(q_ref, k_ref, v_ref, kc_ref, vc_ref, bias_ref, o_ref, *, rows_per_step, n_rows):
    i = pl.program_id(2)
    kc = kc_ref[0, 0]
    vc = vc_ref[0, 0]
    win = NA_WIN_ROWS * GRID_W
    for rr in range(rows_per_step):
        r = i * rows_per_step + rr
        rs = jnp.clip(r - NA_WIN_ROWS // 2, 0, n_rows - NA_WIN_ROWS)
        start = pl.multiple_of(rs * GRID_W, GRID_W)
        kw = k_ref[0, 0, pl.ds(start, win), :]
        vw = v_ref[0, 0, pl.ds(start, win), :]
        q = q_ref[0, 0, rr * GRID_W:(rr + 1) * GRID_W, :]
        s = jnp.einsum('qd,kd->qk', q, kw, preferred_element_type=F32) + bias_ref[0, r - rs]
        sc = jnp.einsum('qd,kd->qk', q, kc, preferred_element_type=F32)
        m = jnp.maximum(jnp.max(s, axis=-1, keepdims=True), jnp.max(sc, axis=-1, keepdims=True))
        p = jnp.exp(s - m)
        pc = jnp.exp(sc - m)
        l = jnp.sum(p, axis=-1, keepdims=True) + jnp.sum(pc, axis=-1, keepdims=True)
        o = (jnp.dot(p.astype(BF16), vw, preferred_element_type=F32)
             + jnp.dot(pc.astype(BF16), vc, preferred_element_type=F32))
        o_ref[0, 0, rr * GRID_W:(rr + 1) * GRID_W, :] = o / l


def na_bias_table(rpb, n_rows):
    wr = NA_WIN_ROWS
    var = np.arange(wr)[:, None, None, None]
    rp = np.arange(wr)[None, :, None, None]
    qv = np.arange(GRID_W)[None, None, :, None]
    cv = np.arange(GRID_W)[None, None, None, :]
    col_start = np.clip(qv - NA_WIN_COLS // 2, 0, GRID_W - NA_WIN_COLS)
    valid = (cv >= col_start) & (cv < col_start + NA_WIN_COLS)
    row_idx = np.broadcast_to(rp - var + (NA_WIN_ROWS - 1), (wr, wr, GRID_W, GRID_W))
    col_idx = np.broadcast_to(np.clip(cv - qv + (NA_WIN_COLS - 1), 0, 2 * NA_WIN_COLS - 2), (wr, wr, GRID_W, GRID_W))
    tab = rpb[:, row_idx, col_idx]
    tab = jnp.where(np.broadcast_to(valid, tab.shape[1:]), tab, NA_MASK)
    tab = jnp.transpose(tab, (0, 1, 3, 2, 4))
    return tab.reshape(rpb.shape[0], wr, GRID_W, wr * GRID_W)


def neighborhood_attention(q, k, v, kc, vc, bias):
    bsz, n_h, length, d = q.shape
    n_rows = length // GRID_W
    rows_per_step = 8
    n_c = kc.shape[2]
    blk = rows_per_step * GRID_W
    return pl.pallas_call(
        functools.partial(_na_body, rows_per_step=rows_per_step, n_rows=n_rows),
        grid=(bsz, n_h, n_rows // rows_per_step),
        in_specs=[pl.BlockSpec((1, 1, blk, d), lambda b, h, i: (b, h, i, 0)),
                  pl.BlockSpec((1, 1, length, d), lambda b, h, i: (b, h, 0, 0)),
                  pl.BlockSpec((1, 1, length, d), lambda b, h, i: (b, h, 0, 0)),
                  pl.BlockSpec((1, 1, n_c, d), lambda b, h, i: (b, h, 0, 0)),
                  pl.BlockSpec((1, 1, n_c, d), lambda b, h, i: (b, h, 0, 0)),
                  pl.BlockSpec((1, NA_WIN_ROWS, GRID_W, NA_WIN_ROWS * GRID_W), lambda b, h, i: (h, 0, 0, 0))],
        out_specs=pl.BlockSpec((1, 1, blk, d), lambda b, h, i: (b, h, i, 0)),
        out_shape=jax.ShapeDtypeStruct((bsz, n_h, length, d), F32),
        compiler_params=_params("parallel", "parallel", "arbitrary"),
        name="neighborhood_attention",
    )(q, k, v, kc, vc, bias)


def _moe_body(h_ref, gate_ref, wg_ref, wu_ref, wd_ref, o_ref, acc_ref):
    e = pl.program_id(1)

    @pl.when(e == 0)
    def _():
        acc_ref[...] = jnp.zeros_like(acc_ref)

    h = h_ref[...]
    hid = jax.nn.silu(jnp.dot(h, wg_ref[0], preferred_element_type=F32)) * jnp.dot(h, wu_ref[0], preferred_element_type=F32)
    acc_ref[...] += gate_ref[0] * jnp.dot(hid.astype(BF16), wd_ref[0], preferred_element_type=F32)

    @pl.when(e == pl.num_programs(1) - 1)
    def _():
        o_ref[...] = acc_ref[...]


def moe_experts(h, gate, w_gate, w_up, w_down):
    m, d = h.shape
    n_e, _, d_e = w_gate.shape
    tm = _pick_tile(m, 544, 16)
    return pl.pallas_call(
        _moe_body,
        grid=(m // tm, n_e),
        in_specs=[pl.BlockSpec((tm, d), lambda i, e: (i, 0)),
                  pl.BlockSpec((1, tm, 1), lambda i, e: (e, i, 0)),
                  pl.BlockSpec((1, d, d_e), lambda i, e: (e, 0, 0)),
                  pl.BlockSpec((1, d, d_e), lambda i, e: (e, 0, 0)),
                  pl.BlockSpec((1, d_e, d), lambda i, e: (e, 0, 0))],
        out_specs=pl.BlockSpec((tm, d), lambda i, e: (i, 0)),
        out_shape=jax.ShapeDtypeStruct((m, d), F32),
        scratch_shapes=[pltpu.VMEM((tm, d), F32)],
        compiler_params=_params("parallel", "arbitrary"),
        name="moe_experts",
    )(h, gate, w_gate, w_up, w_down)


def _layer_norm(x, g, b):
    mu = jnp.mean(x, -1, keepdims=True)
    var = jnp.mean(jnp.square(x - mu), -1, keepdims=True)
    return (x - mu) * lax.rsqrt(var + LN_EPS) * g + b


def _rms_norm(x, g):
    return x * lax.rsqrt(jnp.mean(x * x, -1, keepdims=True) + RMS_EPS) * g


def _seg_shift(x, offset, n_ctx):
    t = x.shape[1]
    if offset == 0:
        return x
    if offset < 0:
        y = jnp.pad(x, ((0, 0), (-offset, 0), (0, 0)))[:, :t]
    else:
        y = jnp.pad(x, ((0, 0), (0, offset), (0, 0)))[:, offset:]
    pos = np.arange(t)
    src = pos + offset
    same = (src >= 0) & (src < t) & ((pos < n_ctx) == (src < n_ctx))
    return y * jnp.asarray(same, x.dtype)[None, :, None]


def _heads(x):
    bsz, t, c = x.shape
    return jnp.transpose(x.reshape(bsz, t, c // HEAD_DIM, HEAD_DIM), (0, 2, 1, 3))


def _unheads(x):
    bsz, n_h, t, n = x.shape
    return jnp.transpose(x, (0, 2, 1, 3)).reshape(bsz, t, n_h * n)


def _axial_rope(x, n_ctx):
    bsz, t, n_h, dim = x.shape
    n_freq = dim // 4
    pos_t = jnp.arange(t - n_ctx)
    pos = jnp.stack([pos_t // GRID_W, pos_t % GRID_W], axis=-1).astype(F32)
    inv_freq = ROPE_BASE ** (-jnp.arange(n_freq, dtype=F32) / n_freq)
    ang = pos[:, :, None] * inv_freq
    ang = jnp.concatenate([jnp.zeros((n_ctx, 2, n_freq), F32), ang], axis=0)
    cos = jnp.cos(ang)[None, :, None]
    sin = jnp.sin(ang)[None, :, None]
    xr = x.reshape(bsz, t, n_h, 2, 2, n_freq)
    x1, x2 = xr[..., 0, :], xr[..., 1, :]
    out = jnp.stack([x1 * cos - x2 * sin, x2 * cos + x1 * sin], axis=-2)
    return out.reshape(x.shape)


def _rwkv_mixer(p, n_ctx, mu, w0, w_up, a0, a_up, g_up, k_k, k_a, r_k, gn_g, gn_b):
    bsz, t, _ = p.shape
    gd = GROUP_DIM
    p = p + mu[0] * (_seg_shift(p, -1, n_ctx) - p) + mu[1] * (_seg_shift(p, 1, n_ctx) - p)
    r, k, v = p[..., :gd], p[..., gd:2 * gd], p[..., 2 * gd:3 * gd]
    wd = p[..., 3 * gd:3 * gd + 2 * RW_LORA]
    ad = p[..., 3 * gd + 2 * RW_LORA:3 * gd + 4 * RW_LORA]
    g_in = p[..., 3 * gd + 4 * RW_LORA:]
    lora_in = jnp.concatenate([jnp.tanh(wd), ad, jax.nn.sigmoid(g_in)], axis=-1)
    n_lora = 4 * RW_LORA + RW_G_LORA
    lora_w = jnp.zeros((n_lora, 5 * gd), F32)
    for j, blk in enumerate((w_up[0], w_up[1], a_up[0], a_up[1])):
        lora_w = lora_w.at[j * RW_LORA:(j + 1) * RW_LORA, j * gd:(j + 1) * gd].set(blk)
    lora_w = lora_w.at[4 * RW_LORA:, 4 * gd:].set(g_up)
    lo = matmul(lora_in.reshape(bsz * t, n_lora).astype(BF16), lora_w.astype(BF16)).reshape(bsz, t, 5 * gd)
    w_pre = w0 + lo[..., :2 * gd].reshape(bsz, t, 2, gd)
    lw = -RW_DECAY_SCALE * jax.nn.sigmoid(w_pre)
    a = jax.nn.sigmoid(a0 + lo[..., 2 * gd:4 * gd].reshape(bsz, t, 2, gd))
    g = lo[..., 4 * gd:]
    kk = (k * k_k).reshape(bsz, t, N_HEADS, HEAD_DIM)
    kk = (kk * lax.rsqrt(jnp.maximum(jnp.sum(kk * kk, -1, keepdims=True), 1e-12))).reshape(bsz, t, gd)
    k_dir = k[:, :, None] * (1 + (a - 1) * k_a)
    rh, kh, vh, kkh = _heads(r), _heads(k), _heads(v), _heads(kk)
    y = None
    for d in range(2):
        yd = rwkv_scan(rh, _heads(k_dir[:, :, d]), vh, kkh, _heads(a[:, :, d]), _heads(lw[:, :, d]),
                       n_ctx=n_ctx, reverse=bool(d))
        y = yd if y is None else y + yd
    m = jnp.mean(y, -1, keepdims=True)
    var = jnp.mean(jnp.square(y - m), -1, keepdims=True)
    per_head = lambda w: w.reshape(N_HEADS, HEAD_DIM)[None, :, None, :]
    y = (y - m) * lax.rsqrt(var + RW_GN_EPS) * per_head(gn_g) + per_head(gn_b)
    y = y + jnp.sum(rh * kh * r_k[None, :, None, :], -1, keepdims=True) * vh
    return _unheads(y) * g


def _mamba_mixer(z, xbc, dt_raw, n_ctx, conv_w, conv_b, a_log, dt_bias, d_skip, norm_g):
    bsz, t, _ = z.shape
    conv = conv_b + sum(conv_w[j] * _seg_shift(xbc, j - MB_CONV // 2, n_ctx) for j in range(MB_CONV))
    xbc = jax.nn.silu(conv)
    gd, gs = GROUP_DIM, MB_GROUPS * MB_STATE
    xs, bm, cm = xbc[..., :gd], xbc[..., gd:gd + gs], xbc[..., gd + gs:]
    dt = jax.nn.softplus(dt_raw.reshape(bsz, t, 2, N_HEADS) + dt_bias)
    a_neg = -jnp.exp(a_log)
    xh = _heads(xs)
    groups = lambda m: jnp.transpose(m.reshape(bsz, t, MB_GROUPS, MB_STATE), (0, 2, 1, 3))
    bg, cg = groups(bm), groups(cm)
    y = d_skip[None, :, None, None] * xh
    for d in range(2):
        y = y + ssd_scan(xh, bg, cg, dt[:, :, d], dt[:, :, d] * a_neg[d], n_ctx=n_ctx, reverse=bool(d))
    y = _unheads(y) * jax.nn.silu(z)
    y = y.reshape(bsz, t, MB_GROUPS, gd // MB_GROUPS)
    y = y * lax.rsqrt(jnp.mean(y * y, -1, keepdims=True) + RMS_EPS)
    return y.reshape(bsz, t, gd) * norm_g


def _mla_mixer(q_lora, kv_lora, k_pe, n_ctx, q_norm, w_uq, kv_norm, w_ukv, tile):
    bsz, t, _ = q_lora.shape
    q = matmul(_rms_norm(q_lora, q_norm).reshape(bsz * t, -1).astype(BF16), w_uq.astype(BF16))
    q = q.reshape(bsz, t, N_HEADS, MLA_QK)
    kv = matmul(_rms_norm(kv_lora, kv_norm).reshape(bsz * t, -1).astype(BF16), w_ukv.astype(BF16))
    kv = kv.reshape(bsz, t, N_HEADS, MLA_NOPE + HEAD_DIM)
    q = jnp.concatenate([q[..., :MLA_NOPE], _axial_rope(q[..., MLA_NOPE:], n_ctx)], axis=-1)
    k_pe = jnp.broadcast_to(_axial_rope(k_pe[:, :, None, :], n_ctx), (bsz, t, N_HEADS, MLA_ROPE))
    k = jnp.concatenate([kv[..., :MLA_NOPE], k_pe], axis=-1)
    v = kv[..., MLA_NOPE:]
    pad = ((0, 0), (0, 0), (0, 0), (0, MLA_QK_PAD - MLA_QK))
    to_heads = lambda m: jnp.transpose(m, (0, 2, 1, 3)).astype(BF16)
    o = attention(to_heads(jnp.pad(q * MLA_QK ** -0.5, pad)), to_heads(jnp.pad(k, pad)), to_heads(v),
                  n_ctx=n_ctx, tile=tile)
    return _unheads(o)


def _natten_mixer(p, n_ctx, rpb, tile):
    gd = GROUP_DIM
    scale = HEAD_DIM ** -0.5
    q = _heads(p[..., :gd] * scale).astype(BF16)
    k = _heads(p[..., gd:2 * gd]).astype(BF16)
    v = _heads(p[..., 2 * gd:]).astype(BF16)
    n_rows = (p.shape[1] - n_ctx) // GRID_W
    o_lat = neighborhood_attention(q[:, :, n_ctx:], k[:, :, n_ctx:], v[:, :, n_ctx:],
                                   k[:, :, :n_ctx], v[:, :, :n_ctx], na_bias_table(rpb, n_rows))
    o_ctx = attention(q[:, :, :n_ctx], k[:, :, :n_ctx], v[:, :, :n_ctx], n_ctx=n_ctx, tile=tile)
    return _unheads(jnp.concatenate([o_ctx, o_lat], axis=2))


def _moe(h, router_w, router_b, w_gate, w_up, w_down):
    n_tok = h.shape[0]
    rw_pad = jnp.pad(router_w, ((0, 0), (0, LANE - N_EXPERTS)))
    logits = matmul(h, rw_pad, precision=HIGHEST, tm_target=256)[:, :N_EXPERTS]
    scores = jax.nn.sigmoid(logits)
    grouped = (scores + router_b).reshape(n_tok, N_EXPERT_GROUPS, EXPERTS_PER_GROUP)
    group_score = jnp.sum(lax.top_k(grouped, TOP_K)[0], axis=-1)
    g_sel = jnp.argmax(group_score, axis=-1)
    in_group = jnp.take_along_axis(grouped, g_sel[:, None, None], axis=1)[:, 0]
    _, local = lax.top_k(in_group, TOP_K)
    idx = g_sel[:, None] * EXPERTS_PER_GROUP + local
    w_sel = jnp.take_along_axis(scores, idx, axis=-1)
    w_sel = w_sel / jnp.sum(w_sel, -1, keepdims=True) * ROUTED_SCALE
    gate = jnp.sum(jax.nn.one_hot(idx, N_EXPERTS, dtype=F32) * w_sel[..., None], axis=1)
    return moe_experts(h.astype(BF16), jnp.transpose(gate)[:, :, None],
                       w_gate.astype(BF16), w_up.astype(BF16), w_down.astype(BF16))


def _pad_w_in(w):
    d = w.shape[0]
    dt_end = RW_IN + MB_IN
    pe_end = dt_end + MLA_IN
    return jnp.concatenate([w[:, :dt_end], jnp.zeros((d, LANE - 2 * N_HEADS), w.dtype),
                            w[:, dt_end:pe_end], jnp.zeros((d, LANE - MLA_ROPE), w.dtype),
                            w[:, pe_end:]], axis=1)


def kernel(x, c, ctx, c_ctx, ada_w, ada_b, w_in, w_out, ln1_g, ln1_b, ln2_g, ln2_b, rw_mu, rw_w0, rw_w_up, rw_a0, rw_a_up, rw_g_up, rw_k_k, rw_k_a, rw_r_k, rw_gn_g, rw_gn_b, mb_conv_w, mb_conv_b, mb_a_log, mb_dt_bias, mb_d, mb_norm_g, mla_q_norm, mla_w_uq, mla_kv_norm, mla_w_ukv, na_rpb, router_w, router_b, exp_w_gate, exp_w_up, exp_w_down):
    bsz, seq, d = x.shape
    n_ctx = ctx.shape[1]
    t = n_ctx + seq
    depth = ada_w.shape[0]
    tile = min(256, n_ctx)
    cond = jnp.concatenate([jax.nn.silu(c), jax.nn.silu(c_ctx)[None]], axis=0)
    mod = modulation(cond, ada_w, ada_b)
    xs = jnp.concatenate([ctx, x], axis=1)
    is_ctx = jnp.asarray(np.arange(t) < n_ctx)[None, :, None]

    def per_token(l, j):
        rows = mod[l, :, j * d:(j + 1) * d]
        return jnp.where(is_ctx, rows[bsz][None, None, :], rows[:bsz][:, None, :])

    for l in range(depth):
        sh1, sc1, g1, sh2, sc2, g2 = (per_token(l, j) for j in range(6))
        xm = (xs * (1 + sc1) + sh1).astype(BF16).reshape(bsz * t, d)
        p = matmul(xm, _pad_w_in(w_in[l]).astype(BF16), tm_target=544, tn_target=1152).reshape(bsz, t, N_IN_PAD)
        o_rw = _rwkv_mixer(p[..., OFF_RW:OFF_MB_Z], n_ctx, rw_mu[l], rw_w0[l], rw_w_up[l], rw_a0[l], rw_a_up[l],
                           rw_g_up[l], rw_k_k[l], rw_k_a[l], rw_r_k[l], rw_gn_g[l], rw_gn_b[l])
        o_mb = _mamba_mixer(p[..., OFF_MB_Z:OFF_MB_XBC], p[..., OFF_MB_XBC:OFF_MB_DT],
                            p[..., OFF_MB_DT:OFF_MB_DT + 2 * N_HEADS], n_ctx, mb_conv_w[l], mb_conv_b[l],
                            mb_a_log[l], mb_dt_bias[l], mb_d[l], mb_norm_g[l])
        o_mla = _mla_mixer(p[..., OFF_MLA_Q:OFF_MLA_KV], p[..., OFF_MLA_KV:OFF_MLA_PE],
                           p[..., OFF_MLA_PE:OFF_MLA_PE + MLA_ROPE], n_ctx, mla_q_norm[l], mla_w_uq[l],
                           mla_kv_norm[l], mla_w_ukv[l], tile)
        o_na = _natten_mixer(p[..., OFF_NA:], n_ctx, na_rpb[l], tile)
        mix = jnp.concatenate([o_rw, o_mb, o_mla, o_na], axis=-1).astype(BF16).reshape(bsz * t, d)
        y = matmul(mix, w_out[l].astype(BF16), tm_target=544).reshape(bsz, t, d)
        xs = _layer_norm(DEEPNORM_ALPHA * xs + g1 * y, ln1_g[l], ln1_b[l])
        h = (xs * (1 + sc2) + sh2).reshape(bsz * t, d)
        f = _moe(h, router_w, router_b, exp_w_gate[l], exp_w_up[l], exp_w_down[l]).reshape(bsz, t, d)
        xs = _layer_norm(DEEPNORM_ALPHA * xs + g2 * f, ln2_g[l], ln2_b[l])
    return xs[:, n_ctx:]
```

```python
import functools
import math

import numpy as np
import jax
import jax.numpy as jnp
from jax import lax
from jax.experimental import pallas as pl
from jax.experimental.pallas import tpu as pltpu

F32 = jnp.float32
BF16 = jnp.bfloat16
HIGHEST = lax.Precision.HIGHEST

D_MODEL = 2048
DEPTH = 4
GRID_W = 64
GROUP_DIM = D_MODEL // 4
HEAD_DIM = 64
N_HEADS = GROUP_DIM // HEAD_DIM

RW_LORA = 64
RW_G_LORA = 128
RW_DECAY_SCALE = 0.606531
RW_GN_EPS = 64e-5
RW_IN = 3 * GROUP_DIM + 4 * RW_LORA + RW_G_LORA
RW_CHUNK = 64

MB_GROUPS = 2
MB_STATE = 128
MB_CONV = 5
MB_CHUNK = 128
MB_CONV_DIM = GROUP_DIM + 2 * MB_GROUPS * MB_STATE
MB_IN = GROUP_DIM + MB_CONV_DIM + 2 * N_HEADS

MLA_Q_LORA = 3 * D_MODEL // 16
MLA_KV_LORA = D_MODEL // 16
MLA_NOPE = 64
MLA_ROPE = 32
MLA_QK = MLA_NOPE + MLA_ROPE
MLA_QK_PAD = 128
MLA_IN = MLA_Q_LORA + MLA_KV_LORA + MLA_ROPE
ROPE_BASE = 10000.0

NA_WIN_ROWS = 8
NA_WIN_COLS = 16
NA_IN = 3 * GROUP_DIM
NA_MASK = -1e30

N_EXPERTS = 16
N_EXPERT_GROUPS = 4
EXPERTS_PER_GROUP = N_EXPERTS // N_EXPERT_GROUPS
TOP_K = 2
D_EXPERT = D_MODEL // 4
ROUTED_SCALE = 2.5

DEEPNORM_ALPHA = (2 * DEPTH) ** 0.25
LN_EPS = 1e-6
RMS_EPS = 1e-6

LANE = 128
OFF_RW = 0
OFF_MB_Z = OFF_RW + RW_IN
OFF_MB_XBC = OFF_MB_Z + GROUP_DIM
OFF_MB_DT = OFF_MB_XBC + MB_CONV_DIM
OFF_MLA_Q = OFF_MB_DT + LANE
OFF_MLA_KV = OFF_MLA_Q + MLA_Q_LORA
OFF_MLA_PE = OFF_MLA_KV + MLA_KV_LORA
OFF_NA = OFF_MLA_PE + LANE
N_IN_PAD = OFF_NA + NA_IN

VMEM_LIMIT = 56 * 1024 * 1024


def _params(*sem):
    return pltpu.CompilerParams(dimension_semantics=sem, vmem_limit_bytes=VMEM_LIMIT)


def _pick_tile(n, target, quantum):
    best = None
    for t in range(quantum, min(n, target) + 1, quantum):
        if n % t == 0:
            best = t
    assert best is not None, (n, target, quantum)
    return best


def _mm_body(x_ref, w_ref, o_ref, *, precision):
    o_ref[...] = jnp.dot(x_ref[...], w_ref[...], preferred_element_type=F32,
                         precision=precision).astype(o_ref.dtype)


def matmul(x, w, *, tm_target=512, tn_target=1024, precision=None, out_dtype=F32):
    m, k = x.shape
    n = w.shape[1]
    tm = _pick_tile(m, tm_target, 16)
    tn = _pick_tile(n, tn_target, LANE)
    return pl.pallas_call(
        functools.partial(_mm_body, precision=precision),
        grid=(m // tm, n // tn),
        in_specs=[pl.BlockSpec((tm, k), lambda i, j: (i, 0)),
                  pl.BlockSpec((k, tn), lambda i, j: (0, j))],
        out_specs=pl.BlockSpec((tm, tn), lambda i, j: (i, j)),
        out_shape=jax.ShapeDtypeStruct((m, n), out_dtype),
        compiler_params=_params("parallel", "parallel"),
        name="matmul",
    )(x, w)


def _mod_body(c_ref, w_ref, b_ref, o_ref, *, n_rows):
    w = w_ref[...]
    reps = w.shape[1] // LANE
    o_ref[...] = jnp.zeros_like(o_ref)
    for m in range(n_rows):
        cb = c_ref[m]
        cbt = jnp.concatenate([cb] * reps, axis=1)
        o_ref[m:m + 1, :] = jnp.sum(w * cbt, axis=0, keepdims=True) + b_ref[...]


def modulation(cond, ada_w, ada_b):
    n_rows, k = cond.shape
    depth, _, n = ada_w.shape
    tn = 512
    cond_b = jnp.broadcast_to(cond[:, :, None], (n_rows, k, LANE))
    return pl.pallas_call(
        functools.partial(_mod_body, n_rows=n_rows),
        grid=(depth, n // tn),
        in_specs=[pl.BlockSpec((n_rows, k, LANE), lambda l, j: (0, 0, 0)),
                  pl.BlockSpec((None, k, tn), lambda l, j: (l, 0, j)),
                  pl.BlockSpec((None, 1, tn), lambda l, j: (l, 0, j))],
        out_specs=pl.BlockSpec((None, 8, tn), lambda l, j: (l, 0, j)),
        out_shape=jax.ShapeDtypeStruct((depth, 8, n), F32),
        compiler_params=_params("parallel", "parallel"),
        name="modulation",
    )(cond_b, ada_w, ada_b.reshape(depth, 1, n))


def _bmm(a, b, spec, mode):
    if mode == 'f32':
        return jnp.einsum(spec, a, b, preferred_element_type=F32, precision=HIGHEST)
    a_hi, b_hi = a.astype(BF16), b.astype(BF16)
    out = jnp.einsum(spec, a_hi, b_hi, preferred_element_type=F32)
    if mode == 'x3':
        a_lo = (a - a_hi.astype(F32)).astype(BF16)
        b_lo = (b - b_hi.astype(F32)).astype(BF16)
        out = out + (jnp.einsum(spec, a_hi, b_lo, preferred_element_type=F32)
                     + jnp.einsum(spec, a_lo, b_hi, preferred_element_type=F32))
    return out


def _unit_triangular_inverse(l_mat, ti, si):
    c = l_mat.shape[1]
    nn = 'hts,hsn->htn'
    blk = 8
    eye = (ti == si).astype(F32)
    l_d = jnp.where((ti // blk) == (si // blk), l_mat, 0.0)
    p2 = _bmm(l_d, l_d, nn, 'x3')
    p4 = _bmm(p2, p2, nn, 'x3')
    inv = _bmm(eye - l_d, eye + p2, nn, 'x3')
    inv = _bmm(inv, eye + p4, nn, 'x3')
    while blk < c:
        pair = ((ti // (2 * blk)) == (si // (2 * blk))) & ((ti // blk) != (si // blk))
        off = jnp.where(pair, l_mat, 0.0)
        inv = inv - _bmm(_bmm(inv, off, nn, 'x3'), inv, nn, 'x3')
        blk *= 2
    return inv


def _rwkv_body(r_ref, k_ref, v_ref, kk_ref, a_ref, lw_ref, y_ref, st_ref, *, reverse):
    @pl.when(pl.program_id(1) == 0)
    def _():
        st_ref[...] = jnp.zeros_like(st_ref)

    r, k, v, kk, a, lw = (ref[0] for ref in (r_ref, k_ref, v_ref, kk_ref, a_ref, lw_ref))
    n_h, c, n = r.shape
    ti = lax.broadcasted_iota(jnp.int32, (c, c), 0)
    si = lax.broadcasted_iota(jnp.int32, (c, c), 1)
    incl = (si >= ti) if reverse else (si <= ti)
    strict = (si > ti) if reverse else (si < ti)
    tri = jnp.broadcast_to(incl.astype(F32)[None], (n_h, c, c))
    cum = _bmm(tri, lw, 'hts,hsn->htn', 'f32')
    tot = cum[:, :1, :] if reverse else cum[:, c - 1:, :]
    b = kk * a
    e_neg = jnp.exp(-cum)
    e_rem = jnp.exp(tot - cum)
    rq = r * jnp.exp(cum)
    kq = kk * jnp.exp(cum - lw)
    kd = k * e_neg
    bd = b * e_neg
    kdc = k * e_rem
    bdc = b * e_rem

    qq = jnp.concatenate([rq, kq], axis=1)
    ak = _bmm(qq, kd, 'htn,hsn->hts', 'bf16')
    ab = _bmm(qq, bd, 'htn,hsn->hts', 'bf16')
    a_rk = jnp.where(incl, ak[:, :c], 0.0)
    a_kk = jnp.where(strict, ak[:, c:], 0.0)
    a_rb = jnp.where(incl, ab[:, :c], 0.0)
    a_kb = jnp.where(strict, ab[:, c:], 0.0)

    x0 = jnp.concatenate([_bmm(a_kk, v, 'hts,hsn->htn', 'bf16'), kq], axis=2)
    x = _bmm(_unit_triangular_inverse(a_kb, ti, si), x0, 'hts,hsn->htn', 'x3')

    arx = _bmm(a_rb, x, 'hts,hsn->htn', 'bf16')
    y0 = _bmm(a_rk, v, 'hts,hsn->htn', 'bf16') - arx[:, :, :n]
    rqp = rq - arx[:, :, n:]
    bx = _bmm(bdc, x, 'hsn,hsm->hnm', 'bf16')
    tadd = _bmm(kdc, v, 'hsn,hsm->hnm', 'bf16') - bx[:, :, :n]
    eye = lax.broadcasted_iota(jnp.int32, (n, n), 0) == lax.broadcasted_iota(jnp.int32, (n, n), 1)
    p = jnp.where(eye, jnp.broadcast_to(jnp.exp(tot), (n_h, n, n)), 0.0) - bx[:, :, n:]

    t0 = st_ref[...]
    y_ref[0] = _bmm(rqp, t0, 'htn,hnm->htm', 'f32') + y0
    st_ref[...] = _bmm(p, t0, 'hjn,hnm->hjm', 'f32') + tadd


def _scan_chunk_index(c, n_ctx_chunks, n_chunks, reverse):
    if not reverse:
        return c
    return jnp.where(c < n_ctx_chunks, n_ctx_chunks - 1 - c, n_chunks - 1 - (c - n_ctx_chunks))


def rwkv_scan(r, k, v, kk, a, lw, *, n_ctx, reverse):
    bsz, n_h, t, n = r.shape
    c = RW_CHUNK
    n_chunks, n_ctx_chunks = t // c, n_ctx // c
    spec = pl.BlockSpec((1, n_h, c, n),
                        lambda b, i: (b, 0, _scan_chunk_index(i, n_ctx_chunks, n_chunks, reverse), 0))
    return pl.pallas_call(
        functools.partial(_rwkv_body, reverse=reverse),
        grid=(bsz, n_chunks),
        in_specs=[spec] * 6,
        out_specs=spec,
        out_shape=jax.ShapeDtypeStruct((bsz, n_h, t, n), F32),
        scratch_shapes=[pltpu.VMEM((n_h, n, n), F32)],
        compiler_params=_params("parallel", "arbitrary"),
        name="rwkv_scan_bwd" if reverse else "rwkv_scan_fwd",
    )(r, k, v, kk, a, lw)


def _ssd_body(x_ref, b_ref, c_ref, dt_ref, dac_ref, dar_ref, y_ref, st_ref, *, reverse):
    @pl.when(pl.program_id(1) == 0)
    def _():
        st_ref[...] = jnp.zeros_like(st_ref)

    xs = x_ref[0]
    bm = b_ref[0].astype(BF16)
    cm = c_ref[0].astype(BF16)
    dt = dt_ref[0]
    n_h, c, _ = xs.shape
    heads_per_group = n_h // bm.shape[0]
    ti = lax.broadcasted_iota(jnp.int32, (c, c), 0)
    si = lax.broadcasted_iota(jnp.int32, (c, c), 1)
    incl = (si >= ti) if reverse else (si <= ti)
    incl_t = (ti >= si) if reverse else (ti <= si)
    cs_c = jnp.dot(incl.astype(F32), dac_ref[0], preferred_element_type=F32, precision=HIGHEST)
    cs_r = jnp.dot(dar_ref[0], incl_t.astype(F32), preferred_element_type=F32, precision=HIGHEST)
    tot = cs_c[:1, :] if reverse else cs_c[c - 1:, :]
    gmat = jnp.einsum('gln,gsn->gls', cm, bm, preferred_element_type=F32)
    for h in range(n_h):
        g = h // heads_per_group
        cs_h = cs_c[:, h:h + 1]
        seg = cs_h - cs_r[h:h + 1, :]
        lmat = jnp.exp(jnp.where(incl, seg, NA_MASK))
        xdt = xs[h] * dt[:, h:h + 1]
        st = st_ref[h]
        y_diag = jnp.dot((gmat[g] * lmat).astype(BF16), xdt.astype(BF16), preferred_element_type=F32)
        y_off = jnp.einsum('ln,pn->lp', cm[g], st.astype(BF16), preferred_element_type=F32) * jnp.exp(cs_h)
        y_ref[0, h] = y_diag + y_off
        tot_h = tot[:, h:h + 1]
        xdec = (xdt * jnp.exp(tot_h - cs_h)).astype(BF16)
        st_ref[h] = jnp.exp(tot_h) * st + jnp.einsum('lp,ln->pn', xdec, bm[g], preferred_element_type=F32)


def ssd_scan(xs, bm, cm, dt, da, *, n_ctx, reverse):
    bsz, n_h, t, p = xs.shape
    n_g, n = bm.shape[1], bm.shape[3]
    c = MB_CHUNK
    n_chunks, n_ctx_chunks = t // c, n_ctx // c
    idx = lambda i: _scan_chunk_index(i, n_ctx_chunks, n_chunks, reverse)
    da_row = jnp.swapaxes(da, 1, 2)
    return pl.pallas_call(
        functools.partial(_ssd_body, reverse=reverse),
        grid=(bsz, n_chunks),
        in_specs=[pl.BlockSpec((1, n_h, c, p), lambda b, i: (b, 0, idx(i), 0)),
                  pl.BlockSpec((1, n_g, c, n), lambda b, i: (b, 0, idx(i), 0)),
                  pl.BlockSpec((1, n_g, c, n), lambda b, i: (b, 0, idx(i), 0)),
                  pl.BlockSpec((1, c, n_h), lambda b, i: (b, idx(i), 0)),
                  pl.BlockSpec((1, c, n_h), lambda b, i: (b, idx(i), 0)),
                  pl.BlockSpec((1, n_h, c), lambda b, i: (b, 0, idx(i)))],
        out_specs=pl.BlockSpec((1, n_h, c, p), lambda b, i: (b, 0, idx(i), 0)),
        out_shape=jax.ShapeDtypeStruct((bsz, n_h, t, p), F32),
        scratch_shapes=[pltpu.VMEM((n_h, p, n), F32)],
        compiler_params=_params("parallel", "arbitrary"),
        name="ssd_scan_bwd" if reverse else "ssd_scan_fwd",
    )(xs, bm, cm, dt, da, da_row)


def _attn_body(q_ref, k_ref, v_ref, o_ref, *, n_ctx, n_ctx_tiles):
    s = jnp.einsum('qd,kd->qk', q_ref[0, 0], k_ref[0, 0], preferred_element_type=F32)
    key_is_ctx = lax.broadcasted_iota(jnp.int32, s.shape, 1) < n_ctx
    s = jnp.where(key_is_ctx | (pl.program_id(2) >= n_ctx_tiles), s, NA_MASK)
    p = jnp.exp(s - jnp.max(s, axis=-1, keepdims=True))
    l = jnp.sum(p, axis=-1, keepdims=True)
    o_ref[0, 0] = jnp.dot(p.astype(BF16), v_ref[0, 0], preferred_element_type=F32) / l


def attention(q, k, v, *, n_ctx, tile):
    bsz, n_h, t, dq = q.shape
    dv = v.shape[3]
    return pl.pallas_call(
        functools.partial(_attn_body, n_ctx=n_ctx, n_ctx_tiles=n_ctx // tile),
        grid=(bsz, n_h, t // tile),
        in_specs=[pl.BlockSpec((1, 1, tile, dq), lambda b, h, i: (b, h, i, 0)),
                  pl.BlockSpec((1, 1, t, dq), lambda b, h, i: (b, h, 0, 0)),
                  pl.BlockSpec((1, 1, t, dv), lambda b, h, i: (b, h, 0, 0))],
        out_specs=pl.BlockSpec((1, 1, tile, dv), lambda b, h, i: (b, h, i, 0)),
        out_shape=jax.ShapeDtypeStruct((bsz, n_h, t, dv), F32),
        compiler_params=_params("parallel", "parallel", "arbitrary"),
        name="attention",
    )(q, k, v)


def _na_body(q_ref, k_ref, v_ref, kc_ref, vc_ref, bias_ref, o_ref, *, rows_per_step, n_rows):
    i = pl.program_id(2)
    kc = kc_ref[0, 0]
    vc = vc_ref[0, 0]
    win = NA_WIN_ROWS * GRID_W
    for rr in range(rows_per_step):
        r = i * rows_per_step + rr
        rs = jnp.clip(r - NA_WIN_ROWS // 2, 0, n_rows - NA_WIN_ROWS)
        start = pl.multiple_of(rs * GRID_W, GRID_W)
        kw = k_ref[0, 0, pl.ds(start, win), :]
        vw = v_ref[0, 0, pl.ds(start, win), :]
        q = q_ref[0, 0, rr * GRID_W:(rr + 1) * GRID_W, :]
        s = jnp.einsum('qd,kd->qk', q, kw, preferred_element_type=F32) + bias_ref[0, r - rs]
        sc = jnp.einsum('qd,kd->qk', q, kc, preferred_element_type=F32)
        m = jnp.maximum(jnp.max(s, axis=-1, keepdims=True), jnp.max(sc, axis=-1, keepdims=True))
        p = jnp.exp(s - m)
        pc = jnp.exp(sc - m)
        l = jnp.sum(p, axis=-1, keepdims=True) + jnp.sum(pc, axis=-1, keepdims=True)
        o = (jnp.dot(p.astype(BF16), vw, preferred_element_type=F32)
             + jnp.dot(pc.astype(BF16), vc, preferred_element_type=F32))
        o_ref[0, 0, rr * GRID_W:(rr + 1) * GRID_W, :] = o / l


def na_bias_table(rpb, n_rows):
    wr = NA_WIN_ROWS
    qv = np.arange(GRID_W)[:, None]
    cv = np.arange(GRID_W)[None, :]
    col_start = np.clip(qv - NA_WIN_COLS // 2, 0, GRID_W - NA_WIN_COLS)
    valid = (cv >= col_start) & (cv < col_start + NA_WIN_COLS)
    offs = np.arange(2 * NA_WIN_COLS - 1)[:, None, None]
    pick = ((cv - qv + (NA_WIN_COLS - 1))[None] == offs) & valid[None]
    cols = jnp.einsum('hro,oqc->hrqc', rpb, jnp.asarray(pick, F32), precision=HIGHEST)
    cols = jnp.where(valid, cols, NA_MASK)
    tab = jnp.stack([cols[:, wr - 1 - var:2 * wr - 1 - var] for var in range(wr)], axis=1)
    tab = jnp.transpose(tab, (0, 1, 3, 2, 4))
    return tab.reshape(rpb.shape[0], wr, GRID_W, wr * GRID_W)


def neighborhood_attention(q, k, v, kc, vc, bias):
    bsz, n_h, length, d = q.shape
    n_rows = length // GRID_W
    rows_per_step = 8
    n_c = kc.shape[2]
    blk = rows_per_step * GRID_W
    return pl.pallas_call(
        functools.partial(_na_body, rows_per_step=rows_per_step, n_rows=n_rows),
        grid=(bsz, n_h, n_rows // rows_per_step),
        in_specs=[pl.BlockSpec((1, 1, blk, d), lambda b, h, i: (b, h, i, 0)),
                  pl.BlockSpec((1, 1, length, d), lambda b, h, i: (b, h, 0, 0)),
                  pl.BlockSpec((1, 1, length, d), lambda b, h, i: (b, h, 0, 0)),
                  pl.BlockSpec((1, 1, n_c, d), lambda b, h, i: (b, h, 0, 0)),
                  pl.BlockSpec((1, 1, n_c, d), lambda b, h, i: (b, h, 0, 0)),
                  pl.BlockSpec((1, NA_WIN_ROWS, GRID_W, NA_WIN_ROWS * GRID_W), lambda b, h, i: (h, 0, 0, 0))],
        out_specs=pl.BlockSpec((1, 1, blk, d), lambda b, h, i: (b, h, i, 0)),
        out_shape=jax.ShapeDtypeStruct((bsz, n_h, length, d), F32),
        compiler_params=_params("parallel", "parallel", "arbitrary"),
        name="neighborhood_attention",
    )(q, k, v, kc, vc, bias)


def _moe_body(h_ref, gate_ref, wg_ref, wu_ref, wd_ref, o_ref, acc_ref):
    e = pl.program_id(1)

    @pl.when(e == 0)
    def _():
        acc_ref[...] = jnp.zeros_like(acc_ref)

    h = h_ref[...]
    hid = jax.nn.silu(jnp.dot(h, wg_ref[0], preferred_element_type=F32)) * jnp.dot(h, wu_ref[0], preferred_element_type=F32)
    acc_ref[...] += gate_ref[0] * jnp.dot(hid.astype(BF16), wd_ref[0], preferred_element_type=F32)

    @pl.when(e == pl.num_programs(1) - 1)
    def _():
        o_ref[...] = acc_ref[...]


def moe_experts(h, gate, w_gate, w_up, w_down):
    m, d = h.shape
    n_e, _, d_e = w_gate.shape
    tm = _pick_tile(m, 544, 16)
    return pl.pallas_call(
        _moe_body,
        grid=(m // tm, n_e),
        in_specs=[pl.BlockSpec((tm, d), lambda i, e: (i, 0)),
                  pl.BlockSpec((1, tm, 1), lambda i, e: (e, i, 0)),
                  pl.BlockSpec((1, d, d_e), lambda i, e: (e, 0, 0)),
                  pl.BlockSpec((1, d, d_e), lambda i, e: (e, 0, 0)),
                  pl.BlockSpec((1, d_e, d), lambda i, e: (e, 0, 0))],
        out_specs=pl.BlockSpec((tm, d), lambda i, e: (i, 0)),
        out_shape=jax.ShapeDtypeStruct((m, d), F32),
        scratch_shapes=[pltpu.VMEM((tm, d), F32)],
        compiler_params=_params("parallel", "arbitrary"),
        name="moe_experts",
    )(h, gate, w_gate, w_up, w_down)


def _layer_norm(x, g, b):
    mu = jnp.mean(x, -1, keepdims=True)
    var = jnp.mean(jnp.square(x - mu), -1, keepdims=True)
    return (x - mu) * lax.rsqrt(var + LN_EPS) * g + b


def _rms_norm(x, g):
    return x * lax.rsqrt(jnp.mean(x * x, -1, keepdims=True) + RMS_EPS) * g


def _seg_shift(x, offset, n_ctx):
    t = x.shape[1]
    if offset == 0:
        return x
    if offset < 0:
        y = jnp.pad(x, ((0, 0), (-offset, 0), (0, 0)))[:, :t]
    else:
        y = jnp.pad(x, ((0, 0), (0, offset), (0, 0)))[:, offset:]
    pos = np.arange(t)
    src = pos + offset
    same = (src >= 0) & (src < t) & ((pos < n_ctx) == (src < n_ctx))
    return y * jnp.asarray(same, x.dtype)[None, :, None]


def _heads(x):
    bsz, t, c = x.shape
    return jnp.transpose(x.reshape(bsz, t, c // HEAD_DIM, HEAD_DIM), (0, 2, 1, 3))


def _unheads(x):
    bsz, n_h, t, n = x.shape
    return jnp.transpose(x, (0, 2, 1, 3)).reshape(bsz, t, n_h * n)


def _axial_rope(x, n_ctx):
    bsz, t, n_h, dim = x.shape
    n_freq = dim // 4
    pos_t = jnp.arange(t - n_ctx)
    pos = jnp.stack([pos_t // GRID_W, pos_t % GRID_W], axis=-1).astype(F32)
    inv_freq = ROPE_BASE ** (-jnp.arange(n_freq, dtype=F32) / n_freq)
    ang = pos[:, :, None] * inv_freq
    ang = jnp.concatenate([jnp.zeros((n_ctx, 2, n_freq), F32), ang], axis=0)
    cos = jnp.cos(ang)[None, :, None]
    sin = jnp.sin(ang)[None, :, None]
    xr = x.reshape(bsz, t, n_h, 2, 2, n_freq)
    x1, x2 = xr[..., 0, :], xr[..., 1, :]
    out = jnp.stack([x1 * cos - x2 * sin, x2 * cos + x1 * sin], axis=-2)
    return out.reshape(x.shape)


def _rwkv_mixer(p, n_ctx, mu, w0, w_up, a0, a_up, g_up, k_k, k_a, r_k, gn_g, gn_b):
    bsz, t, _ = p.shape
    gd = GROUP_DIM
    p = p + mu[0] * (_seg_shift(p, -1, n_ctx) - p) + mu[1] * (_seg_shift(p, 1, n_ctx) - p)
    r, k, v = p[..., :gd], p[..., gd:2 * gd], p[..., 2 * gd:3 * gd]
    wd = p[..., 3 * gd:3 * gd + 2 * RW_LORA]
    ad = p[..., 3 * gd + 2 * RW_LORA:3 * gd + 4 * RW_LORA]
    g_in = p[..., 3 * gd + 4 * RW_LORA:]
    lora_in = jnp.concatenate([jnp.tanh(wd), ad, jax.nn.sigmoid(g_in)], axis=-1)
    n_lora = 4 * RW_LORA + RW_G_LORA
    lora_w = jnp.zeros((n_lora, 5 * gd), F32)
    for j, blk in enumerate((w_up[0], w_up[1], a_up[0], a_up[1])):
        lora_w = lora_w.at[j * RW_LORA:(j + 1) * RW_LORA, j * gd:(j + 1) * gd].set(blk)
    lora_w = lora_w.at[4 * RW_LORA:, 4 * gd:].set(g_up)
    lo = matmul(lora_in.reshape(bsz * t, n_lora).astype(BF16), lora_w.astype(BF16)).reshape(bsz, t, 5 * gd)
    w_pre = w0 + lo[..., :2 * gd].reshape(bsz, t, 2, gd)
    lw = -RW_DECAY_SCALE * jax.nn.sigmoid(w_pre)
    a = jax.nn.sigmoid(a0 + lo[..., 2 * gd:4 * gd].reshape(bsz, t, 2, gd))
    g = lo[..., 4 * gd:]
    kk = (k * k_k).reshape(bsz, t, N_HEADS, HEAD_DIM)
    kk = (kk * lax.rsqrt(jnp.maximum(jnp.sum(kk * kk, -1, keepdims=True), 1e-12))).reshape(bsz, t, gd)
    k_dir = k[:, :, None] * (1 + (a - 1) * k_a)
    rh, kh, vh, kkh = _heads(r), _heads(k), _heads(v), _heads(kk)
    y = None
    for d in range(2):
        yd = rwkv_scan(rh, _heads(k_dir[:, :, d]), vh, kkh, _heads(a[:, :, d]), _heads(lw[:, :, d]),
                       n_ctx=n_ctx, reverse=bool(d))
        y = yd if y is None else y + yd
    m = jnp.mean(y, -1, keepdims=True)
    var = jnp.mean(jnp.square(y - m), -1, keepdims=True)
    per_head = lambda w: w.reshape(N_HEADS, HEAD_DIM)[None, :, None, :]
    y = (y - m) * lax.rsqrt(var + RW_GN_EPS) * per_head(gn_g) + per_head(gn_b)
    y = y + jnp.sum(rh * kh * r_k[None, :, None, :], -1, keepdims=True) * vh
    return _unheads(y) * g


def _mamba_mixer(z, xbc, dt_raw, n_ctx, conv_w, conv_b, a_log, dt_bias, d_skip, norm_g):
    bsz, t, _ = z.shape
    conv = conv_b + sum(conv_w[j] * _seg_shift(xbc, j - MB_CONV // 2, n_ctx) for j in range(MB_CONV))
    xbc = jax.nn.silu(conv)
    gd, gs = GROUP_DIM, MB_GROUPS * MB_STATE
    xs, bm, cm = xbc[..., :gd], xbc[..., gd:gd + gs], xbc[..., gd + gs:]
    dt = jax.nn.softplus(dt_raw.reshape(bsz, t, 2, N_HEADS) + dt_bias)
    a_neg = -jnp.exp(a_log)
    xh = _heads(xs)
    groups = lambda m: jnp.transpose(m.reshape(bsz, t, MB_GROUPS, MB_STATE), (0, 2, 1, 3))
    bg, cg = groups(bm), groups(cm)
    y = d_skip[None, :, None, None] * xh
    for d in range(2):
        y = y + ssd_scan(xh, bg, cg, dt[:, :, d], dt[:, :, d] * a_neg[d], n_ctx=n_ctx, reverse=bool(d))
    y = _unheads(y) * jax.nn.silu(z)
    y = y.reshape(bsz, t, MB_GROUPS, gd // MB_GROUPS)
    y = y * lax.rsqrt(jnp.mean(y * y, -1, keepdims=True) + RMS_EPS)
    return y.reshape(bsz, t, gd) * norm_g


def _mla_mixer(q_lora, kv_lora, k_pe, n_ctx, q_norm, w_uq, kv_norm, w_ukv, tile):
    bsz, t, _ = q_lora.shape
    q = matmul(_rms_norm(q_lora, q_norm).reshape(bsz * t, -1).astype(BF16), w_uq.astype(BF16))
    q = q.reshape(bsz, t, N_HEADS, MLA_QK)
    kv = matmul(_rms_norm(kv_lora, kv_norm).reshape(bsz * t, -1).astype(BF16), w_ukv.astype(BF16))
    kv = kv.reshape(bsz, t, N_HEADS, MLA_NOPE + HEAD_DIM)
    q = jnp.concatenate([q[..., :MLA_NOPE], _axial_rope(q[..., MLA_NOPE:], n_ctx)], axis=-1)
    k_pe = jnp.broadcast_to(_axial_rope(k_pe[:, :, None, :], n_ctx), (bsz, t, N_HEADS, MLA_ROPE))
    k = jnp.concatenate([kv[..., :MLA_NOPE], k_pe], axis=-1)
    v = kv[..., MLA_NOPE:]
    pad = ((0, 0), (0, 0), (0, 0), (0, MLA_QK_PAD - MLA_QK))
    to_heads = lambda m: jnp.transpose(m, (0, 2, 1, 3)).astype(BF16)
    o = attention(to_heads(jnp.pad(q * MLA_QK ** -0.5, pad)), to_heads(jnp.pad(k, pad)), to_heads(v),
                  n_ctx=n_ctx, tile=tile)
    return _unheads(o)


def _natten_mixer(p, n_ctx, rpb, tile):
    gd = GROUP_DIM
    scale = HEAD_DIM ** -0.5
    q = _heads(p[..., :gd] * scale).astype(BF16)
    k = _heads(p[..., gd:2 * gd]).astype(BF16)
    v = _heads(p[..., 2 * gd:]).astype(BF16)
    n_rows = (p.shape[1] - n_ctx) // GRID_W
    o_lat = neighborhood_attention(q[:, :, n_ctx:], k[:, :, n_ctx:], v[:, :, n_ctx:],
                                   k[:, :, :n_ctx], v[:, :, :n_ctx], na_bias_table(rpb, n_rows))
    o_ctx = attention(q[:, :, :n_ctx], k[:, :, :n_ctx], v[:, :, :n_ctx], n_ctx=n_ctx, tile=tile)
    return _unheads(jnp.concatenate([o_ctx, o_lat], axis=2))


def _first_max(vals, excluded):
    live = [jnp.where(x, -jnp.inf, v) for v, x in zip(vals, excluded)]
    top = functools.reduce(jnp.maximum, live)
    found = jnp.zeros_like(top, dtype=jnp.bool_)
    first = []
    for v, x in zip(live, excluded):
        hit = (v == top) & ~found & ~x
        first.append(hit)
        found = found | hit
    return top, first


def _router_body(x_ref, sc_ref, sh_ref, rw_ref, rb_ref, h_ref, gate_ref):
    h = x_ref[...] * (1.0 + sc_ref[0]) + sh_ref[0]
    h_ref[...] = h.astype(BF16)
    logits = jnp.dot(h, rw_ref[...], preferred_element_type=F32, precision=HIGHEST)
    scores = jax.nn.sigmoid(logits.T[:N_EXPERTS])
    biased = scores + rb_ref[...]
    rows = [biased[e:e + 1] for e in range(N_EXPERTS)]
    never = jnp.zeros_like(rows[0], dtype=jnp.bool_)
    group_scores, picked = [], []
    for g in range(N_EXPERT_GROUPS):
        vals = rows[g * EXPERTS_PER_GROUP:(g + 1) * EXPERTS_PER_GROUP]
        top1, first = _first_max(vals, [never] * EXPERTS_PER_GROUP)
        top2, second = _first_max(vals, first)
        group_scores.append(top1 + top2)
        picked.append([a | b for a, b in zip(first, second)])
    _, group_sel = _first_max(group_scores, [never] * N_EXPERT_GROUPS)
    chosen = [jnp.where(group_sel[e // EXPERTS_PER_GROUP] & picked[e // EXPERTS_PER_GROUP][e % EXPERTS_PER_GROUP],
                        scores[e:e + 1], 0.0) for e in range(N_EXPERTS)]
    denom = functools.reduce(jnp.add, chosen)
    gate_ref[...] = jnp.concatenate([w / denom * ROUTED_SCALE for w in chosen], axis=0)


def router(xs, mod_l, router_w, router_b, *, tokens_per_batch, n_ctx, tile):
    m, d = xs.shape
    bsz = m // tokens_per_batch
    tiles_per_batch, n_ctx_tiles = tokens_per_batch // tile, n_ctx // tile

    def mod_row(i):
        return jnp.where(i % tiles_per_batch < n_ctx_tiles, bsz, i // tiles_per_batch)

    rw_pad = jnp.pad(router_w, ((0, 0), (0, LANE - N_EXPERTS)))
    return pl.pallas_call(
        _router_body,
        grid=(m // tile,),
        in_specs=[pl.BlockSpec((tile, d), lambda i: (i, 0)),
                  pl.BlockSpec((None, 1, d), lambda i: (mod_row(i), 0, 4)),
                  pl.BlockSpec((None, 1, d), lambda i: (mod_row(i), 0, 3)),
                  pl.BlockSpec((d, LANE), lambda i: (0, 0)),
                  pl.BlockSpec((N_EXPERTS, 1), lambda i: (0, 0))],
        out_specs=[pl.BlockSpec((tile, d), lambda i: (i, 0)),
                   pl.BlockSpec((N_EXPERTS, tile), lambda i: (0, i))],
        out_shape=[jax.ShapeDtypeStruct((m, d), BF16), jax.ShapeDtypeStruct((N_EXPERTS, m), F32)],
        compiler_params=_params("parallel"),
        name="router",
    )(xs, mod_l, mod_l, rw_pad, router_b.reshape(N_EXPERTS, 1))


def _pad_w_in(w):
    d = w.shape[0]
    dt_end = RW_IN + MB_IN
    pe_end = dt_end + MLA_IN
    return jnp.concatenate([w[:, :dt_end], jnp.zeros((d, LANE - 2 * N_HEADS), w.dtype),
                            w[:, dt_end:pe_end], jnp.zeros((d, LANE - MLA_ROPE), w.dtype),
                            w[:, pe_end:]], axis=1)


def kernel(x, c, ctx, c_ctx, ada_w, ada_b, w_in, w_out, ln1_g, ln1_b, ln2_g, ln2_b, rw_mu, rw_w0, rw_w_up, rw_a0, rw_a_up, rw_g_up, rw_k_k, rw_k_a, rw_r_k, rw_gn_g, rw_gn_b, mb_conv_w, mb_conv_b, mb_a_log, mb_dt_bias, mb_d, mb_norm_g, mla_q_norm, mla_w_uq, mla_kv_norm, mla_w_ukv, na_rpb, router_w, router_b, exp_w_gate, exp_w_up, exp_w_down):
    bsz, seq, d = x.shape
    n_ctx = ctx.shape[1]
    t = n_ctx + seq
    depth = ada_w.shape[0]
    tile = min(256, n_ctx)
    cond = jnp.concatenate([jax.nn.silu(c), jax.nn.silu(c_ctx)[None]], axis=0)
    mod = modulation(cond, ada_w, ada_b)
    xs = jnp.concatenate([ctx, x], axis=1)
    is_ctx = jnp.asarray(np.arange(t) < n_ctx)[None, :, None]

    def per_token(l, j):
        rows = mod[l, :, j * d:(j + 1) * d]
        return jnp.where(is_ctx, rows[bsz][None, None, :], rows[:bsz][:, None, :])

    for l in range(depth):
        sh1, sc1, g1, g2 = (per_token(l, j) for j in (0, 1, 2, 5))
        xm = (xs * (1 + sc1) + sh1).astype(BF16).reshape(bsz * t, d)
        p = matmul(xm, _pad_w_in(w_in[l]).astype(BF16), tm_target=544, tn_target=1152).reshape(bsz, t, N_IN_PAD)
        o_rw = _rwkv_mixer(p[..., OFF_RW:OFF_MB_Z], n_ctx, rw_mu[l], rw_w0[l], rw_w_up[l], rw_a0[l], rw_a_up[l],
                           rw_g_up[l], rw_k_k[l], rw_k_a[l], rw_r_k[l], rw_gn_g[l], rw_gn_b[l])
        o_mb = _mamba_mixer(p[..., OFF_MB_Z:OFF_MB_XBC], p[..., OFF_MB_XBC:OFF_MB_DT],
                            p[..., OFF_MB_DT:OFF_MB_DT + 2 * N_HEADS], n_ctx, mb_conv_w[l], mb_conv_b[l],
                            mb_a_log[l], mb_dt_bias[l], mb_d[l], mb_norm_g[l])
        o_mla = _mla_mixer(p[..., OFF_MLA_Q:OFF_MLA_KV], p[..., OFF_MLA_KV:OFF_MLA_PE],
                           p[..., OFF_MLA_PE:OFF_MLA_PE + MLA_ROPE], n_ctx, mla_q_norm[l], mla_w_uq[l],
                           mla_kv_norm[l], mla_w_ukv[l], tile)
        o_na = _natten_mixer(p[..., OFF_NA:], n_ctx, na_rpb[l], tile)
        mix = jnp.concatenate([o_rw, o_mb, o_mla, o_na], axis=-1).astype(BF16).reshape(bsz * t, d)
        y = matmul(mix, w_out[l].astype(BF16), tm_target=544).reshape(bsz, t, d)
        xs = _layer_norm(DEEPNORM_ALPHA * xs + g1 * y, ln1_g[l], ln1_b[l])
        h, gate = router(xs.reshape(bsz * t, d), mod[l][:, None, :], router_w, router_b,
                         tokens_per_batch=t, n_ctx=n_ctx, tile=tile)
        f = moe_experts(h, gate[:, :, None], exp_w_gate[l].astype(BF16), exp_w_up[l].astype(BF16),
                        exp_w_down[l].astype(BF16)).reshape(bsz, t, d)
        xs = _layer_norm(DEEPNORM_ALPHA * xs + g2 * f, ln2_g[l], ln2_b[l])
    return xs[:, n_ctx:]
```

```python
import functools
import math

import numpy as np
import jax
import jax.numpy as jnp
from jax import lax
from jax.experimental import pallas as pl
from jax.experimental.pallas import tpu as pltpu

F32 = jnp.float32
BF16 = jnp.bfloat16
HIGHEST = lax.Precision.HIGHEST

D_MODEL = 2048
DEPTH = 4
GRID_W = 64
GROUP_DIM = D_MODEL // 4
HEAD_DIM = 64
N_HEADS = GROUP_DIM // HEAD_DIM

RW_LORA = 64
RW_G_LORA = 128
RW_DECAY_SCALE = 0.606531
RW_GN_EPS = 64e-5
RW_IN = 3 * GROUP_DIM + 4 * RW_LORA + RW_G_LORA
RW_CHUNK = 64

MB_GROUPS = 2
MB_STATE = 128
MB_CONV = 5
MB_CHUNK = 128
MB_CONV_DIM = GROUP_DIM + 2 * MB_GROUPS * MB_STATE
MB_IN = GROUP_DIM + MB_CONV_DIM + 2 * N_HEADS

MLA_Q_LORA = 3 * D_MODEL // 16
MLA_KV_LORA = D_MODEL // 16
MLA_NOPE = 64
MLA_ROPE = 32
MLA_QK = MLA_NOPE + MLA_ROPE
MLA_QK_PAD = 128
MLA_IN = MLA_Q_LORA + MLA_KV_LORA + MLA_ROPE
ROPE_BASE = 10000.0

NA_WIN_ROWS = 8
NA_WIN_COLS = 16
NA_IN = 3 * GROUP_DIM
NA_MASK = -1e30

N_EXPERTS = 16
N_EXPERT_GROUPS = 4
EXPERTS_PER_GROUP = N_EXPERTS // N_EXPERT_GROUPS
TOP_K = 2
D_EXPERT = D_MODEL // 4
ROUTED_SCALE = 2.5

DEEPNORM_ALPHA = (2 * DEPTH) ** 0.25
LN_EPS = 1e-6
RMS_EPS = 1e-6

LANE = 128
OFF_RW = 0
OFF_MB_Z = OFF_RW + RW_IN
OFF_MB_XBC = OFF_MB_Z + GROUP_DIM
OFF_MB_DT = OFF_MB_XBC + MB_CONV_DIM
OFF_MLA_Q = OFF_MB_DT + LANE
OFF_MLA_KV = OFF_MLA_Q + MLA_Q_LORA
OFF_MLA_PE = OFF_MLA_KV + MLA_KV_LORA
OFF_NA = OFF_MLA_PE + LANE
N_IN_PAD = OFF_NA + NA_IN

VMEM_LIMIT = 56 * 1024 * 1024


def _params(*sem):
    return pltpu.CompilerParams(dimension_semantics=sem, vmem_limit_bytes=VMEM_LIMIT)


def _pick_tile(n, target, quantum):
    best = None
    for t in range(quantum, min(n, target) + 1, quantum):
        if n % t == 0:
            best = t
    assert best is not None, (n, target, quantum)
    return best


def _mm_body(x_ref, w_ref, o_ref, *, precision):
    o_ref[...] = jnp.dot(x_ref[...], w_ref[...], preferred_element_type=F32,
                         precision=precision).astype(o_ref.dtype)


def matmul(x, w, *, tm_target=512, tn_target=1024, precision=None, out_dtype=F32):
    m, k = x.shape
    n = w.shape[1]
    tm = _pick_tile(m, tm_target, 16)
    tn = _pick_tile(n, tn_target, LANE)
    return pl.pallas_call(
        functools.partial(_mm_body, precision=precision),
        grid=(m // tm, n // tn),
        in_specs=[pl.BlockSpec((tm, k), lambda i, j: (i, 0)),
                  pl.BlockSpec((k, tn), lambda i, j: (0, j))],
        out_specs=pl.BlockSpec((tm, tn), lambda i, j: (i, j)),
        out_shape=jax.ShapeDtypeStruct((m, n), out_dtype),
        compiler_params=_params("parallel", "parallel"),
        name="matmul",
    )(x, w)


def _is_ctx_row(tm, tokens_per_batch, n_ctx):
    pos = (pl.program_id(0) * tm) % tokens_per_batch + lax.broadcasted_iota(jnp.int32, (tm, 1), 0)
    return pos < n_ctx


def _proj_in_body(x_ref, scc_ref, scb_ref, shc_ref, shb_ref, w_ref, o_ref, xm_ref, *, tokens_per_batch, n_ctx):
    is_ctx = _is_ctx_row(x_ref.shape[0], tokens_per_batch, n_ctx)

    @pl.when(pl.program_id(1) == 0)
    def _():
        sc = jnp.where(is_ctx, scc_ref[0], scb_ref[0])
        sh = jnp.where(is_ctx, shc_ref[0], shb_ref[0])
        xm_ref[...] = (x_ref[...] * (1.0 + sc) + sh).astype(BF16)

    o_ref[...] = jnp.dot(xm_ref[...], w_ref[...], preferred_element_type=F32)


def _mod_specs(bsz, tiles_per_batch, chunk, d):
    return [pl.BlockSpec((None, 1, d), lambda i, *_: (bsz, 0, chunk)),
            pl.BlockSpec((None, 1, d), lambda i, *_: (i // tiles_per_batch, 0, chunk))]


def proj_in(xs, mod_l, w, *, tokens_per_batch, n_ctx):
    m, d = xs.shape
    n = w.shape[1]
    bsz = m // tokens_per_batch
    tm = _pick_tile(tokens_per_batch, 640, 16)
    tn = _pick_tile(n, 1152, LANE)
    tiles_per_batch = tokens_per_batch // tm
    return pl.pallas_call(
        functools.partial(_proj_in_body, tokens_per_batch=tokens_per_batch, n_ctx=n_ctx),
        grid=(m // tm, n // tn),
        in_specs=[pl.BlockSpec((tm, d), lambda i, j: (i, 0)),
                  *_mod_specs(bsz, tiles_per_batch, 1, d), *_mod_specs(bsz, tiles_per_batch, 0, d),
                  pl.BlockSpec((d, tn), lambda i, j: (0, j))],
        out_specs=pl.BlockSpec((tm, tn), lambda i, j: (i, j)),
        out_shape=jax.ShapeDtypeStruct((m, n), F32),
        scratch_shapes=[pltpu.VMEM((tm, d), BF16)],
        compiler_params=_params("parallel", "arbitrary"),
        name="proj_in",
    )(xs, mod_l, mod_l, mod_l, mod_l, w)


def _proj_out_body(m0_ref, m1_ref, m2_ref, m3_ref, w_ref, x_ref, gc_ref, gb_ref, lg_ref, lb_ref, o_ref,
                   *, tokens_per_batch, n_ctx):
    gd = m0_ref.shape[1]
    y = None
    for j, m_ref in enumerate((m0_ref, m1_ref, m2_ref, m3_ref)):
        part = jnp.dot(m_ref[...].astype(BF16), w_ref[j * gd:(j + 1) * gd, :], preferred_element_type=F32)
        y = part if y is None else y + part
    gate = jnp.where(_is_ctx_row(x_ref.shape[0], tokens_per_batch, n_ctx), gc_ref[0], gb_ref[0])
    z = DEEPNORM_ALPHA * x_ref[...] + gate * y
    mu = jnp.mean(z, axis=-1, keepdims=True)
    zc = z - mu
    var = jnp.mean(zc * zc, axis=-1, keepdims=True)
    o_ref[...] = zc * lax.rsqrt(var + LN_EPS) * lg_ref[...] + lb_ref[...]


def proj_out(mixes, w, xs, mod_l, ln_g, ln_b, *, tokens_per_batch, n_ctx):
    m, d = xs.shape
    gd = mixes[0].shape[1]
    bsz = m // tokens_per_batch
    tm = _pick_tile(tokens_per_batch, 320, 16)
    tiles_per_batch = tokens_per_batch // tm
    row = lambda i: (i, 0)
    fixed = lambda i: (0, 0)
    return pl.pallas_call(
        functools.partial(_proj_out_body, tokens_per_batch=tokens_per_batch, n_ctx=n_ctx),
        grid=(m // tm,),
        in_specs=[pl.BlockSpec((tm, gd), row)] * 4
        + [pl.BlockSpec((d, d), fixed), pl.BlockSpec((tm, d), row), *_mod_specs(bsz, tiles_per_batch, 2, d),
           pl.BlockSpec((1, d), fixed), pl.BlockSpec((1, d), fixed)],
        out_specs=pl.BlockSpec((tm, d), row),
        out_shape=jax.ShapeDtypeStruct((m, d), F32),
        compiler_params=_params("parallel"),
        name="proj_out",
    )(*mixes, w, xs, mod_l, mod_l, ln_g.reshape(1, d), ln_b.reshape(1, d))


def _mod_body(c_ref, w_ref, b_ref, o_ref, *, n_rows):
    w = w_ref[...]
    reps = w.shape[1] // LANE
    o_ref[...] = jnp.zeros_like(o_ref)
    for m in range(n_rows):
        cb = c_ref[m]
        cbt = jnp.concatenate([cb] * reps, axis=1)
        o_ref[m:m + 1, :] = jnp.sum(w * cbt, axis=0, keepdims=True) + b_ref[...]


def modulation(cond, ada_w, ada_b):
    n_rows, k = cond.shape
    depth, _, n = ada_w.shape
    tn = 512
    cond_b = jnp.broadcast_to(cond[:, :, None], (n_rows, k, LANE))
    return pl.pallas_call(
        functools.partial(_mod_body, n_rows=n_rows),
        grid=(depth, n // tn),
        in_specs=[pl.BlockSpec((n_rows, k, LANE), lambda l, j: (0, 0, 0)),
                  pl.BlockSpec((None, k, tn), lambda l, j: (l, 0, j)),
                  pl.BlockSpec((None, 1, tn), lambda l, j: (l, 0, j))],
        out_specs=pl.BlockSpec((None, 8, tn), lambda l, j: (l, 0, j)),
        out_shape=jax.ShapeDtypeStruct((depth, 8, n), F32),
        compiler_params=_params("parallel", "parallel"),
        name="modulation",
    )(cond_b, ada_w, ada_b.reshape(depth, 1, n))


def _bmm(a, b, spec, mode):
    if mode == 'f32':
        return jnp.einsum(spec, a, b, preferred_element_type=F32, precision=HIGHEST)
    a_hi, b_hi = a.astype(BF16), b.astype(BF16)
    out = jnp.einsum(spec, a_hi, b_hi, preferred_element_type=F32)
    if mode == 'x3':
        a_lo = (a - a_hi.astype(F32)).astype(BF16)
        b_lo = (b - b_hi.astype(F32)).astype(BF16)
        out = out + (jnp.einsum(spec, a_hi, b_lo, preferred_element_type=F32)
                     + jnp.einsum(spec, a_lo, b_hi, preferred_element_type=F32))
    return out


def _unit_triangular_inverse(l_mat, ti, si):
    c = l_mat.shape[1]
    nn = 'hts,hsn->htn'
    blk = 8
    eye = (ti == si).astype(F32)
    l_d = jnp.where((ti // blk) == (si // blk), l_mat, 0.0)
    p2 = _bmm(l_d, l_d, nn, 'bf16')
    p4 = _bmm(p2, p2, nn, 'bf16')
    inv = _bmm(eye - l_d, eye + p2, nn, 'bf16')
    inv = _bmm(inv, eye + p4, nn, 'bf16')
    while blk < c:
        pair = ((ti // (2 * blk)) == (si // (2 * blk))) & ((ti // blk) != (si // blk))
        off = jnp.where(pair, l_mat, 0.0)
        inv = inv - _bmm(_bmm(inv, off, nn, 'bf16'), inv, nn, 'bf16')
        blk *= 2
    return inv


def _first_head_lanes(shape):
    return lax.broadcasted_iota(jnp.int32, shape, len(shape) - 1) % LANE < HEAD_DIM


def _rwkv_body(*refs):
    fwd_in, bwd_in, (yf_ref, yb_ref, stf_ref, stb_ref) = refs[:6], refs[6:12], refs[12:]

    @pl.when(pl.program_id(1) == 0)
    def _():
        stf_ref[...] = jnp.zeros_like(stf_ref)
        stb_ref[...] = jnp.zeros_like(stb_ref)

    n_pairs = stf_ref.shape[0]
    a_kb_f, parts_f = _rwkv_chunk_matrices(*(ref[0] for ref in fwd_in), reverse=False)
    a_kb_b, parts_b = _rwkv_chunk_matrices(*(ref[0] for ref in bwd_in), reverse=True)
    n_h = len(a_kb_f)
    c = a_kb_f[0].shape[0]
    ti = lax.broadcasted_iota(jnp.int32, (c, c), 0)
    si = lax.broadcasted_iota(jnp.int32, (c, c), 1)
    inv = _unit_triangular_inverse(jnp.stack(a_kb_f + a_kb_b, axis=0), ti, si)
    for parts, inv_d, y_ref, st_ref in ((parts_f, inv[:n_h], yf_ref, stf_ref), (parts_b, inv[n_h:], yb_ref, stb_ref)):
        ys, new_states = _rwkv_chunk_apply(parts, inv_d, [st_ref[j] for j in range(n_pairs)])
        y_ref[0] = jnp.concatenate(ys, axis=1)
        for j in range(n_pairs):
            st_ref[j] = new_states[j]


def _rwkv_chunk_matrices(r, k, v, kk, a, lw, *, reverse):
    c, width = r.shape
    n_pairs = width // LANE
    ti = lax.broadcasted_iota(jnp.int32, (c, c), 0)
    si = lax.broadcasted_iota(jnp.int32, (c, c), 1)
    incl = (si >= ti) if reverse else (si <= ti)
    strict = (si > ti) if reverse else (si < ti)
    cum = _bmm(incl.astype(F32), lw, 'ts,sn->tn', 'f32')
    tot = cum[:1] if reverse else cum[c - 1:]
    b = kk * a
    e_neg = jnp.exp(-cum)
    e_rem = jnp.exp(tot - cum)
    e_tot = jnp.exp(tot)
    rq = r * jnp.exp(cum)
    kq = kk * jnp.exp(cum - lw)
    kd = k * e_neg
    bd = b * e_neg
    kdc = k * e_rem
    bdc = b * e_rem

    first = _first_head_lanes((c, LANE))
    a_rk, a_kk, a_rb, a_kb = [], [], [], []
    for j in range(n_pairs):
        sl = slice(j * LANE, (j + 1) * LANE)
        rq2, kq2 = rq[:, sl], kq[:, sl]
        qq = jnp.concatenate([jnp.where(first, rq2, 0.0), jnp.where(first, kq2, 0.0),
                              jnp.where(first, 0.0, rq2), jnp.where(first, 0.0, kq2)], axis=0)
        ak = _bmm(qq, kd[:, sl], 'tn,sn->ts', 'bf16')
        ab = _bmm(qq, bd[:, sl], 'tn,sn->ts', 'bf16')
        for x in range(2):
            a_rk.append(jnp.where(incl, ak[2 * x * c:(2 * x + 1) * c], 0.0))
            a_kk.append(jnp.where(strict, ak[(2 * x + 1) * c:(2 * x + 2) * c], 0.0))
            a_rb.append(jnp.where(incl, ab[2 * x * c:(2 * x + 1) * c], 0.0))
            a_kb.append(jnp.where(strict, ab[(2 * x + 1) * c:(2 * x + 2) * c], 0.0))
    return a_kb, (v, rq, kq, kdc, bdc, e_tot, a_rk, a_kk, a_rb)


def _rwkv_chunk_apply(parts, inv, states):
    v, rq, kq, kdc, bdc, e_tot, a_rk, a_kk, a_rb = parts
    c = v.shape[0]
    n_pairs = len(states)
    first = _first_head_lanes((c, LANE))
    first2 = _first_head_lanes((c, 2 * LANE))
    ys, new_states = [], []
    row_n = lax.broadcasted_iota(jnp.int32, (LANE, LANE), 0)
    col_n = lax.broadcasted_iota(jnp.int32, (LANE, LANE), 1)
    same_head = (row_n // HEAD_DIM) == (col_n // HEAD_DIM)
    eye_n = row_n == col_n
    nn = 'ts,sn->tn'
    tn = 'sn,sm->nm'
    for j in range(n_pairs):
        sl = slice(j * LANE, (j + 1) * LANE)
        ha, hb = 2 * j, 2 * j + 1
        v2, kq2 = v[:, sl], kq[:, sl]
        xa = _bmm(inv[ha], jnp.concatenate([_bmm(a_kk[ha], v2, nn, 'bf16'), kq2], axis=1), nn, 'bf16')
        xb = _bmm(inv[hb], jnp.concatenate([_bmm(a_kk[hb], v2, nn, 'bf16'), kq2], axis=1), nn, 'bf16')
        x2 = jnp.where(first2, xa, xb)
        arx = jnp.where(first2, _bmm(a_rb[ha], x2, nn, 'bf16'), _bmm(a_rb[hb], x2, nn, 'bf16'))
        y0 = jnp.where(first, _bmm(a_rk[ha], v2, nn, 'bf16'), _bmm(a_rk[hb], v2, nn, 'bf16')) - arx[:, :LANE]
        rqp = rq[:, sl] - arx[:, LANE:]
        bx = _bmm(bdc[:, sl], x2, tn, 'bf16')
        tadd = jnp.where(same_head, _bmm(kdc[:, sl], v2, tn, 'bf16') - bx[:, :LANE], 0.0)
        decay = jnp.where(eye_n, jnp.broadcast_to(e_tot[:, sl], (LANE, LANE)), 0.0)
        p = jnp.where(same_head, decay - bx[:, LANE:], 0.0)
        t0 = states[j]
        ys.append(_bmm(rqp, t0, 'tn,nm->tm', 'f32') + y0)
        new_states.append(_bmm(p, t0, 'jn,nm->jm', 'f32') + tadd)
    return ys, new_states


def _scan_chunk_index(c, n_ctx_chunks, n_chunks, reverse):
    if not reverse:
        return c
    return jnp.where(c < n_ctx_chunks, n_ctx_chunks - 1 - c, n_chunks - 1 - (c - n_ctx_chunks))


def rwkv_scan(r, v, kk, k_dirs, a_dirs, lw_dirs, *, n_ctx):
    bsz, t, width = r.shape
    c = RW_CHUNK
    n_chunks, n_ctx_chunks = t // c, n_ctx // c
    specs = [pl.BlockSpec((1, c, width),
                          lambda b, i, rev=rev: (b, _scan_chunk_index(i, n_ctx_chunks, n_chunks, rev), 0))
             for rev in (False, True)]
    state = pltpu.VMEM((width // LANE, LANE, LANE), F32)
    args = [(r, k_dirs[d], v, kk, a_dirs[d], lw_dirs[d]) for d in range(2)]
    return pl.pallas_call(
        _rwkv_body,
        grid=(bsz, n_chunks),
        in_specs=[specs[0]] * 6 + [specs[1]] * 6,
        out_specs=specs,
        out_shape=[jax.ShapeDtypeStruct((bsz, t, width), F32)] * 2,
        scratch_shapes=[state, state],
        compiler_params=_params("parallel", "arbitrary"),
        name="rwkv_scan",
    )(*args[0], *args[1])


def _ssd_body(x_ref, dt_ref, dac_ref, dar_ref, y_ref, st_ref, *, reverse):
    @pl.when(pl.program_id(1) == 0)
    def _():
        st_ref[...] = jnp.zeros_like(st_ref)

    xbc = x_ref[0]
    dt = dt_ref[0]
    c = xbc.shape[0]
    n_h = dt.shape[1]
    n_pairs = n_h // 2
    gd = n_h * HEAD_DIM
    n_g = (xbc.shape[1] - gd) // (2 * MB_STATE)
    pairs_per_group = n_pairs // n_g
    ti = lax.broadcasted_iota(jnp.int32, (c, c), 0)
    si = lax.broadcasted_iota(jnp.int32, (c, c), 1)
    incl = (si >= ti) if reverse else (si <= ti)
    incl_t = (ti >= si) if reverse else (ti <= si)
    cs_c = jnp.dot(incl.astype(F32), dac_ref[0], preferred_element_type=F32, precision=HIGHEST)
    cs_r = jnp.dot(dar_ref[0], incl_t.astype(F32), preferred_element_type=F32, precision=HIGHEST)
    tot = cs_c[:1, :] if reverse else cs_c[c - 1:, :]
    e_cs = jnp.exp(cs_c)
    e_rem = jnp.exp(tot - cs_c)
    e_tot = jnp.exp(tot)
    bm = [xbc[:, gd + g * MB_STATE:gd + (g + 1) * MB_STATE].astype(BF16) for g in range(n_g)]
    cm = [xbc[:, gd + (n_g + g) * MB_STATE:gd + (n_g + g + 1) * MB_STATE].astype(BF16) for g in range(n_g)]
    gmat = [jnp.einsum('ln,sn->ls', cm[g], bm[g], preferred_element_type=F32) for g in range(n_g)]
    first = _first_head_lanes((c, LANE))
    first_rows = lax.broadcasted_iota(jnp.int32, (LANE, 1), 0) < HEAD_DIM
    pick = lambda m, ha: jnp.where(first, m[:, ha:ha + 1], m[:, ha + 1:ha + 2])
    for j in range(n_pairs):
        g = j // pairs_per_group
        ha = 2 * j
        sl = slice(j * LANE, (j + 1) * LANE)
        xdt = xbc[:, sl] * pick(dt, ha)
        xdt_b = xdt.astype(BF16)
        y_heads = []
        for h in (ha, ha + 1):
            seg = cs_c[:, h:h + 1] - cs_r[h:h + 1, :]
            lmat = jnp.exp(jnp.where(incl, seg, NA_MASK))
            y_heads.append(jnp.dot((gmat[g] * lmat).astype(BF16), xdt_b, preferred_element_type=F32))
        st = st_ref[j]
        y_off = jnp.einsum('ln,pn->lp', cm[g], st.astype(BF16), preferred_element_type=F32) * pick(e_cs, ha)
        y_ref[0, :, sl] = jnp.where(first, y_heads[0], y_heads[1]) + y_off
        xdec = (xdt * pick(e_rem, ha)).astype(BF16)
        keep = jnp.where(first_rows, e_tot[:, ha:ha + 1], e_tot[:, ha + 1:ha + 2])
        st_ref[j] = keep * st + jnp.einsum('lp,ln->pn', xdec, bm[g], preferred_element_type=F32)


def ssd_scan(xbc, dt, da, *, n_ctx, reverse):
    bsz, t, width = xbc.shape
    n_h = dt.shape[2]
    gd = n_h * HEAD_DIM
    c = MB_CHUNK
    n_chunks, n_ctx_chunks = t // c, n_ctx // c
    idx = lambda i: _scan_chunk_index(i, n_ctx_chunks, n_chunks, reverse)
    da_row = jnp.swapaxes(da, 1, 2)
    return pl.pallas_call(
        functools.partial(_ssd_body, reverse=reverse),
        grid=(bsz, n_chunks),
        in_specs=[pl.BlockSpec((1, c, width), lambda b, i: (b, idx(i), 0)),
                  pl.BlockSpec((1, c, n_h), lambda b, i: (b, idx(i), 0)),
                  pl.BlockSpec((1, c, n_h), lambda b, i: (b, idx(i), 0)),
                  pl.BlockSpec((1, n_h, c), lambda b, i: (b, 0, idx(i)))],
        out_specs=pl.BlockSpec((1, c, gd), lambda b, i: (b, idx(i), 0)),
        out_shape=jax.ShapeDtypeStruct((bsz, t, gd), F32),
        scratch_shapes=[pltpu.VMEM((n_h // 2, 2 * HEAD_DIM, MB_STATE), F32)],
        compiler_params=_params("parallel", "arbitrary"),
        name="ssd_scan_bwd" if reverse else "ssd_scan_fwd",
    )(xbc, dt, da, da_row)


def _softmax_pv(s_parts, v_parts):
    m = functools.reduce(jnp.maximum, [jnp.max(s, axis=-1, keepdims=True) for s in s_parts])
    ps = [jnp.exp(s - m) for s in s_parts]
    l = functools.reduce(jnp.add, [jnp.sum(p, axis=-1, keepdims=True) for p in ps])
    o = functools.reduce(jnp.add, [jnp.dot(p.astype(BF16), v, preferred_element_type=F32)
                                   for p, v in zip(ps, v_parts)])
    return o / l


def _attn_body(q_ref, k_ref, v_ref, o_ref, *, n_ctx, n_ctx_tiles):
    dq = q_ref.shape[2] // 2

    def attend(n_keys):
        outs = []
        for x in range(2):
            s = jnp.einsum('qd,kd->qk', q_ref[0, :, x * dq:(x + 1) * dq], k_ref[0, 0:n_keys, x * dq:(x + 1) * dq],
                           preferred_element_type=F32)
            outs.append(_softmax_pv([s], [v_ref[0, 0:n_keys]]))
        o_ref[0] = jnp.where(_first_head_lanes(outs[0].shape), outs[0], outs[1])

    @pl.when(pl.program_id(2) < n_ctx_tiles)
    def _():
        attend(n_ctx)

    @pl.when(pl.program_id(2) >= n_ctx_tiles)
    def _():
        attend(k_ref.shape[1])


def attention(q, k, v, *, n_ctx, tile):
    bsz, t, width = v.shape
    n_pairs = width // LANE
    dq2 = q.shape[2] // n_pairs
    return pl.pallas_call(
        functools.partial(_attn_body, n_ctx=n_ctx, n_ctx_tiles=n_ctx // tile),
        grid=(bsz, n_pairs, t // tile),
        in_specs=[pl.BlockSpec((1, tile, dq2), lambda b, j, i: (b, i, j)),
                  pl.BlockSpec((1, t, dq2), lambda b, j, i: (b, 0, j)),
                  pl.BlockSpec((1, t, LANE), lambda b, j, i: (b, 0, j))],
        out_specs=pl.BlockSpec((1, tile, LANE), lambda b, j, i: (b, i, j)),
        out_shape=jax.ShapeDtypeStruct((bsz, t, width), F32),
        compiler_params=_params("parallel", "parallel", "arbitrary"),
        name="attention",
    )(q, k, v)


def _na_body(q_ref, k_ref, v_ref, bias_ref, o_ref, kb_ref, vb_ref, *, n_ctx, n_rows):
    i = pl.program_id(2)
    rows_per_step = q_ref.shape[1] // GRID_W
    win = NA_WIN_ROWS * GRID_W
    q2 = q_ref[0] * (HEAD_DIM ** -0.5)

    def head_queries(q_rows, x):
        first = _first_head_lanes(q_rows.shape)
        return jnp.where(first if x == 0 else ~first, q_rows, 0.0).astype(BF16)

    def scores(q, k):
        return jnp.einsum('qd,kd->qk', q, k, preferred_element_type=F32)

    @pl.when(i == 0)
    def _():
        kb_ref[...] = k_ref[0].astype(BF16)
        vb_ref[...] = v_ref[0].astype(BF16)
        kc, vc = kb_ref[0:n_ctx], vb_ref[0:n_ctx]
        outs = [_softmax_pv([scores(head_queries(q2, x), kc)], [vc]) for x in range(2)]
        o_ref[0] = jnp.where(_first_head_lanes(outs[0].shape), outs[0], outs[1])

    @pl.when(i > 0)
    def _():
        kc, vc = kb_ref[0:n_ctx], vb_ref[0:n_ctx]
        for rr in range(rows_per_step):
            r = (i - 1) * rows_per_step + rr
            rs = jnp.clip(r - NA_WIN_ROWS // 2, 0, n_rows - NA_WIN_ROWS)
            start = pl.multiple_of(n_ctx + rs * GRID_W, GRID_W)
            kw = kb_ref[pl.ds(start, win)]
            vw = vb_ref[pl.ds(start, win)]
            q_row = q2[rr * GRID_W:(rr + 1) * GRID_W]
            outs = []
            for x in range(2):
                qx = head_queries(q_row, x)
                outs.append(_softmax_pv([scores(qx, kw) + bias_ref[x, r - rs], scores(qx, kc)], [vw, vc]))
            o_ref[0, rr * GRID_W:(rr + 1) * GRID_W] = jnp.where(_first_head_lanes(outs[0].shape), outs[0], outs[1])


def na_bias_table(rpb):
    wr = NA_WIN_ROWS
    qv = np.arange(GRID_W)[:, None]
    cv = np.arange(GRID_W)[None, :]
    col_start = np.clip(qv - NA_WIN_COLS // 2, 0, GRID_W - NA_WIN_COLS)
    valid = (cv >= col_start) & (cv < col_start + NA_WIN_COLS)
    offs = np.arange(2 * NA_WIN_COLS - 1)[:, None, None]
    pick = ((cv - qv + (NA_WIN_COLS - 1))[None] == offs) & valid[None]
    cols = jnp.einsum('hro,oqc->hrqc', rpb, jnp.asarray(pick, F32), precision=HIGHEST)
    cols = jnp.where(valid, cols, NA_MASK)
    tab = jnp.stack([cols[:, wr - 1 - var:2 * wr - 1 - var] for var in range(wr)], axis=1)
    tab = jnp.transpose(tab, (0, 1, 3, 2, 4))
    return tab.reshape(rpb.shape[0], wr, GRID_W, wr * GRID_W)


def neighborhood_attention(p, bias, *, n_ctx, col0):
    bsz, t, _ = p.shape
    n_rows = (t - n_ctx) // GRID_W
    n_pairs = GROUP_DIM // LANE
    c0 = col0 // LANE
    return pl.pallas_call(
        functools.partial(_na_body, n_ctx=n_ctx, n_rows=n_rows),
        grid=(bsz, n_pairs, t // n_ctx),
        in_specs=[pl.BlockSpec((1, n_ctx, LANE), lambda b, j, i: (b, i, c0 + j)),
                  pl.BlockSpec((1, t, LANE), lambda b, j, i: (b, 0, c0 + n_pairs + j)),
                  pl.BlockSpec((1, t, LANE), lambda b, j, i: (b, 0, c0 + 2 * n_pairs + j)),
                  pl.BlockSpec((2, NA_WIN_ROWS, GRID_W, NA_WIN_ROWS * GRID_W), lambda b, j, i: (j, 0, 0, 0))],
        out_specs=pl.BlockSpec((1, n_ctx, LANE), lambda b, j, i: (b, i, j)),
        out_shape=jax.ShapeDtypeStruct((bsz, t, GROUP_DIM), F32),
        scratch_shapes=[pltpu.VMEM((t, LANE), BF16), pltpu.VMEM((t, LANE), BF16)],
        compiler_params=_params("parallel", "parallel", "arbitrary"),
        name="neighborhood_attention",
    )(p, p, p, bias)


def _moe_body(h_ref, gate_ref, wg_ref, wu_ref, wd_ref, o_ref, acc_ref):
    e = pl.program_id(1)

    @pl.when(e == 0)
    def _():
        acc_ref[...] = jnp.zeros_like(acc_ref)

    h = h_ref[...]
    hid = jax.nn.silu(jnp.dot(h, wg_ref[0], preferred_element_type=F32)) * jnp.dot(h, wu_ref[0], preferred_element_type=F32)
    acc_ref[...] += gate_ref[0] * jnp.dot(hid.astype(BF16), wd_ref[0], preferred_element_type=F32)

    @pl.when(e == pl.num_programs(1) - 1)
    def _():
        o_ref[...] = acc_ref[...]


def moe_experts(h, gate, w_gate, w_up, w_down):
    m, d = h.shape
    n_e, _, d_e = w_gate.shape
    tm = _pick_tile(m, 544, 16)
    return pl.pallas_call(
        _moe_body,
        grid=(m // tm, n_e),
        in_specs=[pl.BlockSpec((tm, d), lambda i, e: (i, 0)),
                  pl.BlockSpec((1, tm, 1), lambda i, e: (e, i, 0)),
                  pl.BlockSpec((1, d, d_e), lambda i, e: (e, 0, 0)),
                  pl.BlockSpec((1, d, d_e), lambda i, e: (e, 0, 0)),
                  pl.BlockSpec((1, d_e, d), lambda i, e: (e, 0, 0))],
        out_specs=pl.BlockSpec((tm, d), lambda i, e: (i, 0)),
        out_shape=jax.ShapeDtypeStruct((m, d), F32),
        scratch_shapes=[pltpu.VMEM((tm, d), F32)],
        compiler_params=_params("parallel", "arbitrary"),
        name="moe_experts",
    )(h, gate, w_gate, w_up, w_down)


def _layer_norm(x, g, b):
    mu = jnp.mean(x, -1, keepdims=True)
    var = jnp.mean(jnp.square(x - mu), -1, keepdims=True)
    return (x - mu) * lax.rsqrt(var + LN_EPS) * g + b


def _rms_norm(x, g):
    return x * lax.rsqrt(jnp.mean(x * x, -1, keepdims=True) + RMS_EPS) * g


def _seg_shift(x, offset, n_ctx):
    t = x.shape[1]
    if offset == 0:
        return x
    if offset < 0:
        y = jnp.pad(x, ((0, 0), (-offset, 0), (0, 0)))[:, :t]
    else:
        y = jnp.pad(x, ((0, 0), (0, offset), (0, 0)))[:, offset:]
    pos = np.arange(t)
    src = pos + offset
    same = (src >= 0) & (src < t) & ((pos < n_ctx) == (src < n_ctx))
    return y * jnp.asarray(same, x.dtype)[None, :, None]


def _heads(x):
    bsz, t, c = x.shape
    return jnp.transpose(x.reshape(bsz, t, c // HEAD_DIM, HEAD_DIM), (0, 2, 1, 3))


def _unheads(x):
    bsz, n_h, t, n = x.shape
    return jnp.transpose(x, (0, 2, 1, 3)).reshape(bsz, t, n_h * n)


def _axial_rope(x, n_ctx):
    bsz, t, n_h, dim = x.shape
    n_freq = dim // 4
    pos_t = jnp.arange(t - n_ctx)
    pos = jnp.stack([pos_t // GRID_W, pos_t % GRID_W], axis=-1).astype(F32)
    inv_freq = ROPE_BASE ** (-jnp.arange(n_freq, dtype=F32) / n_freq)
    ang = pos[:, :, None] * inv_freq
    ang = jnp.concatenate([jnp.zeros((n_ctx, 2, n_freq), F32), ang], axis=0)
    cos = jnp.cos(ang)[None, :, None]
    sin = jnp.sin(ang)[None, :, None]
    xr = x.reshape(bsz, t, n_h, 2, 2, n_freq)
    x1, x2 = xr[..., 0, :], xr[..., 1, :]
    out = jnp.stack([x1 * cos - x2 * sin, x2 * cos + x1 * sin], axis=-2)
    return out.reshape(x.shape)


def _rwkv_mixer(p, n_ctx, mu, w0, w_up, a0, a_up, g_up, k_k, k_a, r_k, gn_g, gn_b):
    bsz, t, _ = p.shape
    gd = GROUP_DIM
    p = p + mu[0] * (_seg_shift(p, -1, n_ctx) - p) + mu[1] * (_seg_shift(p, 1, n_ctx) - p)
    r, k, v = p[..., :gd], p[..., gd:2 * gd], p[..., 2 * gd:3 * gd]
    wd = p[..., 3 * gd:3 * gd + 2 * RW_LORA]
    ad = p[..., 3 * gd + 2 * RW_LORA:3 * gd + 4 * RW_LORA]
    g_in = p[..., 3 * gd + 4 * RW_LORA:]
    lora_in = jnp.concatenate([jnp.tanh(wd), ad, jax.nn.sigmoid(g_in)], axis=-1)
    n_lora = 4 * RW_LORA + RW_G_LORA
    lora_w = jnp.zeros((n_lora, 5 * gd), F32)
    for j, blk in enumerate((w_up[0], w_up[1], a_up[0], a_up[1])):
        lora_w = lora_w.at[j * RW_LORA:(j + 1) * RW_LORA, j * gd:(j + 1) * gd].set(blk)
    lora_w = lora_w.at[4 * RW_LORA:, 4 * gd:].set(g_up)
    lo = matmul(lora_in.reshape(bsz * t, n_lora).astype(BF16), lora_w.astype(BF16)).reshape(bsz, t, 5 * gd)
    g = lo[..., 4 * gd:]
    split = lambda m: m.reshape(bsz, t, N_HEADS, HEAD_DIM)
    kk = split(k * k_k)
    kk = (kk * lax.rsqrt(jnp.maximum(jnp.sum(kk * kk, -1, keepdims=True), 1e-12))).reshape(bsz, t, gd)
    lw = [-RW_DECAY_SCALE * jax.nn.sigmoid(w0[d] + lo[..., d * gd:(d + 1) * gd]) for d in range(2)]
    a = [jax.nn.sigmoid(a0[d] + lo[..., (2 + d) * gd:(3 + d) * gd]) for d in range(2)]
    k_dir = [k * (1 + (a[d] - 1) * k_a) for d in range(2)]
    y_fwd, y_bwd = rwkv_scan(r, v, kk, k_dir, a, lw, n_ctx=n_ctx)
    y = split(y_fwd + y_bwd)
    m = jnp.mean(y, -1, keepdims=True)
    var = jnp.mean(jnp.square(y - m), -1, keepdims=True)
    y = (y - m) * lax.rsqrt(var + RW_GN_EPS) * gn_g.reshape(N_HEADS, HEAD_DIM) + gn_b.reshape(N_HEADS, HEAD_DIM)
    y = y + jnp.sum(split(r) * split(k) * r_k, -1, keepdims=True) * split(v)
    return y.reshape(bsz, t, gd) * g


def _mamba_mixer(z, xbc, dt_raw, n_ctx, conv_w, conv_b, a_log, dt_bias, d_skip, norm_g):
    bsz, t, _ = z.shape
    conv = conv_b + sum(conv_w[j] * _seg_shift(xbc, j - MB_CONV // 2, n_ctx) for j in range(MB_CONV))
    xbc = jax.nn.silu(conv)
    gd = GROUP_DIM
    a_neg = -jnp.exp(a_log)
    y = (xbc[..., :gd].reshape(bsz, t, N_HEADS, HEAD_DIM) * d_skip[:, None]).reshape(bsz, t, gd)
    for d in range(2):
        dt = jax.nn.softplus(dt_raw[..., d * N_HEADS:(d + 1) * N_HEADS] + dt_bias[d])
        y = y + ssd_scan(xbc, dt, dt * a_neg[d], n_ctx=n_ctx, reverse=bool(d))
    y = y * jax.nn.silu(z)
    y = y.reshape(bsz, t, MB_GROUPS, gd // MB_GROUPS)
    y = y * lax.rsqrt(jnp.mean(y * y, -1, keepdims=True) + RMS_EPS)
    return y.reshape(bsz, t, gd) * norm_g


def _mla_mixer(q_lora, kv_lora, k_pe, n_ctx, q_norm, w_uq, kv_norm, w_ukv, tile):
    bsz, t, _ = q_lora.shape
    q = matmul(_rms_norm(q_lora, q_norm).reshape(bsz * t, -1).astype(BF16), w_uq.astype(BF16))
    q = q.reshape(bsz, t, N_HEADS, MLA_QK)
    kv = matmul(_rms_norm(kv_lora, kv_norm).reshape(bsz * t, -1).astype(BF16), w_ukv.astype(BF16))
    kv = kv.reshape(bsz, t, N_HEADS, MLA_NOPE + HEAD_DIM)
    zeros = jnp.zeros((bsz, t, N_HEADS, MLA_QK_PAD - MLA_QK), F32)
    q = jnp.concatenate([q[..., :MLA_NOPE], _axial_rope(q[..., MLA_NOPE:], n_ctx), zeros], axis=-1) * MLA_QK ** -0.5
    k_pe = jnp.broadcast_to(_axial_rope(k_pe[:, :, None, :], n_ctx), (bsz, t, N_HEADS, MLA_ROPE))
    k = jnp.concatenate([kv[..., :MLA_NOPE], k_pe, zeros], axis=-1)
    flat = lambda m: m.reshape(bsz, t, -1).astype(BF16)
    return attention(flat(q), flat(k), flat(kv[..., MLA_NOPE:]), n_ctx=n_ctx, tile=tile)


def _first_max(vals, excluded):
    live = [jnp.where(x, -jnp.inf, v) for v, x in zip(vals, excluded)]
    top = functools.reduce(jnp.maximum, live)
    found = jnp.zeros_like(top, dtype=jnp.bool_)
    first = []
    for v, x in zip(live, excluded):
        hit = (v == top) & ~found & ~x
        first.append(hit)
        found = found | hit
    return top, first


def _router_body(x_ref, sc_ref, sh_ref, rw_ref, rb_ref, h_ref, gate_ref):
    h = x_ref[...] * (1.0 + sc_ref[0]) + sh_ref[0]
    h_ref[...] = h.astype(BF16)
    logits = jnp.dot(h, rw_ref[...], preferred_element_type=F32, precision=HIGHEST)
    scores = jax.nn.sigmoid(logits.T[:N_EXPERTS])
    biased = scores + rb_ref[...]
    rows = [biased[e:e + 1] for e in range(N_EXPERTS)]
    never = jnp.zeros_like(rows[0], dtype=jnp.bool_)
    group_scores, picked = [], []
    for g in range(N_EXPERT_GROUPS):
        vals = rows[g * EXPERTS_PER_GROUP:(g + 1) * EXPERTS_PER_GROUP]
        top1, first = _first_max(vals, [never] * EXPERTS_PER_GROUP)
        top2, second = _first_max(vals, first)
        group_scores.append(top1 + top2)
        picked.append([a | b for a, b in zip(first, second)])
    _, group_sel = _first_max(group_scores, [never] * N_EXPERT_GROUPS)
    chosen = [jnp.where(group_sel[e // EXPERTS_PER_GROUP] & picked[e // EXPERTS_PER_GROUP][e % EXPERTS_PER_GROUP],
                        scores[e:e + 1], 0.0) for e in range(N_EXPERTS)]
    denom = functools.reduce(jnp.add, chosen)
    gate_ref[...] = jnp.concatenate([w / denom * ROUTED_SCALE for w in chosen], axis=0)


def router(xs, mod_l, router_w, router_b, *, tokens_per_batch, n_ctx, tile):
    m, d = xs.shape
    bsz = m // tokens_per_batch
    tiles_per_batch, n_ctx_tiles = tokens_per_batch // tile, n_ctx // tile

    def mod_row(i):
        return jnp.where(i % tiles_per_batch < n_ctx_tiles, bsz, i // tiles_per_batch)

    rw_pad = jnp.pad(router_w, ((0, 0), (0, LANE - N_EXPERTS)))
    return pl.pallas_call(
        _router_body,
        grid=(m // tile,),
        in_specs=[pl.BlockSpec((tile, d), lambda i: (i, 0)),
                  pl.BlockSpec((None, 1, d), lambda i: (mod_row(i), 0, 4)),
                  pl.BlockSpec((None, 1, d), lambda i: (mod_row(i), 0, 3)),
                  pl.BlockSpec((d, LANE), lambda i: (0, 0)),
                  pl.BlockSpec((N_EXPERTS, 1), lambda i: (0, 0))],
        out_specs=[pl.BlockSpec((tile, d), lambda i: (i, 0)),
                   pl.BlockSpec((N_EXPERTS, tile), lambda i: (0, i))],
        out_shape=[jax.ShapeDtypeStruct((m, d), BF16), jax.ShapeDtypeStruct((N_EXPERTS, m), F32)],
        compiler_params=_params("parallel"),
        name="router",
    )(xs, mod_l, mod_l, rw_pad, router_b.reshape(N_EXPERTS, 1))


def _pad_w_in(w):
    d = w.shape[0]
    dt_end = RW_IN + MB_IN
    pe_end = dt_end + MLA_IN
    return jnp.concatenate([w[:, :dt_end], jnp.zeros((d, LANE - 2 * N_HEADS), w.dtype),
                            w[:, dt_end:pe_end], jnp.zeros((d, LANE - MLA_ROPE), w.dtype),
                            w[:, pe_end:]], axis=1)


def kernel(x, c, ctx, c_ctx, ada_w, ada_b, w_in, w_out, ln1_g, ln1_b, ln2_g, ln2_b, rw_mu, rw_w0, rw_w_up, rw_a0, rw_a_up, rw_g_up, rw_k_k, rw_k_a, rw_r_k, rw_gn_g, rw_gn_b, mb_conv_w, mb_conv_b, mb_a_log, mb_dt_bias, mb_d, mb_norm_g, mla_q_norm, mla_w_uq, mla_kv_norm, mla_w_ukv, na_rpb, router_w, router_b, exp_w_gate, exp_w_up, exp_w_down):
    bsz, seq, d = x.shape
    n_ctx = ctx.shape[1]
    t = n_ctx + seq
    depth = ada_w.shape[0]
    tile = min(256, n_ctx)
    cond = jnp.concatenate([jax.nn.silu(c), jax.nn.silu(c_ctx)[None]], axis=0)
    mod = modulation(cond, ada_w, ada_b)
    xs = jnp.concatenate([ctx, x], axis=1)
    is_ctx = jnp.asarray(np.arange(t) < n_ctx)[None, :, None]

    def per_token(l, j):
        rows = mod[l, :, j * d:(j + 1) * d]
        return jnp.where(is_ctx, rows[bsz][None, None, :], rows[:bsz][:, None, :])

    m = bsz * t
    for l in range(depth):
        mod_l = mod[l][:, None, :]
        p = proj_in(xs.reshape(m, d), mod_l, _pad_w_in(w_in[l]).astype(BF16),
                    tokens_per_batch=t, n_ctx=n_ctx).reshape(bsz, t, N_IN_PAD)
        o_rw = _rwkv_mixer(p[..., OFF_RW:OFF_MB_Z], n_ctx, rw_mu[l], rw_w0[l], rw_w_up[l], rw_a0[l], rw_a_up[l],
                           rw_g_up[l], rw_k_k[l], rw_k_a[l], rw_r_k[l], rw_gn_g[l], rw_gn_b[l])
        o_mb = _mamba_mixer(p[..., OFF_MB_Z:OFF_MB_XBC], p[..., OFF_MB_XBC:OFF_MB_DT],
                            p[..., OFF_MB_DT:OFF_MB_DT + 2 * N_HEADS], n_ctx, mb_conv_w[l], mb_conv_b[l],
                            mb_a_log[l], mb_dt_bias[l], mb_d[l], mb_norm_g[l])
        o_mla = _mla_mixer(p[..., OFF_MLA_Q:OFF_MLA_KV], p[..., OFF_MLA_KV:OFF_MLA_PE],
                           p[..., OFF_MLA_PE:OFF_MLA_PE + MLA_ROPE], n_ctx, mla_q_norm[l], mla_w_uq[l],
                           mla_kv_norm[l], mla_w_ukv[l], tile)
        o_na = neighborhood_attention(p, na_bias_table(na_rpb[l]), n_ctx=n_ctx, col0=OFF_NA)
        xs2 = proj_out([o.reshape(m, GROUP_DIM) for o in (o_rw, o_mb, o_mla, o_na)], w_out[l].astype(BF16),
                       xs.reshape(m, d), mod_l, ln1_g[l], ln1_b[l], tokens_per_batch=t, n_ctx=n_ctx)
        h, gate = router(xs2, mod_l, router_w, router_b, tokens_per_batch=t, n_ctx=n_ctx, tile=tile)
        f = moe_experts(h, gate[:, :, None], exp_w_gate[l].astype(BF16), exp_w_up[l].astype(BF16),
                        exp_w_down[l].astype(BF16)).reshape(bsz, t, d)
        xs = _layer_norm(DEEPNORM_ALPHA * xs2.reshape(bsz, t, d) + per_token(l, 5) * f, ln2_g[l], ln2_b[l])
    return xs[:, n_ctx:]
```

```python
import functools
import math

import numpy as np
import jax
import jax.numpy as jnp
from jax import lax
from jax.experimental import pallas as pl
from jax.experimental.pallas import tpu as pltpu

F32 = jnp.float32
BF16 = jnp.bfloat16
HIGHEST = lax.Precision.HIGHEST

D_MODEL = 2048
DEPTH = 4
GRID_W = 64
GROUP_DIM = D_MODEL // 4
HEAD_DIM = 64
N_HEADS = GROUP_DIM // HEAD_DIM

RW_LORA = 64
RW_G_LORA = 128
RW_DECAY_SCALE = 0.606531
RW_GN_EPS = 64e-5
RW_IN = 3 * GROUP_DIM + 4 * RW_LORA + RW_G_LORA
RW_CHUNK = 64

MB_GROUPS = 2
MB_STATE = 128
MB_CONV = 5
MB_CHUNK = 128
MB_CONV_DIM = GROUP_DIM + 2 * MB_GROUPS * MB_STATE
MB_IN = GROUP_DIM + MB_CONV_DIM + 2 * N_HEADS

MLA_Q_LORA = 3 * D_MODEL // 16
MLA_KV_LORA = D_MODEL // 16
MLA_NOPE = 64
MLA_ROPE = 32
MLA_QK = MLA_NOPE + MLA_ROPE
MLA_QK_PAD = 128
MLA_IN = MLA_Q_LORA + MLA_KV_LORA + MLA_ROPE
ROPE_BASE = 10000.0

NA_WIN_ROWS = 8
NA_WIN_COLS = 16
NA_IN = 3 * GROUP_DIM
NA_MASK = -1e30

N_EXPERTS = 16
N_EXPERT_GROUPS = 4
EXPERTS_PER_GROUP = N_EXPERTS // N_EXPERT_GROUPS
TOP_K = 2
D_EXPERT = D_MODEL // 4
ROUTED_SCALE = 2.5

DEEPNORM_ALPHA = (2 * DEPTH) ** 0.25
LN_EPS = 1e-6
RMS_EPS = 1e-6

LANE = 128
OFF_RW = 0
OFF_MB_Z = OFF_RW + RW_IN
OFF_MB_XBC = OFF_MB_Z + GROUP_DIM
OFF_MB_DT = OFF_MB_XBC + MB_CONV_DIM
OFF_MLA_Q = OFF_MB_DT + LANE
OFF_MLA_KV = OFF_MLA_Q + MLA_Q_LORA
OFF_MLA_PE = OFF_MLA_KV + MLA_KV_LORA
OFF_NA = OFF_MLA_PE + LANE
N_IN_PAD = OFF_NA + NA_IN

VMEM_LIMIT = 56 * 1024 * 1024


def _params(*sem):
    return pltpu.CompilerParams(dimension_semantics=sem, vmem_limit_bytes=VMEM_LIMIT)


def _pick_tile(n, target, quantum):
    best = None
    for t in range(quantum, min(n, target) + 1, quantum):
        if n % t == 0:
            best = t
    assert best is not None, (n, target, quantum)
    return best


def _mm_body(x_ref, w_ref, o_ref, *, precision):
    o_ref[...] = jnp.dot(x_ref[...], w_ref[...], preferred_element_type=F32,
                         precision=precision).astype(o_ref.dtype)


def matmul(x, w, *, tm_target=512, tn_target=1024, precision=None, out_dtype=F32):
    m, k = x.shape
    n = w.shape[1]
    tm = _pick_tile(m, tm_target, 16)
    tn = _pick_tile(n, tn_target, LANE)
    return pl.pallas_call(
        functools.partial(_mm_body, precision=precision),
        grid=(m // tm, n // tn),
        in_specs=[pl.BlockSpec((tm, k), lambda i, j: (i, 0)),
                  pl.BlockSpec((k, tn), lambda i, j: (0, j))],
        out_specs=pl.BlockSpec((tm, tn), lambda i, j: (i, j)),
        out_shape=jax.ShapeDtypeStruct((m, n), out_dtype),
        compiler_params=_params("parallel", "parallel"),
        name="matmul",
    )(x, w)


def _is_ctx_row(tm, tokens_per_batch, n_ctx):
    pos = (pl.program_id(0) * tm) % tokens_per_batch + lax.broadcasted_iota(jnp.int32, (tm, 1), 0)
    return pos < n_ctx


def _proj_in_body(x_ref, scc_ref, scb_ref, shc_ref, shb_ref, w_ref, o_ref, xm_ref, *, tokens_per_batch, n_ctx):
    is_ctx = _is_ctx_row(x_ref.shape[0], tokens_per_batch, n_ctx)

    @pl.when(pl.program_id(1) == 0)
    def _():
        sc = jnp.where(is_ctx, scc_ref[0], scb_ref[0])
        sh = jnp.where(is_ctx, shc_ref[0], shb_ref[0])
        xm_ref[...] = (x_ref[...] * (1.0 + sc) + sh).astype(BF16)

    o_ref[...] = jnp.dot(xm_ref[...], w_ref[...], preferred_element_type=F32)


def _mod_specs(bsz, tiles_per_batch, chunk, d):
    return [pl.BlockSpec((None, 1, d), lambda i, *_: (bsz, 0, chunk)),
            pl.BlockSpec((None, 1, d), lambda i, *_: (i // tiles_per_batch, 0, chunk))]


def proj_in(xs, mod_l, w, *, tokens_per_batch, n_ctx):
    m, d = xs.shape
    n = w.shape[1]
    bsz = m // tokens_per_batch
    tm = _pick_tile(tokens_per_batch, 640, 16)
    tn = _pick_tile(n, 1152, LANE)
    tiles_per_batch = tokens_per_batch // tm
    return pl.pallas_call(
        functools.partial(_proj_in_body, tokens_per_batch=tokens_per_batch, n_ctx=n_ctx),
        grid=(m // tm, n // tn),
        in_specs=[pl.BlockSpec((tm, d), lambda i, j: (i, 0)),
                  *_mod_specs(bsz, tiles_per_batch, 1, d), *_mod_specs(bsz, tiles_per_batch, 0, d),
                  pl.BlockSpec((d, tn), lambda i, j: (0, j))],
        out_specs=pl.BlockSpec((tm, tn), lambda i, j: (i, j)),
        out_shape=jax.ShapeDtypeStruct((m, n), F32),
        scratch_shapes=[pltpu.VMEM((tm, d), BF16)],
        compiler_params=_params("parallel", "arbitrary"),
        name="proj_in",
    )(xs, mod_l, mod_l, mod_l, mod_l, w)


def _proj_out_body(m0_ref, m1_ref, m2_ref, m3_ref, w_ref, x_ref, gc_ref, gb_ref, lg_ref, lb_ref, o_ref,
                   *, tokens_per_batch, n_ctx):
    gd = m0_ref.shape[1]
    y = None
    for j, m_ref in enumerate((m0_ref, m1_ref, m2_ref, m3_ref)):
        part = jnp.dot(m_ref[...].astype(BF16), w_ref[j * gd:(j + 1) * gd, :], preferred_element_type=F32)
        y = part if y is None else y + part
    gate = jnp.where(_is_ctx_row(x_ref.shape[0], tokens_per_batch, n_ctx), gc_ref[0], gb_ref[0])
    z = DEEPNORM_ALPHA * x_ref[...] + gate * y
    mu = jnp.mean(z, axis=-1, keepdims=True)
    zc = z - mu
    var = jnp.mean(zc * zc, axis=-1, keepdims=True)
    o_ref[...] = zc * lax.rsqrt(var + LN_EPS) * lg_ref[...] + lb_ref[...]


def proj_out(mixes, w, xs, mod_l, ln_g, ln_b, *, tokens_per_batch, n_ctx):
    m, d = xs.shape
    gd = mixes[0].shape[1]
    bsz = m // tokens_per_batch
    tm = _pick_tile(tokens_per_batch, 320, 16)
    tiles_per_batch = tokens_per_batch // tm
    row = lambda i: (i, 0)
    fixed = lambda i: (0, 0)
    return pl.pallas_call(
        functools.partial(_proj_out_body, tokens_per_batch=tokens_per_batch, n_ctx=n_ctx),
        grid=(m // tm,),
        in_specs=[pl.BlockSpec((tm, gd), row)] * 4
        + [pl.BlockSpec((d, d), fixed), pl.BlockSpec((tm, d), row), *_mod_specs(bsz, tiles_per_batch, 2, d),
           pl.BlockSpec((1, d), fixed), pl.BlockSpec((1, d), fixed)],
        out_specs=pl.BlockSpec((tm, d), row),
        out_shape=jax.ShapeDtypeStruct((m, d), F32),
        compiler_params=_params("parallel"),
        name="proj_out",
    )(*mixes, w, xs, mod_l, mod_l, ln_g.reshape(1, d), ln_b.reshape(1, d))


def _mod_body(c_ref, w_ref, b_ref, o_ref, *, n_rows):
    w = w_ref[...]
    reps = w.shape[1] // LANE
    o_ref[...] = jnp.zeros_like(o_ref)
    for m in range(n_rows):
        cb = c_ref[m]
        cbt = jnp.concatenate([cb] * reps, axis=1)
        o_ref[m:m + 1, :] = jnp.sum(w * cbt, axis=0, keepdims=True) + b_ref[...]


def modulation(cond, ada_w, ada_b):
    n_rows, k = cond.shape
    depth, _, n = ada_w.shape
    tn = 512
    cond_b = jnp.broadcast_to(cond[:, :, None], (n_rows, k, LANE))
    return pl.pallas_call(
        functools.partial(_mod_body, n_rows=n_rows),
        grid=(depth, n // tn),
        in_specs=[pl.BlockSpec((n_rows, k, LANE), lambda l, j: (0, 0, 0)),
                  pl.BlockSpec((None, k, tn), lambda l, j: (l, 0, j)),
                  pl.BlockSpec((None, 1, tn), lambda l, j: (l, 0, j))],
        out_specs=pl.BlockSpec((None, 8, tn), lambda l, j: (l, 0, j)),
        out_shape=jax.ShapeDtypeStruct((depth, 8, n), F32),
        compiler_params=_params("parallel", "parallel"),
        name="modulation",
    )(cond_b, ada_w, ada_b.reshape(depth, 1, n))


def _bmm(a, b, spec, mode):
    if mode == 'f32':
        return jnp.einsum(spec, a, b, preferred_element_type=F32, precision=HIGHEST)
    a_hi, b_hi = a.astype(BF16), b.astype(BF16)
    out = jnp.einsum(spec, a_hi, b_hi, preferred_element_type=F32)
    if mode == 'x3':
        a_lo = (a - a_hi.astype(F32)).astype(BF16)
        b_lo = (b - b_hi.astype(F32)).astype(BF16)
        out = out + (jnp.einsum(spec, a_hi, b_lo, preferred_element_type=F32)
                     + jnp.einsum(spec, a_lo, b_hi, preferred_element_type=F32))
    return out


def _unit_triangular_inverse(l_mat, ti, si):
    c = l_mat.shape[1]
    nn = 'hts,hsn->htn'
    blk = 8
    eye = (ti == si).astype(F32)
    l_d = jnp.where((ti // blk) == (si // blk), l_mat, 0.0)
    p2 = _bmm(l_d, l_d, nn, 'bf16')
    p4 = _bmm(p2, p2, nn, 'bf16')
    inv = _bmm(eye - l_d, eye + p2, nn, 'bf16')
    inv = _bmm(inv, eye + p4, nn, 'bf16')
    while blk < c:
        pair = ((ti // (2 * blk)) == (si // (2 * blk))) & ((ti // blk) != (si // blk))
        off = jnp.where(pair, l_mat, 0.0)
        inv = inv - _bmm(_bmm(inv, off, nn, 'bf16'), inv, nn, 'bf16')
        blk *= 2
    return inv


def _first_head_lanes(shape):
    return lax.broadcasted_iota(jnp.int32, shape, len(shape) - 1) % LANE < HEAD_DIM


def _rwkv_body(*refs):
    fwd_in, bwd_in, (yf_ref, yb_ref, st_ref) = refs[:6], refs[6:12], refs[12:]

    @pl.when(pl.program_id(0) == 0)
    def _():
        st_ref[...] = jnp.zeros_like(st_ref)

    bsz, c, width = fwd_in[0].shape
    n_pairs = width // LANE
    ti = lax.broadcasted_iota(jnp.int32, (c, c), 0)
    si = lax.broadcasted_iota(jnp.int32, (c, c), 1)
    units = []
    for in_refs, reverse in ((fwd_in, False), (bwd_in, True)):
        tri = ((si >= ti) if reverse else (si <= ti)).astype(F32)
        for row in range(bsz):
            r, k, v, kk, a, lw = (ref[row] for ref in in_refs)
            cum = _bmm(tri, lw, 'ts,sn->tn', 'f32')
            tot = cum[:1] if reverse else cum[c - 1:]
            b = kk * a
            e_neg = jnp.exp(-cum)
            e_rem = jnp.exp(tot - cum)
            pieces = (r * jnp.exp(cum), kk * jnp.exp(cum - lw), k * e_neg, b * e_neg, k * e_rem, b * e_rem, v,
                      jnp.exp(tot))
            for j in range(n_pairs):
                units.append([m[:, j * LANE:(j + 1) * LANE] for m in pieces])
    n_units = len(units)
    rq, kq, kd, bd, kdc, bdc, v, e_tot = (jnp.stack([u[i] for u in units], axis=0) for i in range(8))

    backward = lax.broadcasted_iota(jnp.int32, (n_units, 1, 1), 0) >= n_units // 2
    order = (si - ti) * jnp.where(backward, -1, 1)
    incl = order <= 0
    strict = order < 0
    first = _first_head_lanes((c, LANE))
    first2 = _first_head_lanes((c, 2 * LANE))
    nt, nn, tn = 'utn,usn->uts', 'uts,usn->utn', 'usn,usm->unm'
    qq = jnp.concatenate([jnp.where(first, rq, 0.0), jnp.where(first, kq, 0.0),
                          jnp.where(first, 0.0, rq), jnp.where(first, 0.0, kq)], axis=1)
    ak = _bmm(qq, kd, nt, 'bf16')
    ab = _bmm(qq, bd, nt, 'bf16')
    a_rk = [jnp.where(incl, ak[:, 2 * x * c:(2 * x + 1) * c], 0.0) for x in range(2)]
    a_kk = [jnp.where(strict, ak[:, (2 * x + 1) * c:(2 * x + 2) * c], 0.0) for x in range(2)]
    a_rb = [jnp.where(incl, ab[:, 2 * x * c:(2 * x + 1) * c], 0.0) for x in range(2)]
    a_kb = [jnp.where(strict, ab[:, (2 * x + 1) * c:(2 * x + 2) * c], 0.0) for x in range(2)]
    inv = _unit_triangular_inverse(jnp.concatenate(a_kb, axis=0), ti, si)
    inv = [inv[:n_units], inv[n_units:]]

    xs = [_bmm(inv[x], jnp.concatenate([_bmm(a_kk[x], v, nn, 'bf16'), kq], axis=2), nn, 'bf16') for x in range(2)]
    x2 = jnp.where(first2, xs[0], xs[1])
    arx = jnp.where(first2, _bmm(a_rb[0], x2, nn, 'bf16'), _bmm(a_rb[1], x2, nn, 'bf16'))
    y0 = jnp.where(first, _bmm(a_rk[0], v, nn, 'bf16'), _bmm(a_rk[1], v, nn, 'bf16')) - arx[:, :, :LANE]
    rqp = rq - arx[:, :, LANE:]
    row_n = lax.broadcasted_iota(jnp.int32, (LANE, LANE), 0)
    col_n = lax.broadcasted_iota(jnp.int32, (LANE, LANE), 1)
    same_head = (row_n // HEAD_DIM) == (col_n // HEAD_DIM)
    bx = _bmm(bdc, x2, tn, 'bf16')
    tadd = jnp.where(same_head, _bmm(kdc, v, tn, 'bf16') - bx[:, :, :LANE], 0.0)
    decay = jnp.where(row_n == col_n, jnp.broadcast_to(e_tot, (n_units, LANE, LANE)), 0.0)
    p = jnp.where(same_head, decay - bx[:, :, LANE:], 0.0)
    t0 = st_ref[...]
    y = _bmm(rqp, t0, 'utn,unm->utm', 'x3') + y0
    st_ref[...] = _bmm(p, t0, 'ujn,unm->ujm', 'x3') + tadd
    for d, y_ref in enumerate((yf_ref, yb_ref)):
        for row in range(bsz):
            u0 = (d * bsz + row) * n_pairs
            y_ref[row] = jnp.concatenate([y[u0 + j] for j in range(n_pairs)], axis=1)


def _scan_chunk_index(c, n_ctx_chunks, n_chunks, reverse):
    if not reverse:
        return c
    return jnp.where(c < n_ctx_chunks, n_ctx_chunks - 1 - c, n_chunks - 1 - (c - n_ctx_chunks))


def rwkv_scan(r, v, kk, k_dirs, a_dirs, lw_dirs, *, n_ctx):
    bsz, t, width = r.shape
    c = RW_CHUNK
    n_chunks, n_ctx_chunks = t // c, n_ctx // c
    specs = [pl.BlockSpec((bsz, c, width),
                          lambda i, rev=rev: (0, _scan_chunk_index(i, n_ctx_chunks, n_chunks, rev), 0))
             for rev in (False, True)]
    n_units = 2 * bsz * (width // LANE)
    args = [(r, k_dirs[d], v, kk, a_dirs[d], lw_dirs[d]) for d in range(2)]
    return pl.pallas_call(
        _rwkv_body,
        grid=(n_chunks,),
        in_specs=[specs[0]] * 6 + [specs[1]] * 6,
        out_specs=specs,
        out_shape=[jax.ShapeDtypeStruct((bsz, t, width), F32)] * 2,
        scratch_shapes=[pltpu.VMEM((n_units, LANE, LANE), F32)],
        compiler_params=_params("arbitrary"),
        name="rwkv_scan",
    )(*args[0], *args[1])


def _ssd_body(x_ref, dt_ref, dac_ref, dar_ref, y_ref, st_ref, *, reverse):
    @pl.when(pl.program_id(1) == 0)
    def _():
        st_ref[...] = jnp.zeros_like(st_ref)

    xbc = x_ref[0]
    dt = dt_ref[0]
    c = xbc.shape[0]
    n_h = dt.shape[1]
    n_pairs = n_h // 2
    gd = n_h * HEAD_DIM
    n_g = (xbc.shape[1] - gd) // (2 * MB_STATE)
    pairs_per_group = n_pairs // n_g
    ti = lax.broadcasted_iota(jnp.int32, (c, c), 0)
    si = lax.broadcasted_iota(jnp.int32, (c, c), 1)
    incl = (si >= ti) if reverse else (si <= ti)
    incl_t = (ti >= si) if reverse else (ti <= si)
    cs_c = jnp.dot(incl.astype(F32), dac_ref[0], preferred_element_type=F32, precision=HIGHEST)
    cs_r = jnp.dot(dar_ref[0], incl_t.astype(F32), preferred_element_type=F32, precision=HIGHEST)
    tot = cs_c[:1, :] if reverse else cs_c[c - 1:, :]
    e_cs = jnp.exp(cs_c)
    e_rem = jnp.exp(tot - cs_c)
    e_tot = jnp.exp(tot)
    bm = [xbc[:, gd + g * MB_STATE:gd + (g + 1) * MB_STATE].astype(BF16) for g in range(n_g)]
    cm = [xbc[:, gd + (n_g + g) * MB_STATE:gd + (n_g + g + 1) * MB_STATE].astype(BF16) for g in range(n_g)]
    gmat = [jnp.einsum('ln,sn->ls', cm[g], bm[g], preferred_element_type=F32) for g in range(n_g)]
    first = _first_head_lanes((c, LANE))
    first_rows = lax.broadcasted_iota(jnp.int32, (LANE, 1), 0) < HEAD_DIM
    pick = lambda m, ha: jnp.where(first, m[:, ha:ha + 1], m[:, ha + 1:ha + 2])
    for j in range(n_pairs):
        g = j // pairs_per_group
        ha = 2 * j
        sl = slice(j * LANE, (j + 1) * LANE)
        xdt = xbc[:, sl] * pick(dt, ha)
        xdt_b = xdt.astype(BF16)
        y_heads = []
        for h in (ha, ha + 1):
            seg = cs_c[:, h:h + 1] - cs_r[h:h + 1, :]
            lmat = jnp.exp(jnp.where(incl, seg, NA_MASK))
            y_heads.append(jnp.dot((gmat[g] * lmat).astype(BF16), xdt_b, preferred_element_type=F32))
        st = st_ref[j]
        y_off = jnp.einsum('ln,pn->lp', cm[g], st.astype(BF16), preferred_element_type=F32) * pick(e_cs, ha)
        y_ref[0, :, sl] = jnp.where(first, y_heads[0], y_heads[1]) + y_off
        xdec = (xdt * pick(e_rem, ha)).astype(BF16)
        keep = jnp.where(first_rows, e_tot[:, ha:ha + 1], e_tot[:, ha + 1:ha + 2])
        st_ref[j] = keep * st + jnp.einsum('lp,ln->pn', xdec, bm[g], preferred_element_type=F32)


def ssd_scan(xbc, dt, da, *, n_ctx, reverse):
    bsz, t, width = xbc.shape
    n_h = dt.shape[2]
    gd = n_h * HEAD_DIM
    c = MB_CHUNK
    n_chunks, n_ctx_chunks = t // c, n_ctx // c
    idx = lambda i: _scan_chunk_index(i, n_ctx_chunks, n_chunks, reverse)
    da_row = jnp.swapaxes(da, 1, 2)
    return pl.pallas_call(
        functools.partial(_ssd_body, reverse=reverse),
        grid=(bsz, n_chunks),
        in_specs=[pl.BlockSpec((1, c, width), lambda b, i: (b, idx(i), 0)),
                  pl.BlockSpec((1, c, n_h), lambda b, i: (b, idx(i), 0)),
                  pl.BlockSpec((1, c, n_h), lambda b, i: (b, idx(i), 0)),
                  pl.BlockSpec((1, n_h, c), lambda b, i: (b, 0, idx(i)))],
        out_specs=pl.BlockSpec((1, c, gd), lambda b, i: (b, idx(i), 0)),
        out_shape=jax.ShapeDtypeStruct((bsz, t, gd), F32),
        scratch_shapes=[pltpu.VMEM((n_h // 2, 2 * HEAD_DIM, MB_STATE), F32)],
        compiler_params=_params("parallel", "arbitrary"),
        name="ssd_scan_bwd" if reverse else "ssd_scan_fwd",
    )(xbc, dt, da, da_row)


def _softmax_pv(s_parts, v_parts):
    m = functools.reduce(jnp.maximum, [jnp.max(s, axis=-1, keepdims=True) for s in s_parts])
    ps = [jnp.exp(s - m) for s in s_parts]
    l = functools.reduce(jnp.add, [jnp.sum(p, axis=-1, keepdims=True) for p in ps])
    o = functools.reduce(jnp.add, [jnp.dot(p.astype(BF16), v, preferred_element_type=F32)
                                   for p, v in zip(ps, v_parts)])
    return o / l


def _attn_body(q_ref, k_ref, v_ref, o_ref, *, n_ctx, n_ctx_tiles):
    dq = q_ref.shape[2] // 2

    def attend(n_keys):
        outs = []
        for x in range(2):
            s = jnp.einsum('qd,kd->qk', q_ref[0, :, x * dq:(x + 1) * dq], k_ref[0, 0:n_keys, x * dq:(x + 1) * dq],
                           preferred_element_type=F32)
            outs.append(_softmax_pv([s], [v_ref[0, 0:n_keys]]))
        o_ref[0] = jnp.where(_first_head_lanes(outs[0].shape), outs[0], outs[1])

    @pl.when(pl.program_id(2) < n_ctx_tiles)
    def _():
        attend(n_ctx)

    @pl.when(pl.program_id(2) >= n_ctx_tiles)
    def _():
        attend(k_ref.shape[1])


def attention(q, k, v, *, n_ctx, tile):
    bsz, t, width = v.shape
    n_pairs = width // LANE
    dq2 = q.shape[2] // n_pairs
    return pl.pallas_call(
        functools.partial(_attn_body, n_ctx=n_ctx, n_ctx_tiles=n_ctx // tile),
        grid=(bsz, n_pairs, t // tile),
        in_specs=[pl.BlockSpec((1, tile, dq2), lambda b, j, i: (b, i, j)),
                  pl.BlockSpec((1, t, dq2), lambda b, j, i: (b, 0, j)),
                  pl.BlockSpec((1, t, LANE), lambda b, j, i: (b, 0, j))],
        out_specs=pl.BlockSpec((1, tile, LANE), lambda b, j, i: (b, i, j)),
        out_shape=jax.ShapeDtypeStruct((bsz, t, width), F32),
        compiler_params=_params("parallel", "parallel", "arbitrary"),
        name="attention",
    )(q, k, v)


def _na_body(q_ref, k_ref, v_ref, bias_ref, o_ref, kb_ref, vb_ref, *, n_ctx, n_rows):
    i = pl.program_id(2)
    rows_per_step = q_ref.shape[1] // GRID_W
    win = NA_WIN_ROWS * GRID_W
    q2 = q_ref[0] * (HEAD_DIM ** -0.5)

    def head_queries(q_rows, x):
        first = _first_head_lanes(q_rows.shape)
        return jnp.where(first if x == 0 else ~first, q_rows, 0.0).astype(BF16)

    def scores(q, k):
        return jnp.einsum('qd,kd->qk', q, k, preferred_element_type=F32)

    @pl.when(i == 0)
    def _():
        kb_ref[...] = k_ref[0].astype(BF16)
        vb_ref[...] = v_ref[0].astype(BF16)
        kc, vc = kb_ref[0:n_ctx], vb_ref[0:n_ctx]
        outs = [_softmax_pv([scores(head_queries(q2, x), kc)], [vc]) for x in range(2)]
        o_ref[0] = jnp.where(_first_head_lanes(outs[0].shape), outs[0], outs[1])

    @pl.when(i > 0)
    def _():
        kc, vc = kb_ref[0:n_ctx], vb_ref[0:n_ctx]
        for rr in range(rows_per_step):
            r = (i - 1) * rows_per_step + rr
            rs = jnp.clip(r - NA_WIN_ROWS // 2, 0, n_rows - NA_WIN_ROWS)
            start = pl.multiple_of(n_ctx + rs * GRID_W, GRID_W)
            kw = kb_ref[pl.ds(start, win)]
            vw = vb_ref[pl.ds(start, win)]
            q_row = q2[rr * GRID_W:(rr + 1) * GRID_W]
            outs = []
            for x in range(2):
                qx = head_queries(q_row, x)
                outs.append(_softmax_pv([scores(qx, kw) + bias_ref[x, r - rs], scores(qx, kc)], [vw, vc]))
            o_ref[0, rr * GRID_W:(rr + 1) * GRID_W] = jnp.where(_first_head_lanes(outs[0].shape), outs[0], outs[1])


def na_bias_table(rpb):
    wr = NA_WIN_ROWS
    qv = np.arange(GRID_W)[:, None]
    cv = np.arange(GRID_W)[None, :]
    col_start = np.clip(qv - NA_WIN_COLS // 2, 0, GRID_W - NA_WIN_COLS)
    valid = (cv >= col_start) & (cv < col_start + NA_WIN_COLS)
    offs = np.arange(2 * NA_WIN_COLS - 1)[:, None, None]
    pick = ((cv - qv + (NA_WIN_COLS - 1))[None] == offs) & valid[None]
    cols = jnp.einsum('hro,oqc->hrqc', rpb, jnp.asarray(pick, F32), precision=HIGHEST)
    cols = jnp.where(valid, cols, NA_MASK)
    tab = jnp.stack([cols[:, wr - 1 - var:2 * wr - 1 - var] for var in range(wr)], axis=1)
    tab = jnp.transpose(tab, (0, 1, 3, 2, 4))
    return tab.reshape(rpb.shape[0], wr, GRID_W, wr * GRID_W)


def neighborhood_attention(p, bias, *, n_ctx, col0):
    bsz, t, _ = p.shape
    n_rows = (t - n_ctx) // GRID_W
    n_pairs = GROUP_DIM // LANE
    c0 = col0 // LANE
    return pl.pallas_call(
        functools.partial(_na_body, n_ctx=n_ctx, n_rows=n_rows),
        grid=(bsz, n_pairs, t // n_ctx),
        in_specs=[pl.BlockSpec((1, n_ctx, LANE), lambda b, j, i: (b, i, c0 + j)),
                  pl.BlockSpec((1, t, LANE), lambda b, j, i: (b, 0, c0 + n_pairs + j)),
                  pl.BlockSpec((1, t, LANE), lambda b, j, i: (b, 0, c0 + 2 * n_pairs + j)),
                  pl.BlockSpec((2, NA_WIN_ROWS, GRID_W, NA_WIN_ROWS * GRID_W), lambda b, j, i: (j, 0, 0, 0))],
        out_specs=pl.BlockSpec((1, n_ctx, LANE), lambda b, j, i: (b, i, j)),
        out_shape=jax.ShapeDtypeStruct((bsz, t, GROUP_DIM), F32),
        scratch_shapes=[pltpu.VMEM((t, LANE), BF16), pltpu.VMEM((t, LANE), BF16)],
        compiler_params=_params("parallel", "parallel", "arbitrary"),
        name="neighborhood_attention",
    )(p, p, p, bias)


def _moe_body(h_ref, gate_ref, wg_ref, wu_ref, wd_ref, x_ref, gc_ref, gb_ref, lg_ref, lb_ref, o_ref, acc_ref,
              *, tokens_per_batch, n_ctx):
    e = pl.program_id(1)
    is_ctx = _is_ctx_row(x_ref.shape[0], tokens_per_batch, n_ctx)

    @pl.when(e == 0)
    def _():
        acc_ref[...] = jnp.zeros_like(acc_ref)

    h = h_ref[...]
    hid = jax.nn.silu(jnp.dot(h, wg_ref[0], preferred_element_type=F32)) * jnp.dot(h, wu_ref[0], preferred_element_type=F32)
    acc_ref[...] += gate_ref[0] * jnp.dot(hid.astype(BF16), wd_ref[0], preferred_element_type=F32)

    @pl.when(e == pl.num_programs(1) - 1)
    def _():
        z = DEEPNORM_ALPHA * x_ref[...] + jnp.where(is_ctx, gc_ref[0], gb_ref[0]) * acc_ref[...]
        mu = jnp.mean(z, axis=-1, keepdims=True)
        zc = z - mu
        var = jnp.mean(zc * zc, axis=-1, keepdims=True)
        o_ref[...] = zc * lax.rsqrt(var + LN_EPS) * lg_ref[...] + lb_ref[...]


def moe_experts(h, gate, w_gate, w_up, w_down, layer, xs, mod_l, ln_g, ln_b, *, tokens_per_batch, n_ctx):
    m, d = h.shape
    n_e, d_e = w_gate.shape[1], w_gate.shape[3]
    bsz = m // tokens_per_batch
    tm = _pick_tile(tokens_per_batch, 640, 16)
    tiles_per_batch = tokens_per_batch // tm
    row = lambda i, e: (i, 0)
    fixed = lambda i, e: (0, 0)
    return pl.pallas_call(
        functools.partial(_moe_body, tokens_per_batch=tokens_per_batch, n_ctx=n_ctx),
        grid=(m // tm, n_e),
        in_specs=[pl.BlockSpec((tm, d), row),
                  pl.BlockSpec((1, tm, 1), lambda i, e: (e, i, 0)),
                  pl.BlockSpec((None, 1, d, d_e), lambda i, e: (layer, e, 0, 0)),
                  pl.BlockSpec((None, 1, d, d_e), lambda i, e: (layer, e, 0, 0)),
                  pl.BlockSpec((None, 1, d_e, d), lambda i, e: (layer, e, 0, 0)),
                  pl.BlockSpec((tm, d), row), *_mod_specs(bsz, tiles_per_batch, 5, d),
                  pl.BlockSpec((1, d), fixed), pl.BlockSpec((1, d), fixed)],
        out_specs=pl.BlockSpec((tm, d), row),
        out_shape=jax.ShapeDtypeStruct((m, d), F32),
        scratch_shapes=[pltpu.VMEM((tm, d), F32)],
        compiler_params=_params("parallel", "arbitrary"),
        name="moe_experts",
    )(h, gate, w_gate, w_up, w_down, xs, mod_l, mod_l, ln_g.reshape(1, d), ln_b.reshape(1, d))


def _layer_norm(x, g, b):
    mu = jnp.mean(x, -1, keepdims=True)
    var = jnp.mean(jnp.square(x - mu), -1, keepdims=True)
    return (x - mu) * lax.rsqrt(var + LN_EPS) * g + b


def _rms_norm(x, g):
    return x * lax.rsqrt(jnp.mean(x * x, -1, keepdims=True) + RMS_EPS) * g


def _seg_shift(x, offset, n_ctx):
    t = x.shape[1]
    if offset == 0:
        return x
    if offset < 0:
        y = jnp.pad(x, ((0, 0), (-offset, 0), (0, 0)))[:, :t]
    else:
        y = jnp.pad(x, ((0, 0), (0, offset), (0, 0)))[:, offset:]
    pos = np.arange(t)
    src = pos + offset
    same = (src >= 0) & (src < t) & ((pos < n_ctx) == (src < n_ctx))
    return y * jnp.asarray(same, x.dtype)[None, :, None]


def _heads(x):
    bsz, t, c = x.shape
    return jnp.transpose(x.reshape(bsz, t, c // HEAD_DIM, HEAD_DIM), (0, 2, 1, 3))


def _unheads(x):
    bsz, n_h, t, n = x.shape
    return jnp.transpose(x, (0, 2, 1, 3)).reshape(bsz, t, n_h * n)


def _rope_tables(t, n_ctx):
    n_freq = MLA_ROPE // 4
    pos_t = np.arange(t - n_ctx)
    pos = np.stack([pos_t // GRID_W, pos_t % GRID_W], axis=-1).astype(np.float32)
    inv_freq = jnp.asarray(ROPE_BASE, F32) ** (-jnp.arange(n_freq, dtype=F32) / n_freq)
    ang = jnp.asarray(pos)[:, :, None] * inv_freq
    ang = jnp.concatenate([jnp.zeros((n_ctx, 2, n_freq), F32), ang], axis=0)
    cos = jnp.broadcast_to(jnp.cos(ang)[:, :, None, :], (t, 2, 2, n_freq)).reshape(t, MLA_ROPE)
    sin = jnp.sin(ang)[:, :, None, :] * jnp.asarray([-1.0, 1.0], F32)[None, None, :, None]
    sin = sin.reshape(t, MLA_ROPE)
    pad = MLA_QK_PAD - MLA_QK
    cos_tab = jnp.concatenate([jnp.ones((t, MLA_NOPE), F32), cos, jnp.zeros((t, pad), F32)], axis=1)
    sin_tab = jnp.concatenate([jnp.zeros((t, MLA_NOPE), F32), sin, jnp.zeros((t, pad), F32)], axis=1)
    return cos_tab, sin_tab


def _rope_heads(x, tables, scale):
    cos_tab, sin_tab = (jnp.tile(tab, (1, x.shape[2] // MLA_QK_PAD)) for tab in tables)
    lane = np.arange(x.shape[2]) % MLA_QK_PAD
    first_half = jnp.asarray((lane - MLA_NOPE) % (MLA_ROPE // 2) < MLA_ROPE // 4)
    partner = jnp.where(first_half, jnp.roll(x, -(MLA_ROPE // 4), axis=2), jnp.roll(x, MLA_ROPE // 4, axis=2))
    return ((x * cos_tab + partner * sin_tab) * scale).astype(BF16)


def _rwkv_mixer(p, n_ctx, mu, w0, w_up, a0, a_up, g_up, k_k, k_a, r_k, gn_g, gn_b):
    bsz, t, _ = p.shape
    gd = GROUP_DIM
    p = p + mu[0] * (_seg_shift(p, -1, n_ctx) - p) + mu[1] * (_seg_shift(p, 1, n_ctx) - p)
    r, k, v = p[..., :gd], p[..., gd:2 * gd], p[..., 2 * gd:3 * gd]
    wd = p[..., 3 * gd:3 * gd + 2 * RW_LORA]
    ad = p[..., 3 * gd + 2 * RW_LORA:3 * gd + 4 * RW_LORA]
    g_in = p[..., 3 * gd + 4 * RW_LORA:]
    lora_in = jnp.concatenate([jnp.tanh(wd), ad, jax.nn.sigmoid(g_in)], axis=-1)
    n_lora = 4 * RW_LORA + RW_G_LORA
    lora_w = jnp.zeros((n_lora, 5 * gd), F32)
    for j, blk in enumerate((w_up[0], w_up[1], a_up[0], a_up[1])):
        lora_w = lora_w.at[j * RW_LORA:(j + 1) * RW_LORA, j * gd:(j + 1) * gd].set(blk)
    lora_w = lora_w.at[4 * RW_LORA:, 4 * gd:].set(g_up)
    lo = matmul(lora_in.reshape(bsz * t, n_lora).astype(BF16), lora_w.astype(BF16)).reshape(bsz, t, 5 * gd)
    g = lo[..., 4 * gd:]
    split = lambda m: m.reshape(bsz, t, N_HEADS, HEAD_DIM)
    kk = split(k * k_k)
    kk = (kk * lax.rsqrt(jnp.maximum(jnp.sum(kk * kk, -1, keepdims=True), 1e-12))).reshape(bsz, t, gd)
    lw = [-RW_DECAY_SCALE * jax.nn.sigmoid(w0[d] + lo[..., d * gd:(d + 1) * gd]) for d in range(2)]
    a = [jax.nn.sigmoid(a0[d] + lo[..., (2 + d) * gd:(3 + d) * gd]) for d in range(2)]
    k_dir = [k * (1 + (a[d] - 1) * k_a) for d in range(2)]
    y_fwd, y_bwd = rwkv_scan(r, v, kk, k_dir, a, lw, n_ctx=n_ctx)
    y = split(y_fwd + y_bwd)
    m = jnp.mean(y, -1, keepdims=True)
    var = jnp.mean(jnp.square(y - m), -1, keepdims=True)
    y = (y - m) * lax.rsqrt(var + RW_GN_EPS) * gn_g.reshape(N_HEADS, HEAD_DIM) + gn_b.reshape(N_HEADS, HEAD_DIM)
    y = y + jnp.sum(split(r) * split(k) * r_k, -1, keepdims=True) * split(v)
    return y.reshape(bsz, t, gd) * g


def _mamba_mixer(z, xbc, dt_raw, n_ctx, conv_w, conv_b, a_log, dt_bias, d_skip, norm_g):
    bsz, t, _ = z.shape
    conv = conv_b + sum(conv_w[j] * _seg_shift(xbc, j - MB_CONV // 2, n_ctx) for j in range(MB_CONV))
    xbc = jax.nn.silu(conv)
    gd = GROUP_DIM
    a_neg = -jnp.exp(a_log)
    y = (xbc[..., :gd].reshape(bsz, t, N_HEADS, HEAD_DIM) * d_skip[:, None]).reshape(bsz, t, gd)
    for d in range(2):
        dt = jax.nn.softplus(dt_raw[..., d * N_HEADS:(d + 1) * N_HEADS] + dt_bias[d])
        y = y + ssd_scan(xbc, dt, dt * a_neg[d], n_ctx=n_ctx, reverse=bool(d))
    y = y * jax.nn.silu(z)
    y = y.reshape(bsz, t, MB_GROUPS, gd // MB_GROUPS)
    y = y * lax.rsqrt(jnp.mean(y * y, -1, keepdims=True) + RMS_EPS)
    return y.reshape(bsz, t, gd) * norm_g


def _mla_mixer(q_lora, kv_lora, k_pe, n_ctx, q_norm, w_uq, kv_norm, w_ukv, tile, tables):
    bsz, t, _ = q_lora.shape
    n_h, pad = N_HEADS, MLA_QK_PAD - MLA_QK
    w_q = jnp.pad(w_uq.reshape(-1, n_h, MLA_QK), ((0, 0), (0, 0), (0, pad))).reshape(-1, n_h * MLA_QK_PAD)
    w_kv = w_ukv.reshape(-1, n_h, MLA_NOPE + HEAD_DIM)
    w_k = jnp.pad(w_kv[..., :MLA_NOPE], ((0, 0), (0, 0), (0, MLA_QK_PAD - MLA_NOPE))).reshape(-1, n_h * MLA_QK_PAD)
    w_v = w_kv[..., MLA_NOPE:].reshape(-1, n_h * HEAD_DIM)
    q = matmul(_rms_norm(q_lora, q_norm).reshape(bsz * t, -1).astype(BF16), w_q.astype(BF16))
    kv = matmul(_rms_norm(kv_lora, kv_norm).reshape(bsz * t, -1).astype(BF16),
                jnp.concatenate([w_k, w_v], axis=1).astype(BF16)).reshape(bsz, t, -1)
    pe = jnp.pad(k_pe, ((0, 0), (0, 0), (MLA_NOPE, pad)))
    k = kv[..., :n_h * MLA_QK_PAD] + jnp.tile(pe, (1, 1, n_h))
    q = _rope_heads(q.reshape(bsz, t, -1), tables, MLA_QK ** -0.5)
    k = _rope_heads(k, tables, 1.0)
    return attention(q, k, kv[..., n_h * MLA_QK_PAD:].astype(BF16), n_ctx=n_ctx, tile=tile)


def _first_max(vals, excluded):
    live = [jnp.where(x, -jnp.inf, v) for v, x in zip(vals, excluded)]
    top = functools.reduce(jnp.maximum, live)
    found = jnp.zeros_like(top, dtype=jnp.bool_)
    first = []
    for v, x in zip(live, excluded):
        hit = (v == top) & ~found & ~x
        first.append(hit)
        found = found | hit
    return top, first


def _router_body(x_ref, sc_ref, sh_ref, rw_ref, rb_ref, h_ref, gate_ref):
    h = x_ref[...] * (1.0 + sc_ref[0]) + sh_ref[0]
    h_ref[...] = h.astype(BF16)
    logits = jnp.dot(h, rw_ref[...], preferred_element_type=F32, precision=HIGHEST)
    scores = jax.nn.sigmoid(logits.T[:N_EXPERTS])
    biased = scores + rb_ref[...]
    rows = [biased[e:e + 1] for e in range(N_EXPERTS)]
    never = jnp.zeros_like(rows[0], dtype=jnp.bool_)
    group_scores, picked = [], []
    for g in range(N_EXPERT_GROUPS):
        vals = rows[g * EXPERTS_PER_GROUP:(g + 1) * EXPERTS_PER_GROUP]
        top1, first = _first_max(vals, [never] * EXPERTS_PER_GROUP)
        top2, second = _first_max(vals, first)
        group_scores.append(top1 + top2)
        picked.append([a | b for a, b in zip(first, second)])
    _, group_sel = _first_max(group_scores, [never] * N_EXPERT_GROUPS)
    chosen = [jnp.where(group_sel[e // EXPERTS_PER_GROUP] & picked[e // EXPERTS_PER_GROUP][e % EXPERTS_PER_GROUP],
                        scores[e:e + 1], 0.0) for e in range(N_EXPERTS)]
    denom = functools.reduce(jnp.add, chosen)
    gate_ref[...] = jnp.concatenate([w / denom * ROUTED_SCALE for w in chosen], axis=0)


def router(xs, mod_l, router_w, router_b, *, tokens_per_batch, n_ctx, tile):
    m, d = xs.shape
    bsz = m // tokens_per_batch
    tiles_per_batch, n_ctx_tiles = tokens_per_batch // tile, n_ctx // tile

    def mod_row(i):
        return jnp.where(i % tiles_per_batch < n_ctx_tiles, bsz, i // tiles_per_batch)

    rw_pad = jnp.pad(router_w, ((0, 0), (0, LANE - N_EXPERTS)))
    return pl.pallas_call(
        _router_body,
        grid=(m // tile,),
        in_specs=[pl.BlockSpec((tile, d), lambda i: (i, 0)),
                  pl.BlockSpec((None, 1, d), lambda i: (mod_row(i), 0, 4)),
                  pl.BlockSpec((None, 1, d), lambda i: (mod_row(i), 0, 3)),
                  pl.BlockSpec((d, LANE), lambda i: (0, 0)),
                  pl.BlockSpec((N_EXPERTS, 1), lambda i: (0, 0))],
        out_specs=[pl.BlockSpec((tile, d), lambda i: (i, 0)),
                   pl.BlockSpec((N_EXPERTS, tile), lambda i: (0, i))],
        out_shape=[jax.ShapeDtypeStruct((m, d), BF16), jax.ShapeDtypeStruct((N_EXPERTS, m), F32)],
        compiler_params=_params("parallel"),
        name="router",
    )(xs, mod_l, mod_l, rw_pad, router_b.reshape(N_EXPERTS, 1))


def _pad_w_in(w):
    d = w.shape[0]
    dt_end = RW_IN + MB_IN
    pe_end = dt_end + MLA_IN
    return jnp.concatenate([w[:, :dt_end], jnp.zeros((d, LANE - 2 * N_HEADS), w.dtype),
                            w[:, dt_end:pe_end], jnp.zeros((d, LANE - MLA_ROPE), w.dtype),
                            w[:, pe_end:]], axis=1)


def kernel(x, c, ctx, c_ctx, ada_w, ada_b, w_in, w_out, ln1_g, ln1_b, ln2_g, ln2_b, rw_mu, rw_w0, rw_w_up, rw_a0, rw_a_up, rw_g_up, rw_k_k, rw_k_a, rw_r_k, rw_gn_g, rw_gn_b, mb_conv_w, mb_conv_b, mb_a_log, mb_dt_bias, mb_d, mb_norm_g, mla_q_norm, mla_w_uq, mla_kv_norm, mla_w_ukv, na_rpb, router_w, router_b, exp_w_gate, exp_w_up, exp_w_down):
    bsz, seq, d = x.shape
    n_ctx = ctx.shape[1]
    t = n_ctx + seq
    depth = ada_w.shape[0]
    tile = min(256, n_ctx)
    cond = jnp.concatenate([jax.nn.silu(c), jax.nn.silu(c_ctx)[None]], axis=0)
    mod = modulation(cond, ada_w, ada_b)
    xs = jnp.concatenate([ctx, x], axis=1)
    expert_w = [w.astype(BF16) for w in (exp_w_gate, exp_w_up, exp_w_down)]
    rope_tables = _rope_tables(t, n_ctx)
    m = bsz * t
    for l in range(depth):
        mod_l = mod[l][:, None, :]
        p = proj_in(xs.reshape(m, d), mod_l, _pad_w_in(w_in[l]).astype(BF16),
                    tokens_per_batch=t, n_ctx=n_ctx).reshape(bsz, t, N_IN_PAD)
        o_rw = _rwkv_mixer(p[..., OFF_RW:OFF_MB_Z], n_ctx, rw_mu[l], rw_w0[l], rw_w_up[l], rw_a0[l], rw_a_up[l],
                           rw_g_up[l], rw_k_k[l], rw_k_a[l], rw_r_k[l], rw_gn_g[l], rw_gn_b[l])
        o_mb = _mamba_mixer(p[..., OFF_MB_Z:OFF_MB_XBC], p[..., OFF_MB_XBC:OFF_MB_DT],
                            p[..., OFF_MB_DT:OFF_MB_DT + 2 * N_HEADS], n_ctx, mb_conv_w[l], mb_conv_b[l],
                            mb_a_log[l], mb_dt_bias[l], mb_d[l], mb_norm_g[l])
        o_mla = _mla_mixer(p[..., OFF_MLA_Q:OFF_MLA_KV], p[..., OFF_MLA_KV:OFF_MLA_PE],
                           p[..., OFF_MLA_PE:OFF_MLA_PE + MLA_ROPE], n_ctx, mla_q_norm[l], mla_w_uq[l],
                           mla_kv_norm[l], mla_w_ukv[l], tile, rope_tables)
        o_na = neighborhood_attention(p, na_bias_table(na_rpb[l]), n_ctx=n_ctx, col0=OFF_NA)
        xs2 = proj_out([o.reshape(m, GROUP_DIM) for o in (o_rw, o_mb, o_mla, o_na)], w_out[l].astype(BF16),
                       xs.reshape(m, d), mod_l, ln1_g[l], ln1_b[l], tokens_per_batch=t, n_ctx=n_ctx)
        h, gate = router(xs2, mod_l, router_w, router_b, tokens_per_batch=t, n_ctx=n_ctx, tile=tile)
        xs = moe_experts(h, gate[:, :, None], *expert_w, l, xs2, mod_l, ln2_g[l], ln2_b[l],
                         tokens_per_batch=t, n_ctx=n_ctx).reshape(bsz, t, d)
    return xs[:, n_ctx:]
```

```python
import functools
import math

import numpy as np
import jax
import jax.numpy as jnp
from jax import lax
from jax.experimental import pallas as pl
from jax.experimental.pallas import tpu as pltpu

F32 = jnp.float32
BF16 = jnp.bfloat16
HIGHEST = lax.Precision.HIGHEST

D_MODEL = 2048
DEPTH = 4
GRID_W = 64
GROUP_DIM = D_MODEL // 4
HEAD_DIM = 64
N_HEADS = GROUP_DIM // HEAD_DIM

RW_LORA = 64
RW_G_LORA = 128
RW_DECAY_SCALE = 0.606531
RW_GN_EPS = 64e-5
RW_IN = 3 * GROUP_DIM + 4 * RW_LORA + RW_G_LORA
RW_CHUNK = 64

MB_GROUPS = 2
MB_STATE = 128
MB_CONV = 5
MB_CHUNK = 128
MB_CONV_DIM = GROUP_DIM + 2 * MB_GROUPS * MB_STATE
MB_IN = GROUP_DIM + MB_CONV_DIM + 2 * N_HEADS

MLA_Q_LORA = 3 * D_MODEL // 16
MLA_KV_LORA = D_MODEL // 16
MLA_NOPE = 64
MLA_ROPE = 32
MLA_QK = MLA_NOPE + MLA_ROPE
MLA_QK_PAD = 128
MLA_IN = MLA_Q_LORA + MLA_KV_LORA + MLA_ROPE
ROPE_BASE = 10000.0

NA_WIN_ROWS = 8
NA_WIN_COLS = 16
NA_IN = 3 * GROUP_DIM
NA_MASK = -1e30

N_EXPERTS = 16
N_EXPERT_GROUPS = 4
EXPERTS_PER_GROUP = N_EXPERTS // N_EXPERT_GROUPS
TOP_K = 2
D_EXPERT = D_MODEL // 4
ROUTED_SCALE = 2.5

DEEPNORM_ALPHA = (2 * DEPTH) ** 0.25
LN_EPS = 1e-6
RMS_EPS = 1e-6

LANE = 128
OFF_RW = 0
OFF_MB_Z = OFF_RW + RW_IN
OFF_MB_XBC = OFF_MB_Z + GROUP_DIM
OFF_MB_DT = OFF_MB_XBC + MB_CONV_DIM
OFF_MLA_Q = OFF_MB_DT + LANE
OFF_MLA_KV = OFF_MLA_Q + MLA_Q_LORA
OFF_MLA_PE = OFF_MLA_KV + MLA_KV_LORA
OFF_NA = OFF_MLA_PE + LANE
N_IN_PAD = OFF_NA + NA_IN

VMEM_LIMIT = 56 * 1024 * 1024


def _params(*sem):
    return pltpu.CompilerParams(dimension_semantics=sem, vmem_limit_bytes=VMEM_LIMIT)


def _pick_tile(n, target, quantum):
    best = None
    for t in range(quantum, min(n, target) + 1, quantum):
        if n % t == 0:
            best = t
    assert best is not None, (n, target, quantum)
    return best


def _mm_body(x_ref, w_ref, o_ref, *, precision):
    o_ref[...] = jnp.dot(x_ref[...], w_ref[...], preferred_element_type=F32,
                         precision=precision).astype(o_ref.dtype)


def matmul(x, w, *, tm_target=512, tn_target=1024, precision=None, out_dtype=F32):
    m, k = x.shape
    n = w.shape[1]
    tm = _pick_tile(m, tm_target, 16)
    tn = _pick_tile(n, tn_target, LANE)
    return pl.pallas_call(
        functools.partial(_mm_body, precision=precision),
        grid=(m // tm, n // tn),
        in_specs=[pl.BlockSpec((tm, k), lambda i, j: (i, 0)),
                  pl.BlockSpec((k, tn), lambda i, j: (0, j))],
        out_specs=pl.BlockSpec((tm, tn), lambda i, j: (i, j)),
        out_shape=jax.ShapeDtypeStruct((m, n), out_dtype),
        compiler_params=_params("parallel", "parallel"),
        name="matmul",
    )(x, w)


def _is_ctx_row(tm, tokens_per_batch, n_ctx):
    pos = (pl.program_id(0) * tm) % tokens_per_batch + lax.broadcasted_iota(jnp.int32, (tm, 1), 0)
    return pos < n_ctx


def _proj_in_body(x_ref, scc_ref, scb_ref, shc_ref, shb_ref, w_ref, o_ref, xm_ref, *, tokens_per_batch, n_ctx):
    is_ctx = _is_ctx_row(x_ref.shape[0], tokens_per_batch, n_ctx)

    @pl.when(pl.program_id(1) == 0)
    def _():
        sc = jnp.where(is_ctx, scc_ref[0], scb_ref[0])
        sh = jnp.where(is_ctx, shc_ref[0], shb_ref[0])
        xm_ref[...] = (x_ref[...] * (1.0 + sc) + sh).astype(BF16)

    o_ref[...] = jnp.dot(xm_ref[...], w_ref[...], preferred_element_type=F32)


def _mod_specs(bsz, tiles_per_batch, chunk, d):
    return [pl.BlockSpec((None, 1, d), lambda i, *_: (bsz, 0, chunk)),
            pl.BlockSpec((None, 1, d), lambda i, *_: (i // tiles_per_batch, 0, chunk))]


def proj_in(xs, mod_l, w, *, tokens_per_batch, n_ctx):
    m, d = xs.shape
    n = w.shape[1]
    bsz = m // tokens_per_batch
    tm = _pick_tile(tokens_per_batch, 640, 16)
    tn = _pick_tile(n, 1152, LANE)
    tiles_per_batch = tokens_per_batch // tm
    return pl.pallas_call(
        functools.partial(_proj_in_body, tokens_per_batch=tokens_per_batch, n_ctx=n_ctx),
        grid=(m // tm, n // tn),
        in_specs=[pl.BlockSpec((tm, d), lambda i, j: (i, 0)),
                  *_mod_specs(bsz, tiles_per_batch, 1, d), *_mod_specs(bsz, tiles_per_batch, 0, d),
                  pl.BlockSpec((d, tn), lambda i, j: (0, j))],
        out_specs=pl.BlockSpec((tm, tn), lambda i, j: (i, j)),
        out_shape=jax.ShapeDtypeStruct((m, n), F32),
        scratch_shapes=[pltpu.VMEM((tm, d), BF16)],
        compiler_params=_params("parallel", "arbitrary"),
        name="proj_in",
    )(xs, mod_l, mod_l, mod_l, mod_l, w)


def _proj_out_body(m0_ref, m1_ref, m2_ref, m3_ref, w_ref, x_ref, gc_ref, gb_ref, lg_ref, lb_ref, o_ref,
                   *, tokens_per_batch, n_ctx):
    gd = m0_ref.shape[1]
    y = None
    for j, m_ref in enumerate((m0_ref, m1_ref, m2_ref, m3_ref)):
        part = jnp.dot(m_ref[...].astype(BF16), w_ref[j * gd:(j + 1) * gd, :], preferred_element_type=F32)
        y = part if y is None else y + part
    gate = jnp.where(_is_ctx_row(x_ref.shape[0], tokens_per_batch, n_ctx), gc_ref[0], gb_ref[0])
    z = DEEPNORM_ALPHA * x_ref[...] + gate * y
    mu = jnp.mean(z, axis=-1, keepdims=True)
    zc = z - mu
    var = jnp.mean(zc * zc, axis=-1, keepdims=True)
    o_ref[...] = zc * lax.rsqrt(var + LN_EPS) * lg_ref[...] + lb_ref[...]


def proj_out(mixes, w, xs, mod_l, ln_g, ln_b, *, tokens_per_batch, n_ctx):
    m, d = xs.shape
    gd = mixes[0].shape[1]
    bsz = m // tokens_per_batch
    tm = _pick_tile(tokens_per_batch, 320, 16)
    tiles_per_batch = tokens_per_batch // tm
    row = lambda i: (i, 0)
    fixed = lambda i: (0, 0)
    return pl.pallas_call(
        functools.partial(_proj_out_body, tokens_per_batch=tokens_per_batch, n_ctx=n_ctx),
        grid=(m // tm,),
        in_specs=[pl.BlockSpec((tm, gd), row)] * 4
        + [pl.BlockSpec((d, d), fixed), pl.BlockSpec((tm, d), row), *_mod_specs(bsz, tiles_per_batch, 2, d),
           pl.BlockSpec((1, d), fixed), pl.BlockSpec((1, d), fixed)],
        out_specs=pl.BlockSpec((tm, d), row),
        out_shape=jax.ShapeDtypeStruct((m, d), F32),
        compiler_params=_params("parallel"),
        name="proj_out",
    )(*mixes, w, xs, mod_l, mod_l, ln_g.reshape(1, d), ln_b.reshape(1, d))


def _mod_body(c_ref, w_ref, b_ref, o_ref, *, n_rows):
    w = w_ref[...]
    reps = w.shape[1] // LANE
    o_ref[...] = jnp.zeros_like(o_ref)
    for m in range(n_rows):
        cb = c_ref[m]
        cbt = jnp.concatenate([cb] * reps, axis=1)
        o_ref[m:m + 1, :] = jnp.sum(w * cbt, axis=0, keepdims=True) + b_ref[...]


def modulation(cond, ada_w, ada_b):
    n_rows, k = cond.shape
    depth, _, n = ada_w.shape
    tn = 512
    cond_b = jnp.broadcast_to(cond[:, :, None], (n_rows, k, LANE))
    return pl.pallas_call(
        functools.partial(_mod_body, n_rows=n_rows),
        grid=(depth, n // tn),
        in_specs=[pl.BlockSpec((n_rows, k, LANE), lambda l, j: (0, 0, 0)),
                  pl.BlockSpec((None, k, tn), lambda l, j: (l, 0, j)),
                  pl.BlockSpec((None, 1, tn), lambda l, j: (l, 0, j))],
        out_specs=pl.BlockSpec((None, 8, tn), lambda l, j: (l, 0, j)),
        out_shape=jax.ShapeDtypeStruct((depth, 8, n), F32),
        compiler_params=_params("parallel", "parallel"),
        name="modulation",
    )(cond_b, ada_w, ada_b.reshape(depth, 1, n))


def _bmm(a, b, spec, mode):
    if mode == 'f32':
        return jnp.einsum(spec, a, b, preferred_element_type=F32, precision=HIGHEST)
    a_hi, b_hi = a.astype(BF16), b.astype(BF16)
    out = jnp.einsum(spec, a_hi, b_hi, preferred_element_type=F32)
    if mode == 'x3':
        a_lo = (a - a_hi.astype(F32)).astype(BF16)
        b_lo = (b - b_hi.astype(F32)).astype(BF16)
        out = out + (jnp.einsum(spec, a_hi, b_lo, preferred_element_type=F32)
                     + jnp.einsum(spec, a_lo, b_hi, preferred_element_type=F32))
    return out


def _unit_triangular_inverse(l_mat, ti, si):
    c = l_mat.shape[1]
    nn = 'hts,hsn->htn'
    blk = 8
    eye = (ti == si).astype(F32)
    l_d = jnp.where((ti // blk) == (si // blk), l_mat, 0.0)
    p2 = _bmm(l_d, l_d, nn, 'bf16')
    p4 = _bmm(p2, p2, nn, 'bf16')
    inv = _bmm(eye - l_d, eye + p2, nn, 'bf16')
    inv = _bmm(inv, eye + p4, nn, 'bf16')
    while blk < c:
        pair = ((ti // (2 * blk)) == (si // (2 * blk))) & ((ti // blk) != (si // blk))
        off = jnp.where(pair, l_mat, 0.0)
        inv = inv - _bmm(_bmm(inv, off, nn, 'bf16'), inv, nn, 'bf16')
        blk *= 2
    return inv


def _first_head_lanes(shape):
    return lax.broadcasted_iota(jnp.int32, shape, len(shape) - 1) % LANE < HEAD_DIM


def _rwkv_body(*refs):
    fwd_in, bwd_in, (yf_ref, yb_ref, st_ref) = refs[:6], refs[6:12], refs[12:]

    @pl.when(pl.program_id(0) == 0)
    def _():
        st_ref[...] = jnp.zeros_like(st_ref)

    bsz, c, width = fwd_in[0].shape
    n_pairs = width // LANE
    ti = lax.broadcasted_iota(jnp.int32, (c, c), 0)
    si = lax.broadcasted_iota(jnp.int32, (c, c), 1)
    units = []
    for in_refs, reverse in ((fwd_in, False), (bwd_in, True)):
        tri = ((si >= ti) if reverse else (si <= ti)).astype(F32)
        for row in range(bsz):
            r, k, v, kk, a, lw = (ref[row] for ref in in_refs)
            cum = _bmm(tri, lw, 'ts,sn->tn', 'f32')
            tot = cum[:1] if reverse else cum[c - 1:]
            b = kk * a
            e_neg = jnp.exp(-cum)
            e_rem = jnp.exp(tot - cum)
            pieces = (r * jnp.exp(cum), kk * jnp.exp(cum - lw), k * e_neg, b * e_neg, k * e_rem, b * e_rem, v,
                      jnp.exp(tot))
            for j in range(n_pairs):
                units.append([m[:, j * LANE:(j + 1) * LANE] for m in pieces])
    n_units = len(units)
    rq, kq, kd, bd, kdc, bdc, v, e_tot = (jnp.stack([u[i] for u in units], axis=0) for i in range(8))

    backward = lax.broadcasted_iota(jnp.int32, (n_units, 1, 1), 0) >= n_units // 2
    order = (si - ti) * jnp.where(backward, -1, 1)
    incl = order <= 0
    strict = order < 0
    first = _first_head_lanes((c, LANE))
    first2 = _first_head_lanes((c, 2 * LANE))
    nt, nn, tn = 'utn,usn->uts', 'uts,usn->utn', 'usn,usm->unm'
    qq = jnp.concatenate([jnp.where(first, rq, 0.0), jnp.where(first, kq, 0.0),
                          jnp.where(first, 0.0, rq), jnp.where(first, 0.0, kq)], axis=1)
    ak = _bmm(qq, kd, nt, 'bf16')
    ab = _bmm(qq, bd, nt, 'bf16')
    a_rk = [jnp.where(incl, ak[:, 2 * x * c:(2 * x + 1) * c], 0.0) for x in range(2)]
    a_kk = [jnp.where(strict, ak[:, (2 * x + 1) * c:(2 * x + 2) * c], 0.0) for x in range(2)]
    a_rb = [jnp.where(incl, ab[:, 2 * x * c:(2 * x + 1) * c], 0.0) for x in range(2)]
    a_kb = [jnp.where(strict, ab[:, (2 * x + 1) * c:(2 * x + 2) * c], 0.0) for x in range(2)]
    inv = _unit_triangular_inverse(jnp.concatenate(a_kb, axis=0), ti, si)
    inv = [inv[:n_units], inv[n_units:]]

    xs = [_bmm(inv[x], jnp.concatenate([_bmm(a_kk[x], v, nn, 'bf16'), kq], axis=2), nn, 'bf16') for x in range(2)]
    x2 = jnp.where(first2, xs[0], xs[1])
    arx = jnp.where(first2, _bmm(a_rb[0], x2, nn, 'bf16'), _bmm(a_rb[1], x2, nn, 'bf16'))
    y0 = jnp.where(first, _bmm(a_rk[0], v, nn, 'bf16'), _bmm(a_rk[1], v, nn, 'bf16')) - arx[:, :, :LANE]
    rqp = rq - arx[:, :, LANE:]
    row_n = lax.broadcasted_iota(jnp.int32, (LANE, LANE), 0)
    col_n = lax.broadcasted_iota(jnp.int32, (LANE, LANE), 1)
    same_head = (row_n // HEAD_DIM) == (col_n // HEAD_DIM)
    bx = _bmm(bdc, x2, tn, 'bf16')
    tadd = jnp.where(same_head, _bmm(kdc, v, tn, 'bf16') - bx[:, :, :LANE], 0.0)
    decay = jnp.where(row_n == col_n, jnp.broadcast_to(e_tot, (n_units, LANE, LANE)), 0.0)
    p = jnp.where(same_head, decay - bx[:, :, LANE:], 0.0)
    t0 = st_ref[...]
    y = _bmm(rqp, t0, 'utn,unm->utm', 'x3') + y0
    st_ref[...] = _bmm(p, t0, 'ujn,unm->ujm', 'x3') + tadd
    for d, y_ref in enumerate((yf_ref, yb_ref)):
        for row in range(bsz):
            u0 = (d * bsz + row) * n_pairs
            y_ref[row] = jnp.concatenate([y[u0 + j] for j in range(n_pairs)], axis=1)


def _scan_chunk_index(c, n_ctx_chunks, n_chunks, reverse):
    if not reverse:
        return c
    return jnp.where(c < n_ctx_chunks, n_ctx_chunks - 1 - c, n_chunks - 1 - (c - n_ctx_chunks))


def rwkv_scan(r, v, kk, k_dirs, a_dirs, lw_dirs, *, n_ctx):
    bsz, t, width = r.shape
    c = RW_CHUNK
    n_chunks, n_ctx_chunks = t // c, n_ctx // c
    specs = [pl.BlockSpec((bsz, c, width),
                          lambda i, rev=rev: (0, _scan_chunk_index(i, n_ctx_chunks, n_chunks, rev), 0))
             for rev in (False, True)]
    n_units = 2 * bsz * (width // LANE)
    args = [(r, k_dirs[d], v, kk, a_dirs[d], lw_dirs[d]) for d in range(2)]
    return pl.pallas_call(
        _rwkv_body,
        grid=(n_chunks,),
        in_specs=[specs[0]] * 6 + [specs[1]] * 6,
        out_specs=specs,
        out_shape=[jax.ShapeDtypeStruct((bsz, t, width), F32)] * 2,
        scratch_shapes=[pltpu.VMEM((n_units, LANE, LANE), F32)],
        compiler_params=_params("arbitrary"),
        name="rwkv_scan",
    )(*args[0], *args[1])


def _ssd_body(x_ref, dt_ref, dac_ref, dar_ref, y_ref, st_ref, *, reverse):
    @pl.when(pl.program_id(1) == 0)
    def _():
        st_ref[...] = jnp.zeros_like(st_ref)

    xbc = x_ref[0]
    dt = dt_ref[0]
    c = xbc.shape[0]
    n_h = dt.shape[1]
    n_pairs = n_h // 2
    gd = n_h * HEAD_DIM
    n_g = (xbc.shape[1] - gd) // (2 * MB_STATE)
    pairs_per_group = n_pairs // n_g
    ti = lax.broadcasted_iota(jnp.int32, (c, c), 0)
    si = lax.broadcasted_iota(jnp.int32, (c, c), 1)
    incl = (si >= ti) if reverse else (si <= ti)
    incl_t = (ti >= si) if reverse else (ti <= si)
    cs_c = jnp.dot(incl.astype(F32), dac_ref[0], preferred_element_type=F32, precision=HIGHEST)
    cs_r = jnp.dot(dar_ref[0], incl_t.astype(F32), preferred_element_type=F32, precision=HIGHEST)
    tot = cs_c[:1, :] if reverse else cs_c[c - 1:, :]
    e_cs = jnp.exp(cs_c)
    e_rem = jnp.exp(tot - cs_c)
    e_tot = jnp.exp(tot)
    bm = [xbc[:, gd + g * MB_STATE:gd + (g + 1) * MB_STATE].astype(BF16) for g in range(n_g)]
    cm = [xbc[:, gd + (n_g + g) * MB_STATE:gd + (n_g + g + 1) * MB_STATE].astype(BF16) for g in range(n_g)]
    gmat = [jnp.einsum('ln,sn->ls', cm[g], bm[g], preferred_element_type=F32) for g in range(n_g)]
    first = _first_head_lanes((c, LANE))
    first_rows = lax.broadcasted_iota(jnp.int32, (LANE, 1), 0) < HEAD_DIM
    pick = lambda m, ha: jnp.where(first, m[:, ha:ha + 1], m[:, ha + 1:ha + 2])
    for j in range(n_pairs):
        g = j // pairs_per_group
        ha = 2 * j
        sl = slice(j * LANE, (j + 1) * LANE)
        xdt = xbc[:, sl] * pick(dt, ha)
        xdt_b = xdt.astype(BF16)
        y_heads = []
        for h in (ha, ha + 1):
            seg = cs_c[:, h:h + 1] - cs_r[h:h + 1, :]
            lmat = jnp.exp(jnp.where(incl, seg, NA_MASK))
            y_heads.append(jnp.dot((gmat[g] * lmat).astype(BF16), xdt_b, preferred_element_type=F32))
        st = st_ref[j]
        y_off = jnp.einsum('ln,pn->lp', cm[g], st.astype(BF16), preferred_element_type=F32) * pick(e_cs, ha)
        y_ref[0, :, sl] = jnp.where(first, y_heads[0], y_heads[1]) + y_off
        xdec = (xdt * pick(e_rem, ha)).astype(BF16)
        keep = jnp.where(first_rows, e_tot[:, ha:ha + 1], e_tot[:, ha + 1:ha + 2])
        st_ref[j] = keep * st + jnp.einsum('lp,ln->pn', xdec, bm[g], preferred_element_type=F32)


def ssd_scan(xbc, dt, da, *, n_ctx, reverse):
    bsz, t, width = xbc.shape
    n_h = dt.shape[2]
    gd = n_h * HEAD_DIM
    c = MB_CHUNK
    n_chunks, n_ctx_chunks = t // c, n_ctx // c
    idx = lambda i: _scan_chunk_index(i, n_ctx_chunks, n_chunks, reverse)
    da_row = jnp.swapaxes(da, 1, 2)
    return pl.pallas_call(
        functools.partial(_ssd_body, reverse=reverse),
        grid=(bsz, n_chunks),
        in_specs=[pl.BlockSpec((1, c, width), lambda b, i: (b, idx(i), 0)),
                  pl.BlockSpec((1, c, n_h), lambda b, i: (b, idx(i), 0)),
                  pl.BlockSpec((1, c, n_h), lambda b, i: (b, idx(i), 0)),
                  pl.BlockSpec((1, n_h, c), lambda b, i: (b, 0, idx(i)))],
        out_specs=pl.BlockSpec((1, c, gd), lambda b, i: (b, idx(i), 0)),
        out_shape=jax.ShapeDtypeStruct((bsz, t, gd), F32),
        scratch_shapes=[pltpu.VMEM((n_h // 2, 2 * HEAD_DIM, MB_STATE), F32)],
        compiler_params=_params("parallel", "arbitrary"),
        name="ssd_scan_bwd" if reverse else "ssd_scan_fwd",
    )(xbc, dt, da, da_row)


def _softmax_pv(s_parts, v_parts):
    m = functools.reduce(jnp.maximum, [jnp.max(s, axis=-1, keepdims=True) for s in s_parts])
    ps = [jnp.exp(s - m) for s in s_parts]
    l = functools.reduce(jnp.add, [jnp.sum(p, axis=-1, keepdims=True) for p in ps])
    o = functools.reduce(jnp.add, [jnp.dot(p.astype(BF16), v, preferred_element_type=F32)
                                   for p, v in zip(ps, v_parts)])
    return o / l


def _attn_body(q_ref, k_ref, v_ref, o_ref, *, n_ctx, n_ctx_tiles):
    dq = q_ref.shape[2] // 2

    def attend(n_keys):
        v2 = v_ref[0, 0:n_keys]
        own = _first_head_lanes(v2.shape)
        outs = []
        for x in range(2):
            s = jnp.einsum('qd,kd->qk', q_ref[0, :, x * dq:(x + 1) * dq], k_ref[0, 0:n_keys, x * dq:(x + 1) * dq],
                           preferred_element_type=F32)
            p = jnp.exp((s - jnp.max(s, axis=-1, keepdims=True)).astype(BF16))
            vx = jnp.where(own if x == 0 else ~own, v2, jnp.ones_like(v2))
            o = jnp.dot(p, vx, preferred_element_type=F32)
            outs.append(o / pltpu.roll(o, HEAD_DIM, axis=1))
        o_ref[0] = jnp.where(_first_head_lanes(outs[0].shape), outs[0], outs[1])

    @pl.when(pl.program_id(2) < n_ctx_tiles)
    def _():
        attend(n_ctx)

    @pl.when(pl.program_id(2) >= n_ctx_tiles)
    def _():
        attend(k_ref.shape[1])


def attention(q, k, v, *, n_ctx, tile):
    bsz, t, width = v.shape
    n_pairs = width // LANE
    dq2 = q.shape[2] // n_pairs
    return pl.pallas_call(
        functools.partial(_attn_body, n_ctx=n_ctx, n_ctx_tiles=n_ctx // tile),
        grid=(bsz, n_pairs, t // tile),
        in_specs=[pl.BlockSpec((1, tile, dq2), lambda b, j, i: (b, i, j)),
                  pl.BlockSpec((1, t, dq2), lambda b, j, i: (b, 0, j)),
                  pl.BlockSpec((1, t, LANE), lambda b, j, i: (b, 0, j))],
        out_specs=pl.BlockSpec((1, tile, LANE), lambda b, j, i: (b, i, j)),
        out_shape=jax.ShapeDtypeStruct((bsz, t, width), F32),
        compiler_params=_params("parallel", "parallel", "arbitrary"),
        name="attention",
    )(q, k, v)


def _na_body(q_ref, k_ref, v_ref, bias_ref, o_ref, kb_ref, vb_ref, *, n_ctx, n_rows):
    i = pl.program_id(2)
    rows_per_step = q_ref.shape[1] // GRID_W
    win = NA_WIN_ROWS * GRID_W
    q2 = q_ref[0] * (HEAD_DIM ** -0.5)

    def head_queries(q_rows, x):
        first = _first_head_lanes(q_rows.shape)
        return jnp.where(first if x == 0 else ~first, q_rows, 0.0).astype(BF16)

    def scores(q, k):
        return jnp.einsum('qd,kd->qk', q, k, preferred_element_type=F32)

    @pl.when(i == 0)
    def _():
        kb_ref[...] = k_ref[0].astype(BF16)
        vb_ref[...] = v_ref[0].astype(BF16)
        kc, vc = kb_ref[0:n_ctx], vb_ref[0:n_ctx]
        outs = [_softmax_pv([scores(head_queries(q2, x), kc)], [vc]) for x in range(2)]
        o_ref[0] = jnp.where(_first_head_lanes(outs[0].shape), outs[0], outs[1])

    @pl.when(i > 0)
    def _():
        qs, kws, vws, biases = [], [], [], []
        for rr in range(rows_per_step):
            r = (i - 1) * rows_per_step + rr
            rs = jnp.clip(r - NA_WIN_ROWS // 2, 0, n_rows - NA_WIN_ROWS)
            start = pl.multiple_of(n_ctx + rs * GRID_W, GRID_W)
            kws.append(kb_ref[pl.ds(start, win)])
            vws.append(vb_ref[pl.ds(start, win)])
            q_row = q2[rr * GRID_W:(rr + 1) * GRID_W]
            qs.append(jnp.concatenate([head_queries(q_row, 0), head_queries(q_row, 1)], axis=0))
            biases.append(jnp.concatenate([bias_ref[0, r - rs], bias_ref[1, r - rs]], axis=0))
        qb = jnp.stack(qs, axis=0)
        kc, vc = kb_ref[0:n_ctx], vb_ref[0:n_ctx]
        s_loc = jnp.einsum('uqd,ukd->uqk', qb, jnp.stack(kws, axis=0), preferred_element_type=F32)
        s_loc = s_loc + jnp.stack(biases, axis=0)
        s_ctx = scores(qb.reshape(rows_per_step * 2 * GRID_W, LANE), kc).reshape(rows_per_step, 2 * GRID_W, n_ctx)
        m = jnp.maximum(jnp.max(s_loc, axis=-1, keepdims=True), jnp.max(s_ctx, axis=-1, keepdims=True))
        p_loc = jnp.exp(s_loc - m)
        p_ctx = jnp.exp(s_ctx - m)
        l = jnp.sum(p_loc, axis=-1, keepdims=True) + jnp.sum(p_ctx, axis=-1, keepdims=True)
        o = jnp.einsum('uqk,ukd->uqd', p_loc.astype(BF16), jnp.stack(vws, axis=0), preferred_element_type=F32)
        o = o + jnp.dot(p_ctx.reshape(rows_per_step * 2 * GRID_W, n_ctx).astype(BF16), vc,
                        preferred_element_type=F32).reshape(o.shape)
        o = o / l
        first = _first_head_lanes((GRID_W, LANE))
        for rr in range(rows_per_step):
            o_ref[0, rr * GRID_W:(rr + 1) * GRID_W] = jnp.where(first, o[rr, :GRID_W], o[rr, GRID_W:])


def na_bias_table(rpb):
    wr = NA_WIN_ROWS
    qv = np.arange(GRID_W)[:, None]
    cv = np.arange(GRID_W)[None, :]
    col_start = np.clip(qv - NA_WIN_COLS // 2, 0, GRID_W - NA_WIN_COLS)
    valid = (cv >= col_start) & (cv < col_start + NA_WIN_COLS)
    offs = np.arange(2 * NA_WIN_COLS - 1)[:, None, None]
    pick = ((cv - qv + (NA_WIN_COLS - 1))[None] == offs) & valid[None]
    cols = jnp.einsum('hro,oqc->hrqc', rpb, jnp.asarray(pick, F32), precision=HIGHEST)
    cols = jnp.where(valid, cols, NA_MASK)
    tab = jnp.stack([cols[:, wr - 1 - var:2 * wr - 1 - var] for var in range(wr)], axis=1)
    tab = jnp.transpose(tab, (0, 1, 3, 2, 4))
    return tab.reshape(rpb.shape[0], wr, GRID_W, wr * GRID_W)


def neighborhood_attention(p, bias, *, n_ctx, col0):
    bsz, t, _ = p.shape
    n_rows = (t - n_ctx) // GRID_W
    n_pairs = GROUP_DIM // LANE
    c0 = col0 // LANE
    return pl.pallas_call(
        functools.partial(_na_body, n_ctx=n_ctx, n_rows=n_rows),
        grid=(bsz, n_pairs, t // n_ctx),
        in_specs=[pl.BlockSpec((1, n_ctx, LANE), lambda b, j, i: (b, i, c0 + j)),
                  pl.BlockSpec((1, t, LANE), lambda b, j, i: (b, 0, c0 + n_pairs + j)),
                  pl.BlockSpec((1, t, LANE), lambda b, j, i: (b, 0, c0 + 2 * n_pairs + j)),
                  pl.BlockSpec((2, NA_WIN_ROWS, GRID_W, NA_WIN_ROWS * GRID_W), lambda b, j, i: (j, 0, 0, 0))],
        out_specs=pl.BlockSpec((1, n_ctx, LANE), lambda b, j, i: (b, i, j)),
        out_shape=jax.ShapeDtypeStruct((bsz, t, GROUP_DIM), F32),
        scratch_shapes=[pltpu.VMEM((t, LANE), BF16), pltpu.VMEM((t, LANE), BF16)],
        compiler_params=_params("parallel", "parallel", "arbitrary"),
        name="neighborhood_attention",
    )(p, p, p, bias)


def _moe_body(h_ref, gate_ref, wg_ref, wu_ref, wd_ref, x_ref, gc_ref, gb_ref, lg_ref, lb_ref, o_ref, acc_ref,
              *, tokens_per_batch, n_ctx):
    e = pl.program_id(1)
    is_ctx = _is_ctx_row(x_ref.shape[0], tokens_per_batch, n_ctx)

    @pl.when(e == 0)
    def _():
        acc_ref[...] = jnp.zeros_like(acc_ref)

    h = h_ref[...]
    hid = jax.nn.silu(jnp.dot(h, wg_ref[0], preferred_element_type=F32)) * jnp.dot(h, wu_ref[0], preferred_element_type=F32)
    acc_ref[...] += gate_ref[0] * jnp.dot(hid.astype(BF16), wd_ref[0], preferred_element_type=F32)

    @pl.when(e == pl.num_programs(1) - 1)
    def _():
        z = DEEPNORM_ALPHA * x_ref[...] + jnp.where(is_ctx, gc_ref[0], gb_ref[0]) * acc_ref[...]
        mu = jnp.mean(z, axis=-1, keepdims=True)
        zc = z - mu
        var = jnp.mean(zc * zc, axis=-1, keepdims=True)
        o_ref[...] = zc * lax.rsqrt(var + LN_EPS) * lg_ref[...] + lb_ref[...]


def moe_experts(h, gate, w_gate, w_up, w_down, layer, xs, mod_l, ln_g, ln_b, *, tokens_per_batch, n_ctx):
    m, d = h.shape
    n_e, d_e = w_gate.shape[1], w_gate.shape[3]
    bsz = m // tokens_per_batch
    tm = _pick_tile(tokens_per_batch, 640, 16)
    tiles_per_batch = tokens_per_batch // tm
    row = lambda i, e: (i, 0)
    fixed = lambda i, e: (0, 0)
    return pl.pallas_call(
        functools.partial(_moe_body, tokens_per_batch=tokens_per_batch, n_ctx=n_ctx),
        grid=(m // tm, n_e),
        in_specs=[pl.BlockSpec((tm, d), row),
                  pl.BlockSpec((1, tm, 1), lambda i, e: (e, i, 0)),
                  pl.BlockSpec((None, 1, d, d_e), lambda i, e: (layer, e, 0, 0)),
                  pl.BlockSpec((None, 1, d, d_e), lambda i, e: (layer, e, 0, 0)),
                  pl.BlockSpec((None, 1, d_e, d), lambda i, e: (layer, e, 0, 0)),
                  pl.BlockSpec((tm, d), row), *_mod_specs(bsz, tiles_per_batch, 5, d),
                  pl.BlockSpec((1, d), fixed), pl.BlockSpec((1, d), fixed)],
        out_specs=pl.BlockSpec((tm, d), row),
        out_shape=jax.ShapeDtypeStruct((m, d), F32),
        scratch_shapes=[pltpu.VMEM((tm, d), F32)],
        compiler_params=_params("parallel", "arbitrary"),
        name="moe_experts",
    )(h, gate, w_gate, w_up, w_down, xs, mod_l, mod_l, ln_g.reshape(1, d), ln_b.reshape(1, d))


def _layer_norm(x, g, b):
    mu = jnp.mean(x, -1, keepdims=True)
    var = jnp.mean(jnp.square(x - mu), -1, keepdims=True)
    return (x - mu) * lax.rsqrt(var + LN_EPS) * g + b


def _rms_norm(x, g):
    return x * lax.rsqrt(jnp.mean(x * x, -1, keepdims=True) + RMS_EPS) * g


def _seg_shift(x, offset, n_ctx):
    t = x.shape[1]
    if offset == 0:
        return x
    if offset < 0:
        y = jnp.pad(x, ((0, 0), (-offset, 0), (0, 0)))[:, :t]
    else:
        y = jnp.pad(x, ((0, 0), (0, offset), (0, 0)))[:, offset:]
    pos = np.arange(t)
    src = pos + offset
    same = (src >= 0) & (src < t) & ((pos < n_ctx) == (src < n_ctx))
    return y * jnp.asarray(same, x.dtype)[None, :, None]


def _heads(x):
    bsz, t, c = x.shape
    return jnp.transpose(x.reshape(bsz, t, c // HEAD_DIM, HEAD_DIM), (0, 2, 1, 3))


def _unheads(x):
    bsz, n_h, t, n = x.shape
    return jnp.transpose(x, (0, 2, 1, 3)).reshape(bsz, t, n_h * n)


def _rope_tables(t, n_ctx):
    n_freq = MLA_ROPE // 4
    pos_t = np.arange(t - n_ctx)
    pos = np.stack([pos_t // GRID_W, pos_t % GRID_W], axis=-1).astype(np.float32)
    inv_freq = jnp.asarray(ROPE_BASE, F32) ** (-jnp.arange(n_freq, dtype=F32) / n_freq)
    ang = jnp.asarray(pos)[:, :, None] * inv_freq
    ang = jnp.concatenate([jnp.zeros((n_ctx, 2, n_freq), F32), ang], axis=0)
    cos = jnp.broadcast_to(jnp.cos(ang)[:, :, None, :], (t, 2, 2, n_freq)).reshape(t, MLA_ROPE)
    sin = jnp.sin(ang)[:, :, None, :] * jnp.asarray([-1.0, 1.0], F32)[None, None, :, None]
    sin = sin.reshape(t, MLA_ROPE)
    pad = MLA_QK_PAD - MLA_QK
    cos_tab = jnp.concatenate([jnp.ones((t, MLA_NOPE), F32), cos, jnp.zeros((t, pad), F32)], axis=1)
    sin_tab = jnp.concatenate([jnp.zeros((t, MLA_NOPE), F32), sin, jnp.zeros((t, pad), F32)], axis=1)
    return cos_tab, sin_tab


def _rope_heads(x, tables, scale):
    cos_tab, sin_tab = (jnp.tile(tab, (1, x.shape[2] // MLA_QK_PAD)) for tab in tables)
    lane = np.arange(x.shape[2]) % MLA_QK_PAD
    first_half = jnp.asarray((lane - MLA_NOPE) % (MLA_ROPE // 2) < MLA_ROPE // 4)
    partner = jnp.where(first_half, jnp.roll(x, -(MLA_ROPE // 4), axis=2), jnp.roll(x, MLA_ROPE // 4, axis=2))
    return ((x * cos_tab + partner * sin_tab) * scale).astype(BF16)


HALO = 8


def _shifted_rows(x, halo_prev, halo_next, offset, pos, n_ctx, t):
    tm = x.shape[0]
    out = pltpu.roll(x, (-offset) % tm, axis=0)
    row = lax.broadcasted_iota(jnp.int32, (tm, 1), 0)
    for i in range(abs(offset)):
        if offset < 0:
            out = jnp.where(row == i, halo_prev[HALO + offset + i:HALO + offset + i + 1], out)
        else:
            out = jnp.where(row == tm - offset + i, halo_next[i:i + 1], out)
    src = pos + offset
    same = (src >= 0) & (src < t) & ((pos < n_ctx) == (src < n_ctx))
    return jnp.where(same, out, 0.0)


def _halo_specs(tm, t, width, col):
    per_tile, last = tm // HALO, t // HALO - 1
    return [pl.BlockSpec((1, HALO, width), lambda b, i: (b, jnp.maximum(i * per_tile - 1, 0), col)),
            pl.BlockSpec((1, HALO, width), lambda b, i: (b, jnp.minimum((i + 1) * per_tile, last), col))]


def _rwkv_prep_body(p_ref, pp_ref, pn_ref, mu_ref, w0_ref, a0_ref, kkw_ref, ka_ref, rk_ref, lora_ref,
                    r_ref, v_ref, kk_ref, kd0_ref, kd1_ref, a0o_ref, a1o_ref, lw0_ref, lw1_ref, g_ref, bonus_ref,
                    *, n_ctx, t):
    x = p_ref[0]
    tm = x.shape[0]
    gd = GROUP_DIM
    pos = pl.program_id(1) * tm + lax.broadcasted_iota(jnp.int32, (tm, 1), 0)
    prev = _shifted_rows(x, pp_ref[0], pn_ref[0], -1, pos, n_ctx, t)
    nxt = _shifted_rows(x, pp_ref[0], pn_ref[0], 1, pos, n_ctx, t)
    x = x + mu_ref[0:1] * (prev - x) + mu_ref[1:2] * (nxt - x)
    r, k, v = x[:, :gd], x[:, gd:2 * gd], x[:, 2 * gd:3 * gd]
    lora_in = jnp.concatenate([jnp.tanh(x[:, 3 * gd:3 * gd + 2 * RW_LORA]),
                               x[:, 3 * gd + 2 * RW_LORA:3 * gd + 4 * RW_LORA],
                               jax.nn.sigmoid(x[:, 3 * gd + 4 * RW_LORA:])], axis=1)
    lo = jnp.dot(lora_in.astype(BF16), lora_ref[...], preferred_element_type=F32)
    lane_r = lax.broadcasted_iota(jnp.int32, (gd, gd), 0) // HEAD_DIM
    lane_c = lax.broadcasted_iota(jnp.int32, (gd, gd), 1) // HEAD_DIM
    same_head = (lane_r == lane_c).astype(F32)
    head_sum = lambda m: _bmm(m, same_head, 'tn,nm->tm', 'x3')
    kk = k * kkw_ref[...]
    kk_ref[0] = kk * lax.rsqrt(jnp.maximum(head_sum(kk * kk), 1e-12))
    for d, (kd_ref, a_ref, lw_ref) in enumerate(((kd0_ref, a0o_ref, lw0_ref), (kd1_ref, a1o_ref, lw1_ref))):
        a = jax.nn.sigmoid(a0_ref[d:d + 1] + lo[:, (2 + d) * gd:(3 + d) * gd])
        a_ref[0] = a
        lw_ref[0] = -RW_DECAY_SCALE * jax.nn.sigmoid(w0_ref[d:d + 1] + lo[:, d * gd:(d + 1) * gd])
        kd_ref[0] = k * (1.0 + (a - 1.0) * ka_ref[...])
    r_ref[0] = r
    v_ref[0] = v
    g_ref[0] = lo[:, 4 * gd:]
    bonus_ref[0] = head_sum(r * k * rk_ref[...]) * v


def rwkv_prep(p, mu, w0, a0, k_k, k_a, r_k, lora_w, *, n_ctx):
    bsz, t, _ = p.shape
    gd = GROUP_DIM
    tm = _pick_tile(t, 320, 16)
    row = lambda b, i: (b, i, 0)
    fixed = lambda b, i: (0, 0)
    vec = lambda w: w.reshape(1, gd)
    out = jax.ShapeDtypeStruct((bsz, t, gd), F32)
    return pl.pallas_call(
        functools.partial(_rwkv_prep_body, n_ctx=n_ctx, t=t),
        grid=(bsz, t // tm),
        in_specs=[pl.BlockSpec((1, tm, RW_IN), row), *_halo_specs(tm, t, RW_IN, 0),
                  pl.BlockSpec((2, RW_IN), fixed), pl.BlockSpec((2, gd), fixed), pl.BlockSpec((2, gd), fixed),
                  pl.BlockSpec((1, gd), fixed), pl.BlockSpec((1, gd), fixed), pl.BlockSpec((1, gd), fixed),
                  pl.BlockSpec(lora_w.shape, fixed)],
        out_specs=[pl.BlockSpec((1, tm, gd), row)] * 11,
        out_shape=[out] * 11,
        compiler_params=_params("parallel", "parallel"),
        name="rwkv_prep",
    )(p, p, p, mu, w0, a0, vec(k_k), vec(k_a), vec(r_k), lora_w)


def _rwkv_mixer(p, n_ctx, mu, w0, w_up, a0, a_up, g_up, k_k, k_a, r_k, gn_g, gn_b):
    bsz, t, _ = p.shape
    gd = GROUP_DIM
    n_lora = 4 * RW_LORA + RW_G_LORA
    lora_w = jnp.zeros((n_lora, 5 * gd), F32)
    for j, blk in enumerate((w_up[0], w_up[1], a_up[0], a_up[1])):
        lora_w = lora_w.at[j * RW_LORA:(j + 1) * RW_LORA, j * gd:(j + 1) * gd].set(blk)
    lora_w = lora_w.at[4 * RW_LORA:, 4 * gd:].set(g_up)
    r, v, kk, kd0, kd1, a_f, a_b, lw_f, lw_b, g, bonus = rwkv_prep(p, mu, w0, a0, k_k, k_a, r_k,
                                                                   lora_w.astype(BF16), n_ctx=n_ctx)
    y_fwd, y_bwd = rwkv_scan(r, v, kk, (kd0, kd1), (a_f, a_b), (lw_f, lw_b), n_ctx=n_ctx)
    y = (y_fwd + y_bwd).reshape(bsz, t, N_HEADS, HEAD_DIM)
    m = jnp.mean(y, -1, keepdims=True)
    var = jnp.mean(jnp.square(y - m), -1, keepdims=True)
    y = (y - m) * lax.rsqrt(var + RW_GN_EPS) * gn_g.reshape(N_HEADS, HEAD_DIM) + gn_b.reshape(N_HEADS, HEAD_DIM)
    return (y.reshape(bsz, t, gd) + bonus) * g


def _mamba_mixer(z, xbc, dt_raw, n_ctx, conv_w, conv_b, a_log, dt_bias, d_skip, norm_g):
    bsz, t, _ = z.shape
    conv = conv_b + sum(conv_w[j] * _seg_shift(xbc, j - MB_CONV // 2, n_ctx) for j in range(MB_CONV))
    xbc = jax.nn.silu(conv)
    gd = GROUP_DIM
    a_neg = -jnp.exp(a_log)
    y = (xbc[..., :gd].reshape(bsz, t, N_HEADS, HEAD_DIM) * d_skip[:, None]).reshape(bsz, t, gd)
    for d in range(2):
        dt = jax.nn.softplus(dt_raw[..., d * N_HEADS:(d + 1) * N_HEADS] + dt_bias[d])
        y = y + ssd_scan(xbc, dt, dt * a_neg[d], n_ctx=n_ctx, reverse=bool(d))
    y = y * jax.nn.silu(z)
    y = y.reshape(bsz, t, MB_GROUPS, gd // MB_GROUPS)
    y = y * lax.rsqrt(jnp.mean(y * y, -1, keepdims=True) + RMS_EPS)
    return y.reshape(bsz, t, gd) * norm_g


def _mla_mixer(q_lora, kv_lora, k_pe, n_ctx, q_norm, w_uq, kv_norm, w_ukv, tile, tables):
    bsz, t, _ = q_lora.shape
    n_h, pad = N_HEADS, MLA_QK_PAD - MLA_QK
    w_q = jnp.pad(w_uq.reshape(-1, n_h, MLA_QK), ((0, 0), (0, 0), (0, pad))).reshape(-1, n_h * MLA_QK_PAD)
    w_kv = w_ukv.reshape(-1, n_h, MLA_NOPE + HEAD_DIM)
    w_k = jnp.pad(w_kv[..., :MLA_NOPE], ((0, 0), (0, 0), (0, MLA_QK_PAD - MLA_NOPE))).reshape(-1, n_h * MLA_QK_PAD)
    w_v = w_kv[..., MLA_NOPE:].reshape(-1, n_h * HEAD_DIM)
    q = matmul(_rms_norm(q_lora, q_norm).reshape(bsz * t, -1).astype(BF16), w_q.astype(BF16))
    kv = matmul(_rms_norm(kv_lora, kv_norm).reshape(bsz * t, -1).astype(BF16),
                jnp.concatenate([w_k, w_v], axis=1).astype(BF16)).reshape(bsz, t, -1)
    pe = jnp.pad(k_pe, ((0, 0), (0, 0), (MLA_NOPE, pad)))
    k = kv[..., :n_h * MLA_QK_PAD] + jnp.tile(pe, (1, 1, n_h))
    q = _rope_heads(q.reshape(bsz, t, -1), tables, MLA_QK ** -0.5)
    k = _rope_heads(k, tables, 1.0)
    return attention(q, k, kv[..., n_h * MLA_QK_PAD:].astype(BF16), n_ctx=n_ctx, tile=tile)


def _first_max(vals, excluded):
    live = [jnp.where(x, -jnp.inf, v) for v, x in zip(vals, excluded)]
    top = functools.reduce(jnp.maximum, live)
    found = jnp.zeros_like(top, dtype=jnp.bool_)
    first = []
    for v, x in zip(live, excluded):
        hit = (v == top) & ~found & ~x
        first.append(hit)
        found = found | hit
    return top, first


def _router_body(x_ref, sc_ref, sh_ref, rw_ref, rb_ref, h_ref, gate_ref):
    h = x_ref[...] * (1.0 + sc_ref[0]) + sh_ref[0]
    h_ref[...] = h.astype(BF16)
    logits = jnp.dot(h, rw_ref[...], preferred_element_type=F32, precision=HIGHEST)
    scores = jax.nn.sigmoid(logits.T[:N_EXPERTS])
    biased = scores + rb_ref[...]
    rows = [biased[e:e + 1] for e in range(N_EXPERTS)]
    never = jnp.zeros_like(rows[0], dtype=jnp.bool_)
    group_scores, picked = [], []
    for g in range(N_EXPERT_GROUPS):
        vals = rows[g * EXPERTS_PER_GROUP:(g + 1) * EXPERTS_PER_GROUP]
        top1, first = _first_max(vals, [never] * EXPERTS_PER_GROUP)
        top2, second = _first_max(vals, first)
        group_scores.append(top1 + top2)
        picked.append([a | b for a, b in zip(first, second)])
    _, group_sel = _first_max(group_scores, [never] * N_EXPERT_GROUPS)
    chosen = [jnp.where(group_sel[e // EXPERTS_PER_GROUP] & picked[e // EXPERTS_PER_GROUP][e % EXPERTS_PER_GROUP],
                        scores[e:e + 1], 0.0) for e in range(N_EXPERTS)]
    denom = functools.reduce(jnp.add, chosen)
    gate_ref[...] = jnp.concatenate([w / denom * ROUTED_SCALE for w in chosen], axis=0)


def router(xs, mod_l, router_w, router_b, *, tokens_per_batch, n_ctx, tile):
    m, d = xs.shape
    bsz = m // tokens_per_batch
    tiles_per_batch, n_ctx_tiles = tokens_per_batch // tile, n_ctx // tile

    def mod_row(i):
        return jnp.where(i % tiles_per_batch < n_ctx_tiles, bsz, i // tiles_per_batch)

    rw_pad = jnp.pad(router_w, ((0, 0), (0, LANE - N_EXPERTS)))
    return pl.pallas_call(
        _router_body,
        grid=(m // tile,),
        in_specs=[pl.BlockSpec((tile, d), lambda i: (i, 0)),
                  pl.BlockSpec((None, 1, d), lambda i: (mod_row(i), 0, 4)),
                  pl.BlockSpec((None, 1, d), lambda i: (mod_row(i), 0, 3)),
                  pl.BlockSpec((d, LANE), lambda i: (0, 0)),
                  pl.BlockSpec((N_EXPERTS, 1), lambda i: (0, 0))],
        out_specs=[pl.BlockSpec((tile, d), lambda i: (i, 0)),
                   pl.BlockSpec((N_EXPERTS, tile), lambda i: (0, i))],
        out_shape=[jax.ShapeDtypeStruct((m, d), BF16), jax.ShapeDtypeStruct((N_EXPERTS, m), F32)],
        compiler_params=_params("parallel"),
        name="router",
    )(xs, mod_l, mod_l, rw_pad, router_b.reshape(N_EXPERTS, 1))


def _pad_w_in(w):
    d = w.shape[0]
    dt_end = RW_IN + MB_IN
    pe_end = dt_end + MLA_IN
    return jnp.concatenate([w[:, :dt_end], jnp.zeros((d, LANE - 2 * N_HEADS), w.dtype),
                            w[:, dt_end:pe_end], jnp.zeros((d, LANE - MLA_ROPE), w.dtype),
                            w[:, pe_end:]], axis=1)


def kernel(x, c, ctx, c_ctx, ada_w, ada_b, w_in, w_out, ln1_g, ln1_b, ln2_g, ln2_b, rw_mu, rw_w0, rw_w_up, rw_a0, rw_a_up, rw_g_up, rw_k_k, rw_k_a, rw_r_k, rw_gn_g, rw_gn_b, mb_conv_w, mb_conv_b, mb_a_log, mb_dt_bias, mb_d, mb_norm_g, mla_q_norm, mla_w_uq, mla_kv_norm, mla_w_ukv, na_rpb, router_w, router_b, exp_w_gate, exp_w_up, exp_w_down):
    bsz, seq, d = x.shape
    n_ctx = ctx.shape[1]
    t = n_ctx + seq
    depth = ada_w.shape[0]
    tile = min(256, n_ctx)
    cond = jnp.concatenate([jax.nn.silu(c), jax.nn.silu(c_ctx)[None]], axis=0)
    mod = modulation(cond, ada_w, ada_b)
    xs = jnp.concatenate([ctx, x], axis=1)
    expert_w = [w.astype(BF16) for w in (exp_w_gate, exp_w_up, exp_w_down)]
    rope_tables = _rope_tables(t, n_ctx)
    m = bsz * t
    for l in range(depth):
        mod_l = mod[l][:, None, :]
        p = proj_in(xs.reshape(m, d), mod_l, _pad_w_in(w_in[l]).astype(BF16),
                    tokens_per_batch=t, n_ctx=n_ctx).reshape(bsz, t, N_IN_PAD)
        o_rw = _rwkv_mixer(p, n_ctx, rw_mu[l], rw_w0[l], rw_w_up[l], rw_a0[l], rw_a_up[l],
                           rw_g_up[l], rw_k_k[l], rw_k_a[l], rw_r_k[l], rw_gn_g[l], rw_gn_b[l])
        o_mb = _mamba_mixer(p[..., OFF_MB_Z:OFF_MB_XBC], p[..., OFF_MB_XBC:OFF_MB_DT],
                            p[..., OFF_MB_DT:OFF_MB_DT + 2 * N_HEADS], n_ctx, mb_conv_w[l], mb_conv_b[l],
                            mb_a_log[l], mb_dt_bias[l], mb_d[l], mb_norm_g[l])
        o_mla = _mla_mixer(p[..., OFF_MLA_Q:OFF_MLA_KV], p[..., OFF_MLA_KV:OFF_MLA_PE],
                           p[..., OFF_MLA_PE:OFF_MLA_PE + MLA_ROPE], n_ctx, mla_q_norm[l], mla_w_uq[l],
                           mla_kv_norm[l], mla_w_ukv[l], tile, rope_tables)
        o_na = neighborhood_attention(p, na_bias_table(na_rpb[l]), n_ctx=n_ctx, col0=OFF_NA)
        xs2 = proj_out([o.reshape(m, GROUP_DIM) for o in (o_rw, o_mb, o_mla, o_na)], w_out[l].astype(BF16),
                       xs.reshape(m, d), mod_l, ln1_g[l], ln1_b[l], tokens_per_batch=t, n_ctx=n_ctx)
        h, gate = router(xs2, mod_l, router_w, router_b, tokens_per_batch=t, n_ctx=n_ctx, tile=tile)
        xs = moe_experts(h, gate[:, :, None], *expert_w, l, xs2, mod_l, ln2_g[l], ln2_b[l],
                         tokens_per_batch=t, n_ctx=n_ctx).reshape(bsz, t, d)
    return xs[:, n_ctx:]
```

```python
import functools
import math

import numpy as np
import jax
import jax.numpy as jnp
from jax import lax
from jax.experimental import pallas as pl
from jax.experimental.pallas import tpu as pltpu

F32 = jnp.float32
BF16 = jnp.bfloat16
HIGHEST = lax.Precision.HIGHEST

D_MODEL = 2048
DEPTH = 4
GRID_W = 64
GROUP_DIM = D_MODEL // 4
HEAD_DIM = 64
N_HEADS = GROUP_DIM // HEAD_DIM

RW_LORA = 64
RW_G_LORA = 128
RW_DECAY_SCALE = 0.606531
RW_GN_EPS = 64e-5
RW_IN = 3 * GROUP_DIM + 4 * RW_LORA + RW_G_LORA
RW_CHUNK = 64

MB_GROUPS = 2
MB_STATE = 128
MB_CONV = 5
MB_CHUNK = 128
MB_CONV_DIM = GROUP_DIM + 2 * MB_GROUPS * MB_STATE
MB_IN = GROUP_DIM + MB_CONV_DIM + 2 * N_HEADS

MLA_Q_LORA = 3 * D_MODEL // 16
MLA_KV_LORA = D_MODEL // 16
MLA_NOPE = 64
MLA_ROPE = 32
MLA_QK = MLA_NOPE + MLA_ROPE
MLA_QK_PAD = 128
MLA_IN = MLA_Q_LORA + MLA_KV_LORA + MLA_ROPE
ROPE_BASE = 10000.0

NA_WIN_ROWS = 8
NA_WIN_COLS = 16
NA_IN = 3 * GROUP_DIM
NA_MASK = -1e30

N_EXPERTS = 16
N_EXPERT_GROUPS = 4
EXPERTS_PER_GROUP = N_EXPERTS // N_EXPERT_GROUPS
TOP_K = 2
D_EXPERT = D_MODEL // 4
ROUTED_SCALE = 2.5

DEEPNORM_ALPHA = (2 * DEPTH) ** 0.25
LN_EPS = 1e-6
RMS_EPS = 1e-6

LANE = 128
OFF_RW = 0
OFF_MLA_Q = OFF_RW + RW_IN
OFF_MLA_KV = OFF_MLA_Q + MLA_Q_LORA
OFF_MLA_PE = OFF_MLA_KV + MLA_KV_LORA
PE_LANE = MLA_NOPE
OFF_MB_Z = OFF_MLA_PE + LANE
OFF_MB_XBC = OFF_MB_Z + GROUP_DIM
OFF_MB_DT = OFF_MB_XBC + MB_CONV_DIM
OFF_NA = OFF_MB_DT + LANE
N_IN_PAD = OFF_NA + NA_IN
assert OFF_MLA_Q % MLA_Q_LORA == 0 and OFF_MB_XBC % MB_CONV_DIM == 0 and OFF_MB_Z % GROUP_DIM == 0

VMEM_LIMIT = 56 * 1024 * 1024


def _params(*sem):
    return pltpu.CompilerParams(dimension_semantics=sem, vmem_limit_bytes=VMEM_LIMIT)


def _pick_tile(n, target, quantum):
    best = None
    for t in range(quantum, min(n, target) + 1, quantum):
        if n % t == 0:
            best = t
    assert best is not None, (n, target, quantum)
    return best


def _mm_body(x_ref, w_ref, o_ref, *, precision):
    o_ref[...] = jnp.dot(x_ref[...], w_ref[...], preferred_element_type=F32,
                         precision=precision).astype(o_ref.dtype)


def matmul(x, w, *, tm_target=512, tn_target=1024, precision=None, out_dtype=F32):
    m, k = x.shape
    n = w.shape[1]
    tm = _pick_tile(m, tm_target, 16)
    tn = _pick_tile(n, tn_target, LANE)
    return pl.pallas_call(
        functools.partial(_mm_body, precision=precision),
        grid=(m // tm, n // tn),
        in_specs=[pl.BlockSpec((tm, k), lambda i, j: (i, 0)),
                  pl.BlockSpec((k, tn), lambda i, j: (0, j))],
        out_specs=pl.BlockSpec((tm, tn), lambda i, j: (i, j)),
        out_shape=jax.ShapeDtypeStruct((m, n), out_dtype),
        compiler_params=_params("parallel", "parallel"),
        name="matmul",
    )(x, w)


def _is_ctx_row(tm, tokens_per_batch, n_ctx):
    pos = (pl.program_id(0) * tm) % tokens_per_batch + lax.broadcasted_iota(jnp.int32, (tm, 1), 0)
    return pos < n_ctx


def _proj_in_body(x_ref, scc_ref, scb_ref, shc_ref, shb_ref, w_ref, o_ref, xm_ref, *, tokens_per_batch, n_ctx):
    is_ctx = _is_ctx_row(x_ref.shape[0], tokens_per_batch, n_ctx)

    @pl.when(pl.program_id(1) == 0)
    def _():
        sc = jnp.where(is_ctx, scc_ref[0], scb_ref[0])
        sh = jnp.where(is_ctx, shc_ref[0], shb_ref[0])
        xm_ref[...] = (x_ref[...] * (1.0 + sc) + sh).astype(BF16)

    o_ref[...] = jnp.dot(xm_ref[...], w_ref[...], preferred_element_type=F32)


def _mod_specs(bsz, tiles_per_batch, chunk, d):
    return [pl.BlockSpec((None, 1, d), lambda i, *_: (bsz, 0, chunk)),
            pl.BlockSpec((None, 1, d), lambda i, *_: (i // tiles_per_batch, 0, chunk))]


def proj_in(xs, mod_l, w, *, tokens_per_batch, n_ctx):
    m, d = xs.shape
    n = w.shape[1]
    bsz = m // tokens_per_batch
    tm = _pick_tile(tokens_per_batch, 640, 16)
    tn = _pick_tile(n, 1152, LANE)
    tiles_per_batch = tokens_per_batch // tm
    return pl.pallas_call(
        functools.partial(_proj_in_body, tokens_per_batch=tokens_per_batch, n_ctx=n_ctx),
        grid=(m // tm, n // tn),
        in_specs=[pl.BlockSpec((tm, d), lambda i, j: (i, 0)),
                  *_mod_specs(bsz, tiles_per_batch, 1, d), *_mod_specs(bsz, tiles_per_batch, 0, d),
                  pl.BlockSpec((d, tn), lambda i, j: (0, j))],
        out_specs=pl.BlockSpec((tm, tn), lambda i, j: (i, j)),
        out_shape=jax.ShapeDtypeStruct((m, n), F32),
        scratch_shapes=[pltpu.VMEM((tm, d), BF16)],
        compiler_params=_params("parallel", "arbitrary"),
        name="proj_in",
    )(xs, mod_l, mod_l, mod_l, mod_l, w)


def _proj_out_body(m0_ref, m1_ref, m2_ref, m3_ref, w_ref, x_ref, gc_ref, gb_ref, lg_ref, lb_ref, o_ref,
                   *, tokens_per_batch, n_ctx):
    gd = m0_ref.shape[1]
    y = None
    for j, m_ref in enumerate((m0_ref, m1_ref, m2_ref, m3_ref)):
        part = jnp.dot(m_ref[...].astype(BF16), w_ref[j * gd:(j + 1) * gd, :], preferred_element_type=F32)
        y = part if y is None else y + part
    gate = jnp.where(_is_ctx_row(x_ref.shape[0], tokens_per_batch, n_ctx), gc_ref[0], gb_ref[0])
    z = DEEPNORM_ALPHA * x_ref[...] + gate * y
    mu = jnp.mean(z, axis=-1, keepdims=True)
    zc = z - mu
    var = jnp.mean(zc * zc, axis=-1, keepdims=True)
    o_ref[...] = zc * lax.rsqrt(var + LN_EPS) * lg_ref[...] + lb_ref[...]


def proj_out(mixes, w, xs, mod_l, ln_g, ln_b, *, tokens_per_batch, n_ctx):
    m, d = xs.shape
    gd = mixes[0].shape[1]
    bsz = m // tokens_per_batch
    tm = _pick_tile(tokens_per_batch, 320, 16)
    tiles_per_batch = tokens_per_batch // tm
    row = lambda i: (i, 0)
    fixed = lambda i: (0, 0)
    return pl.pallas_call(
        functools.partial(_proj_out_body, tokens_per_batch=tokens_per_batch, n_ctx=n_ctx),
        grid=(m // tm,),
        in_specs=[pl.BlockSpec((tm, gd), row)] * 4
        + [pl.BlockSpec((d, d), fixed), pl.BlockSpec((tm, d), row), *_mod_specs(bsz, tiles_per_batch, 2, d),
           pl.BlockSpec((1, d), fixed), pl.BlockSpec((1, d), fixed)],
        out_specs=pl.BlockSpec((tm, d), row),
        out_shape=jax.ShapeDtypeStruct((m, d), F32),
        compiler_params=_params("parallel"),
        name="proj_out",
    )(*mixes, w, xs, mod_l, mod_l, ln_g.reshape(1, d), ln_b.reshape(1, d))


def _mod_body(c_ref, w_ref, b_ref, o_ref, *, n_rows):
    w = w_ref[...]
    reps = w.shape[1] // LANE
    o_ref[...] = jnp.zeros_like(o_ref)
    for m in range(n_rows):
        cb = c_ref[m]
        cbt = jnp.concatenate([cb] * reps, axis=1)
        o_ref[m:m + 1, :] = jnp.sum(w * cbt, axis=0, keepdims=True) + b_ref[...]


def modulation(cond, ada_w, ada_b):
    n_rows, k = cond.shape
    depth, _, n = ada_w.shape
    tn = 512
    cond_b = jnp.broadcast_to(cond[:, :, None], (n_rows, k, LANE))
    return pl.pallas_call(
        functools.partial(_mod_body, n_rows=n_rows),
        grid=(depth, n // tn),
        in_specs=[pl.BlockSpec((n_rows, k, LANE), lambda l, j: (0, 0, 0)),
                  pl.BlockSpec((None, k, tn), lambda l, j: (l, 0, j)),
                  pl.BlockSpec((None, 1, tn), lambda l, j: (l, 0, j))],
        out_specs=pl.BlockSpec((None, 8, tn), lambda l, j: (l, 0, j)),
        out_shape=jax.ShapeDtypeStruct((depth, 8, n), F32),
        compiler_params=_params("parallel", "parallel"),
        name="modulation",
    )(cond_b, ada_w, ada_b.reshape(depth, 1, n))


def _bmm(a, b, spec, mode):
    if mode == 'f32':
        return jnp.einsum(spec, a, b, preferred_element_type=F32, precision=HIGHEST)
    a_hi, b_hi = a.astype(BF16), b.astype(BF16)
    out = jnp.einsum(spec, a_hi, b_hi, preferred_element_type=F32)
    if mode == 'x3':
        a_lo = (a - a_hi.astype(F32)).astype(BF16)
        b_lo = (b - b_hi.astype(F32)).astype(BF16)
        out = out + (jnp.einsum(spec, a_hi, b_lo, preferred_element_type=F32)
                     + jnp.einsum(spec, a_lo, b_hi, preferred_element_type=F32))
    return out


def _unit_triangular_inverse(l_mat, ti, si):
    c = l_mat.shape[1]
    nn = 'hts,hsn->htn'
    blk = 8
    eye = (ti == si).astype(F32)
    l_d = jnp.where((ti // blk) == (si // blk), l_mat, 0.0)
    p2 = _bmm(l_d, l_d, nn, 'bf16')
    p4 = _bmm(p2, p2, nn, 'bf16')
    inv = _bmm(eye - l_d, eye + p2, nn, 'bf16')
    inv = _bmm(inv, eye + p4, nn, 'bf16')
    while blk < c:
        pair = ((ti // (2 * blk)) == (si // (2 * blk))) & ((ti // blk) != (si // blk))
        off = jnp.where(pair, l_mat, 0.0)
        inv = inv - _bmm(_bmm(inv, off, nn, 'bf16'), inv, nn, 'bf16')
        blk *= 2
    return inv


def _first_head_lanes(shape):
    return lax.broadcasted_iota(jnp.int32, shape, len(shape) - 1) % LANE < HEAD_DIM


def _rwkv_body(*refs):
    fwd_in, bwd_in, (yf_ref, yb_ref, st_ref) = refs[:6], refs[6:12], refs[12:]

    @pl.when(pl.program_id(0) == 0)
    def _():
        st_ref[...] = jnp.zeros_like(st_ref)

    bsz, c, width = fwd_in[0].shape
    n_pairs = width // LANE
    ti = lax.broadcasted_iota(jnp.int32, (c, c), 0)
    si = lax.broadcasted_iota(jnp.int32, (c, c), 1)
    units = []
    for in_refs, reverse in ((fwd_in, False), (bwd_in, True)):
        tri = ((si >= ti) if reverse else (si <= ti)).astype(F32)
        for row in range(bsz):
            r, k, v, kk, a, lw = (ref[row] for ref in in_refs)
            cum = _bmm(tri, lw, 'ts,sn->tn', 'f32')
            tot = cum[:1] if reverse else cum[c - 1:]
            b = kk * a
            e_neg = jnp.exp(-cum)
            e_rem = jnp.exp(tot - cum)
            pieces = (r * jnp.exp(cum), kk * jnp.exp(cum - lw), k * e_neg, b * e_neg, k * e_rem, b * e_rem, v,
                      jnp.exp(tot))
            for j in range(n_pairs):
                units.append([m[:, j * LANE:(j + 1) * LANE] for m in pieces])
    n_units = len(units)
    rq, kq, kd, bd, kdc, bdc, v, e_tot = (jnp.stack([u[i] for u in units], axis=0) for i in range(8))

    backward = lax.broadcasted_iota(jnp.int32, (n_units, 1, 1), 0) >= n_units // 2
    order = (si - ti) * jnp.where(backward, -1, 1)
    incl = order <= 0
    strict = order < 0
    first = _first_head_lanes((c, LANE))
    first2 = _first_head_lanes((c, 2 * LANE))
    nt, nn, tn = 'utn,usn->uts', 'uts,usn->utn', 'usn,usm->unm'
    qq = jnp.concatenate([jnp.where(first, rq, 0.0), jnp.where(first, kq, 0.0),
                          jnp.where(first, 0.0, rq), jnp.where(first, 0.0, kq)], axis=1)
    ak = _bmm(qq, kd, nt, 'bf16')
    ab = _bmm(qq, bd, nt, 'bf16')
    a_rk = [jnp.where(incl, ak[:, 2 * x * c:(2 * x + 1) * c], 0.0) for x in range(2)]
    a_kk = [jnp.where(strict, ak[:, (2 * x + 1) * c:(2 * x + 2) * c], 0.0) for x in range(2)]
    a_rb = [jnp.where(incl, ab[:, 2 * x * c:(2 * x + 1) * c], 0.0) for x in range(2)]
    a_kb = [jnp.where(strict, ab[:, (2 * x + 1) * c:(2 * x + 2) * c], 0.0) for x in range(2)]
    inv = _unit_triangular_inverse(jnp.concatenate(a_kb, axis=0), ti, si)
    inv = [inv[:n_units], inv[n_units:]]

    xs = [_bmm(inv[x], jnp.concatenate([_bmm(a_kk[x], v, nn, 'bf16'), kq], axis=2), nn, 'bf16') for x in range(2)]
    x2 = jnp.where(first2, xs[0], xs[1])
    arx = jnp.where(first2, _bmm(a_rb[0], x2, nn, 'bf16'), _bmm(a_rb[1], x2, nn, 'bf16'))
    y0 = jnp.where(first, _bmm(a_rk[0], v, nn, 'bf16'), _bmm(a_rk[1], v, nn, 'bf16')) - arx[:, :, :LANE]
    rqp = rq - arx[:, :, LANE:]
    row_n = lax.broadcasted_iota(jnp.int32, (LANE, LANE), 0)
    col_n = lax.broadcasted_iota(jnp.int32, (LANE, LANE), 1)
    same_head = (row_n // HEAD_DIM) == (col_n // HEAD_DIM)
    bx = _bmm(bdc, x2, tn, 'bf16')
    tadd = jnp.where(same_head, _bmm(kdc, v, tn, 'bf16') - bx[:, :, :LANE], 0.0)
    decay = jnp.where(row_n == col_n, jnp.broadcast_to(e_tot, (n_units, LANE, LANE)), 0.0)
    p = jnp.where(same_head, decay - bx[:, :, LANE:], 0.0)
    t0 = st_ref[...]
    y = _bmm(rqp, t0, 'utn,unm->utm', 'x3') + y0
    st_ref[...] = _bmm(p, t0, 'ujn,unm->ujm', 'x3') + tadd
    for d, y_ref in enumerate((yf_ref, yb_ref)):
        for row in range(bsz):
            u0 = (d * bsz + row) * n_pairs
            y_ref[row] = jnp.concatenate([y[u0 + j] for j in range(n_pairs)], axis=1)


def _scan_chunk_index(c, n_ctx_chunks, n_chunks, reverse):
    if not reverse:
        return c
    return jnp.where(c < n_ctx_chunks, n_ctx_chunks - 1 - c, n_chunks - 1 - (c - n_ctx_chunks))


def rwkv_scan(r, v, kk, k_dirs, a_dirs, lw_dirs, *, n_ctx):
    bsz, t, width = r.shape
    c = RW_CHUNK
    n_chunks, n_ctx_chunks = t // c, n_ctx // c
    specs = [pl.BlockSpec((bsz, c, width),
                          lambda i, rev=rev: (0, _scan_chunk_index(i, n_ctx_chunks, n_chunks, rev), 0))
             for rev in (False, True)]
    n_units = 2 * bsz * (width // LANE)
    args = [(r, k_dirs[d], v, kk, a_dirs[d], lw_dirs[d]) for d in range(2)]
    return pl.pallas_call(
        _rwkv_body,
        grid=(n_chunks,),
        in_specs=[specs[0]] * 6 + [specs[1]] * 6,
        out_specs=specs,
        out_shape=[jax.ShapeDtypeStruct((bsz, t, width), F32)] * 2,
        scratch_shapes=[pltpu.VMEM((n_units, LANE, LANE), F32)],
        compiler_params=_params("arbitrary"),
        name="rwkv_scan",
    )(*args[0], *args[1])


def _ssd_body(x_ref, dt_ref, dac_ref, dar_ref, y_ref, st_ref, *, reverse):
    @pl.when(pl.program_id(1) == 0)
    def _():
        st_ref[...] = jnp.zeros_like(st_ref)

    xbc = x_ref[0]
    dt = dt_ref[0]
    c = xbc.shape[0]
    n_h = dt.shape[1]
    n_pairs = n_h // 2
    gd = n_h * HEAD_DIM
    n_g = (xbc.shape[1] - gd) // (2 * MB_STATE)
    pairs_per_group = n_pairs // n_g
    ti = lax.broadcasted_iota(jnp.int32, (c, c), 0)
    si = lax.broadcasted_iota(jnp.int32, (c, c), 1)
    incl = (si >= ti) if reverse else (si <= ti)
    incl_t = (ti >= si) if reverse else (ti <= si)
    cs_c = jnp.dot(incl.astype(F32), dac_ref[0], preferred_element_type=F32, precision=HIGHEST)
    cs_r = jnp.dot(dar_ref[0], incl_t.astype(F32), preferred_element_type=F32, precision=HIGHEST)
    tot = cs_c[:1, :] if reverse else cs_c[c - 1:, :]
    e_cs = jnp.exp(cs_c)
    e_rem = jnp.exp(tot - cs_c)
    e_tot = jnp.exp(tot)
    bm = [xbc[:, gd + g * MB_STATE:gd + (g + 1) * MB_STATE].astype(BF16) for g in range(n_g)]
    cm = [xbc[:, gd + (n_g + g) * MB_STATE:gd + (n_g + g + 1) * MB_STATE].astype(BF16) for g in range(n_g)]
    gmat = [jnp.einsum('ln,sn->ls', cm[g], bm[g], preferred_element_type=F32) for g in range(n_g)]
    first = _first_head_lanes((c, LANE))
    first_rows = lax.broadcasted_iota(jnp.int32, (LANE, 1), 0) < HEAD_DIM
    pick = lambda m, ha: jnp.where(first, m[:, ha:ha + 1], m[:, ha + 1:ha + 2])
    for j in range(n_pairs):
        g = j // pairs_per_group
        ha = 2 * j
        sl = slice(j * LANE, (j + 1) * LANE)
        xdt = xbc[:, sl] * pick(dt, ha)
        xdt_b = xdt.astype(BF16)
        y_heads = []
        for h in (ha, ha + 1):
            seg = cs_c[:, h:h + 1] - cs_r[h:h + 1, :]
            lmat = jnp.exp(jnp.where(incl, seg, NA_MASK))
            y_heads.append(jnp.dot((gmat[g] * lmat).astype(BF16), xdt_b, preferred_element_type=F32))
        st = st_ref[j]
        y_off = jnp.einsum('ln,pn->lp', cm[g], st.astype(BF16), preferred_element_type=F32) * pick(e_cs, ha)
        y_ref[0, :, sl] = jnp.where(first, y_heads[0], y_heads[1]) + y_off
        xdec = (xdt * pick(e_rem, ha)).astype(BF16)
        keep = jnp.where(first_rows, e_tot[:, ha:ha + 1], e_tot[:, ha + 1:ha + 2])
        st_ref[j] = keep * st + jnp.einsum('lp,ln->pn', xdec, bm[g], preferred_element_type=F32)


def ssd_scan(xbc, dt, da, *, n_ctx, reverse):
    bsz, t, width = xbc.shape
    n_h = dt.shape[2]
    gd = n_h * HEAD_DIM
    c = MB_CHUNK
    n_chunks, n_ctx_chunks = t // c, n_ctx // c
    idx = lambda i: _scan_chunk_index(i, n_ctx_chunks, n_chunks, reverse)
    da_row = jnp.swapaxes(da, 1, 2)
    return pl.pallas_call(
        functools.partial(_ssd_body, reverse=reverse),
        grid=(bsz, n_chunks),
        in_specs=[pl.BlockSpec((1, c, width), lambda b, i: (b, idx(i), 0)),
                  pl.BlockSpec((1, c, n_h), lambda b, i: (b, idx(i), 0)),
                  pl.BlockSpec((1, c, n_h), lambda b, i: (b, idx(i), 0)),
                  pl.BlockSpec((1, n_h, c), lambda b, i: (b, 0, idx(i)))],
        out_specs=pl.BlockSpec((1, c, gd), lambda b, i: (b, idx(i), 0)),
        out_shape=jax.ShapeDtypeStruct((bsz, t, gd), F32),
        scratch_shapes=[pltpu.VMEM((n_h // 2, 2 * HEAD_DIM, MB_STATE), F32)],
        compiler_params=_params("parallel", "arbitrary"),
        name="ssd_scan_bwd" if reverse else "ssd_scan_fwd",
    )(xbc, dt, da, da_row)


def _softmax_pv(s_parts, v_parts):
    m = functools.reduce(jnp.maximum, [jnp.max(s, axis=-1, keepdims=True) for s in s_parts])
    ps = [jnp.exp(s - m) for s in s_parts]
    l = functools.reduce(jnp.add, [jnp.sum(p, axis=-1, keepdims=True) for p in ps])
    o = functools.reduce(jnp.add, [jnp.dot(p.astype(BF16), v, preferred_element_type=F32)
                                   for p, v in zip(ps, v_parts)])
    return o / l


def _rope_lanes(x, cos_tab, sin_tab):
    width = x.shape[1]
    reps = width // MLA_QK_PAD
    cos_w = jnp.concatenate([cos_tab] * reps, axis=1) if reps > 1 else cos_tab
    sin_w = jnp.concatenate([sin_tab] * reps, axis=1) if reps > 1 else sin_tab
    lane = lax.broadcasted_iota(jnp.int32, x.shape, 1) % MLA_QK_PAD
    first_half = (lane - MLA_NOPE) % (MLA_ROPE // 2) < MLA_ROPE // 4
    shift = MLA_ROPE // 4
    partner = jnp.where(first_half, pltpu.roll(x, width - shift, axis=1), pltpu.roll(x, shift, axis=1))
    return x * cos_w + partner * sin_w


def _attn_body(q_ref, k_ref, v_ref, pe_ref, cq_ref, sq_ref, ck_ref, sk_ref, o_ref, kb_ref, vb_ref,
               *, n_ctx, n_ctx_tiles):
    dq = q_ref.shape[2] // 2

    @pl.when(pl.program_id(2) == 0)
    def _():
        pe_rot = _rope_lanes(pe_ref[0], ck_ref[...], sk_ref[...])
        kb_ref[...] = (k_ref[0] + jnp.concatenate([pe_rot, pe_rot], axis=1)).astype(BF16)
        vb_ref[...] = v_ref[0].astype(BF16)

    q2 = (_rope_lanes(q_ref[0], cq_ref[...], sq_ref[...]) * (MLA_QK ** -0.5)).astype(BF16)

    def attend(n_keys):
        v2 = vb_ref[0:n_keys]
        own = _first_head_lanes(v2.shape)
        outs = []
        for x in range(2):
            s = jnp.einsum('qd,kd->qk', q2[:, x * dq:(x + 1) * dq], kb_ref[0:n_keys, x * dq:(x + 1) * dq],
                           preferred_element_type=F32)
            p = jnp.exp((s - jnp.max(s, axis=-1, keepdims=True)).astype(BF16))
            vx = jnp.where(own if x == 0 else ~own, v2, jnp.ones_like(v2))
            o = jnp.dot(p, vx, preferred_element_type=F32)
            outs.append(o / pltpu.roll(o, HEAD_DIM, axis=1))
        o_ref[0] = jnp.where(_first_head_lanes(outs[0].shape), outs[0], outs[1])

    @pl.when(pl.program_id(2) < n_ctx_tiles)
    def _():
        attend(n_ctx)

    @pl.when(pl.program_id(2) >= n_ctx_tiles)
    def _():
        attend(k_ref.shape[1])


def attention(q, kv, p, tables, *, n_ctx, tile):
    bsz, t, _ = q.shape
    n_pairs = N_HEADS // 2
    dq2 = 2 * MLA_QK_PAD
    v_col0 = N_HEADS * MLA_QK_PAD // LANE
    cos_tab, sin_tab = tables
    q_tab = pl.BlockSpec((tile, MLA_QK_PAD), lambda b, j, i: (i, 0))
    k_tab = pl.BlockSpec((t, MLA_QK_PAD), lambda b, j, i: (0, 0))
    return pl.pallas_call(
        functools.partial(_attn_body, n_ctx=n_ctx, n_ctx_tiles=n_ctx // tile),
        grid=(bsz, n_pairs, t // tile),
        in_specs=[pl.BlockSpec((1, tile, dq2), lambda b, j, i: (b, i, j)),
                  pl.BlockSpec((1, t, dq2), lambda b, j, i: (b, 0, j)),
                  pl.BlockSpec((1, t, LANE), lambda b, j, i: (b, 0, v_col0 + j)),
                  pl.BlockSpec((1, t, LANE), lambda b, j, i: (b, 0, OFF_MLA_PE // LANE)),
                  q_tab, q_tab, k_tab, k_tab],
        out_specs=pl.BlockSpec((1, tile, LANE), lambda b, j, i: (b, i, j)),
        out_shape=jax.ShapeDtypeStruct((bsz, t, N_HEADS * HEAD_DIM), F32),
        scratch_shapes=[pltpu.VMEM((t, dq2), BF16), pltpu.VMEM((t, LANE), BF16)],
        compiler_params=_params("parallel", "parallel", "arbitrary"),
        name="attention",
    )(q, kv, kv, p, cos_tab, sin_tab, cos_tab, sin_tab)


def _norm_mm_body(x_ref, g_ref, w_ref, o_ref):
    x = x_ref[...]
    xn = x * lax.rsqrt(jnp.mean(x * x, axis=-1, keepdims=True) + RMS_EPS) * g_ref[...]
    o_ref[...] = jnp.dot(xn.astype(BF16), w_ref[...], preferred_element_type=F32)


def norm_matmul(p, col, g, w):
    m = p.shape[0]
    k, n = w.shape
    tm = _pick_tile(m, 640, 16)
    return pl.pallas_call(
        _norm_mm_body,
        grid=(m // tm,),
        in_specs=[pl.BlockSpec((tm, k), lambda i: (i, col // k)),
                  pl.BlockSpec((1, k), lambda i: (0, 0)),
                  pl.BlockSpec((k, n), lambda i: (0, 0))],
        out_specs=pl.BlockSpec((tm, n), lambda i: (i, 0)),
        out_shape=jax.ShapeDtypeStruct((m, n), F32),
        compiler_params=_params("parallel"),
        name="norm_matmul",
    )(p, g.reshape(1, k), w)


def _na_body(q_ref, k_ref, v_ref, bias_ref, o_ref, kb_ref, vb_ref, *, n_ctx, n_rows):
    i = pl.program_id(2)
    rows_per_step = q_ref.shape[1] // GRID_W
    win = NA_WIN_ROWS * GRID_W
    q2 = q_ref[0] * (HEAD_DIM ** -0.5)

    def head_queries(q_rows, x):
        first = _first_head_lanes(q_rows.shape)
        return jnp.where(first if x == 0 else ~first, q_rows, 0.0).astype(BF16)

    def scores(q, k):
        return jnp.einsum('qd,kd->qk', q, k, preferred_element_type=F32)

    @pl.when(i == 0)
    def _():
        kb_ref[...] = k_ref[0].astype(BF16)
        vb_ref[...] = v_ref[0].astype(BF16)
        kc, vc = kb_ref[0:n_ctx], vb_ref[0:n_ctx]
        outs = [_softmax_pv([scores(head_queries(q2, x), kc)], [vc]) for x in range(2)]
        o_ref[0] = jnp.where(_first_head_lanes(outs[0].shape), outs[0], outs[1])

    @pl.when(i > 0)
    def _():
        qs, kws, vws, biases = [], [], [], []
        for rr in range(rows_per_step):
            r = (i - 1) * rows_per_step + rr
            rs = jnp.clip(r - NA_WIN_ROWS // 2, 0, n_rows - NA_WIN_ROWS)
            start = pl.multiple_of(n_ctx + rs * GRID_W, GRID_W)
            kws.append(kb_ref[pl.ds(start, win)])
            vws.append(vb_ref[pl.ds(start, win)])
            q_row = q2[rr * GRID_W:(rr + 1) * GRID_W]
            qs.append(jnp.concatenate([head_queries(q_row, 0), head_queries(q_row, 1)], axis=0))
            biases.append(jnp.concatenate([bias_ref[0, r - rs], bias_ref[1, r - rs]], axis=0))
        qb = jnp.stack(qs, axis=0)
        kc, vc = kb_ref[0:n_ctx], vb_ref[0:n_ctx]
        s_loc = jnp.einsum('uqd,ukd->uqk', qb, jnp.stack(kws, axis=0), preferred_element_type=F32)
        s_loc = s_loc + jnp.stack(biases, axis=0)
        s_ctx = scores(qb.reshape(rows_per_step * 2 * GRID_W, LANE), kc).reshape(rows_per_step, 2 * GRID_W, n_ctx)
        m = jnp.maximum(jnp.max(s_loc, axis=-1, keepdims=True), jnp.max(s_ctx, axis=-1, keepdims=True))
        p_loc = jnp.exp(s_loc - m)
        p_ctx = jnp.exp(s_ctx - m)
        l = jnp.sum(p_loc, axis=-1, keepdims=True) + jnp.sum(p_ctx, axis=-1, keepdims=True)
        o = jnp.einsum('uqk,ukd->uqd', p_loc.astype(BF16), jnp.stack(vws, axis=0), preferred_element_type=F32)
        o = o + jnp.dot(p_ctx.reshape(rows_per_step * 2 * GRID_W, n_ctx).astype(BF16), vc,
                        preferred_element_type=F32).reshape(o.shape)
        o = o / l
        first = _first_head_lanes((GRID_W, LANE))
        for rr in range(rows_per_step):
            o_ref[0, rr * GRID_W:(rr + 1) * GRID_W] = jnp.where(first, o[rr, :GRID_W], o[rr, GRID_W:])


def na_bias_table(rpb):
    wr = NA_WIN_ROWS
    qv = np.arange(GRID_W)[:, None]
    cv = np.arange(GRID_W)[None, :]
    col_start = np.clip(qv - NA_WIN_COLS // 2, 0, GRID_W - NA_WIN_COLS)
    valid = (cv >= col_start) & (cv < col_start + NA_WIN_COLS)
    offs = np.arange(2 * NA_WIN_COLS - 1)[:, None, None]
    pick = ((cv - qv + (NA_WIN_COLS - 1))[None] == offs) & valid[None]
    cols = jnp.einsum('hro,oqc->hrqc', rpb, jnp.asarray(pick, F32), precision=HIGHEST)
    cols = jnp.where(valid, cols, NA_MASK)
    tab = jnp.stack([cols[:, wr - 1 - var:2 * wr - 1 - var] for var in range(wr)], axis=1)
    tab = jnp.transpose(tab, (0, 1, 3, 2, 4))
    return tab.reshape(rpb.shape[0], wr, GRID_W, wr * GRID_W)


def neighborhood_attention(p, bias, *, n_ctx, col0):
    bsz, t, _ = p.shape
    n_rows = (t - n_ctx) // GRID_W
    n_pairs = GROUP_DIM // LANE
    c0 = col0 // LANE
    return pl.pallas_call(
        functools.partial(_na_body, n_ctx=n_ctx, n_rows=n_rows),
        grid=(bsz, n_pairs, t // n_ctx),
        in_specs=[pl.BlockSpec((1, n_ctx, LANE), lambda b, j, i: (b, i, c0 + j)),
                  pl.BlockSpec((1, t, LANE), lambda b, j, i: (b, 0, c0 + n_pairs + j)),
                  pl.BlockSpec((1, t, LANE), lambda b, j, i: (b, 0, c0 + 2 * n_pairs + j)),
                  pl.BlockSpec((2, NA_WIN_ROWS, GRID_W, NA_WIN_ROWS * GRID_W), lambda b, j, i: (j, 0, 0, 0))],
        out_specs=pl.BlockSpec((1, n_ctx, LANE), lambda b, j, i: (b, i, j)),
        out_shape=jax.ShapeDtypeStruct((bsz, t, GROUP_DIM), F32),
        scratch_shapes=[pltpu.VMEM((t, LANE), BF16), pltpu.VMEM((t, LANE), BF16)],
        compiler_params=_params("parallel", "parallel", "arbitrary"),
        name="neighborhood_attention",
    )(p, p, p, bias)


def _moe_body(h_ref, gate_ref, wg_ref, wu_ref, wd_ref, x_ref, gc_ref, gb_ref, lg_ref, lb_ref, o_ref, acc_ref,
              *, tokens_per_batch, n_ctx):
    e = pl.program_id(1)
    is_ctx = _is_ctx_row(x_ref.shape[0], tokens_per_batch, n_ctx)

    @pl.when(e == 0)
    def _():
        acc_ref[...] = jnp.zeros_like(acc_ref)

    h = h_ref[...]
    hid = jax.nn.silu(jnp.dot(h, wg_ref[0], preferred_element_type=F32)) * jnp.dot(h, wu_ref[0], preferred_element_type=F32)
    acc_ref[...] += gate_ref[0] * jnp.dot(hid.astype(BF16), wd_ref[0], preferred_element_type=F32)

    @pl.when(e == pl.num_programs(1) - 1)
    def _():
        z = DEEPNORM_ALPHA * x_ref[...] + jnp.where(is_ctx, gc_ref[0], gb_ref[0]) * acc_ref[...]
        mu = jnp.mean(z, axis=-1, keepdims=True)
        zc = z - mu
        var = jnp.mean(zc * zc, axis=-1, keepdims=True)
        o_ref[...] = zc * lax.rsqrt(var + LN_EPS) * lg_ref[...] + lb_ref[...]


def moe_experts(h, gate, w_gate, w_up, w_down, layer, xs, mod_l, ln_g, ln_b, *, tokens_per_batch, n_ctx):
    m, d = h.shape
    n_e, d_e = w_gate.shape[1], w_gate.shape[3]
    bsz = m // tokens_per_batch
    tm = _pick_tile(tokens_per_batch, 640, 16)
    tiles_per_batch = tokens_per_batch // tm
    row = lambda i, e: (i, 0)
    fixed = lambda i, e: (0, 0)
    return pl.pallas_call(
        functools.partial(_moe_body, tokens_per_batch=tokens_per_batch, n_ctx=n_ctx),
        grid=(m // tm, n_e),
        in_specs=[pl.BlockSpec((tm, d), row),
                  pl.BlockSpec((1, tm, 1), lambda i, e: (e, i, 0)),
                  pl.BlockSpec((None, 1, d, d_e), lambda i, e: (layer, e, 0, 0)),
                  pl.BlockSpec((None, 1, d, d_e), lambda i, e: (layer, e, 0, 0)),
                  pl.BlockSpec((None, 1, d_e, d), lambda i, e: (layer, e, 0, 0)),
                  pl.BlockSpec((tm, d), row), *_mod_specs(bsz, tiles_per_batch, 5, d),
                  pl.BlockSpec((1, d), fixed), pl.BlockSpec((1, d), fixed)],
        out_specs=pl.BlockSpec((tm, d), row),
        out_shape=jax.ShapeDtypeStruct((m, d), F32),
        scratch_shapes=[pltpu.VMEM((tm, d), F32)],
        compiler_params=_params("parallel", "arbitrary"),
        name="moe_experts",
    )(h, gate, w_gate, w_up, w_down, xs, mod_l, mod_l, ln_g.reshape(1, d), ln_b.reshape(1, d))


def _rope_tables(t, n_ctx):
    n_freq = MLA_ROPE // 4
    pos_t = np.arange(t - n_ctx)
    pos = np.stack([pos_t // GRID_W, pos_t % GRID_W], axis=-1).astype(np.float32)
    inv_freq = jnp.asarray(ROPE_BASE, F32) ** (-jnp.arange(n_freq, dtype=F32) / n_freq)
    ang = jnp.asarray(pos)[:, :, None] * inv_freq
    ang = jnp.concatenate([jnp.zeros((n_ctx, 2, n_freq), F32), ang], axis=0)
    cos = jnp.broadcast_to(jnp.cos(ang)[:, :, None, :], (t, 2, 2, n_freq)).reshape(t, MLA_ROPE)
    sin = jnp.sin(ang)[:, :, None, :] * jnp.asarray([-1.0, 1.0], F32)[None, None, :, None]
    sin = sin.reshape(t, MLA_ROPE)
    pad = MLA_QK_PAD - MLA_QK
    cos_tab = jnp.concatenate([jnp.ones((t, MLA_NOPE), F32), cos, jnp.zeros((t, pad), F32)], axis=1)
    sin_tab = jnp.concatenate([jnp.zeros((t, MLA_NOPE), F32), sin, jnp.zeros((t, pad), F32)], axis=1)
    return cos_tab, sin_tab


HALO = 8


def _shifted_rows(x, halo_prev, halo_next, offset, pos, n_ctx, t):
    tm = x.shape[0]
    out = pltpu.roll(x, (-offset) % tm, axis=0)
    row = lax.broadcasted_iota(jnp.int32, (tm, 1), 0)
    for i in range(abs(offset)):
        if offset < 0:
            out = jnp.where(row == i, halo_prev[HALO + offset + i:HALO + offset + i + 1], out)
        else:
            out = jnp.where(row == tm - offset + i, halo_next[i:i + 1], out)
    src = pos + offset
    same = (src >= 0) & (src < t) & ((pos < n_ctx) == (src < n_ctx))
    return jnp.where(same, out, 0.0)


def _halo_specs(tm, t, width, col):
    per_tile, last = tm // HALO, t // HALO - 1
    return [pl.BlockSpec((1, HALO, width), lambda b, i: (b, jnp.maximum(i * per_tile - 1, 0), col)),
            pl.BlockSpec((1, HALO, width), lambda b, i: (b, jnp.minimum((i + 1) * per_tile, last), col))]


def _rwkv_prep_body(p_ref, pp_ref, pn_ref, mu_ref, w0_ref, a0_ref, kkw_ref, ka_ref, rk_ref, lora_ref,
                    r_ref, v_ref, kk_ref, kd0_ref, kd1_ref, a0o_ref, a1o_ref, lw0_ref, lw1_ref, g_ref, bonus_ref,
                    *, n_ctx, t):
    x = p_ref[0]
    tm = x.shape[0]
    gd = GROUP_DIM
    pos = pl.program_id(1) * tm + lax.broadcasted_iota(jnp.int32, (tm, 1), 0)
    prev = _shifted_rows(x, pp_ref[0], pn_ref[0], -1, pos, n_ctx, t)
    nxt = _shifted_rows(x, pp_ref[0], pn_ref[0], 1, pos, n_ctx, t)
    x = x + mu_ref[0:1] * (prev - x) + mu_ref[1:2] * (nxt - x)
    r, k, v = x[:, :gd], x[:, gd:2 * gd], x[:, 2 * gd:3 * gd]
    lora_in = jnp.concatenate([jnp.tanh(x[:, 3 * gd:3 * gd + 2 * RW_LORA]),
                               x[:, 3 * gd + 2 * RW_LORA:3 * gd + 4 * RW_LORA],
                               jax.nn.sigmoid(x[:, 3 * gd + 4 * RW_LORA:])], axis=1)
    lo = jnp.dot(lora_in.astype(BF16), lora_ref[...], preferred_element_type=F32)
    lane_r = lax.broadcasted_iota(jnp.int32, (gd, gd), 0) // HEAD_DIM
    lane_c = lax.broadcasted_iota(jnp.int32, (gd, gd), 1) // HEAD_DIM
    same_head = (lane_r == lane_c).astype(F32)
    head_sum = lambda m: _bmm(m, same_head, 'tn,nm->tm', 'x3')
    kk = k * kkw_ref[...]
    kk_ref[0] = kk * lax.rsqrt(jnp.maximum(head_sum(kk * kk), 1e-12))
    for d, (kd_ref, a_ref, lw_ref) in enumerate(((kd0_ref, a0o_ref, lw0_ref), (kd1_ref, a1o_ref, lw1_ref))):
        a = jax.nn.sigmoid(a0_ref[d:d + 1] + lo[:, (2 + d) * gd:(3 + d) * gd])
        a_ref[0] = a
        lw_ref[0] = -RW_DECAY_SCALE * jax.nn.sigmoid(w0_ref[d:d + 1] + lo[:, d * gd:(d + 1) * gd])
        kd_ref[0] = k * (1.0 + (a - 1.0) * ka_ref[...])
    r_ref[0] = r
    v_ref[0] = v
    g_ref[0] = lo[:, 4 * gd:]
    bonus_ref[0] = head_sum(r * k * rk_ref[...]) * v


def rwkv_prep(p, mu, w0, a0, k_k, k_a, r_k, lora_w, *, n_ctx):
    bsz, t, _ = p.shape
    gd = GROUP_DIM
    tm = _pick_tile(t, 320, 16)
    row = lambda b, i: (b, i, 0)
    fixed = lambda b, i: (0, 0)
    vec = lambda w: w.reshape(1, gd)
    out = jax.ShapeDtypeStruct((bsz, t, gd), F32)
    return pl.pallas_call(
        functools.partial(_rwkv_prep_body, n_ctx=n_ctx, t=t),
        grid=(bsz, t // tm),
        in_specs=[pl.BlockSpec((1, tm, RW_IN), row), *_halo_specs(tm, t, RW_IN, 0),
                  pl.BlockSpec((2, RW_IN), fixed), pl.BlockSpec((2, gd), fixed), pl.BlockSpec((2, gd), fixed),
                  pl.BlockSpec((1, gd), fixed), pl.BlockSpec((1, gd), fixed), pl.BlockSpec((1, gd), fixed),
                  pl.BlockSpec(lora_w.shape, fixed)],
        out_specs=[pl.BlockSpec((1, tm, gd), row)] * 11,
        out_shape=[out] * 11,
        compiler_params=_params("parallel", "parallel"),
        name="rwkv_prep",
    )(p, p, p, mu, w0, a0, vec(k_k), vec(k_a), vec(r_k), lora_w)


def _rwkv_mixer(p, n_ctx, mu, w0, w_up, a0, a_up, g_up, k_k, k_a, r_k, gn_g, gn_b):
    bsz, t, _ = p.shape
    gd = GROUP_DIM
    n_lora = 4 * RW_LORA + RW_G_LORA
    lora_w = jnp.zeros((n_lora, 5 * gd), F32)
    for j, blk in enumerate((w_up[0], w_up[1], a_up[0], a_up[1])):
        lora_w = lora_w.at[j * RW_LORA:(j + 1) * RW_LORA, j * gd:(j + 1) * gd].set(blk)
    lora_w = lora_w.at[4 * RW_LORA:, 4 * gd:].set(g_up)
    r, v, kk, kd0, kd1, a_f, a_b, lw_f, lw_b, g, bonus = rwkv_prep(p, mu, w0, a0, k_k, k_a, r_k,
                                                                   lora_w.astype(BF16), n_ctx=n_ctx)
    y_fwd, y_bwd = rwkv_scan(r, v, kk, (kd0, kd1), (a_f, a_b), (lw_f, lw_b), n_ctx=n_ctx)
    y = (y_fwd + y_bwd).reshape(bsz, t, N_HEADS, HEAD_DIM)
    m = jnp.mean(y, -1, keepdims=True)
    var = jnp.mean(jnp.square(y - m), -1, keepdims=True)
    y = (y - m) * lax.rsqrt(var + RW_GN_EPS) * gn_g.reshape(N_HEADS, HEAD_DIM) + gn_b.reshape(N_HEADS, HEAD_DIM)
    return (y.reshape(bsz, t, gd) + bonus) * g


def _conv_body(x_ref, xp_ref, xn_ref, w_ref, b_ref, o_ref, *, n_ctx, t):
    x = x_ref[0]
    tm = x.shape[0]
    pos = pl.program_id(1) * tm + lax.broadcasted_iota(jnp.int32, (tm, 1), 0)
    acc = b_ref[...] + w_ref[MB_CONV // 2:MB_CONV // 2 + 1] * x
    for j in range(MB_CONV):
        offset = j - MB_CONV // 2
        if offset != 0:
            acc = acc + w_ref[j:j + 1] * _shifted_rows(x, xp_ref[0], xn_ref[0], offset, pos, n_ctx, t)
    o_ref[0] = jax.nn.silu(acc)


def conv_silu(p, conv_w, conv_b, *, n_ctx):
    bsz, t, _ = p.shape
    width = MB_CONV_DIM
    tm = _pick_tile(t, 640, 16)
    col = OFF_MB_XBC // width
    return pl.pallas_call(
        functools.partial(_conv_body, n_ctx=n_ctx, t=t),
        grid=(bsz, t // tm),
        in_specs=[pl.BlockSpec((1, tm, width), lambda b, i: (b, i, col)), *_halo_specs(tm, t, width, col),
                  pl.BlockSpec((MB_CONV, width), lambda b, i: (0, 0)), pl.BlockSpec((1, width), lambda b, i: (0, 0))],
        out_specs=pl.BlockSpec((1, tm, width), lambda b, i: (b, i, 0)),
        out_shape=jax.ShapeDtypeStruct((bsz, t, width), F32),
        compiler_params=_params("parallel", "parallel"),
        name="conv_silu",
    )(p, p, p, conv_w, conv_b.reshape(1, width))


def _mamba_mixer(p, n_ctx, conv_w, conv_b, a_log, dt_bias, d_skip, norm_g):
    bsz, t, _ = p.shape
    z = p[..., OFF_MB_Z:OFF_MB_XBC]
    dt_raw = p[..., OFF_MB_DT:OFF_MB_DT + 2 * N_HEADS]
    xbc = conv_silu(p, conv_w, conv_b, n_ctx=n_ctx)
    gd = GROUP_DIM
    a_neg = -jnp.exp(a_log)
    y = (xbc[..., :gd].reshape(bsz, t, N_HEADS, HEAD_DIM) * d_skip[:, None]).reshape(bsz, t, gd)
    for d in range(2):
        dt = jax.nn.softplus(dt_raw[..., d * N_HEADS:(d + 1) * N_HEADS] + dt_bias[d])
        y = y + ssd_scan(xbc, dt, dt * a_neg[d], n_ctx=n_ctx, reverse=bool(d))
    y = y * jax.nn.silu(z)
    y = y.reshape(bsz, t, MB_GROUPS, gd // MB_GROUPS)
    y = y * lax.rsqrt(jnp.mean(y * y, -1, keepdims=True) + RMS_EPS)
    return y.reshape(bsz, t, gd) * norm_g


def _mla_mixer(p, n_ctx, q_norm, w_uq, kv_norm, w_ukv, tile, tables):
    bsz, t, n_cols = p.shape
    n_h, pad = N_HEADS, MLA_QK_PAD - MLA_QK
    w_q = jnp.pad(w_uq.reshape(-1, n_h, MLA_QK), ((0, 0), (0, 0), (0, pad))).reshape(-1, n_h * MLA_QK_PAD)
    w_kv = w_ukv.reshape(-1, n_h, MLA_NOPE + HEAD_DIM)
    w_k = jnp.pad(w_kv[..., :MLA_NOPE], ((0, 0), (0, 0), (0, MLA_QK_PAD - MLA_NOPE))).reshape(-1, n_h * MLA_QK_PAD)
    w_v = w_kv[..., MLA_NOPE:].reshape(-1, n_h * HEAD_DIM)
    p2 = p.reshape(bsz * t, n_cols)
    q = norm_matmul(p2, OFF_MLA_Q, q_norm, w_q.astype(BF16)).reshape(bsz, t, -1)
    kv = norm_matmul(p2, OFF_MLA_KV, kv_norm, jnp.concatenate([w_k, w_v], axis=1).astype(BF16)).reshape(bsz, t, -1)
    return attention(q, kv, p, tables, n_ctx=n_ctx, tile=tile)


def _first_max(vals, excluded):
    live = [jnp.where(x, -jnp.inf, v) for v, x in zip(vals, excluded)]
    top = functools.reduce(jnp.maximum, live)
    found = jnp.zeros_like(top, dtype=jnp.bool_)
    first = []
    for v, x in zip(live, excluded):
        hit = (v == top) & ~found & ~x
        first.append(hit)
        found = found | hit
    return top, first


def _router_body(x_ref, sc_ref, sh_ref, rw_ref, rb_ref, h_ref, gate_ref):
    h = x_ref[...] * (1.0 + sc_ref[0]) + sh_ref[0]
    h_ref[...] = h.astype(BF16)
    logits = jnp.dot(h, rw_ref[...], preferred_element_type=F32, precision=HIGHEST)
    scores = jax.nn.sigmoid(logits.T[:N_EXPERTS])
    biased = scores + rb_ref[...]
    rows = [biased[e:e + 1] for e in range(N_EXPERTS)]
    never = jnp.zeros_like(rows[0], dtype=jnp.bool_)
    group_scores, picked = [], []
    for g in range(N_EXPERT_GROUPS):
        vals = rows[g * EXPERTS_PER_GROUP:(g + 1) * EXPERTS_PER_GROUP]
        top1, first = _first_max(vals, [never] * EXPERTS_PER_GROUP)
        top2, second = _first_max(vals, first)
        group_scores.append(top1 + top2)
        picked.append([a | b for a, b in zip(first, second)])
    _, group_sel = _first_max(group_scores, [never] * N_EXPERT_GROUPS)
    chosen = [jnp.where(group_sel[e // EXPERTS_PER_GROUP] & picked[e // EXPERTS_PER_GROUP][e % EXPERTS_PER_GROUP],
                        scores[e:e + 1], 0.0) for e in range(N_EXPERTS)]
    denom = functools.reduce(jnp.add, chosen)
    gate_ref[...] = jnp.concatenate([w / denom * ROUTED_SCALE for w in chosen], axis=0)


def router(xs, mod_l, router_w, router_b, *, tokens_per_batch, n_ctx, tile):
    m, d = xs.shape
    bsz = m // tokens_per_batch
    tiles_per_batch, n_ctx_tiles = tokens_per_batch // tile, n_ctx // tile

    def mod_row(i):
        return jnp.where(i % tiles_per_batch < n_ctx_tiles, bsz, i // tiles_per_batch)

    rw_pad = jnp.pad(router_w, ((0, 0), (0, LANE - N_EXPERTS)))
    return pl.pallas_call(
        _router_body,
        grid=(m // tile,),
        in_specs=[pl.BlockSpec((tile, d), lambda i: (i, 0)),
                  pl.BlockSpec((None, 1, d), lambda i: (mod_row(i), 0, 4)),
                  pl.BlockSpec((None, 1, d), lambda i: (mod_row(i), 0, 3)),
                  pl.BlockSpec((d, LANE), lambda i: (0, 0)),
                  pl.BlockSpec((N_EXPERTS, 1), lambda i: (0, 0))],
        out_specs=[pl.BlockSpec((tile, d), lambda i: (i, 0)),
                   pl.BlockSpec((N_EXPERTS, tile), lambda i: (0, i))],
        out_shape=[jax.ShapeDtypeStruct((m, d), BF16), jax.ShapeDtypeStruct((N_EXPERTS, m), F32)],
        compiler_params=_params("parallel"),
        name="router",
    )(xs, mod_l, mod_l, rw_pad, router_b.reshape(N_EXPERTS, 1))


def _pad_w_in(w):
    d = w.shape[0]
    zeros = lambda n: jnp.zeros((d, n), w.dtype)
    mb0 = RW_IN
    mla0 = mb0 + MB_IN
    na0 = mla0 + MLA_IN
    rw, mb, mla, na = w[:, :mb0], w[:, mb0:mla0], w[:, mla0:na0], w[:, na0:]
    q_kv, pe = mla[:, :MLA_Q_LORA + MLA_KV_LORA], mla[:, MLA_Q_LORA + MLA_KV_LORA:]
    return jnp.concatenate([rw, q_kv, zeros(PE_LANE), pe, zeros(LANE - PE_LANE - MLA_ROPE),
                            mb, zeros(LANE - 2 * N_HEADS), na], axis=1)


def kernel(x, c, ctx, c_ctx, ada_w, ada_b, w_in, w_out, ln1_g, ln1_b, ln2_g, ln2_b, rw_mu, rw_w0, rw_w_up, rw_a0, rw_a_up, rw_g_up, rw_k_k, rw_k_a, rw_r_k, rw_gn_g, rw_gn_b, mb_conv_w, mb_conv_b, mb_a_log, mb_dt_bias, mb_d, mb_norm_g, mla_q_norm, mla_w_uq, mla_kv_norm, mla_w_ukv, na_rpb, router_w, router_b, exp_w_gate, exp_w_up, exp_w_down):
    bsz, seq, d = x.shape
    n_ctx = ctx.shape[1]
    t = n_ctx + seq
    depth = ada_w.shape[0]
    tile = min(256, n_ctx)
    cond = jnp.concatenate([jax.nn.silu(c), jax.nn.silu(c_ctx)[None]], axis=0)
    mod = modulation(cond, ada_w, ada_b)
    xs = jnp.concatenate([ctx, x], axis=1)
    expert_w = [w.astype(BF16) for w in (exp_w_gate, exp_w_up, exp_w_down)]
    rope_tables = _rope_tables(t, n_ctx)
    m = bsz * t
    for l in range(depth):
        mod_l = mod[l][:, None, :]
        p = proj_in(xs.reshape(m, d), mod_l, _pad_w_in(w_in[l]).astype(BF16),
                    tokens_per_batch=t, n_ctx=n_ctx).reshape(bsz, t, N_IN_PAD)
        o_rw = _rwkv_mixer(p, n_ctx, rw_mu[l], rw_w0[l], rw_w_up[l], rw_a0[l], rw_a_up[l],
                           rw_g_up[l], rw_k_k[l], rw_k_a[l], rw_r_k[l], rw_gn_g[l], rw_gn_b[l])
        o_mb = _mamba_mixer(p, n_ctx, mb_conv_w[l], mb_conv_b[l], mb_a_log[l], mb_dt_bias[l], mb_d[l], mb_norm_g[l])
        o_mla = _mla_mixer(p, n_ctx, mla_q_norm[l], mla_w_uq[l], mla_kv_norm[l], mla_w_ukv[l], tile, rope_tables)
        o_na = neighborhood_attention(p, na_bias_table(na_rpb[l]), n_ctx=n_ctx, col0=OFF_NA)
        xs2 = proj_out([o.reshape(m, GROUP_DIM) for o in (o_rw, o_mb, o_mla, o_na)], w_out[l].astype(BF16),
                       xs.reshape(m, d), mod_l, ln1_g[l], ln1_b[l], tokens_per_batch=t, n_ctx=n_ctx)
        h, gate = router(xs2, mod_l, router_w, router_b, tokens_per_batch=t, n_ctx=n_ctx, tile=tile)
        xs = moe_experts(h, gate[:, :, None], *expert_w, l, xs2, mod_l, ln2_g[l], ln2_b[l],
                         tokens_per_batch=t, n_ctx=n_ctx).reshape(bsz, t, d)
    return xs[:, n_ctx:]
```

```python
import functools
import math

import numpy as np
import jax
import jax.numpy as jnp
from jax import lax
from jax.experimental import pallas as pl
from jax.experimental.pallas import tpu as pltpu

F32 = jnp.float32
BF16 = jnp.bfloat16
HIGHEST = lax.Precision.HIGHEST

D_MODEL = 2048
DEPTH = 4
GRID_W = 64
GROUP_DIM = D_MODEL // 4
HEAD_DIM = 64
N_HEADS = GROUP_DIM // HEAD_DIM

RW_LORA = 64
RW_G_LORA = 128
RW_DECAY_SCALE = 0.606531
RW_GN_EPS = 64e-5
RW_IN = 3 * GROUP_DIM + 4 * RW_LORA + RW_G_LORA
RW_CHUNK = 64

MB_GROUPS = 2
MB_STATE = 128
MB_CONV = 5
MB_CHUNK = 128
MB_CONV_DIM = GROUP_DIM + 2 * MB_GROUPS * MB_STATE
MB_IN = GROUP_DIM + MB_CONV_DIM + 2 * N_HEADS

MLA_Q_LORA = 3 * D_MODEL // 16
MLA_KV_LORA = D_MODEL // 16
MLA_NOPE = 64
MLA_ROPE = 32
MLA_QK = MLA_NOPE + MLA_ROPE
MLA_QK_PAD = 128
MLA_IN = MLA_Q_LORA + MLA_KV_LORA + MLA_ROPE
ROPE_BASE = 10000.0

NA_WIN_ROWS = 8
NA_WIN_COLS = 16
NA_IN = 3 * GROUP_DIM
NA_MASK = -1e30

N_EXPERTS = 16
N_EXPERT_GROUPS = 4
EXPERTS_PER_GROUP = N_EXPERTS // N_EXPERT_GROUPS
TOP_K = 2
D_EXPERT = D_MODEL // 4
ROUTED_SCALE = 2.5

DEEPNORM_ALPHA = (2 * DEPTH) ** 0.25
LN_EPS = 1e-6
RMS_EPS = 1e-6

LANE = 128
OFF_RW = 0
OFF_MLA_Q = OFF_RW + RW_IN
OFF_MLA_KV = OFF_MLA_Q + MLA_Q_LORA
OFF_MLA_PE = OFF_MLA_KV + MLA_KV_LORA
PE_LANE = MLA_NOPE
OFF_MB_Z = OFF_MLA_PE + LANE
OFF_MB_XBC = OFF_MB_Z + GROUP_DIM
OFF_MB_DT = OFF_MB_XBC + MB_CONV_DIM
OFF_NA = OFF_MB_DT + LANE
N_IN_PAD = OFF_NA + NA_IN
assert OFF_MLA_Q % MLA_Q_LORA == 0 and OFF_MB_XBC % MB_CONV_DIM == 0 and OFF_MB_Z % GROUP_DIM == 0

VMEM_LIMIT = 56 * 1024 * 1024


def _params(*sem):
    return pltpu.CompilerParams(dimension_semantics=sem, vmem_limit_bytes=VMEM_LIMIT)


def _pick_tile(n, target, quantum):
    best = None
    for t in range(quantum, min(n, target) + 1, quantum):
        if n % t == 0:
            best = t
    assert best is not None, (n, target, quantum)
    return best


def _mm_body(x_ref, w_ref, o_ref, *, precision):
    o_ref[...] = jnp.dot(x_ref[...], w_ref[...], preferred_element_type=F32,
                         precision=precision).astype(o_ref.dtype)


def matmul(x, w, *, tm_target=512, tn_target=1024, precision=None, out_dtype=F32):
    m, k = x.shape
    n = w.shape[1]
    tm = _pick_tile(m, tm_target, 16)
    tn = _pick_tile(n, tn_target, LANE)
    return pl.pallas_call(
        functools.partial(_mm_body, precision=precision),
        grid=(m // tm, n // tn),
        in_specs=[pl.BlockSpec((tm, k), lambda i, j: (i, 0)),
                  pl.BlockSpec((k, tn), lambda i, j: (0, j))],
        out_specs=pl.BlockSpec((tm, tn), lambda i, j: (i, j)),
        out_shape=jax.ShapeDtypeStruct((m, n), out_dtype),
        compiler_params=_params("parallel", "parallel"),
        name="matmul",
    )(x, w)


def _is_ctx_row(tm, tokens_per_batch, n_ctx):
    pos = (pl.program_id(0) * tm) % tokens_per_batch + lax.broadcasted_iota(jnp.int32, (tm, 1), 0)
    return pos < n_ctx


def _proj_in_body(x_ref, scc_ref, scb_ref, shc_ref, shb_ref, w_ref, o_ref, xm_ref, *, tokens_per_batch, n_ctx):
    is_ctx = _is_ctx_row(x_ref.shape[0], tokens_per_batch, n_ctx)

    @pl.when(pl.program_id(1) == 0)
    def _():
        sc = jnp.where(is_ctx, scc_ref[0], scb_ref[0])
        sh = jnp.where(is_ctx, shc_ref[0], shb_ref[0])
        xm_ref[...] = (x_ref[...] * (1.0 + sc) + sh).astype(BF16)

    o_ref[...] = jnp.dot(xm_ref[...], w_ref[...], preferred_element_type=F32)


def _mod_specs(bsz, tiles_per_batch, chunk, d):
    return [pl.BlockSpec((None, 1, d), lambda i, *_: (bsz, 0, chunk)),
            pl.BlockSpec((None, 1, d), lambda i, *_: (i // tiles_per_batch, 0, chunk))]


def proj_in(xs, mod_l, w, layer, *, tokens_per_batch, n_ctx):
    m, d = xs.shape
    n = w.shape[2]
    bsz = m // tokens_per_batch
    tm = _pick_tile(tokens_per_batch, 1100, 16)
    tn = _pick_tile(n, 1152, LANE)
    tiles_per_batch = tokens_per_batch // tm
    return pl.pallas_call(
        functools.partial(_proj_in_body, tokens_per_batch=tokens_per_batch, n_ctx=n_ctx),
        grid=(m // tm, n // tn),
        in_specs=[pl.BlockSpec((tm, d), lambda i, j: (i, 0)),
                  *_mod_specs(bsz, tiles_per_batch, 1, d), *_mod_specs(bsz, tiles_per_batch, 0, d),
                  pl.BlockSpec((None, d, tn), lambda i, j: (layer, 0, j))],
        out_specs=pl.BlockSpec((tm, tn), lambda i, j: (i, j)),
        out_shape=jax.ShapeDtypeStruct((m, n), F32),
        scratch_shapes=[pltpu.VMEM((tm, d), BF16)],
        compiler_params=_params("parallel", "arbitrary"),
        name="proj_in",
    )(xs, mod_l, mod_l, mod_l, mod_l, w)


def _proj_out_body(ryf_ref, ryb_ref, rbonus_ref, rg_ref, gng_ref, gnb_ref,
                   myf_ref, myb_ref, mx_ref, mz_ref, dskip_ref, mng_ref,
                   mla_ref, na_ref, w_ref, x_ref, gc_ref, gb_ref, lg_ref, lb_ref, o_ref,
                   *, tokens_per_batch, n_ctx):
    gd = GROUP_DIM
    lane_r = lax.broadcasted_iota(jnp.int32, (gd, gd), 0) // HEAD_DIM
    lane_c = lax.broadcasted_iota(jnp.int32, (gd, gd), 1) // HEAD_DIM
    same_head = (lane_r == lane_c).astype(F32)
    head_mean = lambda m: _bmm(m, same_head, 'tn,nm->tm', 'x3') * (1.0 / HEAD_DIM)
    y = ryf_ref[...] + ryb_ref[...]
    yc = y - head_mean(y)
    y = yc * lax.rsqrt(head_mean(yc * yc) + RW_GN_EPS) * gng_ref[...] + gnb_ref[...]
    o_rw = (y + rbonus_ref[...]) * rg_ref[...]
    y = (myf_ref[...] + myb_ref[...] + dskip_ref[...] * mx_ref[...]) * jax.nn.silu(mz_ref[...])
    group = gd // MB_GROUPS
    normed = []
    for g in range(MB_GROUPS):
        yg = y[:, g * group:(g + 1) * group]
        normed.append(yg * lax.rsqrt(jnp.mean(yg * yg, axis=-1, keepdims=True) + RMS_EPS))
    o_mb = jnp.concatenate(normed, axis=1) * mng_ref[...]

    y = None
    for j, mix in enumerate((o_rw, o_mb, mla_ref[...], na_ref[...])):
        part = jnp.dot(mix.astype(BF16), w_ref[j * gd:(j + 1) * gd, :], preferred_element_type=F32)
        y = part if y is None else y + part
    gate = jnp.where(_is_ctx_row(x_ref.shape[0], tokens_per_batch, n_ctx), gc_ref[0], gb_ref[0])
    z = DEEPNORM_ALPHA * x_ref[...] + gate * y
    mu = jnp.mean(z, axis=-1, keepdims=True)
    zc = z - mu
    var = jnp.mean(zc * zc, axis=-1, keepdims=True)
    o_ref[...] = zc * lax.rsqrt(var + LN_EPS) * lg_ref[...] + lb_ref[...]


def proj_out(rwkv, mamba, o_mla, o_na, p, w, layer, xs, mod_l, ln_g, ln_b, *, tokens_per_batch, n_ctx):
    m, d = xs.shape
    gd = GROUP_DIM
    bsz = m // tokens_per_batch
    tm = _pick_tile(tokens_per_batch, 320, 16)
    tiles_per_batch = tokens_per_batch // tm
    row = lambda i: (i, 0)
    fixed = lambda i: (0, 0)
    tok = pl.BlockSpec((tm, gd), row)
    vec = pl.BlockSpec((1, gd), fixed)
    ryf, ryb, rbonus, rg, gn_g, gn_b = rwkv
    myf, myb, xbc, d_skip, norm_g = mamba
    as_vec = lambda v: v.reshape(1, gd)
    return pl.pallas_call(
        functools.partial(_proj_out_body, tokens_per_batch=tokens_per_batch, n_ctx=n_ctx),
        grid=(m // tm,),
        in_specs=[tok, tok, tok, tok, vec, vec,
                  tok, tok, tok, pl.BlockSpec((tm, gd), lambda i: (i, OFF_MB_Z // gd)), vec, vec,
                  tok, tok,
                  pl.BlockSpec((None, d, d), lambda i: (layer, 0, 0)), pl.BlockSpec((tm, d), row),
                  *_mod_specs(bsz, tiles_per_batch, 2, d), pl.BlockSpec((1, d), fixed), pl.BlockSpec((1, d), fixed)],
        out_specs=pl.BlockSpec((tm, d), row),
        out_shape=jax.ShapeDtypeStruct((m, d), F32),
        compiler_params=_params("parallel"),
        name="proj_out",
    )(ryf, ryb, rbonus, rg, as_vec(gn_g), as_vec(gn_b),
      myf, myb, xbc, p, as_vec(jnp.repeat(d_skip, HEAD_DIM)), as_vec(norm_g),
      o_mla, o_na, w, xs, mod_l, mod_l, ln_g.reshape(1, d), ln_b.reshape(1, d))


def _mod_body(c_ref, w_ref, b_ref, o_ref, *, n_rows):
    w = w_ref[...]
    reps = w.shape[1] // LANE
    o_ref[...] = jnp.zeros_like(o_ref)
    for m in range(n_rows):
        cb = c_ref[m]
        cbt = jnp.concatenate([cb] * reps, axis=1)
        o_ref[m:m + 1, :] = jnp.sum(w * cbt, axis=0, keepdims=True) + b_ref[...]


def modulation(cond, ada_w, ada_b):
    n_rows, k = cond.shape
    depth, _, n = ada_w.shape
    tn = 512
    cond_b = jnp.broadcast_to(cond[:, :, None], (n_rows, k, LANE))
    return pl.pallas_call(
        functools.partial(_mod_body, n_rows=n_rows),
        grid=(depth, n // tn),
        in_specs=[pl.BlockSpec((n_rows, k, LANE), lambda l, j: (0, 0, 0)),
                  pl.BlockSpec((None, k, tn), lambda l, j: (l, 0, j)),
                  pl.BlockSpec((None, 1, tn), lambda l, j: (l, 0, j))],
        out_specs=pl.BlockSpec((None, 8, tn), lambda l, j: (l, 0, j)),
        out_shape=jax.ShapeDtypeStruct((depth, 8, n), F32),
        compiler_params=_params("parallel", "parallel"),
        name="modulation",
    )(cond_b, ada_w, ada_b.reshape(depth, 1, n))


def _bmm(a, b, spec, mode):
    if mode == 'f32':
        return jnp.einsum(spec, a, b, preferred_element_type=F32, precision=HIGHEST)
    a_hi, b_hi = a.astype(BF16), b.astype(BF16)
    out = jnp.einsum(spec, a_hi, b_hi, preferred_element_type=F32)
    if mode == 'x3':
        a_lo = (a - a_hi.astype(F32)).astype(BF16)
        b_lo = (b - b_hi.astype(F32)).astype(BF16)
        out = out + (jnp.einsum(spec, a_hi, b_lo, preferred_element_type=F32)
                     + jnp.einsum(spec, a_lo, b_hi, preferred_element_type=F32))
    return out


def _unit_triangular_inverse(l_mat, ti, si):
    c = l_mat.shape[1]
    nn = 'hts,hsn->htn'
    blk = 8
    eye = (ti == si).astype(F32)
    l_d = jnp.where((ti // blk) == (si // blk), l_mat, 0.0)
    p2 = _bmm(l_d, l_d, nn, 'bf16')
    p4 = _bmm(p2, p2, nn, 'bf16')
    inv = _bmm(eye - l_d, eye + p2, nn, 'bf16')
    inv = _bmm(inv, eye + p4, nn, 'bf16')
    while blk < c:
        pair = ((ti // (2 * blk)) == (si // (2 * blk))) & ((ti // blk) != (si // blk))
        off = jnp.where(pair, l_mat, 0.0)
        inv = inv - _bmm(_bmm(inv, off, nn, 'bf16'), inv, nn, 'bf16')
        blk *= 2
    return inv


def _first_head_lanes(shape):
    return lax.broadcasted_iota(jnp.int32, shape, len(shape) - 1) % LANE < HEAD_DIM


def _rwkv_body(*refs):
    fwd_in, bwd_in, (yf_ref, yb_ref, st_ref) = refs[:6], refs[6:12], refs[12:]

    @pl.when(pl.program_id(0) == 0)
    def _():
        st_ref[...] = jnp.zeros_like(st_ref)

    bsz, c, width = fwd_in[0].shape
    n_pairs = width // LANE
    ti = lax.broadcasted_iota(jnp.int32, (c, c), 0)
    si = lax.broadcasted_iota(jnp.int32, (c, c), 1)
    units = []
    for in_refs, reverse in ((fwd_in, False), (bwd_in, True)):
        tri = ((si >= ti) if reverse else (si <= ti)).astype(F32)
        for row in range(bsz):
            r, k, v, kk, a, lw = (ref[row] for ref in in_refs)
            cum = _bmm(tri, lw, 'ts,sn->tn', 'f32')
            tot = cum[:1] if reverse else cum[c - 1:]
            b = kk * a
            e_neg = jnp.exp(-cum)
            e_rem = jnp.exp(tot - cum)
            pieces = (r * jnp.exp(cum), kk * jnp.exp(cum - lw), k * e_neg, b * e_neg, k * e_rem, b * e_rem, v,
                      jnp.exp(tot))
            for j in range(n_pairs):
                units.append([m[:, j * LANE:(j + 1) * LANE] for m in pieces])
    n_units = len(units)
    rq, kq, kd, bd, kdc, bdc, v, e_tot = (jnp.stack([u[i] for u in units], axis=0) for i in range(8))

    backward = lax.broadcasted_iota(jnp.int32, (n_units, 1, 1), 0) >= n_units // 2
    order = (si - ti) * jnp.where(backward, -1, 1)
    incl = order <= 0
    strict = order < 0
    first = _first_head_lanes((c, LANE))
    first2 = _first_head_lanes((c, 2 * LANE))
    nt, nn, tn = 'utn,usn->uts', 'uts,usn->utn', 'usn,usm->unm'
    qq = jnp.concatenate([jnp.where(first, rq, 0.0), jnp.where(first, kq, 0.0),
                          jnp.where(first, 0.0, rq), jnp.where(first, 0.0, kq)], axis=1)
    ak = _bmm(qq, kd, nt, 'bf16')
    ab = _bmm(qq, bd, nt, 'bf16')
    a_rk = [jnp.where(incl, ak[:, 2 * x * c:(2 * x + 1) * c], 0.0) for x in range(2)]
    a_kk = [jnp.where(strict, ak[:, (2 * x + 1) * c:(2 * x + 2) * c], 0.0) for x in range(2)]
    a_rb = [jnp.where(incl, ab[:, 2 * x * c:(2 * x + 1) * c], 0.0) for x in range(2)]
    a_kb = [jnp.where(strict, ab[:, (2 * x + 1) * c:(2 * x + 2) * c], 0.0) for x in range(2)]
    inv = _unit_triangular_inverse(jnp.concatenate(a_kb, axis=0), ti, si)
    inv = [inv[:n_units], inv[n_units:]]

    xs = [_bmm(inv[x], jnp.concatenate([_bmm(a_kk[x], v, nn, 'bf16'), kq], axis=2), nn, 'bf16') for x in range(2)]
    x2 = jnp.where(first2, xs[0], xs[1])
    arx = jnp.where(first2, _bmm(a_rb[0], x2, nn, 'bf16'), _bmm(a_rb[1], x2, nn, 'bf16'))
    y0 = jnp.where(first, _bmm(a_rk[0], v, nn, 'bf16'), _bmm(a_rk[1], v, nn, 'bf16')) - arx[:, :, :LANE]
    rqp = rq - arx[:, :, LANE:]
    row_n = lax.broadcasted_iota(jnp.int32, (LANE, LANE), 0)
    col_n = lax.broadcasted_iota(jnp.int32, (LANE, LANE), 1)
    same_head = (row_n // HEAD_DIM) == (col_n // HEAD_DIM)
    bx = _bmm(bdc, x2, tn, 'bf16')
    tadd = jnp.where(same_head, _bmm(kdc, v, tn, 'bf16') - bx[:, :, :LANE], 0.0)
    decay = jnp.where(row_n == col_n, jnp.broadcast_to(e_tot, (n_units, LANE, LANE)), 0.0)
    p = jnp.where(same_head, decay - bx[:, :, LANE:], 0.0)
    t0 = st_ref[...]
    y = _bmm(rqp, t0, 'utn,unm->utm', 'x3') + y0
    st_ref[...] = _bmm(p, t0, 'ujn,unm->ujm', 'x3') + tadd
    for d, y_ref in enumerate((yf_ref, yb_ref)):
        for row in range(bsz):
            u0 = (d * bsz + row) * n_pairs
            y_ref[row] = jnp.concatenate([y[u0 + j] for j in range(n_pairs)], axis=1)


def _scan_chunk_index(c, n_ctx_chunks, n_chunks, reverse):
    if not reverse:
        return c
    return jnp.where(c < n_ctx_chunks, n_ctx_chunks - 1 - c, n_chunks - 1 - (c - n_ctx_chunks))


def rwkv_scan(r, v, kk, k_dirs, a_dirs, lw_dirs, *, n_ctx):
    bsz, t, width = r.shape
    c = RW_CHUNK
    n_chunks, n_ctx_chunks = t // c, n_ctx // c
    specs = [pl.BlockSpec((bsz, c, width),
                          lambda i, rev=rev: (0, _scan_chunk_index(i, n_ctx_chunks, n_chunks, rev), 0))
             for rev in (False, True)]
    n_units = 2 * bsz * (width // LANE)
    args = [(r, k_dirs[d], v, kk, a_dirs[d], lw_dirs[d]) for d in range(2)]
    return pl.pallas_call(
        _rwkv_body,
        grid=(n_chunks,),
        in_specs=[specs[0]] * 6 + [specs[1]] * 6,
        out_specs=specs,
        out_shape=[jax.ShapeDtypeStruct((bsz, t, width), F32)] * 2,
        scratch_shapes=[pltpu.VMEM((n_units, LANE, LANE), F32)],
        compiler_params=_params("arbitrary"),
        name="rwkv_scan",
    )(*args[0], *args[1])


def _ssd_body(x_ref, dt_ref, dac_ref, dar_ref, y_ref, st_ref, *, reverse):
    @pl.when(pl.program_id(1) == 0)
    def _():
        st_ref[...] = jnp.zeros_like(st_ref)

    xbc = x_ref[0]
    dt = dt_ref[0]
    c = xbc.shape[0]
    n_h = dt.shape[1]
    n_pairs = n_h // 2
    gd = n_h * HEAD_DIM
    n_g = (xbc.shape[1] - gd) // (2 * MB_STATE)
    pairs_per_group = n_pairs // n_g
    ti = lax.broadcasted_iota(jnp.int32, (c, c), 0)
    si = lax.broadcasted_iota(jnp.int32, (c, c), 1)
    incl = (si >= ti) if reverse else (si <= ti)
    incl_t = (ti >= si) if reverse else (ti <= si)
    cs_c = jnp.dot(incl.astype(F32), dac_ref[0], preferred_element_type=F32, precision=HIGHEST)
    cs_r = jnp.dot(dar_ref[0], incl_t.astype(F32), preferred_element_type=F32, precision=HIGHEST)
    tot = cs_c[:1, :] if reverse else cs_c[c - 1:, :]
    e_cs = jnp.exp(cs_c)
    e_rem = jnp.exp(tot - cs_c)
    e_tot = jnp.exp(tot)
    bm = [xbc[:, gd + g * MB_STATE:gd + (g + 1) * MB_STATE].astype(BF16) for g in range(n_g)]
    cm = [xbc[:, gd + (n_g + g) * MB_STATE:gd + (n_g + g + 1) * MB_STATE].astype(BF16) for g in range(n_g)]
    gmat = [jnp.einsum('ln,sn->ls', cm[g], bm[g], preferred_element_type=F32) for g in range(n_g)]
    first = _first_head_lanes((c, LANE))
    first_rows = lax.broadcasted_iota(jnp.int32, (LANE, 1), 0) < HEAD_DIM
    pick = lambda m, ha: jnp.where(first, m[:, ha:ha + 1], m[:, ha + 1:ha + 2])
    for j in range(n_pairs):
        g = j // pairs_per_group
        ha = 2 * j
        sl = slice(j * LANE, (j + 1) * LANE)
        xdt = xbc[:, sl] * pick(dt, ha)
        xdt_b = xdt.astype(BF16)
        y_heads = []
        for h in (ha, ha + 1):
            seg = cs_c[:, h:h + 1] - cs_r[h:h + 1, :]
            lmat = jnp.exp(jnp.where(incl, seg, NA_MASK))
            y_heads.append(jnp.dot((gmat[g] * lmat).astype(BF16), xdt_b, preferred_element_type=F32))
        st = st_ref[j]
        y_off = jnp.einsum('ln,pn->lp', cm[g], st.astype(BF16), preferred_element_type=F32) * pick(e_cs, ha)
        y_ref[0, :, sl] = jnp.where(first, y_heads[0], y_heads[1]) + y_off
        xdec = (xdt * pick(e_rem, ha)).astype(BF16)
        keep = jnp.where(first_rows, e_tot[:, ha:ha + 1], e_tot[:, ha + 1:ha + 2])
        st_ref[j] = keep * st + jnp.einsum('lp,ln->pn', xdec, bm[g], preferred_element_type=F32)


def ssd_scan(xbc, dt, da, *, n_ctx, reverse):
    bsz, t, width = xbc.shape
    n_h = dt.shape[2]
    gd = n_h * HEAD_DIM
    c = MB_CHUNK
    n_chunks, n_ctx_chunks = t // c, n_ctx // c
    idx = lambda i: _scan_chunk_index(i, n_ctx_chunks, n_chunks, reverse)
    da_row = jnp.swapaxes(da, 1, 2)
    return pl.pallas_call(
        functools.partial(_ssd_body, reverse=reverse),
        grid=(bsz, n_chunks),
        in_specs=[pl.BlockSpec((1, c, width), lambda b, i: (b, idx(i), 0)),
                  pl.BlockSpec((1, c, n_h), lambda b, i: (b, idx(i), 0)),
                  pl.BlockSpec((1, c, n_h), lambda b, i: (b, idx(i), 0)),
                  pl.BlockSpec((1, n_h, c), lambda b, i: (b, 0, idx(i)))],
        out_specs=pl.BlockSpec((1, c, gd), lambda b, i: (b, idx(i), 0)),
        out_shape=jax.ShapeDtypeStruct((bsz, t, gd), F32),
        scratch_shapes=[pltpu.VMEM((n_h // 2, 2 * HEAD_DIM, MB_STATE), F32)],
        compiler_params=_params("parallel", "arbitrary"),
        name="ssd_scan_bwd" if reverse else "ssd_scan_fwd",
    )(xbc, dt, da, da_row)


def _softmax_pv(s_parts, v_parts):
    m = functools.reduce(jnp.maximum, [jnp.max(s, axis=-1, keepdims=True) for s in s_parts])
    ps = [jnp.exp(s - m) for s in s_parts]
    l = functools.reduce(jnp.add, [jnp.sum(p, axis=-1, keepdims=True) for p in ps])
    o = functools.reduce(jnp.add, [jnp.dot(p.astype(BF16), v, preferred_element_type=F32)
                                   for p, v in zip(ps, v_parts)])
    return o / l


def _rope_lanes(x, cos_tab, sin_tab):
    width = x.shape[1]
    reps = width // MLA_QK_PAD
    cos_w = jnp.concatenate([cos_tab] * reps, axis=1) if reps > 1 else cos_tab
    sin_w = jnp.concatenate([sin_tab] * reps, axis=1) if reps > 1 else sin_tab
    lane = lax.broadcasted_iota(jnp.int32, x.shape, 1) % MLA_QK_PAD
    first_half = (lane - MLA_NOPE) % (MLA_ROPE // 2) < MLA_ROPE // 4
    shift = MLA_ROPE // 4
    partner = jnp.where(first_half, pltpu.roll(x, width - shift, axis=1), pltpu.roll(x, shift, axis=1))
    return x * cos_w + partner * sin_w


def _attn_body(q_ref, k_ref, v_ref, pe_ref, cq_ref, sq_ref, ck_ref, sk_ref, o_ref, kb_ref, vb_ref,
               *, n_ctx, n_ctx_tiles):
    dq = q_ref.shape[2] // 2

    @pl.when(pl.program_id(2) == 0)
    def _():
        pe_rot = _rope_lanes(pe_ref[0], ck_ref[...], sk_ref[...])
        kb_ref[...] = (k_ref[0] + jnp.concatenate([pe_rot, pe_rot], axis=1)).astype(BF16)
        vb_ref[...] = v_ref[0].astype(BF16)

    q2 = (_rope_lanes(q_ref[0], cq_ref[...], sq_ref[...]) * (MLA_QK ** -0.5)).astype(BF16)

    def attend(n_keys):
        v2 = vb_ref[0:n_keys]
        own = _first_head_lanes(v2.shape)
        outs = []
        for x in range(2):
            s = jnp.einsum('qd,kd->qk', q2[:, x * dq:(x + 1) * dq], kb_ref[0:n_keys, x * dq:(x + 1) * dq],
                           preferred_element_type=F32)
            p = jnp.exp((s - jnp.max(s, axis=-1, keepdims=True)).astype(BF16))
            vx = jnp.where(own if x == 0 else ~own, v2, jnp.ones_like(v2))
            o = jnp.dot(p, vx, preferred_element_type=F32)
            outs.append(o / pltpu.roll(o, HEAD_DIM, axis=1))
        o_ref[0] = jnp.where(_first_head_lanes(outs[0].shape), outs[0], outs[1])

    @pl.when(pl.program_id(2) < n_ctx_tiles)
    def _():
        attend(n_ctx)

    @pl.when(pl.program_id(2) >= n_ctx_tiles)
    def _():
        attend(k_ref.shape[1])


def attention(q, kv, p, tables, *, n_ctx, tile):
    bsz, t, _ = q.shape
    n_pairs = N_HEADS // 2
    dq2 = 2 * MLA_QK_PAD
    v_col0 = N_HEADS * MLA_QK_PAD // LANE
    cos_tab, sin_tab = tables
    q_tab = pl.BlockSpec((tile, MLA_QK_PAD), lambda b, j, i: (i, 0))
    k_tab = pl.BlockSpec((t, MLA_QK_PAD), lambda b, j, i: (0, 0))
    return pl.pallas_call(
        functools.partial(_attn_body, n_ctx=n_ctx, n_ctx_tiles=n_ctx // tile),
        grid=(bsz, n_pairs, t // tile),
        in_specs=[pl.BlockSpec((1, tile, dq2), lambda b, j, i: (b, i, j)),
                  pl.BlockSpec((1, t, dq2), lambda b, j, i: (b, 0, j)),
                  pl.BlockSpec((1, t, LANE), lambda b, j, i: (b, 0, v_col0 + j)),
                  pl.BlockSpec((1, t, LANE), lambda b, j, i: (b, 0, OFF_MLA_PE // LANE)),
                  q_tab, q_tab, k_tab, k_tab],
        out_specs=pl.BlockSpec((1, tile, LANE), lambda b, j, i: (b, i, j)),
        out_shape=jax.ShapeDtypeStruct((bsz, t, N_HEADS * HEAD_DIM), F32),
        scratch_shapes=[pltpu.VMEM((t, dq2), BF16), pltpu.VMEM((t, LANE), BF16)],
        compiler_params=_params("parallel", "parallel", "arbitrary"),
        name="attention",
    )(q, kv, kv, p, cos_tab, sin_tab, cos_tab, sin_tab)


def _norm_mm_body(x_ref, g_ref, w_ref, o_ref):
    x = x_ref[...]
    xn = x * lax.rsqrt(jnp.mean(x * x, axis=-1, keepdims=True) + RMS_EPS) * g_ref[...]
    o_ref[...] = jnp.dot(xn.astype(BF16), w_ref[...], preferred_element_type=F32)


def norm_matmul(p, col, g, w):
    m = p.shape[0]
    k, n = w.shape
    tm = _pick_tile(m, 640, 16)
    return pl.pallas_call(
        _norm_mm_body,
        grid=(m // tm,),
        in_specs=[pl.BlockSpec((tm, k), lambda i: (i, col // k)),
                  pl.BlockSpec((1, k), lambda i: (0, 0)),
                  pl.BlockSpec((k, n), lambda i: (0, 0))],
        out_specs=pl.BlockSpec((tm, n), lambda i: (i, 0)),
        out_shape=jax.ShapeDtypeStruct((m, n), F32),
        compiler_params=_params("parallel"),
        name="norm_matmul",
    )(p, g.reshape(1, k), w)


def _na_body(q_ref, k_ref, v_ref, bias_ref, o_ref, kb_ref, vb_ref, *, n_ctx, n_rows):
    i = pl.program_id(2)
    rows_per_step = q_ref.shape[1] // GRID_W
    win = NA_WIN_ROWS * GRID_W
    q2 = q_ref[0] * (HEAD_DIM ** -0.5)

    def head_queries(q_rows, x):
        first = _first_head_lanes(q_rows.shape)
        return jnp.where(first if x == 0 else ~first, q_rows, 0.0).astype(BF16)

    def scores(q, k):
        return jnp.einsum('qd,kd->qk', q, k, preferred_element_type=F32)

    @pl.when(i == 0)
    def _():
        kb_ref[...] = k_ref[0].astype(BF16)
        vb_ref[...] = v_ref[0].astype(BF16)
        kc, vc = kb_ref[0:n_ctx], vb_ref[0:n_ctx]
        outs = [_softmax_pv([scores(head_queries(q2, x), kc)], [vc]) for x in range(2)]
        o_ref[0] = jnp.where(_first_head_lanes(outs[0].shape), outs[0], outs[1])

    @pl.when(i > 0)
    def _():
        qs, kws, vws, biases = [], [], [], []
        for rr in range(rows_per_step):
            r = (i - 1) * rows_per_step + rr
            rs = jnp.clip(r - NA_WIN_ROWS // 2, 0, n_rows - NA_WIN_ROWS)
            start = pl.multiple_of(n_ctx + rs * GRID_W, GRID_W)
            kws.append(kb_ref[pl.ds(start, win)])
            vws.append(vb_ref[pl.ds(start, win)])
            q_row = q2[rr * GRID_W:(rr + 1) * GRID_W]
            qs.append(jnp.concatenate([head_queries(q_row, 0), head_queries(q_row, 1)], axis=0))
            biases.append(jnp.concatenate([bias_ref[0, r - rs], bias_ref[1, r - rs]], axis=0))
        qb = jnp.stack(qs, axis=0)
        kc, vc = kb_ref[0:n_ctx], vb_ref[0:n_ctx]
        s_loc = jnp.einsum('uqd,ukd->uqk', qb, jnp.stack(kws, axis=0), preferred_element_type=F32)
        s_loc = s_loc + jnp.stack(biases, axis=0)
        s_ctx = scores(qb.reshape(rows_per_step * 2 * GRID_W, LANE), kc).reshape(rows_per_step, 2 * GRID_W, n_ctx)
        m = jnp.maximum(jnp.max(s_loc, axis=-1, keepdims=True), jnp.max(s_ctx, axis=-1, keepdims=True))
        p_loc = jnp.exp(s_loc - m)
        p_ctx = jnp.exp(s_ctx - m)
        l = jnp.sum(p_loc, axis=-1, keepdims=True) + jnp.sum(p_ctx, axis=-1, keepdims=True)
        o = jnp.einsum('uqk,ukd->uqd', p_loc.astype(BF16), jnp.stack(vws, axis=0), preferred_element_type=F32)
        o = o + jnp.dot(p_ctx.reshape(rows_per_step * 2 * GRID_W, n_ctx).astype(BF16), vc,
                        preferred_element_type=F32).reshape(o.shape)
        o = o / l
        first = _first_head_lanes((GRID_W, LANE))
        for rr in range(rows_per_step):
            o_ref[0, rr * GRID_W:(rr + 1) * GRID_W] = jnp.where(first, o[rr, :GRID_W], o[rr, GRID_W:])


def na_bias_table(rpb):
    wr = NA_WIN_ROWS
    qv = np.arange(GRID_W)[:, None]
    cv = np.arange(GRID_W)[None, :]
    col_start = np.clip(qv - NA_WIN_COLS // 2, 0, GRID_W - NA_WIN_COLS)
    valid = (cv >= col_start) & (cv < col_start + NA_WIN_COLS)
    offs = np.arange(2 * NA_WIN_COLS - 1)[:, None, None]
    pick = ((cv - qv + (NA_WIN_COLS - 1))[None] == offs) & valid[None]
    cols = jnp.einsum('hro,oqc->hrqc', rpb, jnp.asarray(pick, F32), precision=HIGHEST)
    cols = jnp.where(valid, cols, NA_MASK)
    tab = jnp.stack([cols[:, wr - 1 - var:2 * wr - 1 - var] for var in range(wr)], axis=1)
    tab = jnp.transpose(tab, (0, 1, 3, 2, 4))
    return tab.reshape(rpb.shape[0], wr, GRID_W, wr * GRID_W)


def neighborhood_attention(p, bias, *, n_ctx, col0):
    bsz, t, _ = p.shape
    n_rows = (t - n_ctx) // GRID_W
    n_pairs = GROUP_DIM // LANE
    c0 = col0 // LANE
    return pl.pallas_call(
        functools.partial(_na_body, n_ctx=n_ctx, n_rows=n_rows),
        grid=(bsz, n_pairs, t // n_ctx),
        in_specs=[pl.BlockSpec((1, n_ctx, LANE), lambda b, j, i: (b, i, c0 + j)),
                  pl.BlockSpec((1, t, LANE), lambda b, j, i: (b, 0, c0 + n_pairs + j)),
                  pl.BlockSpec((1, t, LANE), lambda b, j, i: (b, 0, c0 + 2 * n_pairs + j)),
                  pl.BlockSpec((2, NA_WIN_ROWS, GRID_W, NA_WIN_ROWS * GRID_W), lambda b, j, i: (j, 0, 0, 0))],
        out_specs=pl.BlockSpec((1, n_ctx, LANE), lambda b, j, i: (b, i, j)),
        out_shape=jax.ShapeDtypeStruct((bsz, t, GROUP_DIM), F32),
        scratch_shapes=[pltpu.VMEM((t, LANE), BF16), pltpu.VMEM((t, LANE), BF16)],
        compiler_params=_params("parallel", "parallel", "arbitrary"),
        name="neighborhood_attention",
    )(p, p, p, bias)


def _moe_body(h_ref, gate_ref, wg_ref, wu_ref, wd_ref, x_ref, gc_ref, gb_ref, lg_ref, lb_ref, o_ref, acc_ref,
              *, tokens_per_batch, n_ctx):
    e = pl.program_id(1)
    is_ctx = _is_ctx_row(x_ref.shape[0], tokens_per_batch, n_ctx)

    @pl.when(e == 0)
    def _():
        acc_ref[...] = jnp.zeros_like(acc_ref)

    h = h_ref[...]
    hid = jax.nn.silu(jnp.dot(h, wg_ref[0], preferred_element_type=F32)) * jnp.dot(h, wu_ref[0], preferred_element_type=F32)
    lane = lax.broadcasted_iota(jnp.int32, gate_ref.shape, 1)
    gate = jnp.sum(jnp.where(lane == e, gate_ref[...], 0.0), axis=1, keepdims=True)
    acc_ref[...] += gate * jnp.dot(hid.astype(BF16), wd_ref[0], preferred_element_type=F32)

    @pl.when(e == pl.num_programs(1) - 1)
    def _():
        z = DEEPNORM_ALPHA * x_ref[...] + jnp.where(is_ctx, gc_ref[0], gb_ref[0]) * acc_ref[...]
        mu = jnp.mean(z, axis=-1, keepdims=True)
        zc = z - mu
        var = jnp.mean(zc * zc, axis=-1, keepdims=True)
        o_ref[...] = zc * lax.rsqrt(var + LN_EPS) * lg_ref[...] + lb_ref[...]


def moe_experts(h, gate, w_gate, w_up, w_down, layer, xs, mod_l, ln_g, ln_b, *, tokens_per_batch, n_ctx):
    m, d = h.shape
    n_e, d_e = w_gate.shape[1], w_gate.shape[3]
    bsz = m // tokens_per_batch
    tm = _pick_tile(tokens_per_batch, 640, 16)
    tiles_per_batch = tokens_per_batch // tm
    row = lambda i, e: (i, 0)
    fixed = lambda i, e: (0, 0)
    return pl.pallas_call(
        functools.partial(_moe_body, tokens_per_batch=tokens_per_batch, n_ctx=n_ctx),
        grid=(m // tm, n_e),
        in_specs=[pl.BlockSpec((tm, d), row),
                  pl.BlockSpec((tm, LANE), row),
                  pl.BlockSpec((None, 1, d, d_e), lambda i, e: (layer, e, 0, 0)),
                  pl.BlockSpec((None, 1, d, d_e), lambda i, e: (layer, e, 0, 0)),
                  pl.BlockSpec((None, 1, d_e, d), lambda i, e: (layer, e, 0, 0)),
                  pl.BlockSpec((tm, d), row), *_mod_specs(bsz, tiles_per_batch, 5, d),
                  pl.BlockSpec((1, d), fixed), pl.BlockSpec((1, d), fixed)],
        out_specs=pl.BlockSpec((tm, d), row),
        out_shape=jax.ShapeDtypeStruct((m, d), F32),
        scratch_shapes=[pltpu.VMEM((tm, d), F32)],
        compiler_params=_params("parallel", "arbitrary"),
        name="moe_experts",
    )(h, gate, w_gate, w_up, w_down, xs, mod_l, mod_l, ln_g.reshape(1, d), ln_b.reshape(1, d))


def _rope_tables(t, n_ctx):
    n_freq = MLA_ROPE // 4
    pos_t = np.arange(t - n_ctx)
    pos = np.stack([pos_t // GRID_W, pos_t % GRID_W], axis=-1).astype(np.float32)
    inv_freq = jnp.asarray(ROPE_BASE, F32) ** (-jnp.arange(n_freq, dtype=F32) / n_freq)
    ang = jnp.asarray(pos)[:, :, None] * inv_freq
    ang = jnp.concatenate([jnp.zeros((n_ctx, 2, n_freq), F32), ang], axis=0)
    cos = jnp.broadcast_to(jnp.cos(ang)[:, :, None, :], (t, 2, 2, n_freq)).reshape(t, MLA_ROPE)
    sin = jnp.sin(ang)[:, :, None, :] * jnp.asarray([-1.0, 1.0], F32)[None, None, :, None]
    sin = sin.reshape(t, MLA_ROPE)
    pad = MLA_QK_PAD - MLA_QK
    cos_tab = jnp.concatenate([jnp.ones((t, MLA_NOPE), F32), cos, jnp.zeros((t, pad), F32)], axis=1)
    sin_tab = jnp.concatenate([jnp.zeros((t, MLA_NOPE), F32), sin, jnp.zeros((t, pad), F32)], axis=1)
    return cos_tab, sin_tab


HALO = 8


def _shifted_rows(x, halo_prev, halo_next, offset, pos, n_ctx, t):
    tm = x.shape[0]
    out = pltpu.roll(x, (-offset) % tm, axis=0)
    row = lax.broadcasted_iota(jnp.int32, (tm, 1), 0)
    for i in range(abs(offset)):
        if offset < 0:
            out = jnp.where(row == i, halo_prev[HALO + offset + i:HALO + offset + i + 1], out)
        else:
            out = jnp.where(row == tm - offset + i, halo_next[i:i + 1], out)
    src = pos + offset
    same = (src >= 0) & (src < t) & ((pos < n_ctx) == (src < n_ctx))
    return jnp.where(same, out, 0.0)


def _halo_specs(tm, t, width, col):
    per_tile, last = tm // HALO, t // HALO - 1
    return [pl.BlockSpec((1, HALO, width), lambda b, i: (b, jnp.maximum(i * per_tile - 1, 0), col)),
            pl.BlockSpec((1, HALO, width), lambda b, i: (b, jnp.minimum((i + 1) * per_tile, last), col))]


def _rwkv_prep_body(p_ref, pp_ref, pn_ref, mu_ref, w0_ref, a0_ref, kkw_ref, ka_ref, rk_ref, lora_ref,
                    r_ref, v_ref, kk_ref, kd0_ref, kd1_ref, a0o_ref, a1o_ref, lw0_ref, lw1_ref, g_ref, bonus_ref,
                    *, n_ctx, t):
    x = p_ref[0]
    tm = x.shape[0]
    gd = GROUP_DIM
    pos = pl.program_id(1) * tm + lax.broadcasted_iota(jnp.int32, (tm, 1), 0)
    prev = _shifted_rows(x, pp_ref[0], pn_ref[0], -1, pos, n_ctx, t)
    nxt = _shifted_rows(x, pp_ref[0], pn_ref[0], 1, pos, n_ctx, t)
    x = x + mu_ref[0:1] * (prev - x) + mu_ref[1:2] * (nxt - x)
    r, k, v = x[:, :gd], x[:, gd:2 * gd], x[:, 2 * gd:3 * gd]
    lora_in = jnp.concatenate([jnp.tanh(x[:, 3 * gd:3 * gd + 2 * RW_LORA]),
                               x[:, 3 * gd + 2 * RW_LORA:3 * gd + 4 * RW_LORA],
                               jax.nn.sigmoid(x[:, 3 * gd + 4 * RW_LORA:])], axis=1)
    lo = jnp.dot(lora_in.astype(BF16), lora_ref[...], preferred_element_type=F32)
    lane_r = lax.broadcasted_iota(jnp.int32, (gd, gd), 0) // HEAD_DIM
    lane_c = lax.broadcasted_iota(jnp.int32, (gd, gd), 1) // HEAD_DIM
    same_head = (lane_r == lane_c).astype(F32)
    head_sum = lambda m: _bmm(m, same_head, 'tn,nm->tm', 'x3')
    kk = k * kkw_ref[...]
    kk_ref[0] = kk * lax.rsqrt(jnp.maximum(head_sum(kk * kk), 1e-12))
    for d, (kd_ref, a_ref, lw_ref) in enumerate(((kd0_ref, a0o_ref, lw0_ref), (kd1_ref, a1o_ref, lw1_ref))):
        a = jax.nn.sigmoid(a0_ref[d:d + 1] + lo[:, (2 + d) * gd:(3 + d) * gd])
        a_ref[0] = a
        lw_ref[0] = -RW_DECAY_SCALE * jax.nn.sigmoid(w0_ref[d:d + 1] + lo[:, d * gd:(d + 1) * gd])
        kd_ref[0] = k * (1.0 + (a - 1.0) * ka_ref[...])
    r_ref[0] = r
    v_ref[0] = v
    g_ref[0] = lo[:, 4 * gd:]
    bonus_ref[0] = head_sum(r * k * rk_ref[...]) * v


def rwkv_prep(p, mu, w0, a0, k_k, k_a, r_k, lora_w, *, n_ctx):
    bsz, t, _ = p.shape
    gd = GROUP_DIM
    tm = _pick_tile(t, 320, 16)
    row = lambda b, i: (b, i, 0)
    fixed = lambda b, i: (0, 0)
    vec = lambda w: w.reshape(1, gd)
    out = jax.ShapeDtypeStruct((bsz, t, gd), F32)
    return pl.pallas_call(
        functools.partial(_rwkv_prep_body, n_ctx=n_ctx, t=t),
        grid=(bsz, t // tm),
        in_specs=[pl.BlockSpec((1, tm, RW_IN), row), *_halo_specs(tm, t, RW_IN, 0),
                  pl.BlockSpec((2, RW_IN), fixed), pl.BlockSpec((2, gd), fixed), pl.BlockSpec((2, gd), fixed),
                  pl.BlockSpec((1, gd), fixed), pl.BlockSpec((1, gd), fixed), pl.BlockSpec((1, gd), fixed),
                  pl.BlockSpec(lora_w.shape, fixed)],
        out_specs=[pl.BlockSpec((1, tm, gd), row)] * 11,
        out_shape=[out] * 11,
        compiler_params=_params("parallel", "parallel"),
        name="rwkv_prep",
    )(p, p, p, mu, w0, a0, vec(k_k), vec(k_a), vec(r_k), lora_w)


def _rwkv_mixer(p, n_ctx, mu, w0, w_up, a0, a_up, g_up, k_k, k_a, r_k, gn_g, gn_b):
    bsz, t, _ = p.shape
    gd = GROUP_DIM
    n_lora = 4 * RW_LORA + RW_G_LORA
    lora_w = jnp.zeros((n_lora, 5 * gd), F32)
    for j, blk in enumerate((w_up[0], w_up[1], a_up[0], a_up[1])):
        lora_w = lora_w.at[j * RW_LORA:(j + 1) * RW_LORA, j * gd:(j + 1) * gd].set(blk)
    lora_w = lora_w.at[4 * RW_LORA:, 4 * gd:].set(g_up)
    r, v, kk, kd0, kd1, a_f, a_b, lw_f, lw_b, g, bonus = rwkv_prep(p, mu, w0, a0, k_k, k_a, r_k,
                                                                   lora_w.astype(BF16), n_ctx=n_ctx)
    y_fwd, y_bwd = rwkv_scan(r, v, kk, (kd0, kd1), (a_f, a_b), (lw_f, lw_b), n_ctx=n_ctx)
    flat = lambda m: m.reshape(bsz * t, gd)
    return flat(y_fwd), flat(y_bwd), flat(bonus), flat(g), gn_g, gn_b


def _conv_body(x_ref, xp_ref, xn_ref, w_ref, b_ref, o_ref, *, n_ctx, t):
    x = x_ref[0]
    tm = x.shape[0]
    pos = pl.program_id(1) * tm + lax.broadcasted_iota(jnp.int32, (tm, 1), 0)
    acc = b_ref[...] + w_ref[MB_CONV // 2:MB_CONV // 2 + 1] * x
    for j in range(MB_CONV):
        offset = j - MB_CONV // 2
        if offset != 0:
            acc = acc + w_ref[j:j + 1] * _shifted_rows(x, xp_ref[0], xn_ref[0], offset, pos, n_ctx, t)
    o_ref[0] = jax.nn.silu(acc)


def conv_silu(p, conv_w, conv_b, *, n_ctx):
    bsz, t, _ = p.shape
    width = MB_CONV_DIM
    tm = _pick_tile(t, 640, 16)
    col = OFF_MB_XBC // width
    return pl.pallas_call(
        functools.partial(_conv_body, n_ctx=n_ctx, t=t),
        grid=(bsz, t // tm),
        in_specs=[pl.BlockSpec((1, tm, width), lambda b, i: (b, i, col)), *_halo_specs(tm, t, width, col),
                  pl.BlockSpec((MB_CONV, width), lambda b, i: (0, 0)), pl.BlockSpec((1, width), lambda b, i: (0, 0))],
        out_specs=pl.BlockSpec((1, tm, width), lambda b, i: (b, i, 0)),
        out_shape=jax.ShapeDtypeStruct((bsz, t, width), F32),
        compiler_params=_params("parallel", "parallel"),
        name="conv_silu",
    )(p, p, p, conv_w, conv_b.reshape(1, width))


def _mamba_mixer(p, n_ctx, conv_w, conv_b, a_log, dt_bias, d_skip, norm_g):
    bsz, t, _ = p.shape
    dt_raw = p[..., OFF_MB_DT:OFF_MB_DT + 2 * N_HEADS]
    xbc = conv_silu(p, conv_w, conv_b, n_ctx=n_ctx)
    a_neg = -jnp.exp(a_log)
    ys = []
    for d in range(2):
        dt = jax.nn.softplus(dt_raw[..., d * N_HEADS:(d + 1) * N_HEADS] + dt_bias[d])
        ys.append(ssd_scan(xbc, dt, dt * a_neg[d], n_ctx=n_ctx, reverse=bool(d)).reshape(bsz * t, GROUP_DIM))
    return ys[0], ys[1], xbc.reshape(bsz * t, -1), d_skip, norm_g


def _mla_mixer(p, n_ctx, q_norm, w_uq, kv_norm, w_ukv, tile, tables):
    bsz, t, n_cols = p.shape
    n_h, pad = N_HEADS, MLA_QK_PAD - MLA_QK
    w_q = jnp.pad(w_uq.reshape(-1, n_h, MLA_QK), ((0, 0), (0, 0), (0, pad))).reshape(-1, n_h * MLA_QK_PAD)
    w_kv = w_ukv.reshape(-1, n_h, MLA_NOPE + HEAD_DIM)
    w_k = jnp.pad(w_kv[..., :MLA_NOPE], ((0, 0), (0, 0), (0, MLA_QK_PAD - MLA_NOPE))).reshape(-1, n_h * MLA_QK_PAD)
    w_v = w_kv[..., MLA_NOPE:].reshape(-1, n_h * HEAD_DIM)
    p2 = p.reshape(bsz * t, n_cols)
    q = norm_matmul(p2, OFF_MLA_Q, q_norm, w_q.astype(BF16)).reshape(bsz, t, -1)
    kv = norm_matmul(p2, OFF_MLA_KV, kv_norm, jnp.concatenate([w_k, w_v], axis=1).astype(BF16)).reshape(bsz, t, -1)
    return attention(q, kv, p, tables, n_ctx=n_ctx, tile=tile)


def _first_max(vals, excluded):
    live = [jnp.where(x, -jnp.inf, v) for v, x in zip(vals, excluded)]
    top = functools.reduce(jnp.maximum, live)
    found = jnp.zeros_like(top, dtype=jnp.bool_)
    first = []
    for v, x in zip(live, excluded):
        hit = (v == top) & ~found & ~x
        first.append(hit)
        found = found | hit
    return top, first


def _router_body(x_ref, sc_ref, sh_ref, rw_ref, rb_ref, h_ref, gate_ref):
    h = x_ref[...] * (1.0 + sc_ref[0]) + sh_ref[0]
    h_ref[...] = h.astype(BF16)
    logits = jnp.dot(h, rw_ref[...], preferred_element_type=F32, precision=HIGHEST)
    scores = jax.nn.sigmoid(logits.T[:N_EXPERTS])
    biased = scores + rb_ref[...]
    rows = [biased[e:e + 1] for e in range(N_EXPERTS)]
    never = jnp.zeros_like(rows[0], dtype=jnp.bool_)
    group_scores, picked = [], []
    for g in range(N_EXPERT_GROUPS):
        vals = rows[g * EXPERTS_PER_GROUP:(g + 1) * EXPERTS_PER_GROUP]
        top1, first = _first_max(vals, [never] * EXPERTS_PER_GROUP)
        top2, second = _first_max(vals, first)
        group_scores.append(top1 + top2)
        picked.append([a | b for a, b in zip(first, second)])
    _, group_sel = _first_max(group_scores, [never] * N_EXPERT_GROUPS)
    chosen = [jnp.where(group_sel[e // EXPERTS_PER_GROUP] & picked[e // EXPERTS_PER_GROUP][e % EXPERTS_PER_GROUP],
                        scores[e:e + 1], 0.0) for e in range(N_EXPERTS)]
    denom = functools.reduce(jnp.add, chosen)
    gates = [w / denom * ROUTED_SCALE for w in chosen]
    gates.append(jnp.zeros((LANE - N_EXPERTS, denom.shape[1]), F32))
    gate_ref[...] = jnp.concatenate(gates, axis=0).T


def router(xs, mod_l, router_w, router_b, *, tokens_per_batch, n_ctx, tile):
    m, d = xs.shape
    bsz = m // tokens_per_batch
    tiles_per_batch, n_ctx_tiles = tokens_per_batch // tile, n_ctx // tile

    def mod_row(i):
        return jnp.where(i % tiles_per_batch < n_ctx_tiles, bsz, i // tiles_per_batch)

    rw_pad = jnp.pad(router_w, ((0, 0), (0, LANE - N_EXPERTS)))
    return pl.pallas_call(
        _router_body,
        grid=(m // tile,),
        in_specs=[pl.BlockSpec((tile, d), lambda i: (i, 0)),
                  pl.BlockSpec((None, 1, d), lambda i: (mod_row(i), 0, 4)),
                  pl.BlockSpec((None, 1, d), lambda i: (mod_row(i), 0, 3)),
                  pl.BlockSpec((d, LANE), lambda i: (0, 0)),
                  pl.BlockSpec((N_EXPERTS, 1), lambda i: (0, 0))],
        out_specs=[pl.BlockSpec((tile, d), lambda i: (i, 0)),
                   pl.BlockSpec((tile, LANE), lambda i: (i, 0))],
        out_shape=[jax.ShapeDtypeStruct((m, d), BF16), jax.ShapeDtypeStruct((m, LANE), F32)],
        compiler_params=_params("parallel"),
        name="router",
    )(xs, mod_l, mod_l, rw_pad, router_b.reshape(N_EXPERTS, 1))


def _pad_w_in(w):
    zeros = lambda n: jnp.zeros(w.shape[:-1] + (n,), w.dtype)
    mb0 = RW_IN
    mla0 = mb0 + MB_IN
    na0 = mla0 + MLA_IN
    rw, mb, mla, na = w[..., :mb0], w[..., mb0:mla0], w[..., mla0:na0], w[..., na0:]
    q_kv, pe = mla[..., :MLA_Q_LORA + MLA_KV_LORA], mla[..., MLA_Q_LORA + MLA_KV_LORA:]
    return jnp.concatenate([rw, q_kv, zeros(PE_LANE), pe, zeros(LANE - PE_LANE - MLA_ROPE),
                            mb, zeros(LANE - 2 * N_HEADS), na], axis=-1)


def kernel(x, c, ctx, c_ctx, ada_w, ada_b, w_in, w_out, ln1_g, ln1_b, ln2_g, ln2_b, rw_mu, rw_w0, rw_w_up, rw_a0, rw_a_up, rw_g_up, rw_k_k, rw_k_a, rw_r_k, rw_gn_g, rw_gn_b, mb_conv_w, mb_conv_b, mb_a_log, mb_dt_bias, mb_d, mb_norm_g, mla_q_norm, mla_w_uq, mla_kv_norm, mla_w_ukv, na_rpb, router_w, router_b, exp_w_gate, exp_w_up, exp_w_down):
    bsz, seq, d = x.shape
    n_ctx = ctx.shape[1]
    t = n_ctx + seq
    depth = ada_w.shape[0]
    tile = min(256, n_ctx)
    cond = jnp.concatenate([jax.nn.silu(c), jax.nn.silu(c_ctx)[None]], axis=0)
    mod = modulation(cond, ada_w, ada_b)
    xs = jnp.concatenate([ctx, x], axis=1)
    expert_w = [w.astype(BF16) for w in (exp_w_gate, exp_w_up, exp_w_down)]
    rope_tables = _rope_tables(t, n_ctx)
    w_in_all = _pad_w_in(w_in).astype(BF16)
    w_out_all = w_out.astype(BF16)
    m = bsz * t
    for l in range(depth):
        mod_l = mod[l][:, None, :]
        p = proj_in(xs.reshape(m, d), mod_l, w_in_all, l, tokens_per_batch=t, n_ctx=n_ctx).reshape(bsz, t, N_IN_PAD)
        o_rw = _rwkv_mixer(p, n_ctx, rw_mu[l], rw_w0[l], rw_w_up[l], rw_a0[l], rw_a_up[l],
                           rw_g_up[l], rw_k_k[l], rw_k_a[l], rw_r_k[l], rw_gn_g[l], rw_gn_b[l])
        o_mb = _mamba_mixer(p, n_ctx, mb_conv_w[l], mb_conv_b[l], mb_a_log[l], mb_dt_bias[l], mb_d[l], mb_norm_g[l])
        o_mla = _mla_mixer(p, n_ctx, mla_q_norm[l], mla_w_uq[l], mla_kv_norm[l], mla_w_ukv[l], tile, rope_tables)
        o_na = neighborhood_attention(p, na_bias_table(na_rpb[l]), n_ctx=n_ctx, col0=OFF_NA)
        xs2 = proj_out(o_rw, o_mb, o_mla.reshape(m, GROUP_DIM), o_na.reshape(m, GROUP_DIM), p.reshape(m, N_IN_PAD),
                       w_out_all, l, xs.reshape(m, d), mod_l, ln1_g[l], ln1_b[l],
                       tokens_per_batch=t, n_ctx=n_ctx)
        h, gate = router(xs2, mod_l, router_w, router_b, tokens_per_batch=t, n_ctx=n_ctx, tile=tile)
        xs = moe_experts(h, gate, *expert_w, l, xs2, mod_l, ln2_g[l], ln2_b[l],
                         tokens_per_batch=t, n_ctx=n_ctx).reshape(bsz, t, d)
    return xs[:, n_ctx:]
```

```python
import functools
import math

import numpy as np
import jax
import jax.numpy as jnp
from jax import lax
from jax.experimental import pallas as pl
from jax.experimental.pallas import tpu as pltpu

F32 = jnp.float32
BF16 = jnp.bfloat16
HIGHEST = lax.Precision.HIGHEST

D_MODEL = 2048
DEPTH = 4
GRID_W = 64
GROUP_DIM = D_MODEL // 4
HEAD_DIM = 64
N_HEADS = GROUP_DIM // HEAD_DIM

RW_LORA = 64
RW_G_LORA = 128
RW_DECAY_SCALE = 0.606531
RW_GN_EPS = 64e-5
RW_IN = 3 * GROUP_DIM + 4 * RW_LORA + RW_G_LORA
RW_CHUNK = 64

MB_GROUPS = 2
MB_STATE = 128
MB_CONV = 5
MB_CHUNK = 128
MB_CONV_DIM = GROUP_DIM + 2 * MB_GROUPS * MB_STATE
MB_IN = GROUP_DIM + MB_CONV_DIM + 2 * N_HEADS

MLA_Q_LORA = 3 * D_MODEL // 16
MLA_KV_LORA = D_MODEL // 16
MLA_NOPE = 64
MLA_ROPE = 32
MLA_QK = MLA_NOPE + MLA_ROPE
MLA_QK_PAD = 128
MLA_IN = MLA_Q_LORA + MLA_KV_LORA + MLA_ROPE
ROPE_BASE = 10000.0

NA_WIN_ROWS = 8
NA_WIN_COLS = 16
NA_IN = 3 * GROUP_DIM
NA_MASK = -1e30

N_EXPERTS = 16
N_EXPERT_GROUPS = 4
EXPERTS_PER_GROUP = N_EXPERTS // N_EXPERT_GROUPS
TOP_K = 2
D_EXPERT = D_MODEL // 4
ROUTED_SCALE = 2.5

DEEPNORM_ALPHA = (2 * DEPTH) ** 0.25
LN_EPS = 1e-6
RMS_EPS = 1e-6

LANE = 128
OFF_RW = 0
OFF_MLA_Q = OFF_RW + RW_IN
OFF_MLA_KV = OFF_MLA_Q + MLA_Q_LORA
OFF_MLA_PE = OFF_MLA_KV + MLA_KV_LORA
PE_LANE = MLA_NOPE
OFF_MB_Z = OFF_MLA_PE + LANE
OFF_MB_XBC = OFF_MB_Z + GROUP_DIM
OFF_MB_DT = OFF_MB_XBC + MB_CONV_DIM
OFF_NA = OFF_MB_DT + LANE
N_IN_PAD = OFF_NA + NA_IN
assert OFF_MLA_Q % MLA_Q_LORA == 0 and OFF_MB_XBC % MB_CONV_DIM == 0 and OFF_MB_Z % GROUP_DIM == 0

VMEM_LIMIT = 56 * 1024 * 1024


def _params(*sem):
    return pltpu.CompilerParams(dimension_semantics=sem, vmem_limit_bytes=VMEM_LIMIT)


def _pick_tile(n, target, quantum):
    best = None
    for t in range(quantum, min(n, target) + 1, quantum):
        if n % t == 0:
            best = t
    assert best is not None, (n, target, quantum)
    return best


def _mm_body(x_ref, w_ref, o_ref, *, precision):
    o_ref[...] = jnp.dot(x_ref[...], w_ref[...], preferred_element_type=F32,
                         precision=precision).astype(o_ref.dtype)


def matmul(x, w, *, tm_target=512, tn_target=1024, precision=None, out_dtype=F32):
    m, k = x.shape
    n = w.shape[1]
    tm = _pick_tile(m, tm_target, 16)
    tn = _pick_tile(n, tn_target, LANE)
    return pl.pallas_call(
        functools.partial(_mm_body, precision=precision),
        grid=(m // tm, n // tn),
        in_specs=[pl.BlockSpec((tm, k), lambda i, j: (i, 0)),
                  pl.BlockSpec((k, tn), lambda i, j: (0, j))],
        out_specs=pl.BlockSpec((tm, tn), lambda i, j: (i, j)),
        out_shape=jax.ShapeDtypeStruct((m, n), out_dtype),
        compiler_params=_params("parallel", "parallel"),
        name="matmul",
    )(x, w)


def _is_ctx_row(tm, tokens_per_batch, n_ctx):
    pos = (pl.program_id(0) * tm) % tokens_per_batch + lax.broadcasted_iota(jnp.int32, (tm, 1), 0)
    return pos < n_ctx


def _proj_in_body(x_ref, scc_ref, scb_ref, shc_ref, shb_ref, w_ref, o_ref, xm_ref, *, tokens_per_batch, n_ctx):
    is_ctx = _is_ctx_row(x_ref.shape[0], tokens_per_batch, n_ctx)

    @pl.when(pl.program_id(1) == 0)
    def _():
        sc = jnp.where(is_ctx, scc_ref[0], scb_ref[0])
        sh = jnp.where(is_ctx, shc_ref[0], shb_ref[0])
        xm_ref[...] = (x_ref[...] * (1.0 + sc) + sh).astype(BF16)

    o_ref[...] = jnp.dot(xm_ref[...], w_ref[...], preferred_element_type=F32)


def _mod_specs(bsz, tiles_per_batch, chunk, d):
    return [pl.BlockSpec((None, 1, d), lambda i, *_: (bsz, 0, chunk)),
            pl.BlockSpec((None, 1, d), lambda i, *_: (i // tiles_per_batch, 0, chunk))]


def proj_in(xs, mod_l, w, layer, *, tokens_per_batch, n_ctx):
    m, d = xs.shape
    n = w.shape[2]
    bsz = m // tokens_per_batch
    tm = _pick_tile(tokens_per_batch, 1100, 16)
    tn = _pick_tile(n, 1152, LANE)
    tiles_per_batch = tokens_per_batch // tm
    return pl.pallas_call(
        functools.partial(_proj_in_body, tokens_per_batch=tokens_per_batch, n_ctx=n_ctx),
        grid=(m // tm, n // tn),
        in_specs=[pl.BlockSpec((tm, d), lambda i, j: (i, 0)),
                  *_mod_specs(bsz, tiles_per_batch, 1, d), *_mod_specs(bsz, tiles_per_batch, 0, d),
                  pl.BlockSpec((None, d, tn), lambda i, j: (layer, 0, j))],
        out_specs=pl.BlockSpec((tm, tn), lambda i, j: (i, j)),
        out_shape=jax.ShapeDtypeStruct((m, n), F32),
        scratch_shapes=[pltpu.VMEM((tm, d), BF16)],
        compiler_params=_params("parallel", "arbitrary"),
        name="proj_in",
    )(xs, mod_l, mod_l, mod_l, mod_l, w)


def _proj_out_body(ryf_ref, ryb_ref, rbonus_ref, rg_ref, gng_ref, gnb_ref,
                   myf_ref, myb_ref, mx_ref, mz_ref, dskip_ref, mng_ref,
                   mla_ref, na_ref, w_ref, x_ref, gc_ref, gb_ref, lg_ref, lb_ref, o_ref,
                   *, tokens_per_batch, n_ctx):
    gd = GROUP_DIM
    lane_r = lax.broadcasted_iota(jnp.int32, (gd, gd), 0) // HEAD_DIM
    lane_c = lax.broadcasted_iota(jnp.int32, (gd, gd), 1) // HEAD_DIM
    same_head = (lane_r == lane_c).astype(F32)
    head_mean = lambda m: _bmm(m, same_head, 'tn,nm->tm', 'split_a') * (1.0 / HEAD_DIM)
    y = ryf_ref[...] + ryb_ref[...]
    yc = y - head_mean(y)
    y = yc * lax.rsqrt(head_mean(yc * yc) + RW_GN_EPS) * gng_ref[...] + gnb_ref[...]
    o_rw = (y + rbonus_ref[...]) * rg_ref[...]
    y = (myf_ref[...] + myb_ref[...] + dskip_ref[...] * mx_ref[...]) * jax.nn.silu(mz_ref[...])
    group = gd // MB_GROUPS
    normed = []
    for g in range(MB_GROUPS):
        yg = y[:, g * group:(g + 1) * group]
        normed.append(yg * lax.rsqrt(jnp.mean(yg * yg, axis=-1, keepdims=True) + RMS_EPS))
    o_mb = jnp.concatenate(normed, axis=1) * mng_ref[...]

    y = None
    for j, mix in enumerate((o_rw, o_mb, mla_ref[...], na_ref[...])):
        part = jnp.dot(mix.astype(BF16), w_ref[j * gd:(j + 1) * gd, :], preferred_element_type=F32)
        y = part if y is None else y + part
    gate = jnp.where(_is_ctx_row(x_ref.shape[0], tokens_per_batch, n_ctx), gc_ref[0], gb_ref[0])
    z = DEEPNORM_ALPHA * x_ref[...] + gate * y
    mu = jnp.mean(z, axis=-1, keepdims=True)
    zc = z - mu
    var = jnp.mean(zc * zc, axis=-1, keepdims=True)
    o_ref[...] = zc * lax.rsqrt(var + LN_EPS) * lg_ref[...] + lb_ref[...]


def proj_out(rwkv, mamba, o_mla, o_na, p, w, layer, xs, mod_l, ln_g, ln_b, *, tokens_per_batch, n_ctx):
    m, d = xs.shape
    gd = GROUP_DIM
    bsz = m // tokens_per_batch
    tm = _pick_tile(tokens_per_batch, 320, 16)
    tiles_per_batch = tokens_per_batch // tm
    row = lambda i: (i, 0)
    fixed = lambda i: (0, 0)
    tok = pl.BlockSpec((tm, gd), row)
    vec = pl.BlockSpec((1, gd), fixed)
    ryf, ryb, rbonus, rg, gn_g, gn_b = rwkv
    myf, myb, xbc, d_skip, norm_g = mamba
    as_vec = lambda v: v.reshape(1, gd)
    return pl.pallas_call(
        functools.partial(_proj_out_body, tokens_per_batch=tokens_per_batch, n_ctx=n_ctx),
        grid=(m // tm,),
        in_specs=[tok, tok, tok, tok, vec, vec,
                  tok, tok, tok, pl.BlockSpec((tm, gd), lambda i: (i, OFF_MB_Z // gd)), vec, vec,
                  tok, tok,
                  pl.BlockSpec((None, d, d), lambda i: (layer, 0, 0)), pl.BlockSpec((tm, d), row),
                  *_mod_specs(bsz, tiles_per_batch, 2, d), pl.BlockSpec((1, d), fixed), pl.BlockSpec((1, d), fixed)],
        out_specs=pl.BlockSpec((tm, d), row),
        out_shape=jax.ShapeDtypeStruct((m, d), F32),
        compiler_params=_params("parallel"),
        name="proj_out",
    )(ryf, ryb, rbonus, rg, as_vec(gn_g), as_vec(gn_b),
      myf, myb, xbc, p, as_vec(jnp.repeat(d_skip, HEAD_DIM)), as_vec(norm_g),
      o_mla, o_na, w, xs, mod_l, mod_l, ln_g.reshape(1, d), ln_b.reshape(1, d))


def _mod_body(c_ref, w_ref, b_ref, o_ref, *, n_rows):
    w = w_ref[...]
    reps = w.shape[1] // LANE
    o_ref[...] = jnp.zeros_like(o_ref)
    for m in range(n_rows):
        cb = c_ref[m]
        cbt = jnp.concatenate([cb] * reps, axis=1)
        o_ref[m:m + 1, :] = jnp.sum(w * cbt, axis=0, keepdims=True) + b_ref[...]


def modulation(cond, ada_w, ada_b):
    n_rows, k = cond.shape
    depth, _, n = ada_w.shape
    tn = 512
    cond_b = jnp.broadcast_to(cond[:, :, None], (n_rows, k, LANE))
    return pl.pallas_call(
        functools.partial(_mod_body, n_rows=n_rows),
        grid=(depth, n // tn),
        in_specs=[pl.BlockSpec((n_rows, k, LANE), lambda l, j: (0, 0, 0)),
                  pl.BlockSpec((None, k, tn), lambda l, j: (l, 0, j)),
                  pl.BlockSpec((None, 1, tn), lambda l, j: (l, 0, j))],
        out_specs=pl.BlockSpec((None, 8, tn), lambda l, j: (l, 0, j)),
        out_shape=jax.ShapeDtypeStruct((depth, 8, n), F32),
        compiler_params=_params("parallel", "parallel"),
        name="modulation",
    )(cond_b, ada_w, ada_b.reshape(depth, 1, n))


def _bmm(a, b, spec, mode):
    if mode == 'f32':
        return jnp.einsum(spec, a, b, preferred_element_type=F32, precision=HIGHEST)
    a_hi, b_hi = a.astype(BF16), b.astype(BF16)
    out = jnp.einsum(spec, a_hi, b_hi, preferred_element_type=F32)
    if mode in ('x3', 'split_b'):
        b_lo = (b - b_hi.astype(F32)).astype(BF16)
        out = out + jnp.einsum(spec, a_hi, b_lo, preferred_element_type=F32)
    if mode in ('x3', 'split_a'):
        a_lo = (a - a_hi.astype(F32)).astype(BF16)
        out = out + jnp.einsum(spec, a_lo, b_hi, preferred_element_type=F32)
    return out


def _unit_triangular_inverse(l_mat, ti, si):
    c = l_mat.shape[1]
    nn = 'hts,hsn->htn'
    blk = 8
    eye = (ti == si).astype(F32)
    l_d = jnp.where((ti // blk) == (si // blk), l_mat, 0.0)
    p2 = _bmm(l_d, l_d, nn, 'bf16')
    p4 = _bmm(p2, p2, nn, 'bf16')
    inv = _bmm(eye - l_d, eye + p2, nn, 'bf16')
    inv = _bmm(inv, eye + p4, nn, 'bf16')
    while blk < c:
        pair = ((ti // (2 * blk)) == (si // (2 * blk))) & ((ti // blk) != (si // blk))
        off = jnp.where(pair, l_mat, 0.0)
        inv = inv - _bmm(_bmm(inv, off, nn, 'bf16'), inv, nn, 'bf16')
        blk *= 2
    return inv


def _first_head_lanes(shape):
    return lax.broadcasted_iota(jnp.int32, shape, len(shape) - 1) % LANE < HEAD_DIM


def _rwkv_body(*refs):
    fwd_in, bwd_in, (yf_ref, yb_ref, st_ref) = refs[:6], refs[6:12], refs[12:]

    @pl.when(pl.program_id(0) == 0)
    def _():
        st_ref[...] = jnp.zeros_like(st_ref)

    bsz, c, width = fwd_in[0].shape
    n_pairs = width // LANE
    ti = lax.broadcasted_iota(jnp.int32, (c, c), 0)
    si = lax.broadcasted_iota(jnp.int32, (c, c), 1)
    units = []
    for in_refs, reverse in ((fwd_in, False), (bwd_in, True)):
        tri = ((si >= ti) if reverse else (si <= ti)).astype(F32)
        for row in range(bsz):
            r, k, v, kk, a, lw = (ref[row] for ref in in_refs)
            cum = _bmm(tri, lw, 'ts,sn->tn', 'split_b')
            tot = cum[:1] if reverse else cum[c - 1:]
            b = kk * a
            e_neg = jnp.exp(-cum)
            e_rem = jnp.exp(tot - cum)
            pieces = (r * jnp.exp(cum), kk * jnp.exp(cum - lw), k * e_neg, b * e_neg, k * e_rem, b * e_rem, v,
                      jnp.exp(tot))
            for j in range(n_pairs):
                units.append([m[:, j * LANE:(j + 1) * LANE] for m in pieces])
    n_units = len(units)
    rq, kq, kd, bd, kdc, bdc, v, e_tot = (jnp.stack([u[i] for u in units], axis=0) for i in range(8))

    backward = lax.broadcasted_iota(jnp.int32, (n_units, 1, 1), 0) >= n_units // 2
    order = (si - ti) * jnp.where(backward, -1, 1)
    incl = order <= 0
    strict = order < 0
    first = _first_head_lanes((c, LANE))
    first2 = _first_head_lanes((c, 2 * LANE))
    nt, nn, tn = 'utn,usn->uts', 'uts,usn->utn', 'usn,usm->unm'
    qq = jnp.concatenate([jnp.where(first, rq, 0.0), jnp.where(first, kq, 0.0),
                          jnp.where(first, 0.0, rq), jnp.where(first, 0.0, kq)], axis=1)
    ak = _bmm(qq, kd, nt, 'bf16')
    ab = _bmm(qq, bd, nt, 'bf16')
    a_rk = [jnp.where(incl, ak[:, 2 * x * c:(2 * x + 1) * c], 0.0) for x in range(2)]
    a_kk = [jnp.where(strict, ak[:, (2 * x + 1) * c:(2 * x + 2) * c], 0.0) for x in range(2)]
    a_rb = [jnp.where(incl, ab[:, 2 * x * c:(2 * x + 1) * c], 0.0) for x in range(2)]
    a_kb = [jnp.where(strict, ab[:, (2 * x + 1) * c:(2 * x + 2) * c], 0.0) for x in range(2)]
    inv = _unit_triangular_inverse(jnp.concatenate(a_kb, axis=0), ti, si)
    inv = [inv[:n_units], inv[n_units:]]

    xs = [_bmm(inv[x], jnp.concatenate([_bmm(a_kk[x], v, nn, 'bf16'), kq], axis=2), nn, 'bf16') for x in range(2)]
    x2 = jnp.where(first2, xs[0], xs[1])
    arx = jnp.where(first2, _bmm(a_rb[0], x2, nn, 'bf16'), _bmm(a_rb[1], x2, nn, 'bf16'))
    y0 = jnp.where(first, _bmm(a_rk[0], v, nn, 'bf16'), _bmm(a_rk[1], v, nn, 'bf16')) - arx[:, :, :LANE]
    rqp = rq - arx[:, :, LANE:]
    row_n = lax.broadcasted_iota(jnp.int32, (LANE, LANE), 0)
    col_n = lax.broadcasted_iota(jnp.int32, (LANE, LANE), 1)
    same_head = (row_n // HEAD_DIM) == (col_n // HEAD_DIM)
    bx = _bmm(bdc, x2, tn, 'bf16')
    tadd = jnp.where(same_head, _bmm(kdc, v, tn, 'bf16') - bx[:, :, :LANE], 0.0)
    decay = jnp.where(row_n == col_n, jnp.broadcast_to(e_tot, (n_units, LANE, LANE)), 0.0)
    p = jnp.where(same_head, decay - bx[:, :, LANE:], 0.0)
    t0 = st_ref[...]
    y = _bmm(rqp, t0, 'utn,unm->utm', 'bf16') + y0
    st_ref[...] = _bmm(p, t0, 'ujn,unm->ujm', 'bf16') + tadd
    for d, y_ref in enumerate((yf_ref, yb_ref)):
        for row in range(bsz):
            u0 = (d * bsz + row) * n_pairs
            y_ref[row] = jnp.concatenate([y[u0 + j] for j in range(n_pairs)], axis=1)


def _scan_chunk_index(c, n_ctx_chunks, n_chunks, reverse):
    if not reverse:
        return c
    return jnp.where(c < n_ctx_chunks, n_ctx_chunks - 1 - c, n_chunks - 1 - (c - n_ctx_chunks))


def rwkv_scan(r, v, kk, k_dirs, a_dirs, lw_dirs, *, n_ctx):
    bsz, t, width = r.shape
    c = RW_CHUNK
    n_chunks, n_ctx_chunks = t // c, n_ctx // c
    specs = [pl.BlockSpec((bsz, c, width),
                          lambda i, rev=rev: (0, _scan_chunk_index(i, n_ctx_chunks, n_chunks, rev), 0))
             for rev in (False, True)]
    n_units = 2 * bsz * (width // LANE)
    args = [(r, k_dirs[d], v, kk, a_dirs[d], lw_dirs[d]) for d in range(2)]
    return pl.pallas_call(
        _rwkv_body,
        grid=(n_chunks,),
        in_specs=[specs[0]] * 6 + [specs[1]] * 6,
        out_specs=specs,
        out_shape=[jax.ShapeDtypeStruct((bsz, t, width), F32)] * 2,
        scratch_shapes=[pltpu.VMEM((n_units, LANE, LANE), F32)],
        compiler_params=_params("arbitrary"),
        name="rwkv_scan",
    )(*args[0], *args[1])


def _ssd_body(*refs):
    fwd_in, bwd_in, (yf_ref, yb_ref, st_ref) = refs[:4], refs[4:8], refs[8:]

    @pl.when(pl.program_id(0) == 0)
    def _():
        st_ref[...] = jnp.zeros_like(st_ref)

    bsz = fwd_in[0].shape[0]
    n_pairs = st_ref.shape[0] // (2 * bsz)
    results = []
    for d, in_refs in enumerate((fwd_in, bwd_in)):
        for row in range(bsz):
            base = (d * bsz + row) * n_pairs
            results.append(_ssd_chunk(*(ref[row] for ref in in_refs), [st_ref[base + j] for j in range(n_pairs)],
                                      reverse=bool(d)))
    for u, (ys, new_states) in enumerate(results):
        d, row = divmod(u, bsz)
        (yf_ref, yb_ref)[d][row] = jnp.concatenate(ys, axis=1)
        for j in range(n_pairs):
            st_ref[u * n_pairs + j] = new_states[j]


def _ssd_chunk(xbc, dt, da_col, da_row, states, *, reverse):
    c = xbc.shape[0]
    n_h = dt.shape[1]
    n_pairs = n_h // 2
    gd = n_h * HEAD_DIM
    n_g = (xbc.shape[1] - gd) // (2 * MB_STATE)
    pairs_per_group = n_pairs // n_g
    ti = lax.broadcasted_iota(jnp.int32, (c, c), 0)
    si = lax.broadcasted_iota(jnp.int32, (c, c), 1)
    incl = (si >= ti) if reverse else (si <= ti)
    incl_t = (ti >= si) if reverse else (ti <= si)
    cs_c = jnp.dot(incl.astype(F32), da_col, preferred_element_type=F32, precision=HIGHEST)
    cs_r = jnp.dot(da_row, incl_t.astype(F32), preferred_element_type=F32, precision=HIGHEST)
    tot = cs_c[:1, :] if reverse else cs_c[c - 1:, :]
    e_cs = jnp.exp(cs_c)
    e_rem = jnp.exp(tot - cs_c)
    e_tot = jnp.exp(tot)
    bm = [xbc[:, gd + g * MB_STATE:gd + (g + 1) * MB_STATE].astype(BF16) for g in range(n_g)]
    cm = [xbc[:, gd + (n_g + g) * MB_STATE:gd + (n_g + g + 1) * MB_STATE].astype(BF16) for g in range(n_g)]
    gmat = [jnp.einsum('ln,sn->ls', cm[g], bm[g], preferred_element_type=F32) for g in range(n_g)]
    first = _first_head_lanes((c, LANE))
    first_rows = lax.broadcasted_iota(jnp.int32, (LANE, 1), 0) < HEAD_DIM
    pick = lambda m, ha: jnp.where(first, m[:, ha:ha + 1], m[:, ha + 1:ha + 2])
    ys, new_states = [], []
    for j in range(n_pairs):
        g = j // pairs_per_group
        ha = 2 * j
        sl = slice(j * LANE, (j + 1) * LANE)
        xdt = xbc[:, sl] * pick(dt, ha)
        xdt_b = xdt.astype(BF16)
        y_heads = []
        for h in (ha, ha + 1):
            seg = cs_c[:, h:h + 1] - cs_r[h:h + 1, :]
            lmat = jnp.exp(jnp.where(incl, seg, NA_MASK))
            y_heads.append(jnp.dot((gmat[g] * lmat).astype(BF16), xdt_b, preferred_element_type=F32))
        st = states[j]
        y_off = jnp.einsum('ln,pn->lp', cm[g], st.astype(BF16), preferred_element_type=F32) * pick(e_cs, ha)
        ys.append(jnp.where(first, y_heads[0], y_heads[1]) + y_off)
        xdec = (xdt * pick(e_rem, ha)).astype(BF16)
        keep = jnp.where(first_rows, e_tot[:, ha:ha + 1], e_tot[:, ha + 1:ha + 2])
        new_states.append(keep * st + jnp.einsum('lp,ln->pn', xdec, bm[g], preferred_element_type=F32))
    return ys, new_states


def ssd_scan(xbc, dts, das, *, n_ctx):
    bsz, t, width = xbc.shape
    n_h = dts[0].shape[2]
    gd = n_h * HEAD_DIM
    c = MB_CHUNK
    n_chunks, n_ctx_chunks = t // c, n_ctx // c
    in_specs, args, out_specs = [], [], []
    for d, reverse in enumerate((False, True)):
        idx = lambda i, reverse=reverse: _scan_chunk_index(i, n_ctx_chunks, n_chunks, reverse)
        in_specs += [pl.BlockSpec((bsz, c, width), lambda i, idx=idx: (0, idx(i), 0)),
                     pl.BlockSpec((bsz, c, n_h), lambda i, idx=idx: (0, idx(i), 0)),
                     pl.BlockSpec((bsz, c, n_h), lambda i, idx=idx: (0, idx(i), 0)),
                     pl.BlockSpec((bsz, n_h, c), lambda i, idx=idx: (0, 0, idx(i)))]
        args += [xbc, dts[d], das[d], jnp.swapaxes(das[d], 1, 2)]
        out_specs.append(pl.BlockSpec((bsz, c, gd), lambda i, idx=idx: (0, idx(i), 0)))
    return pl.pallas_call(
        _ssd_body,
        grid=(n_chunks,),
        in_specs=in_specs,
        out_specs=out_specs,
        out_shape=[jax.ShapeDtypeStruct((bsz, t, gd), F32)] * 2,
        scratch_shapes=[pltpu.VMEM((2 * bsz * (n_h // 2), 2 * HEAD_DIM, MB_STATE), F32)],
        compiler_params=_params("arbitrary"),
        name="ssd_scan",
    )(*args)


def _softmax_pv(s_parts, v_parts):
    m = functools.reduce(jnp.maximum, [jnp.max(s, axis=-1, keepdims=True) for s in s_parts])
    ps = [jnp.exp(s - m) for s in s_parts]
    l = functools.reduce(jnp.add, [jnp.sum(p, axis=-1, keepdims=True) for p in ps])
    o = functools.reduce(jnp.add, [jnp.dot(p.astype(BF16), v, preferred_element_type=F32)
                                   for p, v in zip(ps, v_parts)])
    return o / l


def _rope_lanes(x, cos_tab, sin_tab):
    width = x.shape[1]
    reps = width // MLA_QK_PAD
    cos_w = jnp.concatenate([cos_tab] * reps, axis=1) if reps > 1 else cos_tab
    sin_w = jnp.concatenate([sin_tab] * reps, axis=1) if reps > 1 else sin_tab
    lane = lax.broadcasted_iota(jnp.int32, x.shape, 1) % MLA_QK_PAD
    first_half = (lane - MLA_NOPE) % (MLA_ROPE // 2) < MLA_ROPE // 4
    shift = MLA_ROPE // 4
    partner = jnp.where(first_half, pltpu.roll(x, width - shift, axis=1), pltpu.roll(x, shift, axis=1))
    return x * cos_w + partner * sin_w


def _attn_body(q_ref, k_ref, v_ref, pe_ref, cq_ref, sq_ref, ck_ref, sk_ref, o_ref, kb_ref, vb_ref,
               *, n_ctx, n_ctx_tiles):
    dq = q_ref.shape[2] // 2

    @pl.when(pl.program_id(2) == 0)
    def _():
        pe_rot = _rope_lanes(pe_ref[0], ck_ref[...], sk_ref[...])
        kb_ref[...] = (k_ref[0] + jnp.concatenate([pe_rot, pe_rot], axis=1)).astype(BF16)
        vb_ref[...] = v_ref[0].astype(BF16)

    q2 = (_rope_lanes(q_ref[0], cq_ref[...], sq_ref[...]) * (MLA_QK ** -0.5)).astype(BF16)

    def attend(n_keys):
        v2 = vb_ref[0:n_keys]
        own = _first_head_lanes(v2.shape)
        outs = []
        for x in range(2):
            s = jnp.einsum('qd,kd->qk', q2[:, x * dq:(x + 1) * dq], kb_ref[0:n_keys, x * dq:(x + 1) * dq],
                           preferred_element_type=F32)
            p = jnp.exp((s - jnp.max(s, axis=-1, keepdims=True)).astype(BF16))
            vx = jnp.where(own if x == 0 else ~own, v2, jnp.ones_like(v2))
            o = jnp.dot(p, vx, preferred_element_type=F32)
            outs.append(o / pltpu.roll(o, HEAD_DIM, axis=1))
        o_ref[0] = jnp.where(_first_head_lanes(outs[0].shape), outs[0], outs[1])

    @pl.when(pl.program_id(2) < n_ctx_tiles)
    def _():
        attend(n_ctx)

    @pl.when(pl.program_id(2) >= n_ctx_tiles)
    def _():
        attend(k_ref.shape[1])


def attention(q, kv, p, tables, *, n_ctx, tile):
    bsz, t, _ = q.shape
    n_pairs = N_HEADS // 2
    dq2 = 2 * MLA_QK_PAD
    v_col0 = N_HEADS * MLA_QK_PAD // LANE
    cos_tab, sin_tab = tables
    q_tab = pl.BlockSpec((tile, MLA_QK_PAD), lambda b, j, i: (i, 0))
    k_tab = pl.BlockSpec((t, MLA_QK_PAD), lambda b, j, i: (0, 0))
    return pl.pallas_call(
        functools.partial(_attn_body, n_ctx=n_ctx, n_ctx_tiles=n_ctx // tile),
        grid=(bsz, n_pairs, t // tile),
        in_specs=[pl.BlockSpec((1, tile, dq2), lambda b, j, i: (b, i, j)),
                  pl.BlockSpec((1, t, dq2), lambda b, j, i: (b, 0, j)),
                  pl.BlockSpec((1, t, LANE), lambda b, j, i: (b, 0, v_col0 + j)),
                  pl.BlockSpec((1, t, LANE), lambda b, j, i: (b, 0, OFF_MLA_PE // LANE)),
                  q_tab, q_tab, k_tab, k_tab],
        out_specs=pl.BlockSpec((1, tile, LANE), lambda b, j, i: (b, i, j)),
        out_shape=jax.ShapeDtypeStruct((bsz, t, N_HEADS * HEAD_DIM), F32),
        scratch_shapes=[pltpu.VMEM((t, dq2), BF16), pltpu.VMEM((t, LANE), BF16)],
        compiler_params=_params("parallel", "parallel", "arbitrary"),
        name="attention",
    )(q, kv, kv, p, cos_tab, sin_tab, cos_tab, sin_tab)


def _norm_mm_body(x_ref, g_ref, w_ref, o_ref):
    x = x_ref[...]
    xn = x * lax.rsqrt(jnp.mean(x * x, axis=-1, keepdims=True) + RMS_EPS) * g_ref[...]
    o_ref[...] = jnp.dot(xn.astype(BF16), w_ref[...], preferred_element_type=F32)


def norm_matmul(p, col, g, w):
    m = p.shape[0]
    k, n = w.shape
    tm = _pick_tile(m, 640, 16)
    return pl.pallas_call(
        _norm_mm_body,
        grid=(m // tm,),
        in_specs=[pl.BlockSpec((tm, k), lambda i: (i, col // k)),
                  pl.BlockSpec((1, k), lambda i: (0, 0)),
                  pl.BlockSpec((k, n), lambda i: (0, 0))],
        out_specs=pl.BlockSpec((tm, n), lambda i: (i, 0)),
        out_shape=jax.ShapeDtypeStruct((m, n), F32),
        compiler_params=_params("parallel"),
        name="norm_matmul",
    )(p, g.reshape(1, k), w)


def _na_body(q_ref, k_ref, v_ref, bias_ref, o_ref, kb_ref, vb_ref, *, n_ctx, n_rows):
    i = pl.program_id(2)
    rows_per_step = q_ref.shape[1] // GRID_W
    win = NA_WIN_ROWS * GRID_W
    q2 = q_ref[0] * (HEAD_DIM ** -0.5)

    def head_queries(q_rows, x):
        first = _first_head_lanes(q_rows.shape)
        return jnp.where(first if x == 0 else ~first, q_rows, 0.0).astype(BF16)

    def scores(q, k):
        return jnp.einsum('qd,kd->qk', q, k, preferred_element_type=F32)

    @pl.when(i == 0)
    def _():
        kb_ref[...] = k_ref[0].astype(BF16)
        vb_ref[...] = v_ref[0].astype(BF16)
        kc, vc = kb_ref[0:n_ctx], vb_ref[0:n_ctx]
        outs = [_softmax_pv([scores(head_queries(q2, x), kc)], [vc]) for x in range(2)]
        o_ref[0] = jnp.where(_first_head_lanes(outs[0].shape), outs[0], outs[1])

    @pl.when(i > 0)
    def _():
        qs, kws, vws, biases = [], [], [], []
        for rr in range(rows_per_step):
            r = (i - 1) * rows_per_step + rr
            rs = jnp.clip(r - NA_WIN_ROWS // 2, 0, n_rows - NA_WIN_ROWS)
            start = pl.multiple_of(n_ctx + rs * GRID_W, GRID_W)
            kws.append(kb_ref[pl.ds(start, win)])
            vws.append(vb_ref[pl.ds(start, win)])
            q_row = q2[rr * GRID_W:(rr + 1) * GRID_W]
            qs.append(jnp.concatenate([head_queries(q_row, 0), head_queries(q_row, 1)], axis=0))
            biases.append(jnp.concatenate([bias_ref[0, r - rs], bias_ref[1, r - rs]], axis=0))
        qb = jnp.stack(qs, axis=0)
        kc, vc = kb_ref[0:n_ctx], vb_ref[0:n_ctx]
        s_loc = jnp.einsum('uqd,ukd->uqk', qb, jnp.stack(kws, axis=0), preferred_element_type=F32)
        s_loc = s_loc + jnp.stack(biases, axis=0)
        s_ctx = scores(qb.reshape(rows_per_step * 2 * GRID_W, LANE), kc).reshape(rows_per_step, 2 * GRID_W, n_ctx)
        m = jnp.maximum(jnp.max(s_loc, axis=-1, keepdims=True), jnp.max(s_ctx, axis=-1, keepdims=True))
        p_loc = jnp.exp(s_loc - m)
        p_ctx = jnp.exp(s_ctx - m)
        l = jnp.sum(p_loc, axis=-1, keepdims=True) + jnp.sum(p_ctx, axis=-1, keepdims=True)
        o = jnp.einsum('uqk,ukd->uqd', p_loc.astype(BF16), jnp.stack(vws, axis=0), preferred_element_type=F32)
        o = o + jnp.dot(p_ctx.reshape(rows_per_step * 2 * GRID_W, n_ctx).astype(BF16), vc,
                        preferred_element_type=F32).reshape(o.shape)
        o = o / l
        first = _first_head_lanes((GRID_W, LANE))
        for rr in range(rows_per_step):
            o_ref[0, rr * GRID_W:(rr + 1) * GRID_W] = jnp.where(first, o[rr, :GRID_W], o[rr, GRID_W:])


def na_bias_table(rpb):
    wr = NA_WIN_ROWS
    qv = np.arange(GRID_W)[:, None]
    cv = np.arange(GRID_W)[None, :]
    col_start = np.clip(qv - NA_WIN_COLS // 2, 0, GRID_W - NA_WIN_COLS)
    valid = (cv >= col_start) & (cv < col_start + NA_WIN_COLS)
    offs = np.arange(2 * NA_WIN_COLS - 1)[:, None, None]
    pick = ((cv - qv + (NA_WIN_COLS - 1))[None] == offs) & valid[None]
    cols = jnp.einsum('hro,oqc->hrqc', rpb, jnp.asarray(pick, F32), precision=HIGHEST)
    cols = jnp.where(valid, cols, NA_MASK)
    tab = jnp.stack([cols[:, wr - 1 - var:2 * wr - 1 - var] for var in range(wr)], axis=1)
    tab = jnp.transpose(tab, (0, 1, 3, 2, 4))
    return tab.reshape(rpb.shape[0], wr, GRID_W, wr * GRID_W)


def neighborhood_attention(p, bias, *, n_ctx, col0):
    bsz, t, _ = p.shape
    n_rows = (t - n_ctx) // GRID_W
    n_pairs = GROUP_DIM // LANE
    c0 = col0 // LANE
    return pl.pallas_call(
        functools.partial(_na_body, n_ctx=n_ctx, n_rows=n_rows),
        grid=(bsz, n_pairs, t // n_ctx),
        in_specs=[pl.BlockSpec((1, n_ctx, LANE), lambda b, j, i: (b, i, c0 + j)),
                  pl.BlockSpec((1, t, LANE), lambda b, j, i: (b, 0, c0 + n_pairs + j)),
                  pl.BlockSpec((1, t, LANE), lambda b, j, i: (b, 0, c0 + 2 * n_pairs + j)),
                  pl.BlockSpec((2, NA_WIN_ROWS, GRID_W, NA_WIN_ROWS * GRID_W), lambda b, j, i: (j, 0, 0, 0))],
        out_specs=pl.BlockSpec((1, n_ctx, LANE), lambda b, j, i: (b, i, j)),
        out_shape=jax.ShapeDtypeStruct((bsz, t, GROUP_DIM), F32),
        scratch_shapes=[pltpu.VMEM((t, LANE), BF16), pltpu.VMEM((t, LANE), BF16)],
        compiler_params=_params("parallel", "parallel", "arbitrary"),
        name="neighborhood_attention",
    )(p, p, p, bias)


def _moe_body(h_ref, gate_ref, wg_ref, wu_ref, wd_ref, x_ref, gc_ref, gb_ref, lg_ref, lb_ref, o_ref, acc_ref,
              *, tokens_per_batch, n_ctx):
    e = pl.program_id(1)
    is_ctx = _is_ctx_row(x_ref.shape[0], tokens_per_batch, n_ctx)

    @pl.when(e == 0)
    def _():
        acc_ref[...] = jnp.zeros_like(acc_ref)

    h = h_ref[...]
    hid = jax.nn.silu(jnp.dot(h, wg_ref[0], preferred_element_type=F32)) * jnp.dot(h, wu_ref[0], preferred_element_type=F32)
    lane = lax.broadcasted_iota(jnp.int32, gate_ref.shape, 1)
    gate = jnp.sum(jnp.where(lane == e, gate_ref[...], 0.0), axis=1, keepdims=True)
    acc_ref[...] += gate * jnp.dot(hid.astype(BF16), wd_ref[0], preferred_element_type=F32)

    @pl.when(e == pl.num_programs(1) - 1)
    def _():
        z = DEEPNORM_ALPHA * x_ref[...] + jnp.where(is_ctx, gc_ref[0], gb_ref[0]) * acc_ref[...]
        mu = jnp.mean(z, axis=-1, keepdims=True)
        zc = z - mu
        var = jnp.mean(zc * zc, axis=-1, keepdims=True)
        o_ref[...] = zc * lax.rsqrt(var + LN_EPS) * lg_ref[...] + lb_ref[...]


def moe_experts(h, gate, w_gate, w_up, w_down, layer, xs, mod_l, ln_g, ln_b, *, tokens_per_batch, n_ctx):
    m, d = h.shape
    n_e, d_e = w_gate.shape[1], w_gate.shape[3]
    bsz = m // tokens_per_batch
    tm = _pick_tile(tokens_per_batch, 640, 16)
    tiles_per_batch = tokens_per_batch // tm
    row = lambda i, e: (i, 0)
    fixed = lambda i, e: (0, 0)
    return pl.pallas_call(
        functools.partial(_moe_body, tokens_per_batch=tokens_per_batch, n_ctx=n_ctx),
        grid=(m // tm, n_e),
        in_specs=[pl.BlockSpec((tm, d), row),
                  pl.BlockSpec((tm, LANE), row),
                  pl.BlockSpec((None, 1, d, d_e), lambda i, e: (layer, e, 0, 0)),
                  pl.BlockSpec((None, 1, d, d_e), lambda i, e: (layer, e, 0, 0)),
                  pl.BlockSpec((None, 1, d_e, d), lambda i, e: (layer, e, 0, 0)),
                  pl.BlockSpec((tm, d), row), *_mod_specs(bsz, tiles_per_batch, 5, d),
                  pl.BlockSpec((1, d), fixed), pl.BlockSpec((1, d), fixed)],
        out_specs=pl.BlockSpec((tm, d), row),
        out_shape=jax.ShapeDtypeStruct((m, d), F32),
        scratch_shapes=[pltpu.VMEM((tm, d), F32)],
        compiler_params=_params("parallel", "arbitrary"),
        name="moe_experts",
    )(h, gate, w_gate, w_up, w_down, xs, mod_l, mod_l, ln_g.reshape(1, d), ln_b.reshape(1, d))


def _rope_tables(t, n_ctx):
    n_freq = MLA_ROPE // 4
    pos_t = np.arange(t - n_ctx)
    pos = np.stack([pos_t // GRID_W, pos_t % GRID_W], axis=-1).astype(np.float32)
    inv_freq = jnp.asarray(ROPE_BASE, F32) ** (-jnp.arange(n_freq, dtype=F32) / n_freq)
    ang = jnp.asarray(pos)[:, :, None] * inv_freq
    ang = jnp.concatenate([jnp.zeros((n_ctx, 2, n_freq), F32), ang], axis=0)
    cos = jnp.broadcast_to(jnp.cos(ang)[:, :, None, :], (t, 2, 2, n_freq)).reshape(t, MLA_ROPE)
    sin = jnp.sin(ang)[:, :, None, :] * jnp.asarray([-1.0, 1.0], F32)[None, None, :, None]
    sin = sin.reshape(t, MLA_ROPE)
    pad = MLA_QK_PAD - MLA_QK
    cos_tab = jnp.concatenate([jnp.ones((t, MLA_NOPE), F32), cos, jnp.zeros((t, pad), F32)], axis=1)
    sin_tab = jnp.concatenate([jnp.zeros((t, MLA_NOPE), F32), sin, jnp.zeros((t, pad), F32)], axis=1)
    return cos_tab, sin_tab


HALO = 8


def _shifted_rows(x, halo_prev, halo_next, offset, pos, n_ctx, t):
    tm = x.shape[0]
    out = pltpu.roll(x, (-offset) % tm, axis=0)
    row = lax.broadcasted_iota(jnp.int32, (tm, 1), 0)
    for i in range(abs(offset)):
        if offset < 0:
            out = jnp.where(row == i, halo_prev[HALO + offset + i:HALO + offset + i + 1], out)
        else:
            out = jnp.where(row == tm - offset + i, halo_next[i:i + 1], out)
    src = pos + offset
    same = (src >= 0) & (src < t) & ((pos < n_ctx) == (src < n_ctx))
    return jnp.where(same, out, 0.0)


def _halo_specs(tm, t, width, col):
    per_tile, last = tm // HALO, t // HALO - 1
    return [pl.BlockSpec((1, HALO, width), lambda b, i: (b, jnp.maximum(i * per_tile - 1, 0), col)),
            pl.BlockSpec((1, HALO, width), lambda b, i: (b, jnp.minimum((i + 1) * per_tile, last), col))]


def _rwkv_prep_body(p_ref, pp_ref, pn_ref, mu_ref, w0_ref, a0_ref, kkw_ref, ka_ref, rk_ref, lora_ref,
                    r_ref, v_ref, kk_ref, kd0_ref, kd1_ref, a0o_ref, a1o_ref, lw0_ref, lw1_ref, g_ref, bonus_ref,
                    *, n_ctx, t):
    x = p_ref[0]
    tm = x.shape[0]
    gd = GROUP_DIM
    pos = pl.program_id(1) * tm + lax.broadcasted_iota(jnp.int32, (tm, 1), 0)
    prev = _shifted_rows(x, pp_ref[0], pn_ref[0], -1, pos, n_ctx, t)
    nxt = _shifted_rows(x, pp_ref[0], pn_ref[0], 1, pos, n_ctx, t)
    x = x + mu_ref[0:1] * (prev - x) + mu_ref[1:2] * (nxt - x)
    r, k, v = x[:, :gd], x[:, gd:2 * gd], x[:, 2 * gd:3 * gd]
    lora_in = jnp.concatenate([jnp.tanh(x[:, 3 * gd:3 * gd + 2 * RW_LORA]),
                               x[:, 3 * gd + 2 * RW_LORA:3 * gd + 4 * RW_LORA],
                               jax.nn.sigmoid(x[:, 3 * gd + 4 * RW_LORA:])], axis=1)
    lo = jnp.dot(lora_in.astype(BF16), lora_ref[...], preferred_element_type=F32)
    lane_r = lax.broadcasted_iota(jnp.int32, (gd, gd), 0) // HEAD_DIM
    lane_c = lax.broadcasted_iota(jnp.int32, (gd, gd), 1) // HEAD_DIM
    same_head = (lane_r == lane_c).astype(F32)
    head_sum = lambda m: _bmm(m, same_head, 'tn,nm->tm', 'split_a')
    kk = k * kkw_ref[...]
    kk_ref[0] = kk * lax.rsqrt(jnp.maximum(head_sum(kk * kk), 1e-12))
    for d, (kd_ref, a_ref, lw_ref) in enumerate(((kd0_ref, a0o_ref, lw0_ref), (kd1_ref, a1o_ref, lw1_ref))):
        a = jax.nn.sigmoid(a0_ref[d:d + 1] + lo[:, (2 + d) * gd:(3 + d) * gd])
        a_ref[0] = a
        lw_ref[0] = -RW_DECAY_SCALE * jax.nn.sigmoid(w0_ref[d:d + 1] + lo[:, d * gd:(d + 1) * gd])
        kd_ref[0] = k * (1.0 + (a - 1.0) * ka_ref[...])
    r_ref[0] = r
    v_ref[0] = v
    g_ref[0] = lo[:, 4 * gd:]
    bonus_ref[0] = head_sum(r * k * rk_ref[...]) * v


def rwkv_prep(p, mu, w0, a0, k_k, k_a, r_k, lora_w, *, n_ctx):
    bsz, t, _ = p.shape
    gd = GROUP_DIM
    tm = _pick_tile(t, 320, 16)
    row = lambda b, i: (b, i, 0)
    fixed = lambda b, i: (0, 0)
    vec = lambda w: w.reshape(1, gd)
    out = jax.ShapeDtypeStruct((bsz, t, gd), F32)
    return pl.pallas_call(
        functools.partial(_rwkv_prep_body, n_ctx=n_ctx, t=t),
        grid=(bsz, t // tm),
        in_specs=[pl.BlockSpec((1, tm, RW_IN), row), *_halo_specs(tm, t, RW_IN, 0),
                  pl.BlockSpec((2, RW_IN), fixed), pl.BlockSpec((2, gd), fixed), pl.BlockSpec((2, gd), fixed),
                  pl.BlockSpec((1, gd), fixed), pl.BlockSpec((1, gd), fixed), pl.BlockSpec((1, gd), fixed),
                  pl.BlockSpec(lora_w.shape, fixed)],
        out_specs=[pl.BlockSpec((1, tm, gd), row)] * 11,
        out_shape=[out] * 11,
        compiler_params=_params("parallel", "parallel"),
        name="rwkv_prep",
    )(p, p, p, mu, w0, a0, vec(k_k), vec(k_a), vec(r_k), lora_w)


def _rwkv_mixer(p, n_ctx, mu, w0, w_up, a0, a_up, g_up, k_k, k_a, r_k, gn_g, gn_b):
    bsz, t, _ = p.shape
    gd = GROUP_DIM
    n_lora = 4 * RW_LORA + RW_G_LORA
    lora_w = jnp.zeros((n_lora, 5 * gd), F32)
    for j, blk in enumerate((w_up[0], w_up[1], a_up[0], a_up[1])):
        lora_w = lora_w.at[j * RW_LORA:(j + 1) * RW_LORA, j * gd:(j + 1) * gd].set(blk)
    lora_w = lora_w.at[4 * RW_LORA:, 4 * gd:].set(g_up)
    r, v, kk, kd0, kd1, a_f, a_b, lw_f, lw_b, g, bonus = rwkv_prep(p, mu, w0, a0, k_k, k_a, r_k,
                                                                   lora_w.astype(BF16), n_ctx=n_ctx)
    y_fwd, y_bwd = rwkv_scan(r, v, kk, (kd0, kd1), (a_f, a_b), (lw_f, lw_b), n_ctx=n_ctx)
    flat = lambda m: m.reshape(bsz * t, gd)
    return flat(y_fwd), flat(y_bwd), flat(bonus), flat(g), gn_g, gn_b


def _conv_body(x_ref, xp_ref, xn_ref, w_ref, b_ref, o_ref, *, n_ctx, t):
    x = x_ref[0]
    tm = x.shape[0]
    pos = pl.program_id(1) * tm + lax.broadcasted_iota(jnp.int32, (tm, 1), 0)
    acc = b_ref[...] + w_ref[MB_CONV // 2:MB_CONV // 2 + 1] * x
    for j in range(MB_CONV):
        offset = j - MB_CONV // 2
        if offset != 0:
            acc = acc + w_ref[j:j + 1] * _shifted_rows(x, xp_ref[0], xn_ref[0], offset, pos, n_ctx, t)
    o_ref[0] = jax.nn.silu(acc)


def conv_silu(p, conv_w, conv_b, *, n_ctx):
    bsz, t, _ = p.shape
    width = MB_CONV_DIM
    tm = _pick_tile(t, 640, 16)
    col = OFF_MB_XBC // width
    return pl.pallas_call(
        functools.partial(_conv_body, n_ctx=n_ctx, t=t),
        grid=(bsz, t // tm),
        in_specs=[pl.BlockSpec((1, tm, width), lambda b, i: (b, i, col)), *_halo_specs(tm, t, width, col),
                  pl.BlockSpec((MB_CONV, width), lambda b, i: (0, 0)), pl.BlockSpec((1, width), lambda b, i: (0, 0))],
        out_specs=pl.BlockSpec((1, tm, width), lambda b, i: (b, i, 0)),
        out_shape=jax.ShapeDtypeStruct((bsz, t, width), F32),
        compiler_params=_params("parallel", "parallel"),
        name="conv_silu",
    )(p, p, p, conv_w, conv_b.reshape(1, width))


def _mamba_mixer(p, n_ctx, conv_w, conv_b, a_log, dt_bias, d_skip, norm_g):
    bsz, t, _ = p.shape
    dt_raw = p[..., OFF_MB_DT:OFF_MB_DT + 2 * N_HEADS]
    xbc = conv_silu(p, conv_w, conv_b, n_ctx=n_ctx)
    a_neg = -jnp.exp(a_log)
    dts = [jax.nn.softplus(dt_raw[..., d * N_HEADS:(d + 1) * N_HEADS] + dt_bias[d]) for d in range(2)]
    y_fwd, y_bwd = ssd_scan(xbc, dts, [dts[d] * a_neg[d] for d in range(2)], n_ctx=n_ctx)
    flat = lambda m: m.reshape(bsz * t, -1)
    return flat(y_fwd), flat(y_bwd), flat(xbc), d_skip, norm_g


def _mla_mixer(p, n_ctx, q_norm, w_uq, kv_norm, w_ukv, tile, tables):
    bsz, t, n_cols = p.shape
    n_h, pad = N_HEADS, MLA_QK_PAD - MLA_QK
    w_q = jnp.pad(w_uq.reshape(-1, n_h, MLA_QK), ((0, 0), (0, 0), (0, pad))).reshape(-1, n_h * MLA_QK_PAD)
    w_kv = w_ukv.reshape(-1, n_h, MLA_NOPE + HEAD_DIM)
    w_k = jnp.pad(w_kv[..., :MLA_NOPE], ((0, 0), (0, 0), (0, MLA_QK_PAD - MLA_NOPE))).reshape(-1, n_h * MLA_QK_PAD)
    w_v = w_kv[..., MLA_NOPE:].reshape(-1, n_h * HEAD_DIM)
    p2 = p.reshape(bsz * t, n_cols)
    q = norm_matmul(p2, OFF_MLA_Q, q_norm, w_q.astype(BF16)).reshape(bsz, t, -1)
    kv = norm_matmul(p2, OFF_MLA_KV, kv_norm, jnp.concatenate([w_k, w_v], axis=1).astype(BF16)).reshape(bsz, t, -1)
    return attention(q, kv, p, tables, n_ctx=n_ctx, tile=tile)


def _first_max(vals, excluded):
    live = [jnp.where(x, -jnp.inf, v) for v, x in zip(vals, excluded)]
    top = functools.reduce(jnp.maximum, live)
    found = jnp.zeros_like(top, dtype=jnp.bool_)
    first = []
    for v, x in zip(live, excluded):
        hit = (v == top) & ~found & ~x
        first.append(hit)
        found = found | hit
    return top, first


def _router_body(x_ref, sc_ref, sh_ref, rw_ref, rb_ref, h_ref, gate_ref):
    h = x_ref[...] * (1.0 + sc_ref[0]) + sh_ref[0]
    h_ref[...] = h.astype(BF16)
    logits = jnp.dot(h, rw_ref[...], preferred_element_type=F32, precision=HIGHEST)
    scores = jax.nn.sigmoid(logits.T[:N_EXPERTS])
    biased = scores + rb_ref[...]
    rows = [biased[e:e + 1] for e in range(N_EXPERTS)]
    never = jnp.zeros_like(rows[0], dtype=jnp.bool_)
    group_scores, picked = [], []
    for g in range(N_EXPERT_GROUPS):
        vals = rows[g * EXPERTS_PER_GROUP:(g + 1) * EXPERTS_PER_GROUP]
        top1, first = _first_max(vals, [never] * EXPERTS_PER_GROUP)
        top2, second = _first_max(vals, first)
        group_scores.append(top1 + top2)
        picked.append([a | b for a, b in zip(first, second)])
    _, group_sel = _first_max(group_scores, [never] * N_EXPERT_GROUPS)
    chosen = [jnp.where(group_sel[e // EXPERTS_PER_GROUP] & picked[e // EXPERTS_PER_GROUP][e % EXPERTS_PER_GROUP],
                        scores[e:e + 1], 0.0) for e in range(N_EXPERTS)]
    denom = functools.reduce(jnp.add, chosen)
    gates = [w / denom * ROUTED_SCALE for w in chosen]
    gates.append(jnp.zeros((LANE - N_EXPERTS, denom.shape[1]), F32))
    gate_ref[...] = jnp.concatenate(gates, axis=0).T


def router(xs, mod_l, router_w, router_b, *, tokens_per_batch, n_ctx, tile):
    m, d = xs.shape
    bsz = m // tokens_per_batch
    tiles_per_batch, n_ctx_tiles = tokens_per_batch // tile, n_ctx // tile

    def mod_row(i):
        return jnp.where(i % tiles_per_batch < n_ctx_tiles, bsz, i // tiles_per_batch)

    rw_pad = jnp.pad(router_w, ((0, 0), (0, LANE - N_EXPERTS)))
    return pl.pallas_call(
        _router_body,
        grid=(m // tile,),
        in_specs=[pl.BlockSpec((tile, d), lambda i: (i, 0)),
                  pl.BlockSpec((None, 1, d), lambda i: (mod_row(i), 0, 4)),
                  pl.BlockSpec((None, 1, d), lambda i: (mod_row(i), 0, 3)),
                  pl.BlockSpec((d, LANE), lambda i: (0, 0)),
                  pl.BlockSpec((N_EXPERTS, 1), lambda i: (0, 0))],
        out_specs=[pl.BlockSpec((tile, d), lambda i: (i, 0)),
                   pl.BlockSpec((tile, LANE), lambda i: (i, 0))],
        out_shape=[jax.ShapeDtypeStruct((m, d), BF16), jax.ShapeDtypeStruct((m, LANE), F32)],
        compiler_params=_params("parallel"),
        name="router",
    )(xs, mod_l, mod_l, rw_pad, router_b.reshape(N_EXPERTS, 1))


def _pad_w_in(w):
    zeros = lambda n: jnp.zeros(w.shape[:-1] + (n,), w.dtype)
    mb0 = RW_IN
    mla0 = mb0 + MB_IN
    na0 = mla0 + MLA_IN
    rw, mb, mla, na = w[..., :mb0], w[..., mb0:mla0], w[..., mla0:na0], w[..., na0:]
    q_kv, pe = mla[..., :MLA_Q_LORA + MLA_KV_LORA], mla[..., MLA_Q_LORA + MLA_KV_LORA:]
    return jnp.concatenate([rw, q_kv, zeros(PE_LANE), pe, zeros(LANE - PE_LANE - MLA_ROPE),
                            mb, zeros(LANE - 2 * N_HEADS), na], axis=-1)


def kernel(x, c, ctx, c_ctx, ada_w, ada_b, w_in, w_out, ln1_g, ln1_b, ln2_g, ln2_b, rw_mu, rw_w0, rw_w_up, rw_a0, rw_a_up, rw_g_up, rw_k_k, rw_k_a, rw_r_k, rw_gn_g, rw_gn_b, mb_conv_w, mb_conv_b, mb_a_log, mb_dt_bias, mb_d, mb_norm_g, mla_q_norm, mla_w_uq, mla_kv_norm, mla_w_ukv, na_rpb, router_w, router_b, exp_w_gate, exp_w_up, exp_w_down):
    bsz, seq, d = x.shape
    n_ctx = ctx.shape[1]
    t = n_ctx + seq
    depth = ada_w.shape[0]
    tile = min(256, n_ctx)
    cond = jnp.concatenate([jax.nn.silu(c), jax.nn.silu(c_ctx)[None]], axis=0)
    mod = modulation(cond, ada_w, ada_b)
    xs = jnp.concatenate([ctx, x], axis=1)
    expert_w = [w.astype(BF16) for w in (exp_w_gate, exp_w_up, exp_w_down)]
    rope_tables = _rope_tables(t, n_ctx)
    w_in_all = _pad_w_in(w_in).astype(BF16)
    w_out_all = w_out.astype(BF16)
    m = bsz * t
    for l in range(depth):
        mod_l = mod[l][:, None, :]
        p = proj_in(xs.reshape(m, d), mod_l, w_in_all, l, tokens_per_batch=t, n_ctx=n_ctx).reshape(bsz, t, N_IN_PAD)
        o_rw = _rwkv_mixer(p, n_ctx, rw_mu[l], rw_w0[l], rw_w_up[l], rw_a0[l], rw_a_up[l],
                           rw_g_up[l], rw_k_k[l], rw_k_a[l], rw_r_k[l], rw_gn_g[l], rw_gn_b[l])
        o_mb = _mamba_mixer(p, n_ctx, mb_conv_w[l], mb_conv_b[l], mb_a_log[l], mb_dt_bias[l], mb_d[l], mb_norm_g[l])
        o_mla = _mla_mixer(p, n_ctx, mla_q_norm[l], mla_w_uq[l], mla_kv_norm[l], mla_w_ukv[l], tile, rope_tables)
        o_na = neighborhood_attention(p, na_bias_table(na_rpb[l]), n_ctx=n_ctx, col0=OFF_NA)
        xs2 = proj_out(o_rw, o_mb, o_mla.reshape(m, GROUP_DIM), o_na.reshape(m, GROUP_DIM), p.reshape(m, N_IN_PAD),
                       w_out_all, l, xs.reshape(m, d), mod_l, ln1_g[l], ln1_b[l],
                       tokens_per_batch=t, n_ctx=n_ctx)
        h, gate = router(xs2, mod_l, router_w, router_b, tokens_per_batch=t, n_ctx=n_ctx, tile=tile)
        xs = moe_experts(h, gate, *expert_w, l, xs2, mod_l, ln2_g[l], ln2_b[l],
                         tokens_per_batch=t, n_ctx=n_ctx).reshape(bsz, t, d)
    return xs[:, n_ctx:]
```

```python
import functools
import math

import numpy as np
import jax
import jax.numpy as jnp
from jax import lax
from jax.experimental import pallas as pl
from jax.experimental.pallas import tpu as pltpu

F32 = jnp.float32
BF16 = jnp.bfloat16
HIGHEST = lax.Precision.HIGHEST

D_MODEL = 2048
DEPTH = 4
GRID_W = 64
GROUP_DIM = D_MODEL // 4
HEAD_DIM = 64
N_HEADS = GROUP_DIM // HEAD_DIM

RW_LORA = 64
RW_G_LORA = 128
RW_DECAY_SCALE = 0.606531
RW_GN_EPS = 64e-5
RW_IN = 3 * GROUP_DIM + 4 * RW_LORA + RW_G_LORA
RW_CHUNK = 64

MB_GROUPS = 2
MB_STATE = 128
MB_CONV = 5
MB_CHUNK = 128
MB_CONV_DIM = GROUP_DIM + 2 * MB_GROUPS * MB_STATE
MB_IN = GROUP_DIM + MB_CONV_DIM + 2 * N_HEADS

MLA_Q_LORA = 3 * D_MODEL // 16
MLA_KV_LORA = D_MODEL // 16
MLA_NOPE = 64
MLA_ROPE = 32
MLA_QK = MLA_NOPE + MLA_ROPE
MLA_QK_PAD = 128
MLA_IN = MLA_Q_LORA + MLA_KV_LORA + MLA_ROPE
ROPE_BASE = 10000.0

NA_WIN_ROWS = 8
NA_WIN_COLS = 16
NA_IN = 3 * GROUP_DIM
NA_MASK = -1e30

N_EXPERTS = 16
N_EXPERT_GROUPS = 4
EXPERTS_PER_GROUP = N_EXPERTS // N_EXPERT_GROUPS
TOP_K = 2
D_EXPERT = D_MODEL // 4
ROUTED_SCALE = 2.5

DEEPNORM_ALPHA = (2 * DEPTH) ** 0.25
LN_EPS = 1e-6
RMS_EPS = 1e-6

LANE = 128
OFF_RW = 0
OFF_MLA_Q = OFF_RW + RW_IN
OFF_MLA_KV = OFF_MLA_Q + MLA_Q_LORA
OFF_MLA_PE = OFF_MLA_KV + MLA_KV_LORA
PE_LANE = MLA_NOPE
OFF_MB_Z = OFF_MLA_PE + LANE
OFF_MB_XBC = OFF_MB_Z + GROUP_DIM
OFF_MB_DT = OFF_MB_XBC + MB_CONV_DIM
OFF_NA = OFF_MB_DT + LANE
N_IN_PAD = OFF_NA + NA_IN
assert OFF_MLA_Q % MLA_Q_LORA == 0 and OFF_MB_XBC % MB_CONV_DIM == 0 and OFF_MB_Z % GROUP_DIM == 0

VMEM_LIMIT = 56 * 1024 * 1024


def _params(*sem):
    return pltpu.CompilerParams(dimension_semantics=sem, vmem_limit_bytes=VMEM_LIMIT)


def _pick_tile(n, target, quantum):
    best = None
    for t in range(quantum, min(n, target) + 1, quantum):
        if n % t == 0:
            best = t
    assert best is not None, (n, target, quantum)
    return best


def _mm_body(x_ref, w_ref, o_ref, *, precision):
    o_ref[...] = jnp.dot(x_ref[...], w_ref[...], preferred_element_type=F32,
                         precision=precision).astype(o_ref.dtype)


def matmul(x, w, *, tm_target=512, tn_target=1024, precision=None, out_dtype=F32):
    m, k = x.shape
    n = w.shape[1]
    tm = _pick_tile(m, tm_target, 16)
    tn = _pick_tile(n, tn_target, LANE)
    return pl.pallas_call(
        functools.partial(_mm_body, precision=precision),
        grid=(m // tm, n // tn),
        in_specs=[pl.BlockSpec((tm, k), lambda i, j: (i, 0)),
                  pl.BlockSpec((k, tn), lambda i, j: (0, j))],
        out_specs=pl.BlockSpec((tm, tn), lambda i, j: (i, j)),
        out_shape=jax.ShapeDtypeStruct((m, n), out_dtype),
        compiler_params=_params("parallel", "parallel"),
        name="matmul",
    )(x, w)


def _is_ctx_row(tm, tokens_per_batch, n_ctx):
    pos = (pl.program_id(0) * tm) % tokens_per_batch + lax.broadcasted_iota(jnp.int32, (tm, 1), 0)
    return pos < n_ctx


def _proj_in_body(x_ref, scc_ref, scb_ref, shc_ref, shb_ref, w_ref, o_ref, xm_ref, *, tokens_per_batch, n_ctx):
    is_ctx = _is_ctx_row(x_ref.shape[0], tokens_per_batch, n_ctx)

    @pl.when(pl.program_id(1) == 0)
    def _():
        sc = jnp.where(is_ctx, scc_ref[0], scb_ref[0])
        sh = jnp.where(is_ctx, shc_ref[0], shb_ref[0])
        xm_ref[...] = (x_ref[...] * (1.0 + sc) + sh).astype(BF16)

    o_ref[...] = jnp.dot(xm_ref[...], w_ref[...], preferred_element_type=F32)


def _mod_specs(bsz, tiles_per_batch, chunk, d):
    return [pl.BlockSpec((None, 1, d), lambda i, *_: (bsz, 0, chunk)),
            pl.BlockSpec((None, 1, d), lambda i, *_: (i // tiles_per_batch, 0, chunk))]


def proj_in(xs, mod_l, w, layer, *, tokens_per_batch, n_ctx):
    m, d = xs.shape
    n = w.shape[2]
    bsz = m // tokens_per_batch
    tm = _pick_tile(tokens_per_batch, 1100, 16)
    tn = _pick_tile(n, 1152, LANE)
    tiles_per_batch = tokens_per_batch // tm
    return pl.pallas_call(
        functools.partial(_proj_in_body, tokens_per_batch=tokens_per_batch, n_ctx=n_ctx),
        grid=(m // tm, n // tn),
        in_specs=[pl.BlockSpec((tm, d), lambda i, j: (i, 0)),
                  *_mod_specs(bsz, tiles_per_batch, 1, d), *_mod_specs(bsz, tiles_per_batch, 0, d),
                  pl.BlockSpec((None, d, tn), lambda i, j: (layer, 0, j))],
        out_specs=pl.BlockSpec((tm, tn), lambda i, j: (i, j)),
        out_shape=jax.ShapeDtypeStruct((m, n), F32),
        scratch_shapes=[pltpu.VMEM((tm, d), BF16)],
        compiler_params=_params("parallel", "arbitrary"),
        name="proj_in",
    )(xs, mod_l, mod_l, mod_l, mod_l, w)


def _proj_out_body(ryf_ref, ryb_ref, rbonus_ref, rg_ref, gng_ref, gnb_ref,
                   myf_ref, myb_ref, mx_ref, mz_ref, dskip_ref, mng_ref,
                   mla_ref, na_ref, w_ref, x_ref, gc_ref, gb_ref, lg_ref, lb_ref, o_ref,
                   *, tokens_per_batch, n_ctx):
    gd = GROUP_DIM
    lane_r = lax.broadcasted_iota(jnp.int32, (gd, gd), 0) // HEAD_DIM
    lane_c = lax.broadcasted_iota(jnp.int32, (gd, gd), 1) // HEAD_DIM
    same_head = (lane_r == lane_c).astype(F32)
    head_mean = lambda m: _bmm(m, same_head, 'tn,nm->tm', 'split_a') * (1.0 / HEAD_DIM)
    y = ryf_ref[...] + ryb_ref[...]
    yc = y - head_mean(y)
    y = yc * lax.rsqrt(head_mean(yc * yc) + RW_GN_EPS) * gng_ref[...] + gnb_ref[...]
    o_rw = (y + rbonus_ref[...]) * rg_ref[...]
    y = (myf_ref[...] + myb_ref[...] + dskip_ref[...] * mx_ref[...]) * jax.nn.silu(mz_ref[...])
    group = gd // MB_GROUPS
    normed = []
    for g in range(MB_GROUPS):
        yg = y[:, g * group:(g + 1) * group]
        normed.append(yg * lax.rsqrt(jnp.mean(yg * yg, axis=-1, keepdims=True) + RMS_EPS))
    o_mb = jnp.concatenate(normed, axis=1) * mng_ref[...]

    y = None
    for j, mix in enumerate((o_rw, o_mb, mla_ref[...], na_ref[...])):
        part = jnp.dot(mix.astype(BF16), w_ref[j * gd:(j + 1) * gd, :], preferred_element_type=F32)
        y = part if y is None else y + part
    gate = jnp.where(_is_ctx_row(x_ref.shape[0], tokens_per_batch, n_ctx), gc_ref[0], gb_ref[0])
    z = DEEPNORM_ALPHA * x_ref[...] + gate * y
    mu = jnp.mean(z, axis=-1, keepdims=True)
    zc = z - mu
    var = jnp.mean(zc * zc, axis=-1, keepdims=True)
    o_ref[...] = zc * lax.rsqrt(var + LN_EPS) * lg_ref[...] + lb_ref[...]


def proj_out(rwkv, mamba, o_mla, o_na, p, w, layer, xs, mod_l, ln_g, ln_b, *, tokens_per_batch, n_ctx):
    m, d = xs.shape
    gd = GROUP_DIM
    bsz = m // tokens_per_batch
    tm = _pick_tile(tokens_per_batch, 320, 16)
    tiles_per_batch = tokens_per_batch // tm
    row = lambda i: (i, 0)
    fixed = lambda i: (0, 0)
    tok = pl.BlockSpec((tm, gd), row)
    vec = pl.BlockSpec((1, gd), fixed)
    ryf, ryb, rbonus, rg, gn_g, gn_b = rwkv
    myf, myb, xbc, d_skip, norm_g = mamba
    as_vec = lambda v: v.reshape(1, gd)
    return pl.pallas_call(
        functools.partial(_proj_out_body, tokens_per_batch=tokens_per_batch, n_ctx=n_ctx),
        grid=(m // tm,),
        in_specs=[tok, tok, tok, tok, vec, vec,
                  tok, tok, tok, pl.BlockSpec((tm, gd), lambda i: (i, OFF_MB_Z // gd)), vec, vec,
                  tok, tok,
                  pl.BlockSpec((None, d, d), lambda i: (layer, 0, 0)), pl.BlockSpec((tm, d), row),
                  *_mod_specs(bsz, tiles_per_batch, 2, d), pl.BlockSpec((1, d), fixed), pl.BlockSpec((1, d), fixed)],
        out_specs=pl.BlockSpec((tm, d), row),
        out_shape=jax.ShapeDtypeStruct((m, d), F32),
        compiler_params=_params("parallel"),
        name="proj_out",
    )(ryf, ryb, rbonus, rg, as_vec(gn_g), as_vec(gn_b),
      myf, myb, xbc, p, as_vec(jnp.repeat(d_skip, HEAD_DIM)), as_vec(norm_g),
      o_mla, o_na, w, xs, mod_l, mod_l, ln_g.reshape(1, d), ln_b.reshape(1, d))


def _mod_body(c_ref, w_ref, b_ref, o_ref, *, n_rows):
    w = w_ref[...]
    reps = w.shape[1] // LANE
    o_ref[...] = jnp.zeros_like(o_ref)
    for m in range(n_rows):
        cb = c_ref[m]
        cbt = jnp.concatenate([cb] * reps, axis=1)
        o_ref[m:m + 1, :] = jnp.sum(w * cbt, axis=0, keepdims=True) + b_ref[...]


def modulation(cond, ada_w, ada_b):
    n_rows, k = cond.shape
    depth, _, n = ada_w.shape
    tn = 512
    cond_b = jnp.broadcast_to(cond[:, :, None], (n_rows, k, LANE))
    return pl.pallas_call(
        functools.partial(_mod_body, n_rows=n_rows),
        grid=(depth, n // tn),
        in_specs=[pl.BlockSpec((n_rows, k, LANE), lambda l, j: (0, 0, 0)),
                  pl.BlockSpec((None, k, tn), lambda l, j: (l, 0, j)),
                  pl.BlockSpec((None, 1, tn), lambda l, j: (l, 0, j))],
        out_specs=pl.BlockSpec((None, 8, tn), lambda l, j: (l, 0, j)),
        out_shape=jax.ShapeDtypeStruct((depth, 8, n), F32),
        compiler_params=_params("parallel", "parallel"),
        name="modulation",
    )(cond_b, ada_w, ada_b.reshape(depth, 1, n))


def _bmm(a, b, spec, mode):
    if mode == 'f32':
        return jnp.einsum(spec, a, b, preferred_element_type=F32, precision=HIGHEST)
    a_hi, b_hi = a.astype(BF16), b.astype(BF16)
    out = jnp.einsum(spec, a_hi, b_hi, preferred_element_type=F32)
    if mode in ('x3', 'split_b'):
        b_lo = (b - b_hi.astype(F32)).astype(BF16)
        out = out + jnp.einsum(spec, a_hi, b_lo, preferred_element_type=F32)
    if mode in ('x3', 'split_a'):
        a_lo = (a - a_hi.astype(F32)).astype(BF16)
        out = out + jnp.einsum(spec, a_lo, b_hi, preferred_element_type=F32)
    return out


def _unit_triangular_inverse(l_mat, ti, si):
    c = l_mat.shape[1]
    nn = 'hts,hsn->htn'
    blk = 8
    eye = (ti == si).astype(F32)
    l_d = jnp.where((ti // blk) == (si // blk), l_mat, 0.0)
    p2 = _bmm(l_d, l_d, nn, 'bf16')
    p4 = _bmm(p2, p2, nn, 'bf16')
    inv = _bmm(eye - l_d, eye + p2, nn, 'bf16')
    inv = _bmm(inv, eye + p4, nn, 'bf16')
    while blk < c:
        pair = ((ti // (2 * blk)) == (si // (2 * blk))) & ((ti // blk) != (si // blk))
        off = jnp.where(pair, l_mat, 0.0)
        inv = inv - _bmm(_bmm(inv, off, nn, 'bf16'), inv, nn, 'bf16')
        blk *= 2
    return inv


def _first_head_lanes(shape):
    return lax.broadcasted_iota(jnp.int32, shape, len(shape) - 1) % LANE < HEAD_DIM


def _rwkv_body(*refs):
    fwd_in, bwd_in, ka_ref, (yf_ref, yb_ref, st_ref) = refs[:6], refs[6:12], refs[12], refs[13:]

    @pl.when(pl.program_id(0) == 0)
    def _():
        st_ref[...] = jnp.zeros_like(st_ref)

    bsz, c, width = fwd_in[0].shape
    n_pairs = width // LANE
    ti = lax.broadcasted_iota(jnp.int32, (c, c), 0)
    si = lax.broadcasted_iota(jnp.int32, (c, c), 1)
    units = []
    for in_refs, reverse in ((fwd_in, False), (bwd_in, True)):
        tri = ((si >= ti) if reverse else (si <= ti)).astype(F32)
        for row in range(bsz):
            r, k, v, kk, a, lw = (ref[row] for ref in in_refs)
            k = k * (1.0 + (a - 1.0) * ka_ref[...])
            cum = _bmm(tri, lw, 'ts,sn->tn', 'split_b')
            tot = cum[:1] if reverse else cum[c - 1:]
            b = kk * a
            e_neg = jnp.exp(-cum)
            e_rem = jnp.exp(tot - cum)
            pieces = (r * jnp.exp(cum), kk * jnp.exp(cum - lw), k * e_neg, b * e_neg, k * e_rem, b * e_rem, v,
                      jnp.exp(tot))
            for j in range(n_pairs):
                units.append([m[:, j * LANE:(j + 1) * LANE] for m in pieces])
    n_units = len(units)
    rq, kq, kd, bd, kdc, bdc, v, e_tot = (jnp.stack([u[i] for u in units], axis=0) for i in range(8))

    backward = lax.broadcasted_iota(jnp.int32, (n_units, 1, 1), 0) >= n_units // 2
    order = (si - ti) * jnp.where(backward, -1, 1)
    incl = order <= 0
    strict = order < 0
    first = _first_head_lanes((c, LANE))
    first2 = _first_head_lanes((c, 2 * LANE))
    nt, nn, tn = 'utn,usn->uts', 'uts,usn->utn', 'usn,usm->unm'
    qq = jnp.concatenate([jnp.where(first, rq, 0.0), jnp.where(first, kq, 0.0),
                          jnp.where(first, 0.0, rq), jnp.where(first, 0.0, kq)], axis=1)
    ak = _bmm(qq, kd, nt, 'bf16')
    ab = _bmm(qq, bd, nt, 'bf16')
    a_rk = [jnp.where(incl, ak[:, 2 * x * c:(2 * x + 1) * c], 0.0) for x in range(2)]
    a_kk = [jnp.where(strict, ak[:, (2 * x + 1) * c:(2 * x + 2) * c], 0.0) for x in range(2)]
    a_rb = [jnp.where(incl, ab[:, 2 * x * c:(2 * x + 1) * c], 0.0) for x in range(2)]
    a_kb = [jnp.where(strict, ab[:, (2 * x + 1) * c:(2 * x + 2) * c], 0.0) for x in range(2)]
    inv = _unit_triangular_inverse(jnp.concatenate(a_kb, axis=0), ti, si)
    inv = [inv[:n_units], inv[n_units:]]

    xs = [_bmm(inv[x], jnp.concatenate([_bmm(a_kk[x], v, nn, 'bf16'), kq], axis=2), nn, 'bf16') for x in range(2)]
    x2 = jnp.where(first2, xs[0], xs[1])
    arx = jnp.where(first2, _bmm(a_rb[0], x2, nn, 'bf16'), _bmm(a_rb[1], x2, nn, 'bf16'))
    y0 = jnp.where(first, _bmm(a_rk[0], v, nn, 'bf16'), _bmm(a_rk[1], v, nn, 'bf16')) - arx[:, :, :LANE]
    rqp = rq - arx[:, :, LANE:]
    row_n = lax.broadcasted_iota(jnp.int32, (LANE, LANE), 0)
    col_n = lax.broadcasted_iota(jnp.int32, (LANE, LANE), 1)
    same_head = (row_n // HEAD_DIM) == (col_n // HEAD_DIM)
    bx = _bmm(bdc, x2, tn, 'bf16')
    tadd = jnp.where(same_head, _bmm(kdc, v, tn, 'bf16') - bx[:, :, :LANE], 0.0)
    decay = jnp.where(row_n == col_n, jnp.broadcast_to(e_tot, (n_units, LANE, LANE)), 0.0)
    p = jnp.where(same_head, decay - bx[:, :, LANE:], 0.0)
    t0 = st_ref[...]
    y = _bmm(rqp, t0, 'utn,unm->utm', 'bf16') + y0
    st_ref[...] = _bmm(p, t0, 'ujn,unm->ujm', 'bf16') + tadd
    for d, y_ref in enumerate((yf_ref, yb_ref)):
        for row in range(bsz):
            u0 = (d * bsz + row) * n_pairs
            y_ref[row] = jnp.concatenate([y[u0 + j] for j in range(n_pairs)], axis=1)


def _scan_chunk_index(c, n_ctx_chunks, n_chunks, reverse):
    if not reverse:
        return c
    return jnp.where(c < n_ctx_chunks, n_ctx_chunks - 1 - c, n_chunks - 1 - (c - n_ctx_chunks))


def rwkv_scan(r, k, v, kk, k_a, a_dirs, lw_dirs, *, n_ctx):
    bsz, t, width = r.shape
    c = RW_CHUNK
    n_chunks, n_ctx_chunks = t // c, n_ctx // c
    specs = [pl.BlockSpec((bsz, c, width),
                          lambda i, rev=rev: (0, _scan_chunk_index(i, n_ctx_chunks, n_chunks, rev), 0))
             for rev in (False, True)]
    n_units = 2 * bsz * (width // LANE)
    args = [(r, k, v, kk, a_dirs[d], lw_dirs[d]) for d in range(2)]
    return pl.pallas_call(
        _rwkv_body,
        grid=(n_chunks,),
        in_specs=[specs[0]] * 6 + [specs[1]] * 6 + [pl.BlockSpec((1, width), lambda i: (0, 0))],
        out_specs=specs,
        out_shape=[jax.ShapeDtypeStruct((bsz, t, width), F32)] * 2,
        scratch_shapes=[pltpu.VMEM((n_units, LANE, LANE), F32)],
        compiler_params=_params("arbitrary"),
        name="rwkv_scan",
    )(*args[0], *args[1], k_a.reshape(1, width))


def _ssd_body(*refs):
    fwd_in, bwd_in, bias_ref, aneg_ref, (yf_ref, yb_ref, st_ref) = refs[:2], refs[2:4], refs[4], refs[5], refs[6:]

    @pl.when(pl.program_id(0) == 0)
    def _():
        st_ref[...] = jnp.zeros_like(st_ref)

    bsz = fwd_in[0].shape[0]
    n_pairs = st_ref.shape[0] // (2 * bsz)
    results = []
    for d, in_refs in enumerate((fwd_in, bwd_in)):
        for row in range(bsz):
            base = (d * bsz + row) * n_pairs
            results.append(_ssd_chunk(in_refs[0][row], in_refs[1][row], bias_ref[...], aneg_ref[d:d + 1],
                                      [st_ref[base + j] for j in range(n_pairs)], reverse=bool(d)))
    for u, (ys, new_states) in enumerate(results):
        d, row = divmod(u, bsz)
        (yf_ref, yb_ref)[d][row] = jnp.concatenate(ys, axis=1)
        for j in range(n_pairs):
            st_ref[u * n_pairs + j] = new_states[j]


def _ssd_chunk(xbc, dt_raw, dt_bias, a_neg, states, *, reverse):
    c = xbc.shape[0]
    n_pairs = len(states)
    n_h = 2 * n_pairs
    dt = jax.nn.softplus(dt_raw + dt_bias)
    if reverse:
        dt = pltpu.roll(dt, LANE - n_h, axis=1)
    da_col = dt * a_neg
    gd = n_h * HEAD_DIM
    n_g = (xbc.shape[1] - gd) // (2 * MB_STATE)
    pairs_per_group = n_pairs // n_g
    ti = lax.broadcasted_iota(jnp.int32, (c, c), 0)
    si = lax.broadcasted_iota(jnp.int32, (c, c), 1)
    incl = (si >= ti) if reverse else (si <= ti)
    cs_c = jnp.dot(incl.astype(F32), da_col, preferred_element_type=F32, precision=HIGHEST)
    cs_r = cs_c.T
    tot = cs_c[:1, :] if reverse else cs_c[c - 1:, :]
    e_cs = jnp.exp(cs_c)
    e_rem = jnp.exp(tot - cs_c)
    e_tot = jnp.exp(tot)
    bm = [xbc[:, gd + g * MB_STATE:gd + (g + 1) * MB_STATE].astype(BF16) for g in range(n_g)]
    cm = [xbc[:, gd + (n_g + g) * MB_STATE:gd + (n_g + g + 1) * MB_STATE].astype(BF16) for g in range(n_g)]
    gmat = [jnp.einsum('ln,sn->ls', cm[g], bm[g], preferred_element_type=F32) for g in range(n_g)]
    first = _first_head_lanes((c, LANE))
    first_rows = lax.broadcasted_iota(jnp.int32, (LANE, 1), 0) < HEAD_DIM
    pick = lambda m, ha: jnp.where(first, m[:, ha:ha + 1], m[:, ha + 1:ha + 2])
    ys, new_states = [], []
    for j in range(n_pairs):
        g = j // pairs_per_group
        ha = 2 * j
        sl = slice(j * LANE, (j + 1) * LANE)
        xdt = xbc[:, sl] * pick(dt, ha)
        xdt_b = xdt.astype(BF16)
        y_heads = []
        for h in (ha, ha + 1):
            seg = cs_c[:, h:h + 1] - cs_r[h:h + 1, :]
            lmat = jnp.exp(jnp.where(incl, seg, NA_MASK))
            y_heads.append(jnp.dot((gmat[g] * lmat).astype(BF16), xdt_b, preferred_element_type=F32))
        st = states[j]
        y_off = jnp.einsum('ln,pn->lp', cm[g], st.astype(BF16), preferred_element_type=F32) * pick(e_cs, ha)
        ys.append(jnp.where(first, y_heads[0], y_heads[1]) + y_off)
        xdec = (xdt * pick(e_rem, ha)).astype(BF16)
        keep = jnp.where(first_rows, e_tot[:, ha:ha + 1], e_tot[:, ha + 1:ha + 2])
        new_states.append(keep * st + jnp.einsum('lp,ln->pn', xdec, bm[g], preferred_element_type=F32))
    return ys, new_states


def ssd_scan(xbc, p, dt_bias, a_log, *, n_ctx):
    bsz, t, width = xbc.shape
    n_h = dt_bias.shape[1]
    gd = n_h * HEAD_DIM
    c = MB_CHUNK
    n_chunks, n_ctx_chunks = t // c, n_ctx // c
    in_specs, args, out_specs = [], [], []
    for reverse in (False, True):
        idx = lambda i, reverse=reverse: _scan_chunk_index(i, n_ctx_chunks, n_chunks, reverse)
        in_specs += [pl.BlockSpec((bsz, c, width), lambda i, idx=idx: (0, idx(i), 0)),
                     pl.BlockSpec((bsz, c, LANE), lambda i, idx=idx: (0, idx(i), OFF_MB_DT // LANE))]
        args += [xbc, p]
        out_specs.append(pl.BlockSpec((bsz, c, gd), lambda i, idx=idx: (0, idx(i), 0)))
    pad = ((0, 0), (0, LANE - n_h))
    in_specs += [pl.BlockSpec((1, LANE), lambda i: (0, 0)), pl.BlockSpec((2, LANE), lambda i: (0, 0))]
    args += [jnp.pad(dt_bias.reshape(1, 2 * n_h), ((0, 0), (0, LANE - 2 * n_h))), jnp.pad(-jnp.exp(a_log), pad)]
    return pl.pallas_call(
        _ssd_body,
        grid=(n_chunks,),
        in_specs=in_specs,
        out_specs=out_specs,
        out_shape=[jax.ShapeDtypeStruct((bsz, t, gd), F32)] * 2,
        scratch_shapes=[pltpu.VMEM((2 * bsz * (n_h // 2), 2 * HEAD_DIM, MB_STATE), F32)],
        compiler_params=_params("arbitrary"),
        name="ssd_scan",
    )(*args)


def _softmax_pv(s_parts, v_parts):
    m = functools.reduce(jnp.maximum, [jnp.max(s, axis=-1, keepdims=True) for s in s_parts])
    ps = [jnp.exp(s - m) for s in s_parts]
    l = functools.reduce(jnp.add, [jnp.sum(p, axis=-1, keepdims=True) for p in ps])
    o = functools.reduce(jnp.add, [jnp.dot(p.astype(BF16), v, preferred_element_type=F32)
                                   for p, v in zip(ps, v_parts)])
    return o / l


def _rope_lanes(x, cos_tab, sin_tab):
    width = x.shape[1]
    reps = width // MLA_QK_PAD
    cos_w = jnp.concatenate([cos_tab] * reps, axis=1) if reps > 1 else cos_tab
    sin_w = jnp.concatenate([sin_tab] * reps, axis=1) if reps > 1 else sin_tab
    lane = lax.broadcasted_iota(jnp.int32, x.shape, 1) % MLA_QK_PAD
    first_half = (lane - MLA_NOPE) % (MLA_ROPE // 2) < MLA_ROPE // 4
    shift = MLA_ROPE // 4
    partner = jnp.where(first_half, pltpu.roll(x, width - shift, axis=1), pltpu.roll(x, shift, axis=1))
    return x * cos_w + partner * sin_w


def _attn_body(q_ref, k_ref, v_ref, pe_ref, cq_ref, sq_ref, ck_ref, sk_ref, o_ref, kb_ref, vb_ref,
               *, n_ctx, n_ctx_tiles):
    dq = q_ref.shape[2] // 2

    @pl.when(pl.program_id(2) == 0)
    def _():
        pe_rot = _rope_lanes(pe_ref[0], ck_ref[...], sk_ref[...])
        kb_ref[...] = (k_ref[0] + jnp.concatenate([pe_rot, pe_rot], axis=1)).astype(BF16)
        vb_ref[...] = v_ref[0].astype(BF16)

    q2 = (_rope_lanes(q_ref[0], cq_ref[...], sq_ref[...]) * (MLA_QK ** -0.5)).astype(BF16)

    def attend(n_keys):
        v2 = vb_ref[0:n_keys]
        own = _first_head_lanes(v2.shape)
        outs = []
        for x in range(2):
            s = jnp.einsum('qd,kd->qk', q2[:, x * dq:(x + 1) * dq], kb_ref[0:n_keys, x * dq:(x + 1) * dq],
                           preferred_element_type=F32)
            p = jnp.exp((s - jnp.max(s, axis=-1, keepdims=True)).astype(BF16))
            vx = jnp.where(own if x == 0 else ~own, v2, jnp.ones_like(v2))
            o = jnp.dot(p, vx, preferred_element_type=F32)
            outs.append(o / pltpu.roll(o, HEAD_DIM, axis=1))
        o_ref[0] = jnp.where(_first_head_lanes(outs[0].shape), outs[0], outs[1])

    @pl.when(pl.program_id(2) < n_ctx_tiles)
    def _():
        attend(n_ctx)

    @pl.when(pl.program_id(2) >= n_ctx_tiles)
    def _():
        attend(k_ref.shape[1])


def attention(q, kv, p, tables, *, n_ctx, tile):
    bsz, t, _ = q.shape
    n_pairs = N_HEADS // 2
    dq2 = 2 * MLA_QK_PAD
    v_col0 = N_HEADS * MLA_QK_PAD // LANE
    cos_tab, sin_tab = tables
    q_tab = pl.BlockSpec((tile, MLA_QK_PAD), lambda b, j, i: (i, 0))
    k_tab = pl.BlockSpec((t, MLA_QK_PAD), lambda b, j, i: (0, 0))
    return pl.pallas_call(
        functools.partial(_attn_body, n_ctx=n_ctx, n_ctx_tiles=n_ctx // tile),
        grid=(bsz, n_pairs, t // tile),
        in_specs=[pl.BlockSpec((1, tile, dq2), lambda b, j, i: (b, i, j)),
                  pl.BlockSpec((1, t, dq2), lambda b, j, i: (b, 0, j)),
                  pl.BlockSpec((1, t, LANE), lambda b, j, i: (b, 0, v_col0 + j)),
                  pl.BlockSpec((1, t, LANE), lambda b, j, i: (b, 0, OFF_MLA_PE // LANE)),
                  q_tab, q_tab, k_tab, k_tab],
        out_specs=pl.BlockSpec((1, tile, LANE), lambda b, j, i: (b, i, j)),
        out_shape=jax.ShapeDtypeStruct((bsz, t, N_HEADS * HEAD_DIM), F32),
        scratch_shapes=[pltpu.VMEM((t, dq2), BF16), pltpu.VMEM((t, LANE), BF16)],
        compiler_params=_params("parallel", "parallel", "arbitrary"),
        name="attention",
    )(q, kv, kv, p, cos_tab, sin_tab, cos_tab, sin_tab)


def _norm_mm_body(x_ref, g_ref, w_ref, o_ref):
    x = x_ref[...]
    xn = x * lax.rsqrt(jnp.mean(x * x, axis=-1, keepdims=True) + RMS_EPS) * g_ref[...]
    o_ref[...] = jnp.dot(xn.astype(BF16), w_ref[...], preferred_element_type=F32)


def norm_matmul(p, col, g, w):
    m = p.shape[0]
    k, n = w.shape
    tm = _pick_tile(m, 640, 16)
    return pl.pallas_call(
        _norm_mm_body,
        grid=(m // tm,),
        in_specs=[pl.BlockSpec((tm, k), lambda i: (i, col // k)),
                  pl.BlockSpec((1, k), lambda i: (0, 0)),
                  pl.BlockSpec((k, n), lambda i: (0, 0))],
        out_specs=pl.BlockSpec((tm, n), lambda i: (i, 0)),
        out_shape=jax.ShapeDtypeStruct((m, n), F32),
        compiler_params=_params("parallel"),
        name="norm_matmul",
    )(p, g.reshape(1, k), w)


def _na_body(q_ref, k_ref, v_ref, bias_ref, o_ref, kb_ref, vb_ref, *, n_ctx, n_rows):
    i = pl.program_id(2)
    rows_per_step = q_ref.shape[1] // GRID_W
    win = NA_WIN_ROWS * GRID_W
    q2 = q_ref[0] * (HEAD_DIM ** -0.5)

    def head_queries(q_rows, x):
        first = _first_head_lanes(q_rows.shape)
        return jnp.where(first if x == 0 else ~first, q_rows, 0.0).astype(BF16)

    def scores(q, k):
        return jnp.einsum('qd,kd->qk', q, k, preferred_element_type=F32)

    @pl.when(i == 0)
    def _():
        kb_ref[...] = k_ref[0].astype(BF16)
        vb_ref[...] = v_ref[0].astype(BF16)
        kc, vc = kb_ref[0:n_ctx], vb_ref[0:n_ctx]
        outs = [_softmax_pv([scores(head_queries(q2, x), kc)], [vc]) for x in range(2)]
        o_ref[0] = jnp.where(_first_head_lanes(outs[0].shape), outs[0], outs[1])

    @pl.when(i > 0)
    def _():
        qs, kws, vws, biases = [], [], [], []
        for rr in range(rows_per_step):
            r = (i - 1) * rows_per_step + rr
            rs = jnp.clip(r - NA_WIN_ROWS // 2, 0, n_rows - NA_WIN_ROWS)
            start = pl.multiple_of(n_ctx + rs * GRID_W, GRID_W)
            kws.append(kb_ref[pl.ds(start, win)])
            vws.append(vb_ref[pl.ds(start, win)])
            q_row = q2[rr * GRID_W:(rr + 1) * GRID_W]
            qs.append(jnp.concatenate([head_queries(q_row, 0), head_queries(q_row, 1)], axis=0))
            biases.append(jnp.concatenate([bias_ref[0, r - rs], bias_ref[1, r - rs]], axis=0))
        qb = jnp.stack(qs, axis=0)
        kc, vc = kb_ref[0:n_ctx], vb_ref[0:n_ctx]
        s_loc = jnp.einsum('uqd,ukd->uqk', qb, jnp.stack(kws, axis=0), preferred_element_type=F32)
        s_loc = s_loc + jnp.stack(biases, axis=0)
        s_ctx = scores(qb.reshape(rows_per_step * 2 * GRID_W, LANE), kc).reshape(rows_per_step, 2 * GRID_W, n_ctx)
        m = jnp.maximum(jnp.max(s_loc, axis=-1, keepdims=True), jnp.max(s_ctx, axis=-1, keepdims=True))
        p_loc = jnp.exp(s_loc - m)
        p_ctx = jnp.exp(s_ctx - m)
        l = jnp.sum(p_loc, axis=-1, keepdims=True) + jnp.sum(p_ctx, axis=-1, keepdims=True)
        o = jnp.einsum('uqk,ukd->uqd', p_loc.astype(BF16), jnp.stack(vws, axis=0), preferred_element_type=F32)
        o = o + jnp.dot(p_ctx.reshape(rows_per_step * 2 * GRID_W, n_ctx).astype(BF16), vc,
                        preferred_element_type=F32).reshape(o.shape)
        o = o / l
        first = _first_head_lanes((GRID_W, LANE))
        for rr in range(rows_per_step):
            o_ref[0, rr * GRID_W:(rr + 1) * GRID_W] = jnp.where(first, o[rr, :GRID_W], o[rr, GRID_W:])


def na_bias_table(rpb):
    wr = NA_WIN_ROWS
    qv = np.arange(GRID_W)[:, None]
    cv = np.arange(GRID_W)[None, :]
    col_start = np.clip(qv - NA_WIN_COLS // 2, 0, GRID_W - NA_WIN_COLS)
    valid = (cv >= col_start) & (cv < col_start + NA_WIN_COLS)
    offs = np.arange(2 * NA_WIN_COLS - 1)[:, None, None]
    pick = ((cv - qv + (NA_WIN_COLS - 1))[None] == offs) & valid[None]
    cols = jnp.einsum('hro,oqc->hrqc', rpb, jnp.asarray(pick, F32), precision=HIGHEST)
    cols = jnp.where(valid, cols, NA_MASK)
    tab = jnp.stack([cols[:, wr - 1 - var:2 * wr - 1 - var] for var in range(wr)], axis=1)
    tab = jnp.transpose(tab, (0, 1, 3, 2, 4))
    return tab.reshape(rpb.shape[0], wr, GRID_W, wr * GRID_W)


def neighborhood_attention(p, bias, *, n_ctx, col0):
    bsz, t, _ = p.shape
    n_rows = (t - n_ctx) // GRID_W
    n_pairs = GROUP_DIM // LANE
    c0 = col0 // LANE
    return pl.pallas_call(
        functools.partial(_na_body, n_ctx=n_ctx, n_rows=n_rows),
        grid=(bsz, n_pairs, t // n_ctx),
        in_specs=[pl.BlockSpec((1, n_ctx, LANE), lambda b, j, i: (b, i, c0 + j)),
                  pl.BlockSpec((1, t, LANE), lambda b, j, i: (b, 0, c0 + n_pairs + j)),
                  pl.BlockSpec((1, t, LANE), lambda b, j, i: (b, 0, c0 + 2 * n_pairs + j)),
                  pl.BlockSpec((2, NA_WIN_ROWS, GRID_W, NA_WIN_ROWS * GRID_W), lambda b, j, i: (j, 0, 0, 0))],
        out_specs=pl.BlockSpec((1, n_ctx, LANE), lambda b, j, i: (b, i, j)),
        out_shape=jax.ShapeDtypeStruct((bsz, t, GROUP_DIM), F32),
        scratch_shapes=[pltpu.VMEM((t, LANE), BF16), pltpu.VMEM((t, LANE), BF16)],
        compiler_params=_params("parallel", "parallel", "arbitrary"),
        name="neighborhood_attention",
    )(p, p, p, bias)


def _moe_body(h_ref, gate_ref, wg_ref, wu_ref, wd_ref, x_ref, gc_ref, gb_ref, lg_ref, lb_ref, o_ref, acc_ref,
              *, tokens_per_batch, n_ctx):
    e = pl.program_id(1)
    is_ctx = _is_ctx_row(x_ref.shape[0], tokens_per_batch, n_ctx)

    @pl.when(e == 0)
    def _():
        acc_ref[...] = jnp.zeros_like(acc_ref)

    h = h_ref[...]
    hid = jax.nn.silu(jnp.dot(h, wg_ref[0], preferred_element_type=F32)) * jnp.dot(h, wu_ref[0], preferred_element_type=F32)
    lane = lax.broadcasted_iota(jnp.int32, gate_ref.shape, 1)
    gate = jnp.sum(jnp.where(lane == e, gate_ref[...], 0.0), axis=1, keepdims=True)
    acc_ref[...] += gate * jnp.dot(hid.astype(BF16), wd_ref[0], preferred_element_type=F32)

    @pl.when(e == pl.num_programs(1) - 1)
    def _():
        z = DEEPNORM_ALPHA * x_ref[...] + jnp.where(is_ctx, gc_ref[0], gb_ref[0]) * acc_ref[...]
        mu = jnp.mean(z, axis=-1, keepdims=True)
        zc = z - mu
        var = jnp.mean(zc * zc, axis=-1, keepdims=True)
        o_ref[...] = zc * lax.rsqrt(var + LN_EPS) * lg_ref[...] + lb_ref[...]


def moe_experts(h, gate, w_gate, w_up, w_down, layer, xs, mod_l, ln_g, ln_b, *, tokens_per_batch, n_ctx):
    m, d = h.shape
    n_e, d_e = w_gate.shape[1], w_gate.shape[3]
    bsz = m // tokens_per_batch
    tm = _pick_tile(tokens_per_batch, 640, 16)
    tiles_per_batch = tokens_per_batch // tm
    row = lambda i, e: (i, 0)
    fixed = lambda i, e: (0, 0)
    return pl.pallas_call(
        functools.partial(_moe_body, tokens_per_batch=tokens_per_batch, n_ctx=n_ctx),
        grid=(m // tm, n_e),
        in_specs=[pl.BlockSpec((tm, d), row),
                  pl.BlockSpec((tm, LANE), row),
                  pl.BlockSpec((None, 1, d, d_e), lambda i, e: (layer, e, 0, 0)),
                  pl.BlockSpec((None, 1, d, d_e), lambda i, e: (layer, e, 0, 0)),
                  pl.BlockSpec((None, 1, d_e, d), lambda i, e: (layer, e, 0, 0)),
                  pl.BlockSpec((tm, d), row), *_mod_specs(bsz, tiles_per_batch, 5, d),
                  pl.BlockSpec((1, d), fixed), pl.BlockSpec((1, d), fixed)],
        out_specs=pl.BlockSpec((tm, d), row),
        out_shape=jax.ShapeDtypeStruct((m, d), F32),
        scratch_shapes=[pltpu.VMEM((tm, d), F32)],
        compiler_params=_params("parallel", "arbitrary"),
        name="moe_experts",
    )(h, gate, w_gate, w_up, w_down, xs, mod_l, mod_l, ln_g.reshape(1, d), ln_b.reshape(1, d))


def _rope_tables(t, n_ctx):
    n_freq = MLA_ROPE // 4
    pos_t = np.arange(t - n_ctx)
    pos = np.stack([pos_t // GRID_W, pos_t % GRID_W], axis=-1).astype(np.float32)
    inv_freq = jnp.asarray(ROPE_BASE, F32) ** (-jnp.arange(n_freq, dtype=F32) / n_freq)
    ang = jnp.asarray(pos)[:, :, None] * inv_freq
    ang = jnp.concatenate([jnp.zeros((n_ctx, 2, n_freq), F32), ang], axis=0)
    cos = jnp.broadcast_to(jnp.cos(ang)[:, :, None, :], (t, 2, 2, n_freq)).reshape(t, MLA_ROPE)
    sin = jnp.sin(ang)[:, :, None, :] * jnp.asarray([-1.0, 1.0], F32)[None, None, :, None]
    sin = sin.reshape(t, MLA_ROPE)
    pad = MLA_QK_PAD - MLA_QK
    cos_tab = jnp.concatenate([jnp.ones((t, MLA_NOPE), F32), cos, jnp.zeros((t, pad), F32)], axis=1)
    sin_tab = jnp.concatenate([jnp.zeros((t, MLA_NOPE), F32), sin, jnp.zeros((t, pad), F32)], axis=1)
    return cos_tab, sin_tab


HALO = 8


def _shifted_rows(x, halo_prev, halo_next, offset, pos, n_ctx, t):
    tm = x.shape[0]
    out = pltpu.roll(x, (-offset) % tm, axis=0)
    row = lax.broadcasted_iota(jnp.int32, (tm, 1), 0)
    for i in range(abs(offset)):
        if offset < 0:
            out = jnp.where(row == i, halo_prev[HALO + offset + i:HALO + offset + i + 1], out)
        else:
            out = jnp.where(row == tm - offset + i, halo_next[i:i + 1], out)
    src = pos + offset
    same = (src >= 0) & (src < t) & ((pos < n_ctx) == (src < n_ctx))
    return jnp.where(same, out, 0.0)


def _halo_specs(tm, t, width, col):
    per_tile, last = tm // HALO, t // HALO - 1
    return [pl.BlockSpec((1, HALO, width), lambda b, i: (b, jnp.maximum(i * per_tile - 1, 0), col)),
            pl.BlockSpec((1, HALO, width), lambda b, i: (b, jnp.minimum((i + 1) * per_tile, last), col))]


def _rwkv_prep_body(p_ref, pp_ref, pn_ref, mu_ref, w0_ref, a0_ref, kkw_ref, rk_ref, lora_ref,
                    r_ref, ko_ref, v_ref, kk_ref, a0o_ref, a1o_ref, lw0_ref, lw1_ref, g_ref, bonus_ref,
                    *, n_ctx, t):
    x = p_ref[0]
    tm = x.shape[0]
    gd = GROUP_DIM
    pos = pl.program_id(1) * tm + lax.broadcasted_iota(jnp.int32, (tm, 1), 0)
    prev = _shifted_rows(x, pp_ref[0], pn_ref[0], -1, pos, n_ctx, t)
    nxt = _shifted_rows(x, pp_ref[0], pn_ref[0], 1, pos, n_ctx, t)
    x = x + mu_ref[0:1] * (prev - x) + mu_ref[1:2] * (nxt - x)
    r, k, v = x[:, :gd], x[:, gd:2 * gd], x[:, 2 * gd:3 * gd]
    lora_in = jnp.concatenate([jnp.tanh(x[:, 3 * gd:3 * gd + 2 * RW_LORA]),
                               x[:, 3 * gd + 2 * RW_LORA:3 * gd + 4 * RW_LORA],
                               jax.nn.sigmoid(x[:, 3 * gd + 4 * RW_LORA:])], axis=1)
    lo = jnp.dot(lora_in.astype(BF16), lora_ref[...], preferred_element_type=F32)
    lane_r = lax.broadcasted_iota(jnp.int32, (gd, gd), 0) // HEAD_DIM
    lane_c = lax.broadcasted_iota(jnp.int32, (gd, gd), 1) // HEAD_DIM
    same_head = (lane_r == lane_c).astype(F32)
    head_sum = lambda m: _bmm(m, same_head, 'tn,nm->tm', 'split_a')
    kk = k * kkw_ref[...]
    kk_ref[0] = kk * lax.rsqrt(jnp.maximum(head_sum(kk * kk), 1e-12))
    for d, (a_ref, lw_ref) in enumerate(((a0o_ref, lw0_ref), (a1o_ref, lw1_ref))):
        a = jax.nn.sigmoid(a0_ref[d:d + 1] + lo[:, (2 + d) * gd:(3 + d) * gd])
        a_ref[0] = a
        lw_ref[0] = -RW_DECAY_SCALE * jax.nn.sigmoid(w0_ref[d:d + 1] + lo[:, d * gd:(d + 1) * gd])
    r_ref[0] = r
    ko_ref[0] = k
    v_ref[0] = v
    g_ref[0] = lo[:, 4 * gd:]
    bonus_ref[0] = head_sum(r * k * rk_ref[...]) * v


def rwkv_prep(p, mu, w0, a0, k_k, r_k, lora_w, *, n_ctx):
    bsz, t, _ = p.shape
    gd = GROUP_DIM
    tm = _pick_tile(t, 320, 16)
    row = lambda b, i: (b, i, 0)
    fixed = lambda b, i: (0, 0)
    vec = lambda w: w.reshape(1, gd)
    out = jax.ShapeDtypeStruct((bsz, t, gd), F32)
    return pl.pallas_call(
        functools.partial(_rwkv_prep_body, n_ctx=n_ctx, t=t),
        grid=(bsz, t // tm),
        in_specs=[pl.BlockSpec((1, tm, RW_IN), row), *_halo_specs(tm, t, RW_IN, 0),
                  pl.BlockSpec((2, RW_IN), fixed), pl.BlockSpec((2, gd), fixed), pl.BlockSpec((2, gd), fixed),
                  pl.BlockSpec((1, gd), fixed), pl.BlockSpec((1, gd), fixed), pl.BlockSpec(lora_w.shape, fixed)],
        out_specs=[pl.BlockSpec((1, tm, gd), row)] * 10,
        out_shape=[out] * 10,
        compiler_params=_params("parallel", "parallel"),
        name="rwkv_prep",
    )(p, p, p, mu, w0, a0, vec(k_k), vec(r_k), lora_w)


def _rwkv_mixer(p, n_ctx, mu, w0, w_up, a0, a_up, g_up, k_k, k_a, r_k, gn_g, gn_b):
    bsz, t, _ = p.shape
    gd = GROUP_DIM
    n_lora = 4 * RW_LORA + RW_G_LORA
    lora_w = jnp.zeros((n_lora, 5 * gd), F32)
    for j, blk in enumerate((w_up[0], w_up[1], a_up[0], a_up[1])):
        lora_w = lora_w.at[j * RW_LORA:(j + 1) * RW_LORA, j * gd:(j + 1) * gd].set(blk)
    lora_w = lora_w.at[4 * RW_LORA:, 4 * gd:].set(g_up)
    r, k, v, kk, a_f, a_b, lw_f, lw_b, g, bonus = rwkv_prep(p, mu, w0, a0, k_k, r_k, lora_w.astype(BF16), n_ctx=n_ctx)
    y_fwd, y_bwd = rwkv_scan(r, k, v, kk, k_a, (a_f, a_b), (lw_f, lw_b), n_ctx=n_ctx)
    flat = lambda m: m.reshape(bsz * t, gd)
    return flat(y_fwd), flat(y_bwd), flat(bonus), flat(g), gn_g, gn_b


def _conv_body(x_ref, xp_ref, xn_ref, w_ref, b_ref, o_ref, *, n_ctx, t):
    x = x_ref[0]
    tm = x.shape[0]
    pos = pl.program_id(1) * tm + lax.broadcasted_iota(jnp.int32, (tm, 1), 0)
    acc = b_ref[...] + w_ref[MB_CONV // 2:MB_CONV // 2 + 1] * x
    for j in range(MB_CONV):
        offset = j - MB_CONV // 2
        if offset != 0:
            acc = acc + w_ref[j:j + 1] * _shifted_rows(x, xp_ref[0], xn_ref[0], offset, pos, n_ctx, t)
    o_ref[0] = jax.nn.silu(acc)


def conv_silu(p, conv_w, conv_b, *, n_ctx):
    bsz, t, _ = p.shape
    width = MB_CONV_DIM
    tm = _pick_tile(t, 640, 16)
    col = OFF_MB_XBC // width
    return pl.pallas_call(
        functools.partial(_conv_body, n_ctx=n_ctx, t=t),
        grid=(bsz, t // tm),
        in_specs=[pl.BlockSpec((1, tm, width), lambda b, i: (b, i, col)), *_halo_specs(tm, t, width, col),
                  pl.BlockSpec((MB_CONV, width), lambda b, i: (0, 0)), pl.BlockSpec((1, width), lambda b, i: (0, 0))],
        out_specs=pl.BlockSpec((1, tm, width), lambda b, i: (b, i, 0)),
        out_shape=jax.ShapeDtypeStruct((bsz, t, width), F32),
        compiler_params=_params("parallel", "parallel"),
        name="conv_silu",
    )(p, p, p, conv_w, conv_b.reshape(1, width))


def _mamba_mixer(p, n_ctx, conv_w, conv_b, a_log, dt_bias, d_skip, norm_g):
    bsz, t, _ = p.shape
    xbc = conv_silu(p, conv_w, conv_b, n_ctx=n_ctx)
    y_fwd, y_bwd = ssd_scan(xbc, p, dt_bias, a_log, n_ctx=n_ctx)
    flat = lambda m: m.reshape(bsz * t, -1)
    return flat(y_fwd), flat(y_bwd), flat(xbc), d_skip, norm_g


def _mla_mixer(p, n_ctx, q_norm, w_uq, kv_norm, w_ukv, tile, tables):
    bsz, t, n_cols = p.shape
    n_h, pad = N_HEADS, MLA_QK_PAD - MLA_QK
    w_q = jnp.pad(w_uq.reshape(-1, n_h, MLA_QK), ((0, 0), (0, 0), (0, pad))).reshape(-1, n_h * MLA_QK_PAD)
    w_kv = w_ukv.reshape(-1, n_h, MLA_NOPE + HEAD_DIM)
    w_k = jnp.pad(w_kv[..., :MLA_NOPE], ((0, 0), (0, 0), (0, MLA_QK_PAD - MLA_NOPE))).reshape(-1, n_h * MLA_QK_PAD)
    w_v = w_kv[..., MLA_NOPE:].reshape(-1, n_h * HEAD_DIM)
    p2 = p.reshape(bsz * t, n_cols)
    q = norm_matmul(p2, OFF_MLA_Q, q_norm, w_q.astype(BF16)).reshape(bsz, t, -1)
    kv = norm_matmul(p2, OFF_MLA_KV, kv_norm, jnp.concatenate([w_k, w_v], axis=1).astype(BF16)).reshape(bsz, t, -1)
    return attention(q, kv, p, tables, n_ctx=n_ctx, tile=tile)


def _first_max(vals, excluded):
    live = [jnp.where(x, -jnp.inf, v) for v, x in zip(vals, excluded)]
    top = functools.reduce(jnp.maximum, live)
    found = jnp.zeros_like(top, dtype=jnp.bool_)
    first = []
    for v, x in zip(live, excluded):
        hit = (v == top) & ~found & ~x
        first.append(hit)
        found = found | hit
    return top, first


def _router_body(x_ref, sc_ref, sh_ref, rw_ref, rb_ref, h_ref, gate_ref):
    h = x_ref[...] * (1.0 + sc_ref[0]) + sh_ref[0]
    h_ref[...] = h.astype(BF16)
    logits = jnp.dot(h, rw_ref[...], preferred_element_type=F32, precision=HIGHEST)
    scores = jax.nn.sigmoid(logits.T[:N_EXPERTS])
    biased = scores + rb_ref[...]
    rows = [biased[e:e + 1] for e in range(N_EXPERTS)]
    never = jnp.zeros_like(rows[0], dtype=jnp.bool_)
    group_scores, picked = [], []
    for g in range(N_EXPERT_GROUPS):
        vals = rows[g * EXPERTS_PER_GROUP:(g + 1) * EXPERTS_PER_GROUP]
        top1, first = _first_max(vals, [never] * EXPERTS_PER_GROUP)
        top2, second = _first_max(vals, first)
        group_scores.append(top1 + top2)
        picked.append([a | b for a, b in zip(first, second)])
    _, group_sel = _first_max(group_scores, [never] * N_EXPERT_GROUPS)
    chosen = [jnp.where(group_sel[e // EXPERTS_PER_GROUP] & picked[e // EXPERTS_PER_GROUP][e % EXPERTS_PER_GROUP],
                        scores[e:e + 1], 0.0) for e in range(N_EXPERTS)]
    denom = functools.reduce(jnp.add, chosen)
    gates = [w / denom * ROUTED_SCALE for w in chosen]
    gates.append(jnp.zeros((LANE - N_EXPERTS, denom.shape[1]), F32))
    gate_ref[...] = jnp.concatenate(gates, axis=0).T


def router(xs, mod_l, router_w, router_b, *, tokens_per_batch, n_ctx, tile):
    m, d = xs.shape
    bsz = m // tokens_per_batch
    tiles_per_batch, n_ctx_tiles = tokens_per_batch // tile, n_ctx // tile

    def mod_row(i):
        return jnp.where(i % tiles_per_batch < n_ctx_tiles, bsz, i // tiles_per_batch)

    rw_pad = jnp.pad(router_w, ((0, 0), (0, LANE - N_EXPERTS)))
    return pl.pallas_call(
        _router_body,
        grid=(m // tile,),
        in_specs=[pl.BlockSpec((tile, d), lambda i: (i, 0)),
                  pl.BlockSpec((None, 1, d), lambda i: (mod_row(i), 0, 4)),
                  pl.BlockSpec((None, 1, d), lambda i: (mod_row(i), 0, 3)),
                  pl.BlockSpec((d, LANE), lambda i: (0, 0)),
                  pl.BlockSpec((N_EXPERTS, 1), lambda i: (0, 0))],
        out_specs=[pl.BlockSpec((tile, d), lambda i: (i, 0)),
                   pl.BlockSpec((tile, LANE), lambda i: (i, 0))],
        out_shape=[jax.ShapeDtypeStruct((m, d), BF16), jax.ShapeDtypeStruct((m, LANE), F32)],
        compiler_params=_params("parallel"),
        name="router",
    )(xs, mod_l, mod_l, rw_pad, router_b.reshape(N_EXPERTS, 1))


def _pad_w_in(w):
    zeros = lambda n: jnp.zeros(w.shape[:-1] + (n,), w.dtype)
    mb0 = RW_IN
    mla0 = mb0 + MB_IN
    na0 = mla0 + MLA_IN
    rw, mb, mla, na = w[..., :mb0], w[..., mb0:mla0], w[..., mla0:na0], w[..., na0:]
    q_kv, pe = mla[..., :MLA_Q_LORA + MLA_KV_LORA], mla[..., MLA_Q_LORA + MLA_KV_LORA:]
    return jnp.concatenate([rw, q_kv, zeros(PE_LANE), pe, zeros(LANE - PE_LANE - MLA_ROPE),
                            mb, zeros(LANE - 2 * N_HEADS), na], axis=-1)


def kernel(x, c, ctx, c_ctx, ada_w, ada_b, w_in, w_out, ln1_g, ln1_b, ln2_g, ln2_b, rw_mu, rw_w0, rw_w_up, rw_a0, rw_a_up, rw_g_up, rw_k_k, rw_k_a, rw_r_k, rw_gn_g, rw_gn_b, mb_conv_w, mb_conv_b, mb_a_log, mb_dt_bias, mb_d, mb_norm_g, mla_q_norm, mla_w_uq, mla_kv_norm, mla_w_ukv, na_rpb, router_w, router_b, exp_w_gate, exp_w_up, exp_w_down):
    bsz, seq, d = x.shape
    n_ctx = ctx.shape[1]
    t = n_ctx + seq
    depth = ada_w.shape[0]
    tile = min(256, n_ctx)
    cond = jnp.concatenate([jax.nn.silu(c), jax.nn.silu(c_ctx)[None]], axis=0)
    mod = modulation(cond, ada_w, ada_b)
    xs = jnp.concatenate([ctx, x], axis=1)
    expert_w = [w.astype(BF16) for w in (exp_w_gate, exp_w_up, exp_w_down)]
    rope_tables = _rope_tables(t, n_ctx)
    w_in_all = _pad_w_in(w_in.astype(BF16))
    w_out_all = w_out.astype(BF16)
    m = bsz * t
    for l in range(depth):
        mod_l = mod[l][:, None, :]
        p = proj_in(xs.reshape(m, d), mod_l, w_in_all, l, tokens_per_batch=t, n_ctx=n_ctx).reshape(bsz, t, N_IN_PAD)
        o_rw = _rwkv_mixer(p, n_ctx, rw_mu[l], rw_w0[l], rw_w_up[l], rw_a0[l], rw_a_up[l],
                           rw_g_up[l], rw_k_k[l], rw_k_a[l], rw_r_k[l], rw_gn_g[l], rw_gn_b[l])
        o_mb = _mamba_mixer(p, n_ctx, mb_conv_w[l], mb_conv_b[l], mb_a_log[l], mb_dt_bias[l], mb_d[l], mb_norm_g[l])
        o_mla = _mla_mixer(p, n_ctx, mla_q_norm[l], mla_w_uq[l], mla_kv_norm[l], mla_w_ukv[l], tile, rope_tables)
        o_na = neighborhood_attention(p, na_bias_table(na_rpb[l]), n_ctx=n_ctx, col0=OFF_NA)
        xs2 = proj_out(o_rw, o_mb, o_mla.reshape(m, GROUP_DIM), o_na.reshape(m, GROUP_DIM), p.reshape(m, N_IN_PAD),
                       w_out_all, l, xs.reshape(m, d), mod_l, ln1_g[l], ln1_b[l],
                       tokens_per_batch=t, n_ctx=n_ctx)
        h, gate = router(xs2, mod_l, router_w, router_b, tokens_per_batch=t, n_ctx=n_ctx, tile=tile)
        xs = moe_experts(h, gate, *expert_w, l, xs2, mod_l, ln2_g[l], ln2_b[l],
                         tokens_per_batch=t, n_ctx=n_ctx).reshape(bsz, t, d)
    return xs[:, n_ctx:]
```

```python
import functools

import numpy as np
import jax
import jax.numpy as jnp
from jax import lax
from jax.experimental import pallas as pl
from jax.experimental.pallas import tpu as pltpu

F32 = jnp.float32
BF16 = jnp.bfloat16
HIGHEST = lax.Precision.HIGHEST

D_MODEL = 2048
DEPTH = 4
GRID_W = 64
GROUP_DIM = D_MODEL // 4
HEAD_DIM = 64
N_HEADS = GROUP_DIM // HEAD_DIM

RW_LORA = 64
RW_G_LORA = 128
RW_DECAY_SCALE = 0.606531
RW_GN_EPS = 64e-5
RW_IN = 3 * GROUP_DIM + 4 * RW_LORA + RW_G_LORA
RW_CHUNK = 64

MB_GROUPS = 2
MB_STATE = 128
MB_CONV = 5
MB_CHUNK = 128
MB_CONV_DIM = GROUP_DIM + 2 * MB_GROUPS * MB_STATE
MB_IN = GROUP_DIM + MB_CONV_DIM + 2 * N_HEADS

MLA_Q_LORA = 3 * D_MODEL // 16
MLA_KV_LORA = D_MODEL // 16
MLA_NOPE = 64
MLA_ROPE = 32
MLA_QK = MLA_NOPE + MLA_ROPE
MLA_QK_PAD = 128
MLA_IN = MLA_Q_LORA + MLA_KV_LORA + MLA_ROPE
ROPE_BASE = 10000.0

NA_WIN_ROWS = 8
NA_WIN_COLS = 16
NA_IN = 3 * GROUP_DIM
MASKED_SCORE = -1e30

N_EXPERTS = 16
N_EXPERT_GROUPS = 4
EXPERTS_PER_GROUP = N_EXPERTS // N_EXPERT_GROUPS
ROUTED_SCALE = 2.5

DEEPNORM_ALPHA = (2 * DEPTH) ** 0.25
LN_EPS = 1e-6
RMS_EPS = 1e-6

LANE = 128
OFF_RW = 0
OFF_MLA_Q = OFF_RW + RW_IN
OFF_MLA_KV = OFF_MLA_Q + MLA_Q_LORA
OFF_MLA_PE = OFF_MLA_KV + MLA_KV_LORA
PE_LANE = MLA_NOPE
OFF_MB_Z = OFF_MLA_PE + LANE
OFF_MB_XBC = OFF_MB_Z + GROUP_DIM
OFF_MB_DT = OFF_MB_XBC + MB_CONV_DIM
OFF_NA = OFF_MB_DT + LANE
N_IN_PAD = OFF_NA + NA_IN
assert OFF_MLA_Q % MLA_Q_LORA == 0 and OFF_MB_XBC % MB_CONV_DIM == 0 and OFF_MB_Z % GROUP_DIM == 0

V7X_VMEM_BYTES = 64 * 1024 * 1024
VMEM_LIMIT = V7X_VMEM_BYTES * 7 // 8


def _params(*sem):
    return pltpu.CompilerParams(dimension_semantics=sem, vmem_limit_bytes=VMEM_LIMIT)


def _pick_tile(n, target, quantum):
    best = None
    for t in range(quantum, min(n, target) + 1, quantum):
        if n % t == 0:
            best = t
    assert best is not None, (n, target, quantum)
    return best


def _is_ctx_row(tm, tokens_per_batch, n_ctx):
    pos = (pl.program_id(0) * tm) % tokens_per_batch + lax.broadcasted_iota(jnp.int32, (tm, 1), 0)
    return pos < n_ctx


def _proj_in_body(x_ref, scc_ref, scb_ref, shc_ref, shb_ref, w_ref, o_ref, xm_ref, *, tokens_per_batch, n_ctx):
    is_ctx = _is_ctx_row(x_ref.shape[0], tokens_per_batch, n_ctx)

    @pl.when(pl.program_id(1) == 0)
    def _():
        sc = jnp.where(is_ctx, scc_ref[0], scb_ref[0])
        sh = jnp.where(is_ctx, shc_ref[0], shb_ref[0])
        xm_ref[...] = (x_ref[...] * (1.0 + sc) + sh).astype(BF16)

    o_ref[...] = jnp.dot(xm_ref[...], w_ref[...], preferred_element_type=F32)


def _mod_specs(bsz, tiles_per_batch, chunk, d):
    return [pl.BlockSpec((None, 1, d), lambda i, *_: (bsz, 0, chunk)),
            pl.BlockSpec((None, 1, d), lambda i, *_: (i // tiles_per_batch, 0, chunk))]


def proj_in(xs, mod_l, w, layer, *, tokens_per_batch, n_ctx):
    m, d = xs.shape
    n = w.shape[2]
    bsz = m // tokens_per_batch
    tm = _pick_tile(tokens_per_batch, 1100, 16)
    tn = _pick_tile(n, 1152, LANE)
    tiles_per_batch = tokens_per_batch // tm
    return pl.pallas_call(
        functools.partial(_proj_in_body, tokens_per_batch=tokens_per_batch, n_ctx=n_ctx),
        grid=(m // tm, n // tn),
        in_specs=[pl.BlockSpec((tm, d), lambda i, j: (i, 0)),
                  *_mod_specs(bsz, tiles_per_batch, 1, d), *_mod_specs(bsz, tiles_per_batch, 0, d),
                  pl.BlockSpec((None, d, tn), lambda i, j: (layer, 0, j))],
        out_specs=pl.BlockSpec((tm, tn), lambda i, j: (i, j)),
        out_shape=jax.ShapeDtypeStruct((m, n), F32),
        scratch_shapes=[pltpu.VMEM((tm, d), BF16)],
        compiler_params=_params("parallel", "arbitrary"),
        name="proj_in",
    )(xs, mod_l, mod_l, mod_l, mod_l, w)


def _proj_out_body(ryf_ref, ryb_ref, rbonus_ref, rg_ref, gng_ref, gnb_ref,
                   myf_ref, myb_ref, mx_ref, mz_ref, dskip_ref, mng_ref,
                   mla_ref, na_ref, w_ref, x_ref, gc_ref, gb_ref, lg_ref, lb_ref, o_ref,
                   *, tokens_per_batch, n_ctx):
    gd = GROUP_DIM
    lane_r = lax.broadcasted_iota(jnp.int32, (gd, gd), 0) // HEAD_DIM
    lane_c = lax.broadcasted_iota(jnp.int32, (gd, gd), 1) // HEAD_DIM
    same_head = (lane_r == lane_c).astype(F32)
    head_mean = lambda m: _bmm(m, same_head, 'tn,nm->tm', 'split_a') * (1.0 / HEAD_DIM)
    y = ryf_ref[...] + ryb_ref[...]
    yc = y - head_mean(y)
    y = yc * lax.rsqrt(head_mean(yc * yc) + RW_GN_EPS) * gng_ref[...] + gnb_ref[...]
    o_rw = (y + rbonus_ref[...]) * rg_ref[...]
    y = (myf_ref[...] + myb_ref[...] + dskip_ref[...] * mx_ref[...]) * jax.nn.silu(mz_ref[...])
    group = gd // MB_GROUPS
    normed = []
    for g in range(MB_GROUPS):
        yg = y[:, g * group:(g + 1) * group]
        normed.append(yg * lax.rsqrt(jnp.mean(yg * yg, axis=-1, keepdims=True) + RMS_EPS))
    o_mb = jnp.concatenate(normed, axis=1) * mng_ref[...]

    y = None
    for j, mix in enumerate((o_rw, o_mb, mla_ref[...], na_ref[...])):
        part = jnp.dot(mix.astype(BF16), w_ref[j * gd:(j + 1) * gd, :], preferred_element_type=F32)
        y = part if y is None else y + part
    gate = jnp.where(_is_ctx_row(x_ref.shape[0], tokens_per_batch, n_ctx), gc_ref[0], gb_ref[0])
    z = DEEPNORM_ALPHA * x_ref[...] + gate * y
    mu = jnp.mean(z, axis=-1, keepdims=True)
    zc = z - mu
    var = jnp.mean(zc * zc, axis=-1, keepdims=True)
    o_ref[...] = zc * lax.rsqrt(var + LN_EPS) * lg_ref[...] + lb_ref[...]


def proj_out(rwkv, mamba, o_mla, o_na, p, w, layer, xs, mod_l, ln_g, ln_b, *, tokens_per_batch, n_ctx):
    m, d = xs.shape
    gd = GROUP_DIM
    bsz = m // tokens_per_batch
    tm = _pick_tile(tokens_per_batch, 320, 16)
    tiles_per_batch = tokens_per_batch // tm
    row = lambda i: (i, 0)
    fixed = lambda i: (0, 0)
    tok = pl.BlockSpec((tm, gd), row)
    vec = pl.BlockSpec((1, gd), fixed)
    ryf, ryb, rbonus, rg, gn_g, gn_b = rwkv
    myf, myb, xbc, d_skip, norm_g = mamba
    as_vec = lambda v: v.reshape(1, gd)
    return pl.pallas_call(
        functools.partial(_proj_out_body, tokens_per_batch=tokens_per_batch, n_ctx=n_ctx),
        grid=(m // tm,),
        in_specs=[tok, tok, tok, tok, vec, vec,
                  tok, tok, tok, pl.BlockSpec((tm, gd), lambda i: (i, OFF_MB_Z // gd)), vec, vec,
                  tok, tok,
                  pl.BlockSpec((None, d, d), lambda i: (layer, 0, 0)), pl.BlockSpec((tm, d), row),
                  *_mod_specs(bsz, tiles_per_batch, 2, d), pl.BlockSpec((1, d), fixed), pl.BlockSpec((1, d), fixed)],
        out_specs=pl.BlockSpec((tm, d), row),
        out_shape=jax.ShapeDtypeStruct((m, d), F32),
        compiler_params=_params("parallel"),
        name="proj_out",
    )(ryf, ryb, rbonus, rg, as_vec(gn_g), as_vec(gn_b),
      myf, myb, xbc, p, as_vec(jnp.repeat(d_skip, HEAD_DIM)), as_vec(norm_g),
      o_mla, o_na, w, xs, mod_l, mod_l, ln_g.reshape(1, d), ln_b.reshape(1, d))


def _mod_body(c_ref, w_ref, b_ref, o_ref, *, n_rows):
    w = w_ref[...]
    reps = w.shape[1] // LANE
    o_ref[...] = jnp.zeros_like(o_ref)
    for m in range(n_rows):
        cb = c_ref[m]
        cbt = jnp.concatenate([cb] * reps, axis=1)
        o_ref[m:m + 1, :] = jnp.sum(w * cbt, axis=0, keepdims=True) + b_ref[...]


def modulation(cond, ada_w, ada_b):
    n_rows, k = cond.shape
    depth, _, n = ada_w.shape
    tn = 512
    cond_b = jnp.broadcast_to(cond[:, :, None], (n_rows, k, LANE))
    return pl.pallas_call(
        functools.partial(_mod_body, n_rows=n_rows),
        grid=(depth, n // tn),
        in_specs=[pl.BlockSpec((n_rows, k, LANE), lambda l, j: (0, 0, 0)),
                  pl.BlockSpec((None, k, tn), lambda l, j: (l, 0, j)),
                  pl.BlockSpec((None, 1, tn), lambda l, j: (l, 0, j))],
        out_specs=pl.BlockSpec((None, 8, tn), lambda l, j: (l, 0, j)),
        out_shape=jax.ShapeDtypeStruct((depth, 8, n), F32),
        compiler_params=_params("parallel", "parallel"),
        name="modulation",
    )(cond_b, ada_w, ada_b.reshape(depth, 1, n))


def _bmm(a, b, spec, mode):
    if mode == 'f32':
        return jnp.einsum(spec, a, b, preferred_element_type=F32, precision=HIGHEST)
    a_hi, b_hi = a.astype(BF16), b.astype(BF16)
    out = jnp.einsum(spec, a_hi, b_hi, preferred_element_type=F32)
    if mode in ('x3', 'split_b'):
        b_lo = (b - b_hi.astype(F32)).astype(BF16)
        out = out + jnp.einsum(spec, a_hi, b_lo, preferred_element_type=F32)
    if mode in ('x3', 'split_a'):
        a_lo = (a - a_hi.astype(F32)).astype(BF16)
        out = out + jnp.einsum(spec, a_lo, b_hi, preferred_element_type=F32)
    return out


def _unit_triangular_inverse(l_mat, ti, si):
    c = l_mat.shape[1]
    nn = 'hts,hsn->htn'
    blk = 8
    eye = (ti == si).astype(F32)
    l_d = jnp.where((ti // blk) == (si // blk), l_mat, 0.0)
    p2 = _bmm(l_d, l_d, nn, 'bf16')
    p4 = _bmm(p2, p2, nn, 'bf16')
    inv = _bmm(eye - l_d, eye + p2, nn, 'bf16')
    inv = _bmm(inv, eye + p4, nn, 'bf16')
    while blk < c:
        pair = ((ti // (2 * blk)) == (si // (2 * blk))) & ((ti // blk) != (si // blk))
        off = jnp.where(pair, l_mat, 0.0)
        inv = inv - _bmm(_bmm(inv, off, nn, 'bf16'), inv, nn, 'bf16')
        blk *= 2
    return inv


def _first_head_lanes(shape):
    return lax.broadcasted_iota(jnp.int32, shape, len(shape) - 1) % LANE < HEAD_DIM


def _rwkv_body(*refs):
    fwd_in, bwd_in, ka_ref, (yf_ref, yb_ref, st_ref) = refs[:6], refs[6:12], refs[12], refs[13:]

    @pl.when(pl.program_id(0) == 0)
    def _():
        st_ref[...] = jnp.zeros_like(st_ref)

    bsz, c, width = fwd_in[0].shape
    n_pairs = width // LANE
    ti = lax.broadcasted_iota(jnp.int32, (c, c), 0)
    si = lax.broadcasted_iota(jnp.int32, (c, c), 1)
    units = []
    for in_refs, reverse in ((fwd_in, False), (bwd_in, True)):
        tri = ((si >= ti) if reverse else (si <= ti)).astype(F32)
        for row in range(bsz):
            r, k, v, kk, a, lw = (ref[row] for ref in in_refs)
            k = k * (1.0 + (a - 1.0) * ka_ref[...])
            cum = _bmm(tri, lw, 'ts,sn->tn', 'split_b')
            tot = cum[:1] if reverse else cum[c - 1:]
            b = kk * a
            e_neg = jnp.exp(-cum)
            e_rem = jnp.exp(tot - cum)
            pieces = (r * jnp.exp(cum), kk * jnp.exp(cum - lw), k * e_neg, b * e_neg, k * e_rem, b * e_rem, v,
                      jnp.exp(tot))
            for j in range(n_pairs):
                units.append([m[:, j * LANE:(j + 1) * LANE] for m in pieces])
    n_units = len(units)
    rq, kq, kd, bd, kdc, bdc, v, e_tot = (jnp.stack([u[i] for u in units], axis=0) for i in range(8))

    backward = lax.broadcasted_iota(jnp.int32, (n_units, 1, 1), 0) >= n_units // 2
    order = (si - ti) * jnp.where(backward, -1, 1)
    incl = order <= 0
    strict = order < 0
    first = _first_head_lanes((c, LANE))
    first2 = _first_head_lanes((c, 2 * LANE))
    nt, nn, tn = 'utn,usn->uts', 'uts,usn->utn', 'usn,usm->unm'
    qq = jnp.concatenate([jnp.where(first, rq, 0.0), jnp.where(first, kq, 0.0),
                          jnp.where(first, 0.0, rq), jnp.where(first, 0.0, kq)], axis=1)
    ak = _bmm(qq, kd, nt, 'bf16')
    ab = _bmm(qq, bd, nt, 'bf16')
    a_rk = [jnp.where(incl, ak[:, 2 * x * c:(2 * x + 1) * c], 0.0) for x in range(2)]
    a_kk = [jnp.where(strict, ak[:, (2 * x + 1) * c:(2 * x + 2) * c], 0.0) for x in range(2)]
    a_rb = [jnp.where(incl, ab[:, 2 * x * c:(2 * x + 1) * c], 0.0) for x in range(2)]
    a_kb = [jnp.where(strict, ab[:, (2 * x + 1) * c:(2 * x + 2) * c], 0.0) for x in range(2)]
    inv = _unit_triangular_inverse(jnp.concatenate(a_kb, axis=0), ti, si)
    inv = [inv[:n_units], inv[n_units:]]

    xs = [_bmm(inv[x], jnp.concatenate([_bmm(a_kk[x], v, nn, 'bf16'), kq], axis=2), nn, 'bf16') for x in range(2)]
    x2 = jnp.where(first2, xs[0], xs[1])
    arx = jnp.where(first2, _bmm(a_rb[0], x2, nn, 'bf16'), _bmm(a_rb[1], x2, nn, 'bf16'))
    y0 = jnp.where(first, _bmm(a_rk[0], v, nn, 'bf16'), _bmm(a_rk[1], v, nn, 'bf16')) - arx[:, :, :LANE]
    rqp = rq - arx[:, :, LANE:]
    row_n = lax.broadcasted_iota(jnp.int32, (LANE, LANE), 0)
    col_n = lax.broadcasted_iota(jnp.int32, (LANE, LANE), 1)
    same_head = (row_n // HEAD_DIM) == (col_n // HEAD_DIM)
    bx = _bmm(bdc, x2, tn, 'bf16')
    tadd = jnp.where(same_head, _bmm(kdc, v, tn, 'bf16') - bx[:, :, :LANE], 0.0)
    decay = jnp.where(row_n == col_n, jnp.broadcast_to(e_tot, (n_units, LANE, LANE)), 0.0)
    p = jnp.where(same_head, decay - bx[:, :, LANE:], 0.0)
    t0 = st_ref[...]
    y = _bmm(rqp, t0, 'utn,unm->utm', 'bf16') + y0
    st_ref[...] = _bmm(p, t0, 'ujn,unm->ujm', 'bf16') + tadd
    for d, y_ref in enumerate((yf_ref, yb_ref)):
        for row in range(bsz):
            u0 = (d * bsz + row) * n_pairs
            y_ref[row] = jnp.concatenate([y[u0 + j] for j in range(n_pairs)], axis=1)


def _scan_chunk_index(c, n_ctx_chunks, n_chunks, reverse):
    if not reverse:
        return c
    return jnp.where(c < n_ctx_chunks, n_ctx_chunks - 1 - c, n_chunks - 1 - (c - n_ctx_chunks))


def rwkv_scan(r, k, v, kk, k_a, a_dirs, lw_dirs, *, n_ctx):
    bsz, t, width = r.shape
    c = RW_CHUNK
    n_chunks, n_ctx_chunks = t // c, n_ctx // c
    specs = [pl.BlockSpec((bsz, c, width),
                          lambda i, rev=rev: (0, _scan_chunk_index(i, n_ctx_chunks, n_chunks, rev), 0))
             for rev in (False, True)]
    n_units = 2 * bsz * (width // LANE)
    args = [(r, k, v, kk, a_dirs[d], lw_dirs[d]) for d in range(2)]
    return pl.pallas_call(
        _rwkv_body,
        grid=(n_chunks,),
        in_specs=[specs[0]] * 6 + [specs[1]] * 6 + [pl.BlockSpec((1, width), lambda i: (0, 0))],
        out_specs=specs,
        out_shape=[jax.ShapeDtypeStruct((bsz, t, width), F32)] * 2,
        scratch_shapes=[pltpu.VMEM((n_units, LANE, LANE), F32)],
        compiler_params=_params("arbitrary"),
        name="rwkv_scan",
    )(*args[0], *args[1], k_a.reshape(1, width))


def _ssd_body(*refs):
    fwd_in, bwd_in, bias_ref, aneg_ref, (yf_ref, yb_ref, st_ref) = refs[:2], refs[2:4], refs[4], refs[5], refs[6:]

    @pl.when(pl.program_id(0) == 0)
    def _():
        st_ref[...] = jnp.zeros_like(st_ref)

    bsz = fwd_in[0].shape[0]
    n_pairs = st_ref.shape[0] // (2 * bsz)
    results = []
    for d, in_refs in enumerate((fwd_in, bwd_in)):
        for row in range(bsz):
            base = (d * bsz + row) * n_pairs
            results.append(_ssd_chunk(in_refs[0][row], in_refs[1][row], bias_ref[...], aneg_ref[d:d + 1],
                                      [st_ref[base + j] for j in range(n_pairs)], reverse=bool(d)))
    for u, (ys, new_states) in enumerate(results):
        d, row = divmod(u, bsz)
        (yf_ref, yb_ref)[d][row] = jnp.concatenate(ys, axis=1)
        for j in range(n_pairs):
            st_ref[u * n_pairs + j] = new_states[j]


def _ssd_chunk(xbc, dt_raw, dt_bias, a_neg, states, *, reverse):
    c = xbc.shape[0]
    n_pairs = len(states)
    n_h = 2 * n_pairs
    dt = jax.nn.softplus(dt_raw + dt_bias)
    if reverse:
        dt = pltpu.roll(dt, LANE - n_h, axis=1)
    da_col = dt * a_neg
    gd = n_h * HEAD_DIM
    n_g = (xbc.shape[1] - gd) // (2 * MB_STATE)
    pairs_per_group = n_pairs // n_g
    ti = lax.broadcasted_iota(jnp.int32, (c, c), 0)
    si = lax.broadcasted_iota(jnp.int32, (c, c), 1)
    incl = (si >= ti) if reverse else (si <= ti)
    cs_c = jnp.dot(incl.astype(F32), da_col, preferred_element_type=F32, precision=HIGHEST)
    cs_r = cs_c.T
    tot = cs_c[:1, :] if reverse else cs_c[c - 1:, :]
    e_cs = jnp.exp(cs_c)
    e_rem = jnp.exp(tot - cs_c)
    e_tot = jnp.exp(tot)
    bm = [xbc[:, gd + g * MB_STATE:gd + (g + 1) * MB_STATE].astype(BF16) for g in range(n_g)]
    cm = [xbc[:, gd + (n_g + g) * MB_STATE:gd + (n_g + g + 1) * MB_STATE].astype(BF16) for g in range(n_g)]
    gmat = [jnp.einsum('ln,sn->ls', cm[g], bm[g], preferred_element_type=F32) for g in range(n_g)]
    first = _first_head_lanes((c, LANE))
    first_rows = lax.broadcasted_iota(jnp.int32, (LANE, 1), 0) < HEAD_DIM
    pick = lambda m, ha: jnp.where(first, m[:, ha:ha + 1], m[:, ha + 1:ha + 2])
    ys, new_states = [], []
    for j in range(n_pairs):
        g = j // pairs_per_group
        ha = 2 * j
        sl = slice(j * LANE, (j + 1) * LANE)
        xdt = xbc[:, sl] * pick(dt, ha)
        xdt_b = xdt.astype(BF16)
        y_heads = []
        for h in (ha, ha + 1):
            seg = cs_c[:, h:h + 1] - cs_r[h:h + 1, :]
            lmat = jnp.exp(jnp.where(incl, seg, MASKED_SCORE))
            y_heads.append(jnp.dot((gmat[g] * lmat).astype(BF16), xdt_b, preferred_element_type=F32))
        st = states[j]
        y_off = jnp.einsum('ln,pn->lp', cm[g], st.astype(BF16), preferred_element_type=F32) * pick(e_cs, ha)
        ys.append(jnp.where(first, y_heads[0], y_heads[1]) + y_off)
        xdec = (xdt * pick(e_rem, ha)).astype(BF16)
        keep = jnp.where(first_rows, e_tot[:, ha:ha + 1], e_tot[:, ha + 1:ha + 2])
        new_states.append(keep * st + jnp.einsum('lp,ln->pn', xdec, bm[g], preferred_element_type=F32))
    return ys, new_states


def ssd_scan(xbc, p, dt_bias, a_log, *, n_ctx):
    bsz, t, width = xbc.shape
    n_h = dt_bias.shape[1]
    gd = n_h * HEAD_DIM
    c = MB_CHUNK
    n_chunks, n_ctx_chunks = t // c, n_ctx // c
    in_specs, args, out_specs = [], [], []
    for reverse in (False, True):
        idx = lambda i, reverse=reverse: _scan_chunk_index(i, n_ctx_chunks, n_chunks, reverse)
        in_specs += [pl.BlockSpec((bsz, c, width), lambda i, idx=idx: (0, idx(i), 0)),
                     pl.BlockSpec((bsz, c, LANE), lambda i, idx=idx: (0, idx(i), OFF_MB_DT // LANE))]
        args += [xbc, p]
        out_specs.append(pl.BlockSpec((bsz, c, gd), lambda i, idx=idx: (0, idx(i), 0)))
    pad = ((0, 0), (0, LANE - n_h))
    in_specs += [pl.BlockSpec((1, LANE), lambda i: (0, 0)), pl.BlockSpec((2, LANE), lambda i: (0, 0))]
    args += [jnp.pad(dt_bias.reshape(1, 2 * n_h), ((0, 0), (0, LANE - 2 * n_h))), jnp.pad(-jnp.exp(a_log), pad)]
    return pl.pallas_call(
        _ssd_body,
        grid=(n_chunks,),
        in_specs=in_specs,
        out_specs=out_specs,
        out_shape=[jax.ShapeDtypeStruct((bsz, t, gd), F32)] * 2,
        scratch_shapes=[pltpu.VMEM((2 * bsz * (n_h // 2), 2 * HEAD_DIM, MB_STATE), F32)],
        compiler_params=_params("arbitrary"),
        name="ssd_scan",
    )(*args)


def _softmax_pv(s_parts, v_parts):
    m = functools.reduce(jnp.maximum, [jnp.max(s, axis=-1, keepdims=True) for s in s_parts])
    ps = [jnp.exp(s - m) for s in s_parts]
    l = functools.reduce(jnp.add, [jnp.sum(p, axis=-1, keepdims=True) for p in ps])
    o = functools.reduce(jnp.add, [jnp.dot(p.astype(BF16), v, preferred_element_type=F32)
                                   for p, v in zip(ps, v_parts)])
    return o / l


def _rope_lanes(x, cos_tab, sin_tab):
    width = x.shape[1]
    reps = width // MLA_QK_PAD
    cos_w = jnp.concatenate([cos_tab] * reps, axis=1) if reps > 1 else cos_tab
    sin_w = jnp.concatenate([sin_tab] * reps, axis=1) if reps > 1 else sin_tab
    lane = lax.broadcasted_iota(jnp.int32, x.shape, 1) % MLA_QK_PAD
    first_half = (lane - MLA_NOPE) % (MLA_ROPE // 2) < MLA_ROPE // 4
    shift = MLA_ROPE // 4
    partner = jnp.where(first_half, pltpu.roll(x, width - shift, axis=1), pltpu.roll(x, shift, axis=1))
    return x * cos_w + partner * sin_w


def _attn_body(q_ref, k_ref, v_ref, pe_ref, cq_ref, sq_ref, ck_ref, sk_ref, o_ref, kb_ref, vb_ref,
               *, n_ctx, n_ctx_tiles):
    dq = q_ref.shape[2] // 2

    @pl.when(pl.program_id(2) == 0)
    def _():
        pe_rot = _rope_lanes(pe_ref[0], ck_ref[...], sk_ref[...])
        kb_ref[...] = (k_ref[0] + jnp.concatenate([pe_rot, pe_rot], axis=1)).astype(BF16)
        v2 = v_ref[0].astype(BF16)
        own = _first_head_lanes(v2.shape)
        vb_ref[0] = jnp.where(own, v2, jnp.ones_like(v2))
        vb_ref[1] = jnp.where(own, jnp.ones_like(v2), v2)

    q2 = (_rope_lanes(q_ref[0], cq_ref[...], sq_ref[...]) * (MLA_QK ** -0.5)).astype(BF16)

    def attend(n_keys):
        outs = []
        for x in range(2):
            s = jnp.einsum('qd,kd->qk', q2[:, x * dq:(x + 1) * dq], kb_ref[0:n_keys, x * dq:(x + 1) * dq],
                           preferred_element_type=F32)
            p = jnp.exp((s - jnp.max(s, axis=-1, keepdims=True)).astype(BF16))
            o = jnp.dot(p, vb_ref[x, 0:n_keys], preferred_element_type=F32)
            outs.append(o / pltpu.roll(o, HEAD_DIM, axis=1))
        o_ref[0] = jnp.where(_first_head_lanes(outs[0].shape), outs[0], outs[1])

    @pl.when(pl.program_id(2) < n_ctx_tiles)
    def _():
        attend(n_ctx)

    @pl.when(pl.program_id(2) >= n_ctx_tiles)
    def _():
        attend(k_ref.shape[1])


def attention(q, kv, p, tables, *, n_ctx, tile):
    bsz, t, _ = q.shape
    n_pairs = N_HEADS // 2
    dq2 = 2 * MLA_QK_PAD
    v_col0 = N_HEADS * MLA_QK_PAD // LANE
    cos_tab, sin_tab = tables
    q_tab = pl.BlockSpec((tile, MLA_QK_PAD), lambda b, j, i: (i, 0))
    k_tab = pl.BlockSpec((t, MLA_QK_PAD), lambda b, j, i: (0, 0))
    return pl.pallas_call(
        functools.partial(_attn_body, n_ctx=n_ctx, n_ctx_tiles=n_ctx // tile),
        grid=(bsz, n_pairs, t // tile),
        in_specs=[pl.BlockSpec((1, tile, dq2), lambda b, j, i: (b, i, j)),
                  pl.BlockSpec((1, t, dq2), lambda b, j, i: (b, 0, j)),
                  pl.BlockSpec((1, t, LANE), lambda b, j, i: (b, 0, v_col0 + j)),
                  pl.BlockSpec((1, t, LANE), lambda b, j, i: (b, 0, OFF_MLA_PE // LANE)),
                  q_tab, q_tab, k_tab, k_tab],
        out_specs=pl.BlockSpec((1, tile, LANE), lambda b, j, i: (b, i, j)),
        out_shape=jax.ShapeDtypeStruct((bsz, t, N_HEADS * HEAD_DIM), F32),
        scratch_shapes=[pltpu.VMEM((t, dq2), BF16), pltpu.VMEM((2, t, LANE), BF16)],
        compiler_params=_params("parallel", "parallel", "arbitrary"),
        name="attention",
    )(q, kv, kv, p, cos_tab, sin_tab, cos_tab, sin_tab)


def _norm_mm_body(x_ref, g_ref, w_ref, o_ref):
    x = x_ref[...]
    xn = x * lax.rsqrt(jnp.mean(x * x, axis=-1, keepdims=True) + RMS_EPS) * g_ref[...]
    o_ref[...] = jnp.dot(xn.astype(BF16), w_ref[...], preferred_element_type=F32)


def norm_matmul(p, col, g, w):
    m = p.shape[0]
    k, n = w.shape
    tm = _pick_tile(m, 640, 16)
    return pl.pallas_call(
        _norm_mm_body,
        grid=(m // tm,),
        in_specs=[pl.BlockSpec((tm, k), lambda i: (i, col // k)),
                  pl.BlockSpec((1, k), lambda i: (0, 0)),
                  pl.BlockSpec((k, n), lambda i: (0, 0))],
        out_specs=pl.BlockSpec((tm, n), lambda i: (i, 0)),
        out_shape=jax.ShapeDtypeStruct((m, n), F32),
        compiler_params=_params("parallel"),
        name="norm_matmul",
    )(p, g.reshape(1, k), w)


def _na_body(q_ref, k_ref, v_ref, bias_ref, o_ref, kb_ref, vb_ref, *, n_ctx, n_rows):
    i = pl.program_id(2)
    rows_per_step = q_ref.shape[1] // GRID_W
    win = NA_WIN_ROWS * GRID_W
    q2 = q_ref[0] * (HEAD_DIM ** -0.5)

    def head_queries(q_rows, x):
        first = _first_head_lanes(q_rows.shape)
        return jnp.where(first if x == 0 else ~first, q_rows, 0.0).astype(BF16)

    def scores(q, k):
        return jnp.einsum('qd,kd->qk', q, k, preferred_element_type=F32)

    @pl.when(i == 0)
    def _():
        kb_ref[...] = k_ref[0].astype(BF16)
        vb_ref[...] = v_ref[0].astype(BF16)
        kc, vc = kb_ref[0:n_ctx], vb_ref[0:n_ctx]
        outs = [_softmax_pv([scores(head_queries(q2, x), kc)], [vc]) for x in range(2)]
        o_ref[0] = jnp.where(_first_head_lanes(outs[0].shape), outs[0], outs[1])

    @pl.when(i > 0)
    def _():
        qs, kws, vws, biases = [], [], [], []
        for rr in range(rows_per_step):
            r = (i - 1) * rows_per_step + rr
            rs = jnp.clip(r - NA_WIN_ROWS // 2, 0, n_rows - NA_WIN_ROWS)
            start = pl.multiple_of(n_ctx + rs * GRID_W, GRID_W)
            kws.append(kb_ref[pl.ds(start, win)])
            vws.append(vb_ref[pl.ds(start, win)])
            q_row = q2[rr * GRID_W:(rr + 1) * GRID_W]
            qs.append(jnp.concatenate([head_queries(q_row, 0), head_queries(q_row, 1)], axis=0))
            biases.append(jnp.concatenate([bias_ref[0, r - rs], bias_ref[1, r - rs]], axis=0))
        qb = jnp.stack(qs, axis=0)
        kc, vc = kb_ref[0:n_ctx], vb_ref[0:n_ctx]
        s_loc = jnp.einsum('uqd,ukd->uqk', qb, jnp.stack(kws, axis=0), preferred_element_type=F32)
        s_loc = s_loc + jnp.stack(biases, axis=0)
        s_ctx = scores(qb.reshape(rows_per_step * 2 * GRID_W, LANE), kc).reshape(rows_per_step, 2 * GRID_W, n_ctx)
        m = jnp.maximum(jnp.max(s_loc, axis=-1, keepdims=True), jnp.max(s_ctx, axis=-1, keepdims=True))
        p_loc = jnp.exp(s_loc - m)
        p_ctx = jnp.exp(s_ctx - m)
        l = jnp.sum(p_loc, axis=-1, keepdims=True) + jnp.sum(p_ctx, axis=-1, keepdims=True)
        o = jnp.einsum('uqk,ukd->uqd', p_loc.astype(BF16), jnp.stack(vws, axis=0), preferred_element_type=F32)
        o = o + jnp.dot(p_ctx.reshape(rows_per_step * 2 * GRID_W, n_ctx).astype(BF16), vc,
                        preferred_element_type=F32).reshape(o.shape)
        o = o / l
        first = _first_head_lanes((GRID_W, LANE))
        for rr in range(rows_per_step):
            o_ref[0, rr * GRID_W:(rr + 1) * GRID_W] = jnp.where(first, o[rr, :GRID_W], o[rr, GRID_W:])


def na_bias_table(rpb):
    wr = NA_WIN_ROWS
    qv = np.arange(GRID_W)[:, None]
    cv = np.arange(GRID_W)[None, :]
    col_start = np.clip(qv - NA_WIN_COLS // 2, 0, GRID_W - NA_WIN_COLS)
    valid = (cv >= col_start) & (cv < col_start + NA_WIN_COLS)
    offs = np.arange(2 * NA_WIN_COLS - 1)[:, None, None]
    pick = ((cv - qv + (NA_WIN_COLS - 1))[None] == offs) & valid[None]
    cols = jnp.einsum('hro,oqc->hrqc', rpb, jnp.asarray(pick, F32), precision=HIGHEST)
    cols = jnp.where(valid, cols, MASKED_SCORE)
    tab = jnp.stack([cols[:, wr - 1 - var:2 * wr - 1 - var] for var in range(wr)], axis=1)
    tab = jnp.transpose(tab, (0, 1, 3, 2, 4))
    return tab.reshape(rpb.shape[0], wr, GRID_W, wr * GRID_W)


def neighborhood_attention(p, bias, *, n_ctx, col0):
    bsz, t, _ = p.shape
    n_rows = (t - n_ctx) // GRID_W
    n_pairs = GROUP_DIM // LANE
    c0 = col0 // LANE
    return pl.pallas_call(
        functools.partial(_na_body, n_ctx=n_ctx, n_rows=n_rows),
        grid=(bsz, n_pairs, t // n_ctx),
        in_specs=[pl.BlockSpec((1, n_ctx, LANE), lambda b, j, i: (b, i, c0 + j)),
                  pl.BlockSpec((1, t, LANE), lambda b, j, i: (b, 0, c0 + n_pairs + j)),
                  pl.BlockSpec((1, t, LANE), lambda b, j, i: (b, 0, c0 + 2 * n_pairs + j)),
                  pl.BlockSpec((2, NA_WIN_ROWS, GRID_W, NA_WIN_ROWS * GRID_W), lambda b, j, i: (j, 0, 0, 0))],
        out_specs=pl.BlockSpec((1, n_ctx, LANE), lambda b, j, i: (b, i, j)),
        out_shape=jax.ShapeDtypeStruct((bsz, t, GROUP_DIM), F32),
        scratch_shapes=[pltpu.VMEM((t, LANE), BF16), pltpu.VMEM((t, LANE), BF16)],
        compiler_params=_params("parallel", "parallel", "arbitrary"),
        name="neighborhood_attention",
    )(p, p, p, bias)


def _moe_body(h_ref, gate_ref, wg_ref, wu_ref, wd_ref, x_ref, gc_ref, gb_ref, lg_ref, lb_ref, o_ref, acc_ref,
              *, tokens_per_batch, n_ctx):
    e = pl.program_id(1)
    is_ctx = _is_ctx_row(x_ref.shape[0], tokens_per_batch, n_ctx)

    @pl.when(e == 0)
    def _():
        acc_ref[...] = jnp.zeros_like(acc_ref)

    h = h_ref[...]
    hid = jax.nn.silu(jnp.dot(h, wg_ref[0], preferred_element_type=F32)) * jnp.dot(h, wu_ref[0], preferred_element_type=F32)
    lane = lax.broadcasted_iota(jnp.int32, gate_ref.shape, 1)
    gate = jnp.sum(jnp.where(lane == e, gate_ref[...], 0.0), axis=1, keepdims=True)
    acc_ref[...] += gate * jnp.dot(hid.astype(BF16), wd_ref[0], preferred_element_type=F32)

    @pl.when(e == pl.num_programs(1) - 1)
    def _():
        z = DEEPNORM_ALPHA * x_ref[...] + jnp.where(is_ctx, gc_ref[0], gb_ref[0]) * acc_ref[...]
        mu = jnp.mean(z, axis=-1, keepdims=True)
        zc = z - mu
        var = jnp.mean(zc * zc, axis=-1, keepdims=True)
        o_ref[...] = zc * lax.rsqrt(var + LN_EPS) * lg_ref[...] + lb_ref[...]


def moe_experts(h, gate, w_gate, w_up, w_down, layer, xs, mod_l, ln_g, ln_b, *, tokens_per_batch, n_ctx):
    m, d = h.shape
    n_e, d_e = w_gate.shape[1], w_gate.shape[3]
    bsz = m // tokens_per_batch
    tm = _pick_tile(tokens_per_batch, 640, 16)
    tiles_per_batch = tokens_per_batch // tm
    row = lambda i, e: (i, 0)
    fixed = lambda i, e: (0, 0)
    return pl.pallas_call(
        functools.partial(_moe_body, tokens_per_batch=tokens_per_batch, n_ctx=n_ctx),
        grid=(m // tm, n_e),
        in_specs=[pl.BlockSpec((tm, d), row),
                  pl.BlockSpec((tm, LANE), row),
                  pl.BlockSpec((None, 1, d, d_e), lambda i, e: (layer, e, 0, 0)),
                  pl.BlockSpec((None, 1, d, d_e), lambda i, e: (layer, e, 0, 0)),
                  pl.BlockSpec((None, 1, d_e, d), lambda i, e: (layer, e, 0, 0)),
                  pl.BlockSpec((tm, d), row), *_mod_specs(bsz, tiles_per_batch, 5, d),
                  pl.BlockSpec((1, d), fixed), pl.BlockSpec((1, d), fixed)],
        out_specs=pl.BlockSpec((tm, d), row),
        out_shape=jax.ShapeDtypeStruct((m, d), F32),
        scratch_shapes=[pltpu.VMEM((tm, d), F32)],
        compiler_params=_params("parallel", "arbitrary"),
        name="moe_experts",
    )(h, gate, w_gate, w_up, w_down, xs, mod_l, mod_l, ln_g.reshape(1, d), ln_b.reshape(1, d))


def _rope_tables(t, n_ctx):
    n_freq = MLA_ROPE // 4
    pos_t = np.arange(t - n_ctx)
    pos = np.stack([pos_t // GRID_W, pos_t % GRID_W], axis=-1).astype(np.float32)
    inv_freq = jnp.asarray(ROPE_BASE, F32) ** (-jnp.arange(n_freq, dtype=F32) / n_freq)
    ang = jnp.asarray(pos)[:, :, None] * inv_freq
    ang = jnp.concatenate([jnp.zeros((n_ctx, 2, n_freq), F32), ang], axis=0)
    cos = jnp.broadcast_to(jnp.cos(ang)[:, :, None, :], (t, 2, 2, n_freq)).reshape(t, MLA_ROPE)
    sin = jnp.sin(ang)[:, :, None, :] * jnp.asarray([-1.0, 1.0], F32)[None, None, :, None]
    sin = sin.reshape(t, MLA_ROPE)
    pad = MLA_QK_PAD - MLA_QK
    cos_tab = jnp.concatenate([jnp.ones((t, MLA_NOPE), F32), cos, jnp.zeros((t, pad), F32)], axis=1)
    sin_tab = jnp.concatenate([jnp.zeros((t, MLA_NOPE), F32), sin, jnp.zeros((t, pad), F32)], axis=1)
    return cos_tab, sin_tab


HALO = 8


def _shifted_rows(x, halo_prev, halo_next, offset, pos, n_ctx, t):
    tm = x.shape[0]
    out = pltpu.roll(x, (-offset) % tm, axis=0)
    row = lax.broadcasted_iota(jnp.int32, (tm, 1), 0)
    for i in range(abs(offset)):
        if offset < 0:
            out = jnp.where(row == i, halo_prev[HALO + offset + i:HALO + offset + i + 1], out)
        else:
            out = jnp.where(row == tm - offset + i, halo_next[i:i + 1], out)
    src = pos + offset
    same = (src >= 0) & (src < t) & ((pos < n_ctx) == (src < n_ctx))
    return jnp.where(same, out, 0.0)


def _halo_specs(tm, t, width, col):
    per_tile, last = tm // HALO, t // HALO - 1
    return [pl.BlockSpec((1, HALO, width), lambda b, i: (b, jnp.maximum(i * per_tile - 1, 0), col)),
            pl.BlockSpec((1, HALO, width), lambda b, i: (b, jnp.minimum((i + 1) * per_tile, last), col))]


def _rwkv_prep_body(p_ref, pp_ref, pn_ref, mu_ref, w0_ref, a0_ref, kkw_ref, rk_ref, lora_ref,
                    r_ref, ko_ref, v_ref, kk_ref, a0o_ref, a1o_ref, lw0_ref, lw1_ref, g_ref, bonus_ref,
                    *, n_ctx, t):
    x = p_ref[0]
    tm = x.shape[0]
    gd = GROUP_DIM
    pos = pl.program_id(1) * tm + lax.broadcasted_iota(jnp.int32, (tm, 1), 0)
    prev = _shifted_rows(x, pp_ref[0], pn_ref[0], -1, pos, n_ctx, t)
    nxt = _shifted_rows(x, pp_ref[0], pn_ref[0], 1, pos, n_ctx, t)
    x = x + mu_ref[0:1] * (prev - x) + mu_ref[1:2] * (nxt - x)
    r, k, v = x[:, :gd], x[:, gd:2 * gd], x[:, 2 * gd:3 * gd]
    lora_in = jnp.concatenate([jnp.tanh(x[:, 3 * gd:3 * gd + 2 * RW_LORA]),
                               x[:, 3 * gd + 2 * RW_LORA:3 * gd + 4 * RW_LORA],
                               jax.nn.sigmoid(x[:, 3 * gd + 4 * RW_LORA:])], axis=1)
    lo = jnp.dot(lora_in.astype(BF16), lora_ref[...], preferred_element_type=F32)
    lane_r = lax.broadcasted_iota(jnp.int32, (gd, gd), 0) // HEAD_DIM
    lane_c = lax.broadcasted_iota(jnp.int32, (gd, gd), 1) // HEAD_DIM
    same_head = (lane_r == lane_c).astype(F32)
    head_sum = lambda m: _bmm(m, same_head, 'tn,nm->tm', 'split_a')
    kk = k * kkw_ref[...]
    kk_ref[0] = kk * lax.rsqrt(jnp.maximum(head_sum(kk * kk), 1e-12))
    for d, (a_ref, lw_ref) in enumerate(((a0o_ref, lw0_ref), (a1o_ref, lw1_ref))):
        a = jax.nn.sigmoid(a0_ref[d:d + 1] + lo[:, (2 + d) * gd:(3 + d) * gd])
        a_ref[0] = a
        lw_ref[0] = -RW_DECAY_SCALE * jax.nn.sigmoid(w0_ref[d:d + 1] + lo[:, d * gd:(d + 1) * gd])
    r_ref[0] = r
    ko_ref[0] = k
    v_ref[0] = v
    g_ref[0] = lo[:, 4 * gd:]
    bonus_ref[0] = head_sum(r * k * rk_ref[...]) * v


def rwkv_prep(p, mu, w0, a0, k_k, r_k, lora_w, *, n_ctx):
    bsz, t, _ = p.shape
    gd = GROUP_DIM
    tm = _pick_tile(t, 320, 16)
    row = lambda b, i: (b, i, 0)
    fixed = lambda b, i: (0, 0)
    vec = lambda w: w.reshape(1, gd)
    out = jax.ShapeDtypeStruct((bsz, t, gd), F32)
    return pl.pallas_call(
        functools.partial(_rwkv_prep_body, n_ctx=n_ctx, t=t),
        grid=(bsz, t // tm),
        in_specs=[pl.BlockSpec((1, tm, RW_IN), row), *_halo_specs(tm, t, RW_IN, 0),
                  pl.BlockSpec((2, RW_IN), fixed), pl.BlockSpec((2, gd), fixed), pl.BlockSpec((2, gd), fixed),
                  pl.BlockSpec((1, gd), fixed), pl.BlockSpec((1, gd), fixed), pl.BlockSpec(lora_w.shape, fixed)],
        out_specs=[pl.BlockSpec((1, tm, gd), row)] * 10,
        out_shape=[out] * 10,
        compiler_params=_params("parallel", "parallel"),
        name="rwkv_prep",
    )(p, p, p, mu, w0, a0, vec(k_k), vec(r_k), lora_w)


def _rwkv_mixer(p, n_ctx, mu, w0, w_up, a0, a_up, g_up, k_k, k_a, r_k, gn_g, gn_b):
    bsz, t, _ = p.shape
    gd = GROUP_DIM
    n_lora = 4 * RW_LORA + RW_G_LORA
    lora_w = jnp.zeros((n_lora, 5 * gd), F32)
    for j, blk in enumerate((w_up[0], w_up[1], a_up[0], a_up[1])):
        lora_w = lora_w.at[j * RW_LORA:(j + 1) * RW_LORA, j * gd:(j + 1) * gd].set(blk)
    lora_w = lora_w.at[4 * RW_LORA:, 4 * gd:].set(g_up)
    r, k, v, kk, a_f, a_b, lw_f, lw_b, g, bonus = rwkv_prep(p, mu, w0, a0, k_k, r_k, lora_w.astype(BF16), n_ctx=n_ctx)
    y_fwd, y_bwd = rwkv_scan(r, k, v, kk, k_a, (a_f, a_b), (lw_f, lw_b), n_ctx=n_ctx)
    flat = lambda m: m.reshape(bsz * t, gd)
    return flat(y_fwd), flat(y_bwd), flat(bonus), flat(g), gn_g, gn_b


def _conv_body(x_ref, xp_ref, xn_ref, w_ref, b_ref, o_ref, *, n_ctx, t):
    x = x_ref[0]
    tm = x.shape[0]
    pos = pl.program_id(1) * tm + lax.broadcasted_iota(jnp.int32, (tm, 1), 0)
    acc = b_ref[...] + w_ref[MB_CONV // 2:MB_CONV // 2 + 1] * x
    for j in range(MB_CONV):
        offset = j - MB_CONV // 2
        if offset != 0:
            acc = acc + w_ref[j:j + 1] * _shifted_rows(x, xp_ref[0], xn_ref[0], offset, pos, n_ctx, t)
    o_ref[0] = jax.nn.silu(acc)


def conv_silu(p, conv_w, conv_b, *, n_ctx):
    bsz, t, _ = p.shape
    width = MB_CONV_DIM
    tm = _pick_tile(t, 640, 16)
    col = OFF_MB_XBC // width
    return pl.pallas_call(
        functools.partial(_conv_body, n_ctx=n_ctx, t=t),
        grid=(bsz, t // tm),
        in_specs=[pl.BlockSpec((1, tm, width), lambda b, i: (b, i, col)), *_halo_specs(tm, t, width, col),
                  pl.BlockSpec((MB_CONV, width), lambda b, i: (0, 0)), pl.BlockSpec((1, width), lambda b, i: (0, 0))],
        out_specs=pl.BlockSpec((1, tm, width), lambda b, i: (b, i, 0)),
        out_shape=jax.ShapeDtypeStruct((bsz, t, width), F32),
        compiler_params=_params("parallel", "parallel"),
        name="conv_silu",
    )(p, p, p, conv_w, conv_b.reshape(1, width))


def _mamba_mixer(p, n_ctx, conv_w, conv_b, a_log, dt_bias, d_skip, norm_g):
    bsz, t, _ = p.shape
    xbc = conv_silu(p, conv_w, conv_b, n_ctx=n_ctx)
    y_fwd, y_bwd = ssd_scan(xbc, p, dt_bias, a_log, n_ctx=n_ctx)
    flat = lambda m: m.reshape(bsz * t, -1)
    return flat(y_fwd), flat(y_bwd), flat(xbc), d_skip, norm_g


def _mla_mixer(p, n_ctx, q_norm, w_uq, kv_norm, w_ukv, tile, tables):
    bsz, t, n_cols = p.shape
    n_h, pad = N_HEADS, MLA_QK_PAD - MLA_QK
    w_q = jnp.pad(w_uq.reshape(-1, n_h, MLA_QK), ((0, 0), (0, 0), (0, pad))).reshape(-1, n_h * MLA_QK_PAD)
    w_kv = w_ukv.reshape(-1, n_h, MLA_NOPE + HEAD_DIM)
    w_k = jnp.pad(w_kv[..., :MLA_NOPE], ((0, 0), (0, 0), (0, MLA_QK_PAD - MLA_NOPE))).reshape(-1, n_h * MLA_QK_PAD)
    w_v = w_kv[..., MLA_NOPE:].reshape(-1, n_h * HEAD_DIM)
    p2 = p.reshape(bsz * t, n_cols)
    q = norm_matmul(p2, OFF_MLA_Q, q_norm, w_q.astype(BF16)).reshape(bsz, t, -1)
    kv = norm_matmul(p2, OFF_MLA_KV, kv_norm, jnp.concatenate([w_k, w_v], axis=1).astype(BF16)).reshape(bsz, t, -1)
    return attention(q, kv, p, tables, n_ctx=n_ctx, tile=tile)


def _first_max(vals, excluded):
    live = [jnp.where(x, -jnp.inf, v) for v, x in zip(vals, excluded)]
    top = functools.reduce(jnp.maximum, live)
    found = jnp.zeros_like(top, dtype=jnp.bool_)
    first = []
    for v, x in zip(live, excluded):
        hit = (v == top) & ~found & ~x
        first.append(hit)
        found = found | hit
    return top, first


def _router_body(x_ref, sc_ref, sh_ref, rw_ref, rb_ref, h_ref, gate_ref):
    h = x_ref[...] * (1.0 + sc_ref[0]) + sh_ref[0]
    h_ref[...] = h.astype(BF16)
    logits = jnp.dot(h, rw_ref[...], preferred_element_type=F32, precision=HIGHEST)
    scores = jax.nn.sigmoid(logits.T[:N_EXPERTS])
    biased = scores + rb_ref[...]
    rows = [biased[e:e + 1] for e in range(N_EXPERTS)]
    never = jnp.zeros_like(rows[0], dtype=jnp.bool_)
    group_scores, picked = [], []
    for g in range(N_EXPERT_GROUPS):
        vals = rows[g * EXPERTS_PER_GROUP:(g + 1) * EXPERTS_PER_GROUP]
        top1, first = _first_max(vals, [never] * EXPERTS_PER_GROUP)
        top2, second = _first_max(vals, first)
        group_scores.append(top1 + top2)
        picked.append([a | b for a, b in zip(first, second)])
    _, group_sel = _first_max(group_scores, [never] * N_EXPERT_GROUPS)
    chosen = [jnp.where(group_sel[e // EXPERTS_PER_GROUP] & picked[e // EXPERTS_PER_GROUP][e % EXPERTS_PER_GROUP],
                        scores[e:e + 1], 0.0) for e in range(N_EXPERTS)]
    denom = functools.reduce(jnp.add, chosen)
    gates = [w / denom * ROUTED_SCALE for w in chosen]
    gates.append(jnp.zeros((LANE - N_EXPERTS, denom.shape[1]), F32))
    gate_ref[...] = jnp.concatenate(gates, axis=0).T


def router(xs, mod_l, router_w, router_b, *, tokens_per_batch, n_ctx, tile):
    m, d = xs.shape
    bsz = m // tokens_per_batch
    tiles_per_batch, n_ctx_tiles = tokens_per_batch // tile, n_ctx // tile

    def mod_row(i):
        return jnp.where(i % tiles_per_batch < n_ctx_tiles, bsz, i // tiles_per_batch)

    rw_pad = jnp.pad(router_w, ((0, 0), (0, LANE - N_EXPERTS)))
    return pl.pallas_call(
        _router_body,
        grid=(m // tile,),
        in_specs=[pl.BlockSpec((tile, d), lambda i: (i, 0)),
                  pl.BlockSpec((None, 1, d), lambda i: (mod_row(i), 0, 4)),
                  pl.BlockSpec((None, 1, d), lambda i: (mod_row(i), 0, 3)),
                  pl.BlockSpec((d, LANE), lambda i: (0, 0)),
                  pl.BlockSpec((N_EXPERTS, 1), lambda i: (0, 0))],
        out_specs=[pl.BlockSpec((tile, d), lambda i: (i, 0)),
                   pl.BlockSpec((tile, LANE), lambda i: (i, 0))],
        out_shape=[jax.ShapeDtypeStruct((m, d), BF16), jax.ShapeDtypeStruct((m, LANE), F32)],
        compiler_params=_params("parallel"),
        name="router",
    )(xs, mod_l, mod_l, rw_pad, router_b.reshape(N_EXPERTS, 1))


def _pad_w_in(w):
    zeros = lambda n: jnp.zeros(w.shape[:-1] + (n,), w.dtype)
    mb0 = RW_IN
    mla0 = mb0 + MB_IN
    na0 = mla0 + MLA_IN
    rw, mb, mla, na = w[..., :mb0], w[..., mb0:mla0], w[..., mla0:na0], w[..., na0:]
    q_kv, pe = mla[..., :MLA_Q_LORA + MLA_KV_LORA], mla[..., MLA_Q_LORA + MLA_KV_LORA:]
    return jnp.concatenate([rw, q_kv, zeros(PE_LANE), pe, zeros(LANE - PE_LANE - MLA_ROPE),
                            mb, zeros(LANE - 2 * N_HEADS), na], axis=-1)


def kernel(x, c, ctx, c_ctx, ada_w, ada_b, w_in, w_out, ln1_g, ln1_b, ln2_g, ln2_b, rw_mu, rw_w0, rw_w_up, rw_a0, rw_a_up, rw_g_up, rw_k_k, rw_k_a, rw_r_k, rw_gn_g, rw_gn_b, mb_conv_w, mb_conv_b, mb_a_log, mb_dt_bias, mb_d, mb_norm_g, mla_q_norm, mla_w_uq, mla_kv_norm, mla_w_ukv, na_rpb, router_w, router_b, exp_w_gate, exp_w_up, exp_w_down):
    bsz, seq, d = x.shape
    n_ctx = ctx.shape[1]
    t = n_ctx + seq
    depth = ada_w.shape[0]
    tile = min(256, n_ctx)
    cond = jnp.concatenate([jax.nn.silu(c), jax.nn.silu(c_ctx)[None]], axis=0)
    mod = modulation(cond, ada_w, ada_b)
    xs = jnp.concatenate([ctx, x], axis=1)
    expert_w = [w.astype(BF16) for w in (exp_w_gate, exp_w_up, exp_w_down)]
    rope_tables = _rope_tables(t, n_ctx)
    w_in_all = _pad_w_in(w_in.astype(BF16))
    w_out_all = w_out.astype(BF16)
    m = bsz * t
    for l in range(depth):
        mod_l = mod[l][:, None, :]
        p = proj_in(xs.reshape(m, d), mod_l, w_in_all, l, tokens_per_batch=t, n_ctx=n_ctx).reshape(bsz, t, N_IN_PAD)
        o_rw = _rwkv_mixer(p, n_ctx, rw_mu[l], rw_w0[l], rw_w_up[l], rw_a0[l], rw_a_up[l],
                           rw_g_up[l], rw_k_k[l], rw_k_a[l], rw_r_k[l], rw_gn_g[l], rw_gn_b[l])
        o_mb = _mamba_mixer(p, n_ctx, mb_conv_w[l], mb_conv_b[l], mb_a_log[l], mb_dt_bias[l], mb_d[l], mb_norm_g[l])
        o_mla = _mla_mixer(p, n_ctx, mla_q_norm[l], mla_w_uq[l], mla_kv_norm[l], mla_w_ukv[l], tile, rope_tables)
        o_na = neighborhood_attention(p, na_bias_table(na_rpb[l]), n_ctx=n_ctx, col0=OFF_NA)
        xs2 = proj_out(o_rw, o_mb, o_mla.reshape(m, GROUP_DIM), o_na.reshape(m, GROUP_DIM), p.reshape(m, N_IN_PAD),
                       w_out_all, l, xs.reshape(m, d), mod_l, ln1_g[l], ln1_b[l],
                       tokens_per_batch=t, n_ctx=n_ctx)
        h, gate = router(xs2, mod_l, router_w, router_b, tokens_per_batch=t, n_ctx=n_ctx, tile=tile)
        xs = moe_experts(h, gate, *expert_w, l, xs2, mod_l, ln2_g[l], ln2_b[l],
                         tokens_per_batch=t, n_ctx=n_ctx).reshape(bsz, t, d)
    return xs[:, n_ctx:]
```

```python
import functools

import numpy as np
import jax
import jax.numpy as jnp
from jax import lax
from jax.experimental import pallas as pl
from jax.experimental.pallas import tpu as pltpu

F32 = jnp.float32
BF16 = jnp.bfloat16
HIGHEST = lax.Precision.HIGHEST

D_MODEL = 2048
DEPTH = 4
GRID_W = 64
GROUP_DIM = D_MODEL // 4
HEAD_DIM = 64
N_HEADS = GROUP_DIM // HEAD_DIM

RW_LORA = 64
RW_G_LORA = 128
RW_DECAY_SCALE = 0.606531
RW_GN_EPS = 64e-5
RW_IN = 3 * GROUP_DIM + 4 * RW_LORA + RW_G_LORA
RW_CHUNK = 64

MB_GROUPS = 2
MB_STATE = 128
MB_CONV = 5
MB_CHUNK = 128
MB_CONV_DIM = GROUP_DIM + 2 * MB_GROUPS * MB_STATE
MB_IN = GROUP_DIM + MB_CONV_DIM + 2 * N_HEADS

MLA_Q_LORA = 3 * D_MODEL // 16
MLA_KV_LORA = D_MODEL // 16
MLA_NOPE = 64
MLA_ROPE = 32
MLA_QK = MLA_NOPE + MLA_ROPE
MLA_QK_PAD = 128
MLA_IN = MLA_Q_LORA + MLA_KV_LORA + MLA_ROPE
ROPE_BASE = 10000.0

NA_WIN_ROWS = 8
NA_WIN_COLS = 16
NA_IN = 3 * GROUP_DIM
MASKED_SCORE = -1e30

N_EXPERTS = 16
N_EXPERT_GROUPS = 4
EXPERTS_PER_GROUP = N_EXPERTS // N_EXPERT_GROUPS
ROUTED_SCALE = 2.5

DEEPNORM_ALPHA = (2 * DEPTH) ** 0.25
LN_EPS = 1e-6
RMS_EPS = 1e-6

LANE = 128
OFF_RW = 0
OFF_MLA_Q = OFF_RW + RW_IN
OFF_MLA_KV = OFF_MLA_Q + MLA_Q_LORA
OFF_MLA_PE = OFF_MLA_KV + MLA_KV_LORA
PE_LANE = MLA_NOPE
OFF_MB_Z = OFF_MLA_PE + LANE
OFF_MB_XBC = OFF_MB_Z + GROUP_DIM
OFF_MB_DT = OFF_MB_XBC + MB_CONV_DIM
OFF_NA = OFF_MB_DT + LANE
N_IN_PAD = OFF_NA + NA_IN
assert OFF_MLA_Q % MLA_Q_LORA == 0 and OFF_MB_XBC % MB_CONV_DIM == 0 and OFF_MB_Z % GROUP_DIM == 0

V7X_VMEM_BYTES = 64 * 1024 * 1024
VMEM_LIMIT = V7X_VMEM_BYTES * 7 // 8


def _params(*sem):
    return pltpu.CompilerParams(dimension_semantics=sem, vmem_limit_bytes=VMEM_LIMIT)


def _pick_tile(n, target, quantum):
    best = None
    for t in range(quantum, min(n, target) + 1, quantum):
        if n % t == 0:
            best = t
    assert best is not None, (n, target, quantum)
    return best


def _is_ctx_row(tm, tokens_per_batch, n_ctx):
    pos = (pl.program_id(0) * tm) % tokens_per_batch + lax.broadcasted_iota(jnp.int32, (tm, 1), 0)
    return pos < n_ctx


def _proj_in_body(x_ref, scc_ref, scb_ref, shc_ref, shb_ref, w_ref, o_ref, xm_ref, *, tokens_per_batch, n_ctx):
    is_ctx = _is_ctx_row(x_ref.shape[0], tokens_per_batch, n_ctx)

    @pl.when(pl.program_id(1) == 0)
    def _():
        sc = jnp.where(is_ctx, scc_ref[0], scb_ref[0])
        sh = jnp.where(is_ctx, shc_ref[0], shb_ref[0])
        xm_ref[...] = (x_ref[...] * (1.0 + sc) + sh).astype(BF16)

    o_ref[...] = jnp.dot(xm_ref[...], w_ref[...], preferred_element_type=F32)


def _mod_specs(bsz, tiles_per_batch, chunk, d):
    return [pl.BlockSpec((None, 1, d), lambda i, *_: (bsz, 0, chunk)),
            pl.BlockSpec((None, 1, d), lambda i, *_: (i // tiles_per_batch, 0, chunk))]


def proj_in(xs, mod_l, w, layer, *, tokens_per_batch, n_ctx):
    m, d = xs.shape
    n = w.shape[2]
    bsz = m // tokens_per_batch
    tm = _pick_tile(tokens_per_batch, 1100, 16)
    tn = _pick_tile(n, 1152, LANE)
    tiles_per_batch = tokens_per_batch // tm
    return pl.pallas_call(
        functools.partial(_proj_in_body, tokens_per_batch=tokens_per_batch, n_ctx=n_ctx),
        grid=(m // tm, n // tn),
        in_specs=[pl.BlockSpec((tm, d), lambda i, j: (i, 0)),
                  *_mod_specs(bsz, tiles_per_batch, 1, d), *_mod_specs(bsz, tiles_per_batch, 0, d),
                  pl.BlockSpec((None, d, tn), lambda i, j: (layer, 0, j))],
        out_specs=pl.BlockSpec((tm, tn), lambda i, j: (i, j)),
        out_shape=jax.ShapeDtypeStruct((m, n), F32),
        scratch_shapes=[pltpu.VMEM((tm, d), BF16)],
        compiler_params=_params("parallel", "arbitrary"),
        name="proj_in",
    )(xs, mod_l, mod_l, mod_l, mod_l, w)


def _proj_out_body(ryf_ref, ryb_ref, rbonus_ref, rg_ref, gng_ref, gnb_ref,
                   myf_ref, myb_ref, mx_ref, mz_ref, dskip_ref, mng_ref,
                   mla_ref, na_ref, w_ref, x_ref, gc_ref, gb_ref, lg_ref, lb_ref,
                   sc2c_ref, sc2b_ref, sh2c_ref, sh2b_ref, rw_ref, rb_ref, o_ref, h_ref, gate_ref,
                   *, tokens_per_batch, n_ctx):
    gd = GROUP_DIM
    is_ctx = _is_ctx_row(x_ref.shape[0], tokens_per_batch, n_ctx)
    lane_r = lax.broadcasted_iota(jnp.int32, (gd, gd), 0) // HEAD_DIM
    lane_c = lax.broadcasted_iota(jnp.int32, (gd, gd), 1) // HEAD_DIM
    same_head = (lane_r == lane_c).astype(F32)
    head_mean = lambda m: _bmm(m, same_head, 'tn,nm->tm', 'split_a') * (1.0 / HEAD_DIM)
    y = ryf_ref[...] + ryb_ref[...]
    yc = y - head_mean(y)
    y = yc * lax.rsqrt(head_mean(yc * yc) + RW_GN_EPS) * gng_ref[...] + gnb_ref[...]
    o_rw = (y + rbonus_ref[...]) * rg_ref[...]
    y = (myf_ref[...] + myb_ref[...] + dskip_ref[...] * mx_ref[...]) * jax.nn.silu(mz_ref[...])
    group = gd // MB_GROUPS
    normed = []
    for g in range(MB_GROUPS):
        yg = y[:, g * group:(g + 1) * group]
        normed.append(yg * lax.rsqrt(jnp.mean(yg * yg, axis=-1, keepdims=True) + RMS_EPS))
    o_mb = jnp.concatenate(normed, axis=1) * mng_ref[...]

    y = None
    for j, mix in enumerate((o_rw, o_mb, mla_ref[...], na_ref[...])):
        part = jnp.dot(mix.astype(BF16), w_ref[j * gd:(j + 1) * gd, :], preferred_element_type=F32)
        y = part if y is None else y + part
    z = DEEPNORM_ALPHA * x_ref[...] + jnp.where(is_ctx, gc_ref[0], gb_ref[0]) * y
    mu = jnp.mean(z, axis=-1, keepdims=True)
    zc = z - mu
    var = jnp.mean(zc * zc, axis=-1, keepdims=True)
    xs2 = zc * lax.rsqrt(var + LN_EPS) * lg_ref[...] + lb_ref[...]
    o_ref[...] = xs2
    h = xs2 * (1.0 + jnp.where(is_ctx, sc2c_ref[0], sc2b_ref[0])) + jnp.where(is_ctx, sh2c_ref[0], sh2b_ref[0])
    h_ref[...] = h.astype(BF16)
    gate_ref[...] = _route(h, rw_ref, rb_ref)


def proj_out(rwkv, mamba, o_mla, o_na, p, w, layer, xs, mod_l, ln_g, ln_b, router_w, router_b,
             *, tokens_per_batch, n_ctx, tile):
    m, d = xs.shape
    gd = GROUP_DIM
    bsz = m // tokens_per_batch
    tm = tile
    tiles_per_batch = tokens_per_batch // tm
    row = lambda i: (i, 0)
    fixed = lambda i: (0, 0)
    tok = pl.BlockSpec((tm, gd), row)
    vec = pl.BlockSpec((1, gd), fixed)
    ryf, ryb, rbonus, rg, gn_g, gn_b = rwkv
    myf, myb, xbc, d_skip, norm_g = mamba
    as_vec = lambda v: v.reshape(1, gd)
    return pl.pallas_call(
        functools.partial(_proj_out_body, tokens_per_batch=tokens_per_batch, n_ctx=n_ctx),
        grid=(m // tm,),
        in_specs=[tok, tok, tok, tok, vec, vec,
                  tok, tok, tok, pl.BlockSpec((tm, gd), lambda i: (i, OFF_MB_Z // gd)), vec, vec,
                  tok, tok,
                  pl.BlockSpec((None, d, d), lambda i: (layer, 0, 0)), pl.BlockSpec((tm, d), row),
                  *_mod_specs(bsz, tiles_per_batch, 2, d), pl.BlockSpec((1, d), fixed), pl.BlockSpec((1, d), fixed),
                  *_mod_specs(bsz, tiles_per_batch, 4, d), *_mod_specs(bsz, tiles_per_batch, 3, d),
                  pl.BlockSpec((d, LANE), fixed), pl.BlockSpec((N_EXPERTS, 1), fixed)],
        out_specs=[pl.BlockSpec((tm, d), row), pl.BlockSpec((tm, d), row), pl.BlockSpec((tm, LANE), row)],
        out_shape=[jax.ShapeDtypeStruct((m, d), F32), jax.ShapeDtypeStruct((m, d), BF16),
                   jax.ShapeDtypeStruct((m, LANE), F32)],
        compiler_params=_params("parallel"),
        name="proj_out",
    )(ryf, ryb, rbonus, rg, as_vec(gn_g), as_vec(gn_b),
      myf, myb, xbc, p, as_vec(jnp.repeat(d_skip, HEAD_DIM)), as_vec(norm_g),
      o_mla, o_na, w, xs, mod_l, mod_l, ln_g.reshape(1, d), ln_b.reshape(1, d),
      mod_l, mod_l, mod_l, mod_l, jnp.pad(router_w, ((0, 0), (0, LANE - N_EXPERTS))), router_b.reshape(N_EXPERTS, 1))


def _mod_body(c_ref, w_ref, b_ref, o_ref, *, n_rows):
    w = w_ref[...]
    reps = w.shape[1] // LANE
    o_ref[...] = jnp.zeros_like(o_ref)
    for m in range(n_rows):
        cb = c_ref[m]
        cbt = jnp.concatenate([cb] * reps, axis=1)
        o_ref[m:m + 1, :] = jnp.sum(w * cbt, axis=0, keepdims=True) + b_ref[...]


def modulation(cond, ada_w, ada_b):
    n_rows, k = cond.shape
    depth, _, n = ada_w.shape
    tn = 512
    cond_b = jnp.broadcast_to(cond[:, :, None], (n_rows, k, LANE))
    return pl.pallas_call(
        functools.partial(_mod_body, n_rows=n_rows),
        grid=(depth, n // tn),
        in_specs=[pl.BlockSpec((n_rows, k, LANE), lambda l, j: (0, 0, 0)),
                  pl.BlockSpec((None, k, tn), lambda l, j: (l, 0, j)),
                  pl.BlockSpec((None, 1, tn), lambda l, j: (l, 0, j))],
        out_specs=pl.BlockSpec((None, 8, tn), lambda l, j: (l, 0, j)),
        out_shape=jax.ShapeDtypeStruct((depth, 8, n), F32),
        compiler_params=_params("parallel", "parallel"),
        name="modulation",
    )(cond_b, ada_w, ada_b.reshape(depth, 1, n))


def _bmm(a, b, spec, mode):
    if mode == 'f32':
        return jnp.einsum(spec, a, b, preferred_element_type=F32, precision=HIGHEST)
    a_hi, b_hi = a.astype(BF16), b.astype(BF16)
    out = jnp.einsum(spec, a_hi, b_hi, preferred_element_type=F32)
    if mode in ('x3', 'split_b'):
        b_lo = (b - b_hi.astype(F32)).astype(BF16)
        out = out + jnp.einsum(spec, a_hi, b_lo, preferred_element_type=F32)
    if mode in ('x3', 'split_a'):
        a_lo = (a - a_hi.astype(F32)).astype(BF16)
        out = out + jnp.einsum(spec, a_lo, b_hi, preferred_element_type=F32)
    return out


def _unit_triangular_inverse(l_mat, ti, si):
    c = l_mat.shape[1]
    nn = 'hts,hsn->htn'
    blk = 8
    eye = (ti == si).astype(F32)
    l_d = jnp.where((ti // blk) == (si // blk), l_mat, 0.0)
    p2 = _bmm(l_d, l_d, nn, 'bf16')
    p4 = _bmm(p2, p2, nn, 'bf16')
    inv = _bmm(eye - l_d, eye + p2, nn, 'bf16')
    inv = _bmm(inv, eye + p4, nn, 'bf16')
    while blk < c:
        pair = ((ti // (2 * blk)) == (si // (2 * blk))) & ((ti // blk) != (si // blk))
        off = jnp.where(pair, l_mat, 0.0)
        inv = inv - _bmm(_bmm(inv, off, nn, 'bf16'), inv, nn, 'bf16')
        blk *= 2
    return inv


def _first_head_lanes(shape):
    return lax.broadcasted_iota(jnp.int32, shape, len(shape) - 1) % LANE < HEAD_DIM


def _rwkv_body(*refs):
    fwd_in, bwd_in, ka_ref, (yf_ref, yb_ref, st_ref) = refs[:6], refs[6:12], refs[12], refs[13:]

    @pl.when(pl.program_id(0) == 0)
    def _():
        st_ref[...] = jnp.zeros_like(st_ref)

    bsz, c, width = fwd_in[0].shape
    n_pairs = width // LANE
    ti = lax.broadcasted_iota(jnp.int32, (c, c), 0)
    si = lax.broadcasted_iota(jnp.int32, (c, c), 1)
    units = []
    for in_refs, reverse in ((fwd_in, False), (bwd_in, True)):
        tri = ((si >= ti) if reverse else (si <= ti)).astype(F32)
        for row in range(bsz):
            r, k, v, kk, a, lw = (ref[row] for ref in in_refs)
            k = k * (1.0 + (a - 1.0) * ka_ref[...])
            cum = _bmm(tri, lw, 'ts,sn->tn', 'split_b')
            tot = cum[:1] if reverse else cum[c - 1:]
            b = kk * a
            e_neg = jnp.exp(-cum)
            e_rem = jnp.exp(tot - cum)
            pieces = (r * jnp.exp(cum), kk * jnp.exp(cum - lw), k * e_neg, b * e_neg, k * e_rem, b * e_rem, v,
                      jnp.exp(tot))
            for j in range(n_pairs):
                units.append([m[:, j * LANE:(j + 1) * LANE] for m in pieces])
    n_units = len(units)
    rq, kq, kd, bd, kdc, bdc, v, e_tot = (jnp.stack([u[i] for u in units], axis=0) for i in range(8))

    backward = lax.broadcasted_iota(jnp.int32, (n_units, 1, 1), 0) >= n_units // 2
    order = (si - ti) * jnp.where(backward, -1, 1)
    incl = order <= 0
    strict = order < 0
    first = _first_head_lanes((c, LANE))
    first2 = _first_head_lanes((c, 2 * LANE))
    nt, nn, tn = 'utn,usn->uts', 'uts,usn->utn', 'usn,usm->unm'
    qq = jnp.concatenate([jnp.where(first, rq, 0.0), jnp.where(first, kq, 0.0),
                          jnp.where(first, 0.0, rq), jnp.where(first, 0.0, kq)], axis=1)
    ak = _bmm(qq, kd, nt, 'bf16')
    ab = _bmm(qq, bd, nt, 'bf16')
    a_rk = [jnp.where(incl, ak[:, 2 * x * c:(2 * x + 1) * c], 0.0) for x in range(2)]
    a_kk = [jnp.where(strict, ak[:, (2 * x + 1) * c:(2 * x + 2) * c], 0.0) for x in range(2)]
    a_rb = [jnp.where(incl, ab[:, 2 * x * c:(2 * x + 1) * c], 0.0) for x in range(2)]
    a_kb = [jnp.where(strict, ab[:, (2 * x + 1) * c:(2 * x + 2) * c], 0.0) for x in range(2)]
    inv = _unit_triangular_inverse(jnp.concatenate(a_kb, axis=0), ti, si)
    inv = [inv[:n_units], inv[n_units:]]

    xs = [_bmm(inv[x], jnp.concatenate([_bmm(a_kk[x], v, nn, 'bf16'), kq], axis=2), nn, 'bf16') for x in range(2)]
    x2 = jnp.where(first2, xs[0], xs[1])
    arx = jnp.where(first2, _bmm(a_rb[0], x2, nn, 'bf16'), _bmm(a_rb[1], x2, nn, 'bf16'))
    y0 = jnp.where(first, _bmm(a_rk[0], v, nn, 'bf16'), _bmm(a_rk[1], v, nn, 'bf16')) - arx[:, :, :LANE]
    rqp = rq - arx[:, :, LANE:]
    row_n = lax.broadcasted_iota(jnp.int32, (LANE, LANE), 0)
    col_n = lax.broadcasted_iota(jnp.int32, (LANE, LANE), 1)
    same_head = (row_n // HEAD_DIM) == (col_n // HEAD_DIM)
    bx = _bmm(bdc, x2, tn, 'bf16')
    tadd = jnp.where(same_head, _bmm(kdc, v, tn, 'bf16') - bx[:, :, :LANE], 0.0)
    decay = jnp.where(row_n == col_n, jnp.broadcast_to(e_tot, (n_units, LANE, LANE)), 0.0)
    p = jnp.where(same_head, decay - bx[:, :, LANE:], 0.0)
    t0 = st_ref[...]
    y = _bmm(rqp, t0, 'utn,unm->utm', 'bf16') + y0
    st_ref[...] = _bmm(p, t0, 'ujn,unm->ujm', 'bf16') + tadd
    for d, y_ref in enumerate((yf_ref, yb_ref)):
        for row in range(bsz):
            u0 = (d * bsz + row) * n_pairs
            y_ref[row] = jnp.concatenate([y[u0 + j] for j in range(n_pairs)], axis=1)


def _scan_chunk_index(c, n_ctx_chunks, n_chunks, reverse):
    if not reverse:
        return c
    return jnp.where(c < n_ctx_chunks, n_ctx_chunks - 1 - c, n_chunks - 1 - (c - n_ctx_chunks))


def rwkv_scan(r, k, v, kk, k_a, a_dirs, lw_dirs, *, n_ctx):
    bsz, t, width = r.shape
    c = RW_CHUNK
    n_chunks, n_ctx_chunks = t // c, n_ctx // c
    specs = [pl.BlockSpec((bsz, c, width),
                          lambda i, rev=rev: (0, _scan_chunk_index(i, n_ctx_chunks, n_chunks, rev), 0))
             for rev in (False, True)]
    n_units = 2 * bsz * (width // LANE)
    args = [(r, k, v, kk, a_dirs[d], lw_dirs[d]) for d in range(2)]
    return pl.pallas_call(
        _rwkv_body,
        grid=(n_chunks,),
        in_specs=[specs[0]] * 6 + [specs[1]] * 6 + [pl.BlockSpec((1, width), lambda i: (0, 0))],
        out_specs=specs,
        out_shape=[jax.ShapeDtypeStruct((bsz, t, width), F32)] * 2,
        scratch_shapes=[pltpu.VMEM((n_units, LANE, LANE), F32)],
        compiler_params=_params("arbitrary"),
        name="rwkv_scan",
    )(*args[0], *args[1], k_a.reshape(1, width))


def _ssd_body(*refs):
    fwd_in, bwd_in, bias_ref, aneg_ref, (yf_ref, yb_ref, st_ref) = refs[:2], refs[2:4], refs[4], refs[5], refs[6:]

    @pl.when(pl.program_id(0) == 0)
    def _():
        st_ref[...] = jnp.zeros_like(st_ref)

    bsz = fwd_in[0].shape[0]
    n_pairs = st_ref.shape[0] // (2 * bsz)
    results = []
    for d, in_refs in enumerate((fwd_in, bwd_in)):
        for row in range(bsz):
            base = (d * bsz + row) * n_pairs
            results.append(_ssd_chunk(in_refs[0][row], in_refs[1][row], bias_ref[...], aneg_ref[d:d + 1],
                                      [st_ref[base + j] for j in range(n_pairs)], reverse=bool(d)))
    for u, (ys, new_states) in enumerate(results):
        d, row = divmod(u, bsz)
        (yf_ref, yb_ref)[d][row] = jnp.concatenate(ys, axis=1)
        for j in range(n_pairs):
            st_ref[u * n_pairs + j] = new_states[j]


def _ssd_chunk(xbc, dt_raw, dt_bias, a_neg, states, *, reverse):
    c = xbc.shape[0]
    n_pairs = len(states)
    n_h = 2 * n_pairs
    dt = jax.nn.softplus(dt_raw + dt_bias)
    if reverse:
        dt = pltpu.roll(dt, LANE - n_h, axis=1)
    da_col = dt * a_neg
    gd = n_h * HEAD_DIM
    n_g = (xbc.shape[1] - gd) // (2 * MB_STATE)
    pairs_per_group = n_pairs // n_g
    ti = lax.broadcasted_iota(jnp.int32, (c, c), 0)
    si = lax.broadcasted_iota(jnp.int32, (c, c), 1)
    incl = (si >= ti) if reverse else (si <= ti)
    cs_c = jnp.dot(incl.astype(F32), da_col, preferred_element_type=F32, precision=HIGHEST)
    cs_r = cs_c.T
    tot = cs_c[:1, :] if reverse else cs_c[c - 1:, :]
    e_cs = jnp.exp(cs_c)
    e_rem = jnp.exp(tot - cs_c)
    e_tot = jnp.exp(tot)
    bm = [xbc[:, gd + g * MB_STATE:gd + (g + 1) * MB_STATE].astype(BF16) for g in range(n_g)]
    cm = [xbc[:, gd + (n_g + g) * MB_STATE:gd + (n_g + g + 1) * MB_STATE].astype(BF16) for g in range(n_g)]
    gmat = [jnp.einsum('ln,sn->ls', cm[g], bm[g], preferred_element_type=F32) for g in range(n_g)]
    first = _first_head_lanes((c, LANE))
    first_rows = lax.broadcasted_iota(jnp.int32, (LANE, 1), 0) < HEAD_DIM
    pick = lambda m, ha: jnp.where(first, m[:, ha:ha + 1], m[:, ha + 1:ha + 2])
    ys, new_states = [], []
    for j in range(n_pairs):
        g = j // pairs_per_group
        ha = 2 * j
        sl = slice(j * LANE, (j + 1) * LANE)
        xdt = xbc[:, sl] * pick(dt, ha)
        xdt_b = xdt.astype(BF16)
        y_heads = []
        for h in (ha, ha + 1):
            seg = cs_c[:, h:h + 1] - cs_r[h:h + 1, :]
            lmat = jnp.exp(jnp.where(incl, seg, MASKED_SCORE))
            y_heads.append(jnp.dot((gmat[g] * lmat).astype(BF16), xdt_b, preferred_element_type=F32))
        st = states[j]
        y_off = jnp.einsum('ln,pn->lp', cm[g], st.astype(BF16), preferred_element_type=F32) * pick(e_cs, ha)
        ys.append(jnp.where(first, y_heads[0], y_heads[1]) + y_off)
        xdec = (xdt * pick(e_rem, ha)).astype(BF16)
        keep = jnp.where(first_rows, e_tot[:, ha:ha + 1], e_tot[:, ha + 1:ha + 2])
        new_states.append(keep * st + jnp.einsum('lp,ln->pn', xdec, bm[g], preferred_element_type=F32))
    return ys, new_states


def ssd_scan(xbc, p, dt_bias, a_log, *, n_ctx):
    bsz, t, width = xbc.shape
    n_h = dt_bias.shape[1]
    gd = n_h * HEAD_DIM
    c = MB_CHUNK
    n_chunks, n_ctx_chunks = t // c, n_ctx // c
    in_specs, args, out_specs = [], [], []
    for reverse in (False, True):
        idx = lambda i, reverse=reverse: _scan_chunk_index(i, n_ctx_chunks, n_chunks, reverse)
        in_specs += [pl.BlockSpec((bsz, c, width), lambda i, idx=idx: (0, idx(i), 0)),
                     pl.BlockSpec((bsz, c, LANE), lambda i, idx=idx: (0, idx(i), OFF_MB_DT // LANE))]
        args += [xbc, p]
        out_specs.append(pl.BlockSpec((bsz, c, gd), lambda i, idx=idx: (0, idx(i), 0)))
    pad = ((0, 0), (0, LANE - n_h))
    in_specs += [pl.BlockSpec((1, LANE), lambda i: (0, 0)), pl.BlockSpec((2, LANE), lambda i: (0, 0))]
    args += [jnp.pad(dt_bias.reshape(1, 2 * n_h), ((0, 0), (0, LANE - 2 * n_h))), jnp.pad(-jnp.exp(a_log), pad)]
    return pl.pallas_call(
        _ssd_body,
        grid=(n_chunks,),
        in_specs=in_specs,
        out_specs=out_specs,
        out_shape=[jax.ShapeDtypeStruct((bsz, t, gd), F32)] * 2,
        scratch_shapes=[pltpu.VMEM((2 * bsz * (n_h // 2), 2 * HEAD_DIM, MB_STATE), F32)],
        compiler_params=_params("arbitrary"),
        name="ssd_scan",
    )(*args)


def _softmax_pv(s_parts, v_parts):
    m = functools.reduce(jnp.maximum, [jnp.max(s, axis=-1, keepdims=True) for s in s_parts])
    ps = [jnp.exp(s - m) for s in s_parts]
    l = functools.reduce(jnp.add, [jnp.sum(p, axis=-1, keepdims=True) for p in ps])
    o = functools.reduce(jnp.add, [jnp.dot(p.astype(BF16), v, preferred_element_type=F32)
                                   for p, v in zip(ps, v_parts)])
    return o / l


def _rope_lanes(x, cos_tab, sin_tab):
    width = x.shape[1]
    reps = width // MLA_QK_PAD
    cos_w = jnp.concatenate([cos_tab] * reps, axis=1) if reps > 1 else cos_tab
    sin_w = jnp.concatenate([sin_tab] * reps, axis=1) if reps > 1 else sin_tab
    lane = lax.broadcasted_iota(jnp.int32, x.shape, 1) % MLA_QK_PAD
    first_half = (lane - MLA_NOPE) % (MLA_ROPE // 2) < MLA_ROPE // 4
    shift = MLA_ROPE // 4
    partner = jnp.where(first_half, pltpu.roll(x, width - shift, axis=1), pltpu.roll(x, shift, axis=1))
    return x * cos_w + partner * sin_w


def _attn_body(q_ref, k_ref, v_ref, pe_ref, cq_ref, sq_ref, ck_ref, sk_ref, o_ref, kb_ref, vb_ref,
               *, n_ctx, n_ctx_tiles):
    dq = q_ref.shape[2] // 2

    @pl.when(pl.program_id(2) == 0)
    def _():
        pe_rot = _rope_lanes(pe_ref[0], ck_ref[...], sk_ref[...])
        kb_ref[...] = (k_ref[0] + jnp.concatenate([pe_rot, pe_rot], axis=1)).astype(BF16)
        v2 = v_ref[0].astype(BF16)
        own = _first_head_lanes(v2.shape)
        vb_ref[0] = jnp.where(own, v2, jnp.ones_like(v2))
        vb_ref[1] = jnp.where(own, jnp.ones_like(v2), v2)

    q2 = (_rope_lanes(q_ref[0], cq_ref[...], sq_ref[...]) * (MLA_QK ** -0.5)).astype(BF16)

    def attend(n_keys):
        outs = []
        for x in range(2):
            s = jnp.einsum('qd,kd->qk', q2[:, x * dq:(x + 1) * dq], kb_ref[0:n_keys, x * dq:(x + 1) * dq],
                           preferred_element_type=F32)
            p = jnp.exp((s - jnp.max(s, axis=-1, keepdims=True)).astype(BF16))
            o = jnp.dot(p, vb_ref[x, 0:n_keys], preferred_element_type=F32)
            outs.append(o / pltpu.roll(o, HEAD_DIM, axis=1))
        o_ref[0] = jnp.where(_first_head_lanes(outs[0].shape), outs[0], outs[1])

    @pl.when(pl.program_id(2) < n_ctx_tiles)
    def _():
        attend(n_ctx)

    @pl.when(pl.program_id(2) >= n_ctx_tiles)
    def _():
        attend(k_ref.shape[1])


def attention(q, kv, p, tables, *, n_ctx, tile):
    bsz, t, _ = q.shape
    n_pairs = N_HEADS // 2
    dq2 = 2 * MLA_QK_PAD
    v_col0 = N_HEADS * MLA_QK_PAD // LANE
    cos_tab, sin_tab = tables
    q_tab = pl.BlockSpec((tile, MLA_QK_PAD), lambda b, j, i: (i, 0))
    k_tab = pl.BlockSpec((t, MLA_QK_PAD), lambda b, j, i: (0, 0))
    return pl.pallas_call(
        functools.partial(_attn_body, n_ctx=n_ctx, n_ctx_tiles=n_ctx // tile),
        grid=(bsz, n_pairs, t // tile),
        in_specs=[pl.BlockSpec((1, tile, dq2), lambda b, j, i: (b, i, j)),
                  pl.BlockSpec((1, t, dq2), lambda b, j, i: (b, 0, j)),
                  pl.BlockSpec((1, t, LANE), lambda b, j, i: (b, 0, v_col0 + j)),
                  pl.BlockSpec((1, t, LANE), lambda b, j, i: (b, 0, OFF_MLA_PE // LANE)),
                  q_tab, q_tab, k_tab, k_tab],
        out_specs=pl.BlockSpec((1, tile, LANE), lambda b, j, i: (b, i, j)),
        out_shape=jax.ShapeDtypeStruct((bsz, t, N_HEADS * HEAD_DIM), F32),
        scratch_shapes=[pltpu.VMEM((t, dq2), BF16), pltpu.VMEM((2, t, LANE), BF16)],
        compiler_params=_params("parallel", "parallel", "arbitrary"),
        name="attention",
    )(q, kv, kv, p, cos_tab, sin_tab, cos_tab, sin_tab)


def _norm_mm_body(x_ref, g_ref, w_ref, o_ref):
    x = x_ref[...]
    xn = x * lax.rsqrt(jnp.mean(x * x, axis=-1, keepdims=True) + RMS_EPS) * g_ref[...]
    o_ref[...] = jnp.dot(xn.astype(BF16), w_ref[...], preferred_element_type=F32)


def norm_matmul(p, col, g, w):
    m = p.shape[0]
    k, n = w.shape
    tm = _pick_tile(m, 640, 16)
    return pl.pallas_call(
        _norm_mm_body,
        grid=(m // tm,),
        in_specs=[pl.BlockSpec((tm, k), lambda i: (i, col // k)),
                  pl.BlockSpec((1, k), lambda i: (0, 0)),
                  pl.BlockSpec((k, n), lambda i: (0, 0))],
        out_specs=pl.BlockSpec((tm, n), lambda i: (i, 0)),
        out_shape=jax.ShapeDtypeStruct((m, n), F32),
        compiler_params=_params("parallel"),
        name="norm_matmul",
    )(p, g.reshape(1, k), w)


def _na_body(q_ref, k_ref, v_ref, bias_ref, o_ref, kb_ref, vb_ref, *, n_ctx, n_rows):
    i = pl.program_id(2)
    rows_per_step = q_ref.shape[1] // GRID_W
    win = NA_WIN_ROWS * GRID_W
    q2 = q_ref[0] * (HEAD_DIM ** -0.5)

    def head_queries(q_rows, x):
        first = _first_head_lanes(q_rows.shape)
        return jnp.where(first if x == 0 else ~first, q_rows, 0.0).astype(BF16)

    def scores(q, k):
        return jnp.einsum('qd,kd->qk', q, k, preferred_element_type=F32)

    @pl.when(i == 0)
    def _():
        kb_ref[...] = k_ref[0].astype(BF16)
        vb_ref[...] = v_ref[0].astype(BF16)
        kc, vc = kb_ref[0:n_ctx], vb_ref[0:n_ctx]
        outs = [_softmax_pv([scores(head_queries(q2, x), kc)], [vc]) for x in range(2)]
        o_ref[0] = jnp.where(_first_head_lanes(outs[0].shape), outs[0], outs[1])

    @pl.when(i > 0)
    def _():
        qs, kws, vws, biases = [], [], [], []
        for rr in range(rows_per_step):
            r = (i - 1) * rows_per_step + rr
            rs = jnp.clip(r - NA_WIN_ROWS // 2, 0, n_rows - NA_WIN_ROWS)
            start = pl.multiple_of(n_ctx + rs * GRID_W, GRID_W)
            kws.append(kb_ref[pl.ds(start, win)])
            vws.append(vb_ref[pl.ds(start, win)])
            q_row = q2[rr * GRID_W:(rr + 1) * GRID_W]
            qs.append(jnp.concatenate([head_queries(q_row, 0), head_queries(q_row, 1)], axis=0))
            biases.append(jnp.concatenate([bias_ref[0, r - rs], bias_ref[1, r - rs]], axis=0))
        qb = jnp.stack(qs, axis=0)
        kc, vc = kb_ref[0:n_ctx], vb_ref[0:n_ctx]
        s_loc = jnp.einsum('uqd,ukd->uqk', qb, jnp.stack(kws, axis=0), preferred_element_type=F32)
        s_loc = s_loc + jnp.stack(biases, axis=0)
        s_ctx = scores(qb.reshape(rows_per_step * 2 * GRID_W, LANE), kc).reshape(rows_per_step, 2 * GRID_W, n_ctx)
        m = jnp.maximum(jnp.max(s_loc, axis=-1, keepdims=True), jnp.max(s_ctx, axis=-1, keepdims=True))
        p_loc = jnp.exp(s_loc - m)
        p_ctx = jnp.exp(s_ctx - m)
        l = jnp.sum(p_loc, axis=-1, keepdims=True) + jnp.sum(p_ctx, axis=-1, keepdims=True)
        o = jnp.einsum('uqk,ukd->uqd', p_loc.astype(BF16), jnp.stack(vws, axis=0), preferred_element_type=F32)
        o = o + jnp.dot(p_ctx.reshape(rows_per_step * 2 * GRID_W, n_ctx).astype(BF16), vc,
                        preferred_element_type=F32).reshape(o.shape)
        o = o / l
        first = _first_head_lanes((GRID_W, LANE))
        for rr in range(rows_per_step):
            o_ref[0, rr * GRID_W:(rr + 1) * GRID_W] = jnp.where(first, o[rr, :GRID_W], o[rr, GRID_W:])


def na_bias_table(rpb):
    wr = NA_WIN_ROWS
    qv = np.arange(GRID_W)[:, None]
    cv = np.arange(GRID_W)[None, :]
    col_start = np.clip(qv - NA_WIN_COLS // 2, 0, GRID_W - NA_WIN_COLS)
    valid = (cv >= col_start) & (cv < col_start + NA_WIN_COLS)
    offs = np.arange(2 * NA_WIN_COLS - 1)[:, None, None]
    pick = ((cv - qv + (NA_WIN_COLS - 1))[None] == offs) & valid[None]
    cols = jnp.einsum('hro,oqc->hrqc', rpb, jnp.asarray(pick, F32), precision=HIGHEST)
    cols = jnp.where(valid, cols, MASKED_SCORE)
    tab = jnp.stack([cols[:, wr - 1 - var:2 * wr - 1 - var] for var in range(wr)], axis=1)
    tab = jnp.transpose(tab, (0, 1, 3, 2, 4))
    return tab.reshape(rpb.shape[0], wr, GRID_W, wr * GRID_W)


def neighborhood_attention(p, bias, *, n_ctx, col0):
    bsz, t, _ = p.shape
    n_rows = (t - n_ctx) // GRID_W
    n_pairs = GROUP_DIM // LANE
    c0 = col0 // LANE
    return pl.pallas_call(
        functools.partial(_na_body, n_ctx=n_ctx, n_rows=n_rows),
        grid=(bsz, n_pairs, t // n_ctx),
        in_specs=[pl.BlockSpec((1, n_ctx, LANE), lambda b, j, i: (b, i, c0 + j)),
                  pl.BlockSpec((1, t, LANE), lambda b, j, i: (b, 0, c0 + n_pairs + j)),
                  pl.BlockSpec((1, t, LANE), lambda b, j, i: (b, 0, c0 + 2 * n_pairs + j)),
                  pl.BlockSpec((2, NA_WIN_ROWS, GRID_W, NA_WIN_ROWS * GRID_W), lambda b, j, i: (j, 0, 0, 0))],
        out_specs=pl.BlockSpec((1, n_ctx, LANE), lambda b, j, i: (b, i, j)),
        out_shape=jax.ShapeDtypeStruct((bsz, t, GROUP_DIM), F32),
        scratch_shapes=[pltpu.VMEM((t, LANE), BF16), pltpu.VMEM((t, LANE), BF16)],
        compiler_params=_params("parallel", "parallel", "arbitrary"),
        name="neighborhood_attention",
    )(p, p, p, bias)


def _moe_body(h_ref, gate_ref, wg_ref, wu_ref, wd_ref, x_ref, gc_ref, gb_ref, lg_ref, lb_ref, o_ref, acc_ref,
              *, tokens_per_batch, n_ctx):
    e = pl.program_id(1)
    is_ctx = _is_ctx_row(x_ref.shape[0], tokens_per_batch, n_ctx)

    @pl.when(e == 0)
    def _():
        acc_ref[...] = jnp.zeros_like(acc_ref)

    h = h_ref[...]
    hid = jax.nn.silu(jnp.dot(h, wg_ref[0], preferred_element_type=F32)) * jnp.dot(h, wu_ref[0], preferred_element_type=F32)
    lane = lax.broadcasted_iota(jnp.int32, gate_ref.shape, 1)
    gate = jnp.sum(jnp.where(lane == e, gate_ref[...], 0.0), axis=1, keepdims=True)
    acc_ref[...] += gate * jnp.dot(hid.astype(BF16), wd_ref[0], preferred_element_type=F32)

    @pl.when(e == pl.num_programs(1) - 1)
    def _():
        z = DEEPNORM_ALPHA * x_ref[...] + jnp.where(is_ctx, gc_ref[0], gb_ref[0]) * acc_ref[...]
        mu = jnp.mean(z, axis=-1, keepdims=True)
        zc = z - mu
        var = jnp.mean(zc * zc, axis=-1, keepdims=True)
        o_ref[...] = zc * lax.rsqrt(var + LN_EPS) * lg_ref[...] + lb_ref[...]


def moe_experts(h, gate, w_gate, w_up, w_down, layer, xs, mod_l, ln_g, ln_b, *, tokens_per_batch, n_ctx):
    m, d = h.shape
    n_e, d_e = w_gate.shape[1], w_gate.shape[3]
    bsz = m // tokens_per_batch
    tm = _pick_tile(tokens_per_batch, 640, 16)
    tiles_per_batch = tokens_per_batch // tm
    row = lambda i, e: (i, 0)
    fixed = lambda i, e: (0, 0)
    return pl.pallas_call(
        functools.partial(_moe_body, tokens_per_batch=tokens_per_batch, n_ctx=n_ctx),
        grid=(m // tm, n_e),
        in_specs=[pl.BlockSpec((tm, d), row),
                  pl.BlockSpec((tm, LANE), row),
                  pl.BlockSpec((None, 1, d, d_e), lambda i, e: (layer, e, 0, 0)),
                  pl.BlockSpec((None, 1, d, d_e), lambda i, e: (layer, e, 0, 0)),
                  pl.BlockSpec((None, 1, d_e, d), lambda i, e: (layer, e, 0, 0)),
                  pl.BlockSpec((tm, d), row), *_mod_specs(bsz, tiles_per_batch, 5, d),
                  pl.BlockSpec((1, d), fixed), pl.BlockSpec((1, d), fixed)],
        out_specs=pl.BlockSpec((tm, d), row),
        out_shape=jax.ShapeDtypeStruct((m, d), F32),
        scratch_shapes=[pltpu.VMEM((tm, d), F32)],
        compiler_params=_params("parallel", "arbitrary"),
        name="moe_experts",
    )(h, gate, w_gate, w_up, w_down, xs, mod_l, mod_l, ln_g.reshape(1, d), ln_b.reshape(1, d))


def _rope_tables(t, n_ctx):
    n_freq = MLA_ROPE // 4
    pos_t = np.arange(t - n_ctx)
    pos = np.stack([pos_t // GRID_W, pos_t % GRID_W], axis=-1).astype(np.float32)
    inv_freq = jnp.asarray(ROPE_BASE, F32) ** (-jnp.arange(n_freq, dtype=F32) / n_freq)
    ang = jnp.asarray(pos)[:, :, None] * inv_freq
    ang = jnp.concatenate([jnp.zeros((n_ctx, 2, n_freq), F32), ang], axis=0)
    cos = jnp.broadcast_to(jnp.cos(ang)[:, :, None, :], (t, 2, 2, n_freq)).reshape(t, MLA_ROPE)
    sin = jnp.sin(ang)[:, :, None, :] * jnp.asarray([-1.0, 1.0], F32)[None, None, :, None]
    sin = sin.reshape(t, MLA_ROPE)
    pad = MLA_QK_PAD - MLA_QK
    cos_tab = jnp.concatenate([jnp.ones((t, MLA_NOPE), F32), cos, jnp.zeros((t, pad), F32)], axis=1)
    sin_tab = jnp.concatenate([jnp.zeros((t, MLA_NOPE), F32), sin, jnp.zeros((t, pad), F32)], axis=1)
    return cos_tab, sin_tab


HALO = 8


def _shifted_rows(x, halo_prev, halo_next, offset, pos, n_ctx, t):
    tm = x.shape[0]
    out = pltpu.roll(x, (-offset) % tm, axis=0)
    row = lax.broadcasted_iota(jnp.int32, (tm, 1), 0)
    for i in range(abs(offset)):
        if offset < 0:
            out = jnp.where(row == i, halo_prev[HALO + offset + i:HALO + offset + i + 1], out)
        else:
            out = jnp.where(row == tm - offset + i, halo_next[i:i + 1], out)
    src = pos + offset
    same = (src >= 0) & (src < t) & ((pos < n_ctx) == (src < n_ctx))
    return jnp.where(same, out, 0.0)


def _halo_specs(tm, t, width, col):
    per_tile, last = tm // HALO, t // HALO - 1
    return [pl.BlockSpec((1, HALO, width), lambda b, i: (b, jnp.maximum(i * per_tile - 1, 0), col)),
            pl.BlockSpec((1, HALO, width), lambda b, i: (b, jnp.minimum((i + 1) * per_tile, last), col))]


def _rwkv_prep_body(p_ref, pp_ref, pn_ref, mu_ref, w0_ref, a0_ref, kkw_ref, rk_ref, lora_ref,
                    r_ref, ko_ref, v_ref, kk_ref, a0o_ref, a1o_ref, lw0_ref, lw1_ref, g_ref, bonus_ref,
                    *, n_ctx, t):
    x = p_ref[0]
    tm = x.shape[0]
    gd = GROUP_DIM
    pos = pl.program_id(1) * tm + lax.broadcasted_iota(jnp.int32, (tm, 1), 0)
    prev = _shifted_rows(x, pp_ref[0], pn_ref[0], -1, pos, n_ctx, t)
    nxt = _shifted_rows(x, pp_ref[0], pn_ref[0], 1, pos, n_ctx, t)
    x = x + mu_ref[0:1] * (prev - x) + mu_ref[1:2] * (nxt - x)
    r, k, v = x[:, :gd], x[:, gd:2 * gd], x[:, 2 * gd:3 * gd]
    lora_in = jnp.concatenate([jnp.tanh(x[:, 3 * gd:3 * gd + 2 * RW_LORA]),
                               x[:, 3 * gd + 2 * RW_LORA:3 * gd + 4 * RW_LORA],
                               jax.nn.sigmoid(x[:, 3 * gd + 4 * RW_LORA:])], axis=1)
    lo = jnp.dot(lora_in.astype(BF16), lora_ref[...], preferred_element_type=F32)
    lane_r = lax.broadcasted_iota(jnp.int32, (gd, gd), 0) // HEAD_DIM
    lane_c = lax.broadcasted_iota(jnp.int32, (gd, gd), 1) // HEAD_DIM
    same_head = (lane_r == lane_c).astype(F32)
    head_sum = lambda m: _bmm(m, same_head, 'tn,nm->tm', 'split_a')
    kk = k * kkw_ref[...]
    kk_ref[0] = kk * lax.rsqrt(jnp.maximum(head_sum(kk * kk), 1e-12))
    for d, (a_ref, lw_ref) in enumerate(((a0o_ref, lw0_ref), (a1o_ref, lw1_ref))):
        a = jax.nn.sigmoid(a0_ref[d:d + 1] + lo[:, (2 + d) * gd:(3 + d) * gd])
        a_ref[0] = a
        lw_ref[0] = -RW_DECAY_SCALE * jax.nn.sigmoid(w0_ref[d:d + 1] + lo[:, d * gd:(d + 1) * gd])
    r_ref[0] = r
    ko_ref[0] = k
    v_ref[0] = v
    g_ref[0] = lo[:, 4 * gd:]
    bonus_ref[0] = head_sum(r * k * rk_ref[...]) * v


def rwkv_prep(p, mu, w0, a0, k_k, r_k, lora_w, *, n_ctx):
    bsz, t, _ = p.shape
    gd = GROUP_DIM
    tm = _pick_tile(t, 320, 16)
    row = lambda b, i: (b, i, 0)
    fixed = lambda b, i: (0, 0)
    vec = lambda w: w.reshape(1, gd)
    out = jax.ShapeDtypeStruct((bsz, t, gd), F32)
    return pl.pallas_call(
        functools.partial(_rwkv_prep_body, n_ctx=n_ctx, t=t),
        grid=(bsz, t // tm),
        in_specs=[pl.BlockSpec((1, tm, RW_IN), row), *_halo_specs(tm, t, RW_IN, 0),
                  pl.BlockSpec((2, RW_IN), fixed), pl.BlockSpec((2, gd), fixed), pl.BlockSpec((2, gd), fixed),
                  pl.BlockSpec((1, gd), fixed), pl.BlockSpec((1, gd), fixed), pl.BlockSpec(lora_w.shape, fixed)],
        out_specs=[pl.BlockSpec((1, tm, gd), row)] * 10,
        out_shape=[out] * 10,
        compiler_params=_params("parallel", "parallel"),
        name="rwkv_prep",
    )(p, p, p, mu, w0, a0, vec(k_k), vec(r_k), lora_w)


def _rwkv_mixer(p, n_ctx, mu, w0, w_up, a0, a_up, g_up, k_k, k_a, r_k, gn_g, gn_b):
    bsz, t, _ = p.shape
    gd = GROUP_DIM
    n_lora = 4 * RW_LORA + RW_G_LORA
    lora_w = jnp.zeros((n_lora, 5 * gd), F32)
    for j, blk in enumerate((w_up[0], w_up[1], a_up[0], a_up[1])):
        lora_w = lora_w.at[j * RW_LORA:(j + 1) * RW_LORA, j * gd:(j + 1) * gd].set(blk)
    lora_w = lora_w.at[4 * RW_LORA:, 4 * gd:].set(g_up)
    r, k, v, kk, a_f, a_b, lw_f, lw_b, g, bonus = rwkv_prep(p, mu, w0, a0, k_k, r_k, lora_w.astype(BF16), n_ctx=n_ctx)
    y_fwd, y_bwd = rwkv_scan(r, k, v, kk, k_a, (a_f, a_b), (lw_f, lw_b), n_ctx=n_ctx)
    flat = lambda m: m.reshape(bsz * t, gd)
    return flat(y_fwd), flat(y_bwd), flat(bonus), flat(g), gn_g, gn_b


def _conv_body(x_ref, xp_ref, xn_ref, w_ref, b_ref, o_ref, *, n_ctx, t):
    x = x_ref[0]
    tm = x.shape[0]
    pos = pl.program_id(1) * tm + lax.broadcasted_iota(jnp.int32, (tm, 1), 0)
    acc = b_ref[...] + w_ref[MB_CONV // 2:MB_CONV // 2 + 1] * x
    for j in range(MB_CONV):
        offset = j - MB_CONV // 2
        if offset != 0:
            acc = acc + w_ref[j:j + 1] * _shifted_rows(x, xp_ref[0], xn_ref[0], offset, pos, n_ctx, t)
    o_ref[0] = jax.nn.silu(acc)


def conv_silu(p, conv_w, conv_b, *, n_ctx):
    bsz, t, _ = p.shape
    width = MB_CONV_DIM
    tm = _pick_tile(t, 640, 16)
    col = OFF_MB_XBC // width
    return pl.pallas_call(
        functools.partial(_conv_body, n_ctx=n_ctx, t=t),
        grid=(bsz, t // tm),
        in_specs=[pl.BlockSpec((1, tm, width), lambda b, i: (b, i, col)), *_halo_specs(tm, t, width, col),
                  pl.BlockSpec((MB_CONV, width), lambda b, i: (0, 0)), pl.BlockSpec((1, width), lambda b, i: (0, 0))],
        out_specs=pl.BlockSpec((1, tm, width), lambda b, i: (b, i, 0)),
        out_shape=jax.ShapeDtypeStruct((bsz, t, width), F32),
        compiler_params=_params("parallel", "parallel"),
        name="conv_silu",
    )(p, p, p, conv_w, conv_b.reshape(1, width))


def _mamba_mixer(p, n_ctx, conv_w, conv_b, a_log, dt_bias, d_skip, norm_g):
    bsz, t, _ = p.shape
    xbc = conv_silu(p, conv_w, conv_b, n_ctx=n_ctx)
    y_fwd, y_bwd = ssd_scan(xbc, p, dt_bias, a_log, n_ctx=n_ctx)
    flat = lambda m: m.reshape(bsz * t, -1)
    return flat(y_fwd), flat(y_bwd), flat(xbc), d_skip, norm_g


def _mla_mixer(p, n_ctx, q_norm, w_uq, kv_norm, w_ukv, tile, tables):
    bsz, t, n_cols = p.shape
    n_h, pad = N_HEADS, MLA_QK_PAD - MLA_QK
    w_q = jnp.pad(w_uq.reshape(-1, n_h, MLA_QK), ((0, 0), (0, 0), (0, pad))).reshape(-1, n_h * MLA_QK_PAD)
    w_kv = w_ukv.reshape(-1, n_h, MLA_NOPE + HEAD_DIM)
    w_k = jnp.pad(w_kv[..., :MLA_NOPE], ((0, 0), (0, 0), (0, MLA_QK_PAD - MLA_NOPE))).reshape(-1, n_h * MLA_QK_PAD)
    w_v = w_kv[..., MLA_NOPE:].reshape(-1, n_h * HEAD_DIM)
    p2 = p.reshape(bsz * t, n_cols)
    q = norm_matmul(p2, OFF_MLA_Q, q_norm, w_q.astype(BF16)).reshape(bsz, t, -1)
    kv = norm_matmul(p2, OFF_MLA_KV, kv_norm, jnp.concatenate([w_k, w_v], axis=1).astype(BF16)).reshape(bsz, t, -1)
    return attention(q, kv, p, tables, n_ctx=n_ctx, tile=tile)


def _first_max(vals, excluded):
    live = [jnp.where(x, -jnp.inf, v) for v, x in zip(vals, excluded)]
    top = functools.reduce(jnp.maximum, live)
    found = jnp.zeros_like(top, dtype=jnp.bool_)
    first = []
    for v, x in zip(live, excluded):
        hit = (v == top) & ~found & ~x
        first.append(hit)
        found = found | hit
    return top, first


def _route(h, rw_ref, rb_ref):
    logits = _bmm(h, rw_ref[...], 'td,dn->tn', 'x3')
    scores = jax.nn.sigmoid(logits.T[:N_EXPERTS])
    biased = scores + rb_ref[...]
    rows = [biased[e:e + 1] for e in range(N_EXPERTS)]
    never = jnp.zeros_like(rows[0], dtype=jnp.bool_)
    group_scores, picked = [], []
    for g in range(N_EXPERT_GROUPS):
        vals = rows[g * EXPERTS_PER_GROUP:(g + 1) * EXPERTS_PER_GROUP]
        top1, first = _first_max(vals, [never] * EXPERTS_PER_GROUP)
        top2, second = _first_max(vals, first)
        group_scores.append(top1 + top2)
        picked.append([a | b for a, b in zip(first, second)])
    _, group_sel = _first_max(group_scores, [never] * N_EXPERT_GROUPS)
    chosen = [jnp.where(group_sel[e // EXPERTS_PER_GROUP] & picked[e // EXPERTS_PER_GROUP][e % EXPERTS_PER_GROUP],
                        scores[e:e + 1], 0.0) for e in range(N_EXPERTS)]
    denom = functools.reduce(jnp.add, chosen)
    gates = [w / denom * ROUTED_SCALE for w in chosen]
    gates.append(jnp.zeros((LANE - N_EXPERTS, denom.shape[1]), F32))
    return jnp.concatenate(gates, axis=0).T


def _pad_w_in(w):
    zeros = lambda n: jnp.zeros(w.shape[:-1] + (n,), w.dtype)
    mb0 = RW_IN
    mla0 = mb0 + MB_IN
    na0 = mla0 + MLA_IN
    rw, mb, mla, na = w[..., :mb0], w[..., mb0:mla0], w[..., mla0:na0], w[..., na0:]
    q_kv, pe = mla[..., :MLA_Q_LORA + MLA_KV_LORA], mla[..., MLA_Q_LORA + MLA_KV_LORA:]
    return jnp.concatenate([rw, q_kv, zeros(PE_LANE), pe, zeros(LANE - PE_LANE - MLA_ROPE),
                            mb, zeros(LANE - 2 * N_HEADS), na], axis=-1)


def kernel(x, c, ctx, c_ctx, ada_w, ada_b, w_in, w_out, ln1_g, ln1_b, ln2_g, ln2_b, rw_mu, rw_w0, rw_w_up, rw_a0, rw_a_up, rw_g_up, rw_k_k, rw_k_a, rw_r_k, rw_gn_g, rw_gn_b, mb_conv_w, mb_conv_b, mb_a_log, mb_dt_bias, mb_d, mb_norm_g, mla_q_norm, mla_w_uq, mla_kv_norm, mla_w_ukv, na_rpb, router_w, router_b, exp_w_gate, exp_w_up, exp_w_down):
    bsz, seq, d = x.shape
    n_ctx = ctx.shape[1]
    t = n_ctx + seq
    depth = ada_w.shape[0]
    tile = min(256, n_ctx)
    cond = jnp.concatenate([jax.nn.silu(c), jax.nn.silu(c_ctx)[None]], axis=0)
    mod = modulation(cond, ada_w, ada_b)
    xs = jnp.concatenate([ctx, x], axis=1)
    expert_w = [w.astype(BF16) for w in (exp_w_gate, exp_w_up, exp_w_down)]
    rope_tables = _rope_tables(t, n_ctx)
    w_in_all = _pad_w_in(w_in.astype(BF16))
    w_out_all = w_out.astype(BF16)
    m = bsz * t
    for l in range(depth):
        mod_l = mod[l][:, None, :]
        p = proj_in(xs.reshape(m, d), mod_l, w_in_all, l, tokens_per_batch=t, n_ctx=n_ctx).reshape(bsz, t, N_IN_PAD)
        o_rw = _rwkv_mixer(p, n_ctx, rw_mu[l], rw_w0[l], rw_w_up[l], rw_a0[l], rw_a_up[l],
                           rw_g_up[l], rw_k_k[l], rw_k_a[l], rw_r_k[l], rw_gn_g[l], rw_gn_b[l])
        o_mb = _mamba_mixer(p, n_ctx, mb_conv_w[l], mb_conv_b[l], mb_a_log[l], mb_dt_bias[l], mb_d[l], mb_norm_g[l])
        o_mla = _mla_mixer(p, n_ctx, mla_q_norm[l], mla_w_uq[l], mla_kv_norm[l], mla_w_ukv[l], tile, rope_tables)
        o_na = neighborhood_attention(p, na_bias_table(na_rpb[l]), n_ctx=n_ctx, col0=OFF_NA)
        xs2, h, gate = proj_out(o_rw, o_mb, o_mla.reshape(m, GROUP_DIM), o_na.reshape(m, GROUP_DIM),
                                p.reshape(m, N_IN_PAD), w_out_all, l, xs.reshape(m, d), mod_l, ln1_g[l], ln1_b[l],
                                router_w, router_b, tokens_per_batch=t, n_ctx=n_ctx, tile=tile)
        xs = moe_experts(h, gate, *expert_w, l, xs2, mod_l, ln2_g[l], ln2_b[l],
                         tokens_per_batch=t, n_ctx=n_ctx).reshape(bsz, t, d)
    return xs[:, n_ctx:]
```

```python
import functools

import numpy as np
import jax
import jax.numpy as jnp
from jax import lax
from jax.experimental import pallas as pl
from jax.experimental.pallas import tpu as pltpu

F32 = jnp.float32
BF16 = jnp.bfloat16
HIGHEST = lax.Precision.HIGHEST

D_MODEL = 2048
DEPTH = 4
GRID_W = 64
GROUP_DIM = D_MODEL // 4
HEAD_DIM = 64
N_HEADS = GROUP_DIM // HEAD_DIM

RW_LORA = 64
RW_G_LORA = 128
RW_DECAY_SCALE = 0.606531
RW_GN_EPS = 64e-5
RW_IN = 3 * GROUP_DIM + 4 * RW_LORA + RW_G_LORA
RW_CHUNK = 64

MB_GROUPS = 2
MB_STATE = 128
MB_CONV = 5
MB_CHUNK = 128
MB_CONV_DIM = GROUP_DIM + 2 * MB_GROUPS * MB_STATE
MB_IN = GROUP_DIM + MB_CONV_DIM + 2 * N_HEADS

MLA_Q_LORA = 3 * D_MODEL // 16
MLA_KV_LORA = D_MODEL // 16
MLA_NOPE = 64
MLA_ROPE = 32
MLA_QK = MLA_NOPE + MLA_ROPE
MLA_QK_PAD = 128
MLA_IN = MLA_Q_LORA + MLA_KV_LORA + MLA_ROPE
ROPE_BASE = 10000.0

NA_WIN_ROWS = 8
NA_WIN_COLS = 16
NA_IN = 3 * GROUP_DIM
MASKED_SCORE = -1e30

N_EXPERTS = 16
N_EXPERT_GROUPS = 4
EXPERTS_PER_GROUP = N_EXPERTS // N_EXPERT_GROUPS
ROUTED_SCALE = 2.5

DEEPNORM_ALPHA = (2 * DEPTH) ** 0.25
LN_EPS = 1e-6
RMS_EPS = 1e-6

LANE = 128
OFF_RW = 0
OFF_MLA_Q = OFF_RW + RW_IN
OFF_MLA_KV = OFF_MLA_Q + MLA_Q_LORA
OFF_MLA_PE = OFF_MLA_KV + MLA_KV_LORA
PE_LANE = MLA_NOPE
OFF_MB_Z = OFF_MLA_PE + LANE
OFF_MB_XBC = OFF_MB_Z + GROUP_DIM
OFF_MB_DT = OFF_MB_XBC + MB_CONV_DIM
OFF_NA = OFF_MB_DT + LANE
N_IN_PAD = OFF_NA + NA_IN
assert OFF_MLA_Q % MLA_Q_LORA == 0 and OFF_MB_XBC % MB_CONV_DIM == 0 and OFF_MB_Z % GROUP_DIM == 0

V7X_VMEM_BYTES = 64 * 1024 * 1024
VMEM_LIMIT = V7X_VMEM_BYTES * 7 // 8


def _params(*sem):
    return pltpu.CompilerParams(dimension_semantics=sem, vmem_limit_bytes=VMEM_LIMIT)


def _pick_tile(n, target, quantum):
    best = None
    for t in range(quantum, min(n, target) + 1, quantum):
        if n % t == 0:
            best = t
    assert best is not None, (n, target, quantum)
    return best


def _is_ctx_row(tm, tokens_per_batch, n_ctx):
    pos = (pl.program_id(0) * tm) % tokens_per_batch + lax.broadcasted_iota(jnp.int32, (tm, 1), 0)
    return pos < n_ctx


def _proj_in_body(x_ref, scc_ref, scb_ref, shc_ref, shb_ref, w_ref, o_ref, xm_ref, *, tokens_per_batch, n_ctx):
    is_ctx = _is_ctx_row(x_ref.shape[0], tokens_per_batch, n_ctx)

    @pl.when(pl.program_id(1) == 0)
    def _():
        sc = jnp.where(is_ctx, scc_ref[0], scb_ref[0])
        sh = jnp.where(is_ctx, shc_ref[0], shb_ref[0])
        xm_ref[...] = (x_ref[...] * (1.0 + sc) + sh).astype(BF16)

    o_ref[...] = jnp.dot(xm_ref[...], w_ref[...], preferred_element_type=F32)


def _mod_specs(bsz, tiles_per_batch, chunk, d):
    return [pl.BlockSpec((None, 1, d), lambda i, *_: (bsz, 0, chunk)),
            pl.BlockSpec((None, 1, d), lambda i, *_: (i // tiles_per_batch, 0, chunk))]


def proj_in(xs, mod_l, w, layer, *, tokens_per_batch, n_ctx):
    m, d = xs.shape
    n = w.shape[2]
    bsz = m // tokens_per_batch
    tm = _pick_tile(tokens_per_batch, 1100, 16)
    tn = _pick_tile(n, 1152, LANE)
    tiles_per_batch = tokens_per_batch // tm
    return pl.pallas_call(
        functools.partial(_proj_in_body, tokens_per_batch=tokens_per_batch, n_ctx=n_ctx),
        grid=(m // tm, n // tn),
        in_specs=[pl.BlockSpec((tm, d), lambda i, j: (i, 0)),
                  *_mod_specs(bsz, tiles_per_batch, 1, d), *_mod_specs(bsz, tiles_per_batch, 0, d),
                  pl.BlockSpec((None, d, tn), lambda i, j: (layer, 0, j))],
        out_specs=pl.BlockSpec((tm, tn), lambda i, j: (i, j)),
        out_shape=jax.ShapeDtypeStruct((m, n), F32),
        scratch_shapes=[pltpu.VMEM((tm, d), BF16)],
        compiler_params=_params("parallel", "arbitrary"),
        name="proj_in",
    )(xs, mod_l, mod_l, mod_l, mod_l, w)


def _proj_out_body(ryf_ref, ryb_ref, rbonus_ref, rg_ref, gng_ref, gnb_ref,
                   myf_ref, myb_ref, mx_ref, mz_ref, dskip_ref, mng_ref,
                   mla_ref, na_ref, w_ref, x_ref, gc_ref, gb_ref, lg_ref, lb_ref,
                   sc2c_ref, sc2b_ref, sh2c_ref, sh2b_ref, rw_ref, rb_ref, o_ref, h_ref, gate_ref,
                   *, tokens_per_batch, n_ctx):
    gd = GROUP_DIM
    is_ctx = _is_ctx_row(x_ref.shape[0], tokens_per_batch, n_ctx)
    lane_r = lax.broadcasted_iota(jnp.int32, (gd, gd), 0) // HEAD_DIM
    lane_c = lax.broadcasted_iota(jnp.int32, (gd, gd), 1) // HEAD_DIM
    same_head = (lane_r == lane_c).astype(F32)
    head_mean = lambda m: _bmm(m, same_head, 'tn,nm->tm', 'split_a') * (1.0 / HEAD_DIM)
    y = ryf_ref[...] + ryb_ref[...]
    yc = y - head_mean(y)
    y = yc * lax.rsqrt(head_mean(yc * yc) + RW_GN_EPS) * gng_ref[...] + gnb_ref[...]
    o_rw = (y + rbonus_ref[...]) * rg_ref[...]
    y = (myf_ref[...] + myb_ref[...] + dskip_ref[...] * mx_ref[...]) * jax.nn.silu(mz_ref[...])
    group = gd // MB_GROUPS
    normed = []
    for g in range(MB_GROUPS):
        yg = y[:, g * group:(g + 1) * group]
        normed.append(yg * lax.rsqrt(jnp.mean(yg * yg, axis=-1, keepdims=True) + RMS_EPS))
    o_mb = jnp.concatenate(normed, axis=1) * mng_ref[...]

    y = None
    for j, mix in enumerate((o_rw, o_mb, mla_ref[...], na_ref[...])):
        part = jnp.dot(mix.astype(BF16), w_ref[j * gd:(j + 1) * gd, :], preferred_element_type=F32)
        y = part if y is None else y + part
    z = DEEPNORM_ALPHA * x_ref[...] + jnp.where(is_ctx, gc_ref[0], gb_ref[0]) * y
    mu = jnp.mean(z, axis=-1, keepdims=True)
    zc = z - mu
    var = jnp.mean(zc * zc, axis=-1, keepdims=True)
    xs2 = zc * lax.rsqrt(var + LN_EPS) * lg_ref[...] + lb_ref[...]
    o_ref[...] = xs2
    h = xs2 * (1.0 + jnp.where(is_ctx, sc2c_ref[0], sc2b_ref[0])) + jnp.where(is_ctx, sh2c_ref[0], sh2b_ref[0])
    h_ref[...] = h.astype(BF16)
    gate_ref[...] = _route(h, rw_ref, rb_ref)


def proj_out(rwkv, mamba, o_mla, o_na, p, w, layer, xs, mod_l, ln_g, ln_b, router_w, router_b,
             *, tokens_per_batch, n_ctx, tile):
    m, d = xs.shape
    gd = GROUP_DIM
    bsz = m // tokens_per_batch
    tm = tile
    tiles_per_batch = tokens_per_batch // tm
    row = lambda i: (i, 0)
    fixed = lambda i: (0, 0)
    tok = pl.BlockSpec((tm, gd), row)
    vec = pl.BlockSpec((1, gd), fixed)
    ryf, ryb, rbonus, rg, gn_g, gn_b = rwkv
    myf, myb, xbc, d_skip, norm_g = mamba
    as_vec = lambda v: v.reshape(1, gd)
    return pl.pallas_call(
        functools.partial(_proj_out_body, tokens_per_batch=tokens_per_batch, n_ctx=n_ctx),
        grid=(m // tm,),
        in_specs=[tok, tok, tok, tok, vec, vec,
                  tok, tok, tok, pl.BlockSpec((tm, gd), lambda i: (i, OFF_MB_Z // gd)), vec, vec,
                  tok, tok,
                  pl.BlockSpec((None, d, d), lambda i: (layer, 0, 0)), pl.BlockSpec((tm, d), row),
                  *_mod_specs(bsz, tiles_per_batch, 2, d), pl.BlockSpec((1, d), fixed), pl.BlockSpec((1, d), fixed),
                  *_mod_specs(bsz, tiles_per_batch, 4, d), *_mod_specs(bsz, tiles_per_batch, 3, d),
                  pl.BlockSpec((d, LANE), fixed), pl.BlockSpec((N_EXPERTS, 1), fixed)],
        out_specs=[pl.BlockSpec((tm, d), row), pl.BlockSpec((tm, d), row), pl.BlockSpec((tm, LANE), row)],
        out_shape=[jax.ShapeDtypeStruct((m, d), F32), jax.ShapeDtypeStruct((m, d), BF16),
                   jax.ShapeDtypeStruct((m, LANE), F32)],
        compiler_params=_params("parallel"),
        name="proj_out",
    )(ryf, ryb, rbonus, rg, as_vec(gn_g), as_vec(gn_b),
      myf, myb, xbc, p, as_vec(jnp.repeat(d_skip, HEAD_DIM)), as_vec(norm_g),
      o_mla, o_na, w, xs, mod_l, mod_l, ln_g.reshape(1, d), ln_b.reshape(1, d),
      mod_l, mod_l, mod_l, mod_l, jnp.pad(router_w, ((0, 0), (0, LANE - N_EXPERTS))), router_b.reshape(N_EXPERTS, 1))


def _mod_body(c_ref, w_ref, b_ref, o_ref, *, n_rows):
    w = w_ref[...]
    reps = w.shape[1] // LANE
    o_ref[...] = jnp.zeros_like(o_ref)
    for m in range(n_rows):
        cb = c_ref[m]
        cbt = jnp.concatenate([cb] * reps, axis=1)
        o_ref[m:m + 1, :] = jnp.sum(w * cbt, axis=0, keepdims=True) + b_ref[...]


def modulation(cond, ada_w, ada_b):
    n_rows, k = cond.shape
    depth, _, n = ada_w.shape
    tn = 512
    cond_b = jnp.broadcast_to(cond[:, :, None], (n_rows, k, LANE))
    return pl.pallas_call(
        functools.partial(_mod_body, n_rows=n_rows),
        grid=(depth, n // tn),
        in_specs=[pl.BlockSpec((n_rows, k, LANE), lambda l, j: (0, 0, 0)),
                  pl.BlockSpec((None, k, tn), lambda l, j: (l, 0, j)),
                  pl.BlockSpec((None, 1, tn), lambda l, j: (l, 0, j))],
        out_specs=pl.BlockSpec((None, 8, tn), lambda l, j: (l, 0, j)),
        out_shape=jax.ShapeDtypeStruct((depth, 8, n), F32),
        compiler_params=_params("parallel", "parallel"),
        name="modulation",
    )(cond_b, ada_w, ada_b.reshape(depth, 1, n))


def _bmm(a, b, spec, mode):
    if mode == 'f32':
        return jnp.einsum(spec, a, b, preferred_element_type=F32, precision=HIGHEST)
    a_hi, b_hi = a.astype(BF16), b.astype(BF16)
    out = jnp.einsum(spec, a_hi, b_hi, preferred_element_type=F32)
    if mode in ('x3', 'split_b'):
        b_lo = (b - b_hi.astype(F32)).astype(BF16)
        out = out + jnp.einsum(spec, a_hi, b_lo, preferred_element_type=F32)
    if mode in ('x3', 'split_a'):
        a_lo = (a - a_hi.astype(F32)).astype(BF16)
        out = out + jnp.einsum(spec, a_lo, b_hi, preferred_element_type=F32)
    return out


def _unit_triangular_inverse(l_mat, ti, si):
    c = l_mat.shape[1]
    nn = 'hts,hsn->htn'
    blk = 8
    eye = (ti == si).astype(F32)
    l_d = jnp.where((ti // blk) == (si // blk), l_mat, 0.0)
    p2 = _bmm(l_d, l_d, nn, 'bf16')
    p4 = _bmm(p2, p2, nn, 'bf16')
    inv = _bmm(eye - l_d, eye + p2, nn, 'bf16')
    inv = _bmm(inv, eye + p4, nn, 'bf16')
    while blk < c:
        pair = ((ti // (2 * blk)) == (si // (2 * blk))) & ((ti // blk) != (si // blk))
        off = jnp.where(pair, l_mat, 0.0)
        inv = inv - _bmm(_bmm(inv, off, nn, 'bf16'), inv, nn, 'bf16')
        blk *= 2
    return inv


def _first_head_lanes(shape):
    return lax.broadcasted_iota(jnp.int32, shape, len(shape) - 1) % LANE < HEAD_DIM


def _rwkv_body(*refs):
    fwd_in, bwd_in, ka_ref, (yf_ref, yb_ref, st_ref) = refs[:6], refs[6:12], refs[12], refs[13:]

    @pl.when(pl.program_id(0) == 0)
    def _():
        st_ref[...] = jnp.zeros_like(st_ref)

    bsz, c, width = fwd_in[0].shape
    n_pairs = width // LANE
    ti = lax.broadcasted_iota(jnp.int32, (c, c), 0)
    si = lax.broadcasted_iota(jnp.int32, (c, c), 1)
    units = []
    for in_refs, reverse in ((fwd_in, False), (bwd_in, True)):
        tri = ((si >= ti) if reverse else (si <= ti)).astype(F32)
        for row in range(bsz):
            r, k, v, kk, a, lw = (ref[row] for ref in in_refs)
            k = k * (1.0 + (a - 1.0) * ka_ref[...])
            cum = _bmm(tri, lw, 'ts,sn->tn', 'split_b')
            tot = cum[:1] if reverse else cum[c - 1:]
            b = kk * a
            e_neg = jnp.exp(-cum)
            e_rem = jnp.exp(tot - cum)
            pieces = (r * jnp.exp(cum), kk * jnp.exp(cum - lw), k * e_neg, b * e_neg, k * e_rem, b * e_rem, v,
                      jnp.exp(tot))
            for j in range(n_pairs):
                units.append([m[:, j * LANE:(j + 1) * LANE] for m in pieces])
    n_units = len(units)
    rq, kq, kd, bd, kdc, bdc, v, e_tot = (jnp.stack([u[i] for u in units], axis=0) for i in range(8))

    backward = lax.broadcasted_iota(jnp.int32, (n_units, 1, 1), 0) >= n_units // 2
    order = (si - ti) * jnp.where(backward, -1, 1)
    incl = order <= 0
    strict = order < 0
    first = _first_head_lanes((c, LANE))
    first2 = _first_head_lanes((c, 2 * LANE))
    nt, nn, tn = 'utn,usn->uts', 'uts,usn->utn', 'usn,usm->unm'
    qq = jnp.concatenate([jnp.where(first, rq, 0.0), jnp.where(first, kq, 0.0),
                          jnp.where(first, 0.0, rq), jnp.where(first, 0.0, kq)], axis=1)
    ak = _bmm(qq, kd, nt, 'bf16')
    ab = _bmm(qq, bd, nt, 'bf16')
    a_rk = [jnp.where(incl, ak[:, 2 * x * c:(2 * x + 1) * c], 0.0) for x in range(2)]
    a_kk = [jnp.where(strict, ak[:, (2 * x + 1) * c:(2 * x + 2) * c], 0.0) for x in range(2)]
    a_rb = [jnp.where(incl, ab[:, 2 * x * c:(2 * x + 1) * c], 0.0) for x in range(2)]
    a_kb = [jnp.where(strict, ab[:, (2 * x + 1) * c:(2 * x + 2) * c], 0.0) for x in range(2)]
    inv = _unit_triangular_inverse(jnp.concatenate(a_kb, axis=0), ti, si)
    inv = [inv[:n_units], inv[n_units:]]

    xs = [_bmm(inv[x], jnp.concatenate([_bmm(a_kk[x], v, nn, 'bf16'), kq], axis=2), nn, 'bf16') for x in range(2)]
    x2 = jnp.where(first2, xs[0], xs[1])
    arx = jnp.where(first2, _bmm(a_rb[0], x2, nn, 'bf16'), _bmm(a_rb[1], x2, nn, 'bf16'))
    y0 = jnp.where(first, _bmm(a_rk[0], v, nn, 'bf16'), _bmm(a_rk[1], v, nn, 'bf16')) - arx[:, :, :LANE]
    rqp = rq - arx[:, :, LANE:]
    row_n = lax.broadcasted_iota(jnp.int32, (LANE, LANE), 0)
    col_n = lax.broadcasted_iota(jnp.int32, (LANE, LANE), 1)
    same_head = (row_n // HEAD_DIM) == (col_n // HEAD_DIM)
    bx = _bmm(bdc, x2, tn, 'bf16')
    tadd = jnp.where(same_head, _bmm(kdc, v, tn, 'bf16') - bx[:, :, :LANE], 0.0)
    decay = jnp.where(row_n == col_n, jnp.broadcast_to(e_tot, (n_units, LANE, LANE)), 0.0)
    p = jnp.where(same_head, decay - bx[:, :, LANE:], 0.0)
    t0 = st_ref[...]
    y = _bmm(rqp, t0, 'utn,unm->utm', 'bf16') + y0
    st_ref[...] = _bmm(p, t0, 'ujn,unm->ujm', 'bf16') + tadd
    for d, y_ref in enumerate((yf_ref, yb_ref)):
        for row in range(bsz):
            u0 = (d * bsz + row) * n_pairs
            y_ref[row] = jnp.concatenate([y[u0 + j] for j in range(n_pairs)], axis=1)


def _scan_chunk_index(c, n_ctx_chunks, n_chunks, reverse):
    if not reverse:
        return c
    return jnp.where(c < n_ctx_chunks, n_ctx_chunks - 1 - c, n_chunks - 1 - (c - n_ctx_chunks))


def rwkv_scan(r, k, v, kk, k_a, a_dirs, lw_dirs, *, n_ctx):
    bsz, t, width = r.shape
    c = RW_CHUNK
    n_chunks, n_ctx_chunks = t // c, n_ctx // c
    specs = [pl.BlockSpec((bsz, c, width),
                          lambda i, rev=rev: (0, _scan_chunk_index(i, n_ctx_chunks, n_chunks, rev), 0))
             for rev in (False, True)]
    n_units = 2 * bsz * (width // LANE)
    args = [(r, k, v, kk, a_dirs[d], lw_dirs[d]) for d in range(2)]
    return pl.pallas_call(
        _rwkv_body,
        grid=(n_chunks,),
        in_specs=[specs[0]] * 6 + [specs[1]] * 6 + [pl.BlockSpec((1, width), lambda i: (0, 0))],
        out_specs=specs,
        out_shape=[jax.ShapeDtypeStruct((bsz, t, width), F32)] * 2,
        scratch_shapes=[pltpu.VMEM((n_units, LANE, LANE), F32)],
        compiler_params=_params("arbitrary"),
        name="rwkv_scan",
    )(*args[0], *args[1], k_a.reshape(1, width))


def _ssd_body(*refs):
    fwd_in, bwd_in, bias_ref, aneg_ref, (yf_ref, yb_ref, st_ref) = refs[:2], refs[2:4], refs[4], refs[5], refs[6:]

    @pl.when(pl.program_id(0) == 0)
    def _():
        st_ref[...] = jnp.zeros_like(st_ref)

    bsz = fwd_in[0].shape[0]
    n_pairs = st_ref.shape[0] // (2 * bsz)
    results = []
    for d, in_refs in enumerate((fwd_in, bwd_in)):
        for row in range(bsz):
            base = (d * bsz + row) * n_pairs
            results.append(_ssd_chunk(in_refs[0][row], in_refs[1][row], bias_ref[...], aneg_ref[d:d + 1],
                                      [st_ref[base + j] for j in range(n_pairs)], reverse=bool(d)))
    for u, (ys, new_states) in enumerate(results):
        d, row = divmod(u, bsz)
        (yf_ref, yb_ref)[d][row] = jnp.concatenate(ys, axis=1)
        for j in range(n_pairs):
            st_ref[u * n_pairs + j] = new_states[j]


def _ssd_chunk(xbc, dt_raw, dt_bias, a_neg, states, *, reverse):
    c = xbc.shape[0]
    n_pairs = len(states)
    n_h = 2 * n_pairs
    dt = jax.nn.softplus(dt_raw + dt_bias)
    if reverse:
        dt = pltpu.roll(dt, LANE - n_h, axis=1)
    da_col = dt * a_neg
    gd = n_h * HEAD_DIM
    n_g = (xbc.shape[1] - gd) // (2 * MB_STATE)
    pairs_per_group = n_pairs // n_g
    ti = lax.broadcasted_iota(jnp.int32, (c, c), 0)
    si = lax.broadcasted_iota(jnp.int32, (c, c), 1)
    incl = (si >= ti) if reverse else (si <= ti)
    cs_c = _bmm(incl.astype(F32), da_col, 'ts,sn->tn', 'split_b')
    cs_r = cs_c.T
    tot = cs_c[:1, :] if reverse else cs_c[c - 1:, :]
    e_cs = jnp.exp(cs_c)
    e_rem = jnp.exp(tot - cs_c)
    e_tot = jnp.exp(tot)
    bm = [xbc[:, gd + g * MB_STATE:gd + (g + 1) * MB_STATE].astype(BF16) for g in range(n_g)]
    cm = [xbc[:, gd + (n_g + g) * MB_STATE:gd + (n_g + g + 1) * MB_STATE].astype(BF16) for g in range(n_g)]
    gmat = [jnp.einsum('ln,sn->ls', cm[g], bm[g], preferred_element_type=F32) for g in range(n_g)]
    first = _first_head_lanes((c, LANE))
    first_rows = lax.broadcasted_iota(jnp.int32, (LANE, 1), 0) < HEAD_DIM
    pick = lambda m, ha: jnp.where(first, m[:, ha:ha + 1], m[:, ha + 1:ha + 2])
    ys, new_states = [], []
    for j in range(n_pairs):
        g = j // pairs_per_group
        ha = 2 * j
        sl = slice(j * LANE, (j + 1) * LANE)
        xdt = xbc[:, sl] * pick(dt, ha)
        xdt_b = xdt.astype(BF16)
        y_heads = []
        for h in (ha, ha + 1):
            seg = cs_c[:, h:h + 1] - cs_r[h:h + 1, :]
            lmat = jnp.exp(jnp.where(incl, seg, MASKED_SCORE))
            y_heads.append(jnp.dot((gmat[g] * lmat).astype(BF16), xdt_b, preferred_element_type=F32))
        st = states[j]
        y_off = jnp.einsum('ln,pn->lp', cm[g], st.astype(BF16), preferred_element_type=F32) * pick(e_cs, ha)
        ys.append(jnp.where(first, y_heads[0], y_heads[1]) + y_off)
        xdec = (xdt * pick(e_rem, ha)).astype(BF16)
        keep = jnp.where(first_rows, e_tot[:, ha:ha + 1], e_tot[:, ha + 1:ha + 2])
        new_states.append(keep * st + jnp.einsum('lp,ln->pn', xdec, bm[g], preferred_element_type=F32))
    return ys, new_states


def ssd_scan(xbc, p, dt_bias, a_log, *, n_ctx):
    bsz, t, width = xbc.shape
    n_h = dt_bias.shape[1]
    gd = n_h * HEAD_DIM
    c = MB_CHUNK
    n_chunks, n_ctx_chunks = t // c, n_ctx // c
    in_specs, args, out_specs = [], [], []
    for reverse in (False, True):
        idx = lambda i, reverse=reverse: _scan_chunk_index(i, n_ctx_chunks, n_chunks, reverse)
        in_specs += [pl.BlockSpec((bsz, c, width), lambda i, idx=idx: (0, idx(i), 0)),
                     pl.BlockSpec((bsz, c, LANE), lambda i, idx=idx: (0, idx(i), OFF_MB_DT // LANE))]
        args += [xbc, p]
        out_specs.append(pl.BlockSpec((bsz, c, gd), lambda i, idx=idx: (0, idx(i), 0)))
    pad = ((0, 0), (0, LANE - n_h))
    in_specs += [pl.BlockSpec((1, LANE), lambda i: (0, 0)), pl.BlockSpec((2, LANE), lambda i: (0, 0))]
    args += [jnp.pad(dt_bias.reshape(1, 2 * n_h), ((0, 0), (0, LANE - 2 * n_h))), jnp.pad(-jnp.exp(a_log), pad)]
    return pl.pallas_call(
        _ssd_body,
        grid=(n_chunks,),
        in_specs=in_specs,
        out_specs=out_specs,
        out_shape=[jax.ShapeDtypeStruct((bsz, t, gd), F32)] * 2,
        scratch_shapes=[pltpu.VMEM((2 * bsz * (n_h // 2), 2 * HEAD_DIM, MB_STATE), F32)],
        compiler_params=_params("arbitrary"),
        name="ssd_scan",
    )(*args)


def _softmax_pv(s_parts, v_parts):
    m = functools.reduce(jnp.maximum, [jnp.max(s, axis=-1, keepdims=True) for s in s_parts])
    ps = [jnp.exp(s - m) for s in s_parts]
    l = functools.reduce(jnp.add, [jnp.sum(p, axis=-1, keepdims=True) for p in ps])
    o = functools.reduce(jnp.add, [jnp.dot(p.astype(BF16), v, preferred_element_type=F32)
                                   for p, v in zip(ps, v_parts)])
    return o / l


def _rope_lanes(x, cos_tab, sin_tab):
    width = x.shape[1]
    reps = width // MLA_QK_PAD
    cos_w = jnp.concatenate([cos_tab] * reps, axis=1) if reps > 1 else cos_tab
    sin_w = jnp.concatenate([sin_tab] * reps, axis=1) if reps > 1 else sin_tab
    lane = lax.broadcasted_iota(jnp.int32, x.shape, 1) % MLA_QK_PAD
    first_half = (lane - MLA_NOPE) % (MLA_ROPE // 2) < MLA_ROPE // 4
    shift = MLA_ROPE // 4
    partner = jnp.where(first_half, pltpu.roll(x, width - shift, axis=1), pltpu.roll(x, shift, axis=1))
    return x * cos_w + partner * sin_w


def _attn_body(q_ref, k_ref, v_ref, pe_ref, cq_ref, sq_ref, ck_ref, sk_ref, o_ref, kb_ref, vb_ref,
               *, n_ctx, n_ctx_tiles):
    dq = q_ref.shape[2] // 2

    @pl.when(pl.program_id(2) == 0)
    def _():
        pe_rot = _rope_lanes(pe_ref[0], ck_ref[...], sk_ref[...])
        kb_ref[...] = (k_ref[0] + jnp.concatenate([pe_rot, pe_rot], axis=1)).astype(BF16)
        v2 = v_ref[0].astype(BF16)
        own = _first_head_lanes(v2.shape)
        vb_ref[0] = jnp.where(own, v2, jnp.ones_like(v2))
        vb_ref[1] = jnp.where(own, jnp.ones_like(v2), v2)

    q2 = (_rope_lanes(q_ref[0], cq_ref[...], sq_ref[...]) * (MLA_QK ** -0.5)).astype(BF16)

    def attend(n_keys):
        outs = []
        for x in range(2):
            s = jnp.einsum('qd,kd->qk', q2[:, x * dq:(x + 1) * dq], kb_ref[0:n_keys, x * dq:(x + 1) * dq],
                           preferred_element_type=F32)
            p = jnp.exp((s - jnp.max(s, axis=-1, keepdims=True)).astype(BF16))
            o = jnp.dot(p, vb_ref[x, 0:n_keys], preferred_element_type=F32)
            outs.append(o / pltpu.roll(o, HEAD_DIM, axis=1))
        o_ref[0] = jnp.where(_first_head_lanes(outs[0].shape), outs[0], outs[1])

    @pl.when(pl.program_id(2) < n_ctx_tiles)
    def _():
        attend(n_ctx)

    @pl.when(pl.program_id(2) >= n_ctx_tiles)
    def _():
        attend(k_ref.shape[1])


def attention(q, kv, p, tables, *, n_ctx, tile):
    bsz, t, _ = q.shape
    n_pairs = N_HEADS // 2
    dq2 = 2 * MLA_QK_PAD
    v_col0 = N_HEADS * MLA_QK_PAD // LANE
    cos_tab, sin_tab = tables
    q_tab = pl.BlockSpec((tile, MLA_QK_PAD), lambda b, j, i: (i, 0))
    k_tab = pl.BlockSpec((t, MLA_QK_PAD), lambda b, j, i: (0, 0))
    return pl.pallas_call(
        functools.partial(_attn_body, n_ctx=n_ctx, n_ctx_tiles=n_ctx // tile),
        grid=(bsz, n_pairs, t // tile),
        in_specs=[pl.BlockSpec((1, tile, dq2), lambda b, j, i: (b, i, j)),
                  pl.BlockSpec((1, t, dq2), lambda b, j, i: (b, 0, j)),
                  pl.BlockSpec((1, t, LANE), lambda b, j, i: (b, 0, v_col0 + j)),
                  pl.BlockSpec((1, t, LANE), lambda b, j, i: (b, 0, OFF_MLA_PE // LANE)),
                  q_tab, q_tab, k_tab, k_tab],
        out_specs=pl.BlockSpec((1, tile, LANE), lambda b, j, i: (b, i, j)),
        out_shape=jax.ShapeDtypeStruct((bsz, t, N_HEADS * HEAD_DIM), F32),
        scratch_shapes=[pltpu.VMEM((t, dq2), BF16), pltpu.VMEM((2, t, LANE), BF16)],
        compiler_params=_params("parallel", "parallel", "arbitrary"),
        name="attention",
    )(q, kv, kv, p, cos_tab, sin_tab, cos_tab, sin_tab)


def _norm_mm_body(x_ref, g_ref, w_ref, o_ref):
    x = x_ref[...]
    xn = x * lax.rsqrt(jnp.mean(x * x, axis=-1, keepdims=True) + RMS_EPS) * g_ref[...]
    o_ref[...] = jnp.dot(xn.astype(BF16), w_ref[...], preferred_element_type=F32)


def norm_matmul(p, col, g, w):
    m = p.shape[0]
    k, n = w.shape
    tm = _pick_tile(m, 640, 16)
    return pl.pallas_call(
        _norm_mm_body,
        grid=(m // tm,),
        in_specs=[pl.BlockSpec((tm, k), lambda i: (i, col // k)),
                  pl.BlockSpec((1, k), lambda i: (0, 0)),
                  pl.BlockSpec((k, n), lambda i: (0, 0))],
        out_specs=pl.BlockSpec((tm, n), lambda i: (i, 0)),
        out_shape=jax.ShapeDtypeStruct((m, n), F32),
        compiler_params=_params("parallel"),
        name="norm_matmul",
    )(p, g.reshape(1, k), w)


def _na_body(q_ref, k_ref, v_ref, bias_ref, o_ref, kb_ref, vb_ref, *, n_ctx, n_rows):
    i = pl.program_id(2)
    rows_per_step = q_ref.shape[1] // GRID_W
    win = NA_WIN_ROWS * GRID_W
    q2 = q_ref[0] * (HEAD_DIM ** -0.5)

    def head_queries(q_rows, x):
        first = _first_head_lanes(q_rows.shape)
        return jnp.where(first if x == 0 else ~first, q_rows, 0.0).astype(BF16)

    def scores(q, k):
        return jnp.einsum('qd,kd->qk', q, k, preferred_element_type=F32)

    @pl.when(i == 0)
    def _():
        kb_ref[...] = k_ref[0].astype(BF16)
        vb_ref[...] = v_ref[0].astype(BF16)
        kc, vc = kb_ref[0:n_ctx], vb_ref[0:n_ctx]
        outs = [_softmax_pv([scores(head_queries(q2, x), kc)], [vc]) for x in range(2)]
        o_ref[0] = jnp.where(_first_head_lanes(outs[0].shape), outs[0], outs[1])

    @pl.when(i > 0)
    def _():
        qs, kws, vws, biases = [], [], [], []
        for rr in range(rows_per_step):
            r = (i - 1) * rows_per_step + rr
            rs = jnp.clip(r - NA_WIN_ROWS // 2, 0, n_rows - NA_WIN_ROWS)
            start = pl.multiple_of(n_ctx + rs * GRID_W, GRID_W)
            kws.append(kb_ref[pl.ds(start, win)])
            vws.append(vb_ref[pl.ds(start, win)])
            q_row = q2[rr * GRID_W:(rr + 1) * GRID_W]
            qs.append(jnp.concatenate([head_queries(q_row, 0), head_queries(q_row, 1)], axis=0))
            biases.append(jnp.concatenate([bias_ref[0, r - rs], bias_ref[1, r - rs]], axis=0))
        qb = jnp.stack(qs, axis=0)
        kc, vc = kb_ref[0:n_ctx], vb_ref[0:n_ctx]
        s_loc = jnp.einsum('uqd,ukd->uqk', qb, jnp.stack(kws, axis=0), preferred_element_type=F32)
        s_loc = s_loc + jnp.stack(biases, axis=0)
        s_ctx = scores(qb.reshape(rows_per_step * 2 * GRID_W, LANE), kc).reshape(rows_per_step, 2 * GRID_W, n_ctx)
        m = jnp.maximum(jnp.max(s_loc, axis=-1, keepdims=True), jnp.max(s_ctx, axis=-1, keepdims=True))
        p_loc = jnp.exp(s_loc - m)
        p_ctx = jnp.exp(s_ctx - m)
        l = jnp.sum(p_loc, axis=-1, keepdims=True) + jnp.sum(p_ctx, axis=-1, keepdims=True)
        o = jnp.einsum('uqk,ukd->uqd', p_loc.astype(BF16), jnp.stack(vws, axis=0), preferred_element_type=F32)
        o = o + jnp.dot(p_ctx.reshape(rows_per_step * 2 * GRID_W, n_ctx).astype(BF16), vc,
                        preferred_element_type=F32).reshape(o.shape)
        o = o / l
        first = _first_head_lanes((GRID_W, LANE))
        for rr in range(rows_per_step):
            o_ref[0, rr * GRID_W:(rr + 1) * GRID_W] = jnp.where(first, o[rr, :GRID_W], o[rr, GRID_W:])


def na_bias_table(rpb):
    wr = NA_WIN_ROWS
    qv = np.arange(GRID_W)[:, None]
    cv = np.arange(GRID_W)[None, :]
    col_start = np.clip(qv - NA_WIN_COLS // 2, 0, GRID_W - NA_WIN_COLS)
    valid = (cv >= col_start) & (cv < col_start + NA_WIN_COLS)
    offs = np.arange(2 * NA_WIN_COLS - 1)[:, None, None]
    pick = ((cv - qv + (NA_WIN_COLS - 1))[None] == offs) & valid[None]
    cols = jnp.einsum('hro,oqc->hrqc', rpb, jnp.asarray(pick, F32), precision=HIGHEST)
    cols = jnp.where(valid, cols, MASKED_SCORE)
    tab = jnp.stack([cols[:, wr - 1 - var:2 * wr - 1 - var] for var in range(wr)], axis=1)
    tab = jnp.transpose(tab, (0, 1, 3, 2, 4))
    return tab.reshape(rpb.shape[0], wr, GRID_W, wr * GRID_W)


def neighborhood_attention(p, bias, *, n_ctx, col0):
    bsz, t, _ = p.shape
    n_rows = (t - n_ctx) // GRID_W
    n_pairs = GROUP_DIM // LANE
    c0 = col0 // LANE
    return pl.pallas_call(
        functools.partial(_na_body, n_ctx=n_ctx, n_rows=n_rows),
        grid=(bsz, n_pairs, t // n_ctx),
        in_specs=[pl.BlockSpec((1, n_ctx, LANE), lambda b, j, i: (b, i, c0 + j)),
                  pl.BlockSpec((1, t, LANE), lambda b, j, i: (b, 0, c0 + n_pairs + j)),
                  pl.BlockSpec((1, t, LANE), lambda b, j, i: (b, 0, c0 + 2 * n_pairs + j)),
                  pl.BlockSpec((2, NA_WIN_ROWS, GRID_W, NA_WIN_ROWS * GRID_W), lambda b, j, i: (j, 0, 0, 0))],
        out_specs=pl.BlockSpec((1, n_ctx, LANE), lambda b, j, i: (b, i, j)),
        out_shape=jax.ShapeDtypeStruct((bsz, t, GROUP_DIM), F32),
        scratch_shapes=[pltpu.VMEM((t, LANE), BF16), pltpu.VMEM((t, LANE), BF16)],
        compiler_params=_params("parallel", "parallel", "arbitrary"),
        name="neighborhood_attention",
    )(p, p, p, bias)


def _moe_body(h_ref, gate_ref, wg_ref, wu_ref, wd_ref, x_ref, gc_ref, gb_ref, lg_ref, lb_ref, o_ref, acc_ref,
              *, tokens_per_batch, n_ctx):
    e = pl.program_id(1)
    is_ctx = _is_ctx_row(x_ref.shape[0], tokens_per_batch, n_ctx)

    @pl.when(e == 0)
    def _():
        acc_ref[...] = jnp.zeros_like(acc_ref)

    h = h_ref[...]
    hid = jax.nn.silu(jnp.dot(h, wg_ref[0], preferred_element_type=F32)) * jnp.dot(h, wu_ref[0], preferred_element_type=F32)
    lane = lax.broadcasted_iota(jnp.int32, gate_ref.shape, 1)
    gate = jnp.sum(jnp.where(lane == e, gate_ref[...], 0.0), axis=1, keepdims=True)
    acc_ref[...] += gate * jnp.dot(hid.astype(BF16), wd_ref[0], preferred_element_type=F32)

    @pl.when(e == pl.num_programs(1) - 1)
    def _():
        z = DEEPNORM_ALPHA * x_ref[...] + jnp.where(is_ctx, gc_ref[0], gb_ref[0]) * acc_ref[...]
        mu = jnp.mean(z, axis=-1, keepdims=True)
        zc = z - mu
        var = jnp.mean(zc * zc, axis=-1, keepdims=True)
        o_ref[...] = zc * lax.rsqrt(var + LN_EPS) * lg_ref[...] + lb_ref[...]


def moe_experts(h, gate, w_gate, w_up, w_down, layer, xs, mod_l, ln_g, ln_b, *, tokens_per_batch, n_ctx):
    m, d = h.shape
    n_e, d_e = w_gate.shape[1], w_gate.shape[3]
    bsz = m // tokens_per_batch
    tm = _pick_tile(tokens_per_batch, 640, 16)
    tiles_per_batch = tokens_per_batch // tm
    row = lambda i, e: (i, 0)
    fixed = lambda i, e: (0, 0)
    return pl.pallas_call(
        functools.partial(_moe_body, tokens_per_batch=tokens_per_batch, n_ctx=n_ctx),
        grid=(m // tm, n_e),
        in_specs=[pl.BlockSpec((tm, d), row),
                  pl.BlockSpec((tm, LANE), row),
                  pl.BlockSpec((None, 1, d, d_e), lambda i, e: (layer, e, 0, 0)),
                  pl.BlockSpec((None, 1, d, d_e), lambda i, e: (layer, e, 0, 0)),
                  pl.BlockSpec((None, 1, d_e, d), lambda i, e: (layer, e, 0, 0)),
                  pl.BlockSpec((tm, d), row), *_mod_specs(bsz, tiles_per_batch, 5, d),
                  pl.BlockSpec((1, d), fixed), pl.BlockSpec((1, d), fixed)],
        out_specs=pl.BlockSpec((tm, d), row),
        out_shape=jax.ShapeDtypeStruct((m, d), F32),
        scratch_shapes=[pltpu.VMEM((tm, d), F32)],
        compiler_params=_params("parallel", "arbitrary"),
        name="moe_experts",
    )(h, gate, w_gate, w_up, w_down, xs, mod_l, mod_l, ln_g.reshape(1, d), ln_b.reshape(1, d))


def _rope_tables(t, n_ctx):
    n_freq = MLA_ROPE // 4
    pos_t = np.arange(t - n_ctx)
    pos = np.stack([pos_t // GRID_W, pos_t % GRID_W], axis=-1).astype(np.float32)
    inv_freq = jnp.asarray(ROPE_BASE, F32) ** (-jnp.arange(n_freq, dtype=F32) / n_freq)
    ang = jnp.asarray(pos)[:, :, None] * inv_freq
    ang = jnp.concatenate([jnp.zeros((n_ctx, 2, n_freq), F32), ang], axis=0)
    cos = jnp.broadcast_to(jnp.cos(ang)[:, :, None, :], (t, 2, 2, n_freq)).reshape(t, MLA_ROPE)
    sin = jnp.sin(ang)[:, :, None, :] * jnp.asarray([-1.0, 1.0], F32)[None, None, :, None]
    sin = sin.reshape(t, MLA_ROPE)
    pad = MLA_QK_PAD - MLA_QK
    cos_tab = jnp.concatenate([jnp.ones((t, MLA_NOPE), F32), cos, jnp.zeros((t, pad), F32)], axis=1)
    sin_tab = jnp.concatenate([jnp.zeros((t, MLA_NOPE), F32), sin, jnp.zeros((t, pad), F32)], axis=1)
    return cos_tab, sin_tab


HALO = 8


def _shifted_rows(x, halo_prev, halo_next, offset, pos, n_ctx, t):
    tm = x.shape[0]
    out = pltpu.roll(x, (-offset) % tm, axis=0)
    row = lax.broadcasted_iota(jnp.int32, (tm, 1), 0)
    for i in range(abs(offset)):
        if offset < 0:
            out = jnp.where(row == i, halo_prev[HALO + offset + i:HALO + offset + i + 1], out)
        else:
            out = jnp.where(row == tm - offset + i, halo_next[i:i + 1], out)
    src = pos + offset
    same = (src >= 0) & (src < t) & ((pos < n_ctx) == (src < n_ctx))
    return jnp.where(same, out, 0.0)


def _halo_specs(tm, t, width, col):
    per_tile, last = tm // HALO, t // HALO - 1
    return [pl.BlockSpec((1, HALO, width), lambda b, i: (b, jnp.maximum(i * per_tile - 1, 0), col)),
            pl.BlockSpec((1, HALO, width), lambda b, i: (b, jnp.minimum((i + 1) * per_tile, last), col))]


def _rwkv_prep_body(p_ref, pp_ref, pn_ref, mu_ref, w0_ref, a0_ref, kkw_ref, rk_ref, lora_ref,
                    r_ref, ko_ref, v_ref, kk_ref, a0o_ref, a1o_ref, lw0_ref, lw1_ref, g_ref, bonus_ref,
                    *, n_ctx, t):
    x = p_ref[0]
    tm = x.shape[0]
    gd = GROUP_DIM
    pos = pl.program_id(1) * tm + lax.broadcasted_iota(jnp.int32, (tm, 1), 0)
    prev = _shifted_rows(x, pp_ref[0], pn_ref[0], -1, pos, n_ctx, t)
    nxt = _shifted_rows(x, pp_ref[0], pn_ref[0], 1, pos, n_ctx, t)
    x = x + mu_ref[0:1] * (prev - x) + mu_ref[1:2] * (nxt - x)
    r, k, v = x[:, :gd], x[:, gd:2 * gd], x[:, 2 * gd:3 * gd]
    lora_in = jnp.concatenate([jnp.tanh(x[:, 3 * gd:3 * gd + 2 * RW_LORA]),
                               x[:, 3 * gd + 2 * RW_LORA:3 * gd + 4 * RW_LORA],
                               jax.nn.sigmoid(x[:, 3 * gd + 4 * RW_LORA:])], axis=1)
    lo = jnp.dot(lora_in.astype(BF16), lora_ref[...], preferred_element_type=F32)
    lane_r = lax.broadcasted_iota(jnp.int32, (gd, gd), 0) // HEAD_DIM
    lane_c = lax.broadcasted_iota(jnp.int32, (gd, gd), 1) // HEAD_DIM
    same_head = (lane_r == lane_c).astype(F32)
    head_sum = lambda m: _bmm(m, same_head, 'tn,nm->tm', 'split_a')
    kk = k * kkw_ref[...]
    kk_ref[0] = kk * lax.rsqrt(jnp.maximum(head_sum(kk * kk), 1e-12))
    for d, (a_ref, lw_ref) in enumerate(((a0o_ref, lw0_ref), (a1o_ref, lw1_ref))):
        a = jax.nn.sigmoid(a0_ref[d:d + 1] + lo[:, (2 + d) * gd:(3 + d) * gd])
        a_ref[0] = a
        lw_ref[0] = -RW_DECAY_SCALE * jax.nn.sigmoid(w0_ref[d:d + 1] + lo[:, d * gd:(d + 1) * gd])
    r_ref[0] = r
    ko_ref[0] = k
    v_ref[0] = v
    g_ref[0] = lo[:, 4 * gd:]
    bonus_ref[0] = head_sum(r * k * rk_ref[...]) * v


def rwkv_prep(p, mu, w0, a0, k_k, r_k, lora_w, *, n_ctx):
    bsz, t, _ = p.shape
    gd = GROUP_DIM
    tm = _pick_tile(t, 320, 16)
    row = lambda b, i: (b, i, 0)
    fixed = lambda b, i: (0, 0)
    vec = lambda w: w.reshape(1, gd)
    out = jax.ShapeDtypeStruct((bsz, t, gd), F32)
    return pl.pallas_call(
        functools.partial(_rwkv_prep_body, n_ctx=n_ctx, t=t),
        grid=(bsz, t // tm),
        in_specs=[pl.BlockSpec((1, tm, RW_IN), row), *_halo_specs(tm, t, RW_IN, 0),
                  pl.BlockSpec((2, RW_IN), fixed), pl.BlockSpec((2, gd), fixed), pl.BlockSpec((2, gd), fixed),
                  pl.BlockSpec((1, gd), fixed), pl.BlockSpec((1, gd), fixed), pl.BlockSpec(lora_w.shape, fixed)],
        out_specs=[pl.BlockSpec((1, tm, gd), row)] * 10,
        out_shape=[out] * 10,
        compiler_params=_params("parallel", "parallel"),
        name="rwkv_prep",
    )(p, p, p, mu, w0, a0, vec(k_k), vec(r_k), lora_w)


def _rwkv_mixer(p, n_ctx, mu, w0, w_up, a0, a_up, g_up, k_k, k_a, r_k, gn_g, gn_b):
    bsz, t, _ = p.shape
    gd = GROUP_DIM
    n_lora = 4 * RW_LORA + RW_G_LORA
    lora_w = jnp.zeros((n_lora, 5 * gd), F32)
    for j, blk in enumerate((w_up[0], w_up[1], a_up[0], a_up[1])):
        lora_w = lora_w.at[j * RW_LORA:(j + 1) * RW_LORA, j * gd:(j + 1) * gd].set(blk)
    lora_w = lora_w.at[4 * RW_LORA:, 4 * gd:].set(g_up)
    r, k, v, kk, a_f, a_b, lw_f, lw_b, g, bonus = rwkv_prep(p, mu, w0, a0, k_k, r_k, lora_w.astype(BF16), n_ctx=n_ctx)
    y_fwd, y_bwd = rwkv_scan(r, k, v, kk, k_a, (a_f, a_b), (lw_f, lw_b), n_ctx=n_ctx)
    flat = lambda m: m.reshape(bsz * t, gd)
    return flat(y_fwd), flat(y_bwd), flat(bonus), flat(g), gn_g, gn_b


def _conv_body(x_ref, xp_ref, xn_ref, w_ref, b_ref, o_ref, *, n_ctx, t):
    x = x_ref[0]
    tm = x.shape[0]
    pos = pl.program_id(1) * tm + lax.broadcasted_iota(jnp.int32, (tm, 1), 0)
    acc = b_ref[...] + w_ref[MB_CONV // 2:MB_CONV // 2 + 1] * x
    for j in range(MB_CONV):
        offset = j - MB_CONV // 2
        if offset != 0:
            acc = acc + w_ref[j:j + 1] * _shifted_rows(x, xp_ref[0], xn_ref[0], offset, pos, n_ctx, t)
    o_ref[0] = jax.nn.silu(acc)


def conv_silu(p, conv_w, conv_b, *, n_ctx):
    bsz, t, _ = p.shape
    width = MB_CONV_DIM
    tm = _pick_tile(t, 640, 16)
    col = OFF_MB_XBC // width
    return pl.pallas_call(
        functools.partial(_conv_body, n_ctx=n_ctx, t=t),
        grid=(bsz, t // tm),
        in_specs=[pl.BlockSpec((1, tm, width), lambda b, i: (b, i, col)), *_halo_specs(tm, t, width, col),
                  pl.BlockSpec((MB_CONV, width), lambda b, i: (0, 0)), pl.BlockSpec((1, width), lambda b, i: (0, 0))],
        out_specs=pl.BlockSpec((1, tm, width), lambda b, i: (b, i, 0)),
        out_shape=jax.ShapeDtypeStruct((bsz, t, width), F32),
        compiler_params=_params("parallel", "parallel"),
        name="conv_silu",
    )(p, p, p, conv_w, conv_b.reshape(1, width))


def _mamba_mixer(p, n_ctx, conv_w, conv_b, a_log, dt_bias, d_skip, norm_g):
    bsz, t, _ = p.shape
    xbc = conv_silu(p, conv_w, conv_b, n_ctx=n_ctx)
    y_fwd, y_bwd = ssd_scan(xbc, p, dt_bias, a_log, n_ctx=n_ctx)
    flat = lambda m: m.reshape(bsz * t, -1)
    return flat(y_fwd), flat(y_bwd), flat(xbc), d_skip, norm_g


def _mla_mixer(p, n_ctx, q_norm, w_uq, kv_norm, w_ukv, tile, tables):
    bsz, t, n_cols = p.shape
    n_h, pad = N_HEADS, MLA_QK_PAD - MLA_QK
    w_q = jnp.pad(w_uq.reshape(-1, n_h, MLA_QK), ((0, 0), (0, 0), (0, pad))).reshape(-1, n_h * MLA_QK_PAD)
    w_kv = w_ukv.reshape(-1, n_h, MLA_NOPE + HEAD_DIM)
    w_k = jnp.pad(w_kv[..., :MLA_NOPE], ((0, 0), (0, 0), (0, MLA_QK_PAD - MLA_NOPE))).reshape(-1, n_h * MLA_QK_PAD)
    w_v = w_kv[..., MLA_NOPE:].reshape(-1, n_h * HEAD_DIM)
    p2 = p.reshape(bsz * t, n_cols)
    q = norm_matmul(p2, OFF_MLA_Q, q_norm, w_q.astype(BF16)).reshape(bsz, t, -1)
    kv = norm_matmul(p2, OFF_MLA_KV, kv_norm, jnp.concatenate([w_k, w_v], axis=1).astype(BF16)).reshape(bsz, t, -1)
    return attention(q, kv, p, tables, n_ctx=n_ctx, tile=tile)


def _first_max(vals, excluded):
    live = [jnp.where(x, -jnp.inf, v) for v, x in zip(vals, excluded)]
    top = functools.reduce(jnp.maximum, live)
    found = jnp.zeros_like(top, dtype=jnp.bool_)
    first = []
    for v, x in zip(live, excluded):
        hit = (v == top) & ~found & ~x
        first.append(hit)
        found = found | hit
    return top, first


def _route(h, rw_ref, rb_ref):
    logits = _bmm(h, rw_ref[...], 'td,dn->tn', 'x3')
    scores = jax.nn.sigmoid(logits.T[:N_EXPERTS])
    biased = scores + rb_ref[...]
    rows = [biased[e:e + 1] for e in range(N_EXPERTS)]
    never = jnp.zeros_like(rows[0], dtype=jnp.bool_)
    group_scores, picked = [], []
    for g in range(N_EXPERT_GROUPS):
        vals = rows[g * EXPERTS_PER_GROUP:(g + 1) * EXPERTS_PER_GROUP]
        top1, first = _first_max(vals, [never] * EXPERTS_PER_GROUP)
        top2, second = _first_max(vals, first)
        group_scores.append(top1 + top2)
        picked.append([a | b for a, b in zip(first, second)])
    _, group_sel = _first_max(group_scores, [never] * N_EXPERT_GROUPS)
    chosen = [jnp.where(group_sel[e // EXPERTS_PER_GROUP] & picked[e // EXPERTS_PER_GROUP][e % EXPERTS_PER_GROUP],
                        scores[e:e + 1], 0.0) for e in range(N_EXPERTS)]
    denom = functools.reduce(jnp.add, chosen)
    gates = [w / denom * ROUTED_SCALE for w in chosen]
    gates.append(jnp.zeros((LANE - N_EXPERTS, denom.shape[1]), F32))
    return jnp.concatenate(gates, axis=0).T


def _pad_w_in(w):
    zeros = lambda n: jnp.zeros(w.shape[:-1] + (n,), w.dtype)
    mb0 = RW_IN
    mla0 = mb0 + MB_IN
    na0 = mla0 + MLA_IN
    rw, mb, mla, na = w[..., :mb0], w[..., mb0:mla0], w[..., mla0:na0], w[..., na0:]
    q_kv, pe = mla[..., :MLA_Q_LORA + MLA_KV_LORA], mla[..., MLA_Q_LORA + MLA_KV_LORA:]
    return jnp.concatenate([rw, q_kv, zeros(PE_LANE), pe, zeros(LANE - PE_LANE - MLA_ROPE),
                            mb, zeros(LANE - 2 * N_HEADS), na], axis=-1)


def kernel(x, c, ctx, c_ctx, ada_w, ada_b, w_in, w_out, ln1_g, ln1_b, ln2_g, ln2_b, rw_mu, rw_w0, rw_w_up, rw_a0, rw_a_up, rw_g_up, rw_k_k, rw_k_a, rw_r_k, rw_gn_g, rw_gn_b, mb_conv_w, mb_conv_b, mb_a_log, mb_dt_bias, mb_d, mb_norm_g, mla_q_norm, mla_w_uq, mla_kv_norm, mla_w_ukv, na_rpb, router_w, router_b, exp_w_gate, exp_w_up, exp_w_down):
    bsz, seq, d = x.shape
    n_ctx = ctx.shape[1]
    t = n_ctx + seq
    depth = ada_w.shape[0]
    tile = min(256, n_ctx)
    cond = jnp.concatenate([jax.nn.silu(c), jax.nn.silu(c_ctx)[None]], axis=0)
    mod = modulation(cond, ada_w, ada_b)
    xs = jnp.concatenate([ctx, x], axis=1)
    expert_w = [w.astype(BF16) for w in (exp_w_gate, exp_w_up, exp_w_down)]
    rope_tables = _rope_tables(t, n_ctx)
    w_in_all = _pad_w_in(w_in.astype(BF16))
    w_out_all = w_out.astype(BF16)
    m = bsz * t
    for l in range(depth):
        mod_l = mod[l][:, None, :]
        p = proj_in(xs.reshape(m, d), mod_l, w_in_all, l, tokens_per_batch=t, n_ctx=n_ctx).reshape(bsz, t, N_IN_PAD)
        o_rw = _rwkv_mixer(p, n_ctx, rw_mu[l], rw_w0[l], rw_w_up[l], rw_a0[l], rw_a_up[l],
                           rw_g_up[l], rw_k_k[l], rw_k_a[l], rw_r_k[l], rw_gn_g[l], rw_gn_b[l])
        o_mb = _mamba_mixer(p, n_ctx, mb_conv_w[l], mb_conv_b[l], mb_a_log[l], mb_dt_bias[l], mb_d[l], mb_norm_g[l])
        o_mla = _mla_mixer(p, n_ctx, mla_q_norm[l], mla_w_uq[l], mla_kv_norm[l], mla_w_ukv[l], tile, rope_tables)
        o_na = neighborhood_attention(p, na_bias_table(na_rpb[l]), n_ctx=n_ctx, col0=OFF_NA)
        xs2, h, gate = proj_out(o_rw, o_mb, o_mla.reshape(m, GROUP_DIM), o_na.reshape(m, GROUP_DIM),
                                p.reshape(m, N_IN_PAD), w_out_all, l, xs.reshape(m, d), mod_l, ln1_g[l], ln1_b[l],
                                router_w, router_b, tokens_per_batch=t, n_ctx=n_ctx, tile=tile)
        xs = moe_experts(h, gate, *expert_w, l, xs2, mod_l, ln2_g[l], ln2_b[l],
                         tokens_per_batch=t, n_ctx=n_ctx).reshape(bsz, t, d)
    return xs[:, n_ctx:]
```
